```python
import math
import jax
import jax.numpy as jnp
from jax import lax
import numpy as np

D_MODEL = 1024
BATCH = 32
SEQ = 2048
DEPTH = 1

MIX_WIDTH = D_MODEL
RG_WIDTH = MIX_WIDTH // 2
RG_BLOCKS = 8
RG_BLOCK = RG_WIDTH // RG_BLOCKS
CONV_W = 4
RG_C = 8.0
NSA_WIDTH = MIX_WIDTH - RG_WIDTH
NSA_HEADS = 8
HEAD_DIM = NSA_WIDTH // NSA_HEADS
NSA_KV = 2
NSA_HPG = NSA_HEADS // NSA_KV
KV_W = NSA_KV * HEAD_DIM
CMP_L = 32
CMP_STRIDE = 16
CMP_HIDDEN = 256
SEL_L = 64
N_SEL = 8
WINDOW = 512
Q_BLK = 64
NUM_BUCKETS = 32
MAX_DIST = 128
MEM_LEN = 256
X_HEADS = 4
X_HEAD_DIM = D_MODEL // X_HEADS
N_GROUPS = 4
EXP_PER_GROUP = 8
N_EXPERTS = N_GROUPS * EXP_PER_GROUP
TOP_K_IN_GROUP = 2
D_EXPERT = 512
MOE_BLK = 128
EPS = 1e-6
NEG_INF = -1e30
SEL_FORCE = 1e9
PROJ_SIZES = [RG_WIDTH, RG_WIDTH, NSA_WIDTH, KV_W, KV_W, KV_W, KV_W, KV_W, KV_W, 3 * NSA_HEADS]
PROJ_COLS = sum(PROJ_SIZES)

kernel_name = 'hymba_rglru_nsa_hiermoe_layer'


def rmsnorm(x, g):
    x32 = x.astype(jnp.float32)
    y = x32 * lax.rsqrt(jnp.mean(x32 * x32, axis=-1, keepdims=True) + EPS)
    return (y * g.astype(jnp.float32)).astype(x.dtype)


def masked_softmax(logits, mask):
    return jax.nn.softmax(jnp.where(mask, logits, NEG_INF), axis=-1)


def rel_bucket(dist):
    n = jnp.maximum(dist, 0)
    max_exact = NUM_BUCKETS // 2
    nf = jnp.maximum(n, 1).astype(jnp.float32)
    large = max_exact + (jnp.log(nf / max_exact) / math.log(MAX_DIST / max_exact)
                         * (NUM_BUCKETS - max_exact)).astype(jnp.int32)
    large = jnp.minimum(large, NUM_BUCKETS - 1)
    return jnp.where(n < max_exact, n, large)


def _lru_combine(c1, c2):
    a1, b1 = c1
    a2, b2 = c2
    return a1 * a2, a2 * b1 + b2


def rglru_group(u, gate, conv_w, conv_b, w_r, b_r, w_i, b_i, lam):
    bsz, seq, width = u.shape
    uc = lax.conv_general_dilated(u, conv_w, window_strides=(1,), padding=[(CONV_W - 1, 0)],
                                  dimension_numbers=('NWC', 'WIO', 'NWC'),
                                  feature_group_count=width) + conv_b
    ub = uc.reshape(bsz, seq, RG_BLOCKS, RG_BLOCK)
    r = jax.nn.sigmoid((jnp.einsum('bshi,hij->bshj', ub, w_r).reshape(bsz, seq, width) + b_r).astype(jnp.float32))
    i_g = jax.nn.sigmoid((jnp.einsum('bshi,hij->bshj', ub, w_i).reshape(bsz, seq, width) + b_i).astype(jnp.float32))
    log_a = -RG_C * r * jax.nn.softplus(-lam.astype(jnp.float32))
    a = jnp.exp(log_a)
    b_in = jnp.sqrt(-jnp.expm1(2.0 * log_a)) * i_g * uc.astype(jnp.float32)
    _, h = lax.associative_scan(_lru_combine, (a, b_in), axis=1)
    return (jax.nn.gelu(gate.astype(jnp.float32)) * h).astype(u.dtype)


def nsa_group(q, kc, vc, ks, vs, kw, vw, gate_logits, rel_bias, g_q, g_kc, g_ks, g_kw,
              pos_k, pos_v, ck_w1, ck_w2, cv_w1, cv_w2):
    bsz, seq = q.shape[:2]
    dt = q.dtype
    q = rmsnorm(q, g_q)
    ks = rmsnorm(ks, g_ks)
    kw = rmsnorm(kw, g_kw)
    scale = HEAD_DIM ** -0.5

    n_cmp = (seq - CMP_L) // CMP_STRIDE + 1
    cmp_start = jnp.arange(n_cmp, dtype=jnp.int32) * CMP_STRIDE
    cmp_end = cmp_start + CMP_L - 1
    cmp_center = cmp_start + CMP_L // 2
    gidx = cmp_start[:, None] + jnp.arange(CMP_L, dtype=jnp.int32)[None, :]

    def compress(t, pos, w1, w2):
        blk = t[:, gidx] + pos[None, None, :, None, :]
        blk = blk.transpose(0, 1, 3, 2, 4).reshape(bsz, n_cmp, NSA_KV, CMP_L * HEAD_DIM)
        return jax.nn.gelu(blk @ w1) @ w2

    k_cmp = rmsnorm(compress(kc, pos_k, ck_w1, ck_w2), g_kc)
    v_cmp = compress(vc, pos_v, cv_w1, cv_w2)

    n_sel = seq // SEL_L
    k_sel = min(N_SEL, n_sel)
    sel_start = jnp.arange(n_sel, dtype=jnp.int32) * SEL_L
    overlap = ((cmp_start[:, None] <= sel_start[None, :] + SEL_L - 1) &
               (cmp_end[:, None] >= sel_start[None, :])).astype(jnp.float32)
    ks_blk = ks.reshape(bsz, n_sel, SEL_L, NSA_KV, HEAD_DIM).transpose(0, 3, 1, 2, 4)
    vs_blk = vs.reshape(bsz, n_sel, SEL_L, NSA_KV, HEAD_DIM).transpose(0, 3, 1, 2, 4)
    gather = jax.vmap(jax.vmap(lambda kb, ib: kb[ib]))

    kw_pad = jnp.pad(kw, ((0, 0), (WINDOW, 0), (0, 0), (0, 0)))
    vw_pad = jnp.pad(vw, ((0, 0), (WINDOW, 0), (0, 0), (0, 0)))

    table_g = rel_bias.reshape(NUM_BUCKETS, NSA_KV, NSA_HPG).transpose(1, 0, 2)
    g_arange = jnp.arange(NSA_KV)[None, :, None, None]
    gates = jax.nn.sigmoid(gate_logits.astype(jnp.float32))
    n_blk = seq // Q_BLK
    q_b = q.reshape(bsz, n_blk, Q_BLK, NSA_KV, NSA_HPG, HEAD_DIM).transpose(1, 0, 2, 3, 4, 5)
    g_b = gates.reshape(bsz, n_blk, Q_BLK, NSA_KV, NSA_HPG, 3).transpose(1, 0, 2, 3, 4, 5)

    def block_fn(args):
        i, qb, gb = args
        t = i * Q_BLK + jnp.arange(Q_BLK, dtype=jnp.int32)
        lc = jnp.einsum('bqgpd,bcgd->bgpqc', qb, k_cmp).astype(jnp.float32) * scale
        lc = lc + table_g[:, rel_bucket(t[:, None] - cmp_center[None, :])].transpose(0, 3, 1, 2)[None]
        mc = cmp_end[None, :] <= t[:, None]
        pc = jnp.where(mc, masked_softmax(lc, mc), 0.0)
        o_c = jnp.einsum('bgpqc,bcgd->bqgpd', pc.astype(dt), v_cmp)
        imp = jnp.einsum('bgpqc,cn->bgqn', pc, overlap)
        cur = t // SEL_L
        blk_ids = jnp.arange(n_sel, dtype=jnp.int32)[None, :]
        forced = (blk_ids == 0) | (blk_ids == cur[:, None]) | (blk_ids == cur[:, None] - 1)
        future = blk_ids > cur[:, None]
        imp = jnp.where(forced, SEL_FORCE, jnp.where(future, -SEL_FORCE, imp))
        _, sel = lax.top_k(imp, k_sel)
        k_s = gather(ks_blk, sel).reshape(bsz, NSA_KV, Q_BLK, k_sel * SEL_L, HEAD_DIM)
        v_s = gather(vs_blk, sel).reshape(bsz, NSA_KV, Q_BLK, k_sel * SEL_L, HEAD_DIM)
        kpos = (sel[..., None] * SEL_L + jnp.arange(SEL_L, dtype=jnp.int32)).reshape(bsz, NSA_KV, Q_BLK, k_sel * SEL_L)
        dist = t[None, None, :, None] - kpos
        bs = table_g[g_arange, rel_bucket(dist)].transpose(0, 1, 4, 2, 3)
        ls = jnp.einsum('bqgpd,bgqkd->bgpqk', qb, k_s).astype(jnp.float32) * scale + bs
        ps = masked_softmax(ls, (dist >= 0)[:, :, None])
        o_s = jnp.einsum('bgpqk,bgqkd->bqgpd', ps.astype(dt), v_s)
        k_w = lax.dynamic_slice_in_dim(kw_pad, i * Q_BLK, WINDOW + Q_BLK, axis=1)
        v_w = lax.dynamic_slice_in_dim(vw_pad, i * Q_BLK, WINDOW + Q_BLK, axis=1)
        wpos = i * Q_BLK - WINDOW + jnp.arange(WINDOW + Q_BLK, dtype=jnp.int32)
        dw = t[:, None] - wpos[None, :]
        mw = (dw >= 0) & (dw < WINDOW) & (wpos[None, :] >= 0)
        lw = jnp.einsum('bqgpd,bkgd->bgpqk', qb, k_w).astype(jnp.float32) * scale
        lw = lw + table_g[:, rel_bucket(dw)].transpose(0, 3, 1, 2)[None]
        pw = masked_softmax(lw, mw)
        o_w = jnp.einsum('bgpqk,bkgd->bqgpd', pw.astype(dt), v_w)
        out = gb[..., 0:1] * o_c + gb[..., 1:2] * o_s + gb[..., 2:3] * o_w
        return out.astype(dt)

    o = lax.map(block_fn, (jnp.arange(n_blk, dtype=jnp.int32), q_b, g_b))
    return o.transpose(1, 0, 2, 3, 4, 5).reshape(bsz, seq, NSA_WIDTH)


def memory_cross_attention(h, mem, g_h, g_m, w_q, w_kv, w_o, g_q, g_k):
    bsz, seq, _ = h.shape
    hn = rmsnorm(h, g_h)
    mn = rmsnorm(mem, g_m)
    q = rmsnorm((hn @ w_q).reshape(bsz, seq, X_HEADS, X_HEAD_DIM), g_q)
    k, v = jnp.split(mn @ w_kv, 2, axis=-1)
    k = rmsnorm(k.reshape(bsz, -1, X_HEADS, X_HEAD_DIM), g_k)
    v = v.reshape(bsz, -1, X_HEADS, X_HEAD_DIM)
    logits = jnp.einsum('bshd,bmhd->bhsm', q, k).astype(jnp.float32) * (X_HEAD_DIM ** -0.5)
    p = jax.nn.softmax(logits, axis=-1)
    o = jnp.einsum('bhsm,bmhd->bshd', p.astype(v.dtype), v).reshape(bsz, seq, D_MODEL)
    return o @ w_o


def hierarchical_moe(h, g_norm, w_grp, b_grp, w_exp, b_exp, w1, w3, w2):
    xt = rmsnorm(h, g_norm).reshape(-1, D_MODEL)
    n_tok = xt.shape[0]
    gl = (xt @ w_grp).astype(jnp.float32) + b_grp
    gp = jax.nn.softmax(gl, axis=-1)
    g_sel = jnp.argmax(gl, axis=-1)
    p_g = jnp.take_along_axis(gp, g_sel[:, None], axis=1)[:, 0]
    el = (xt @ w_exp).astype(jnp.float32).reshape(n_tok, N_GROUPS, EXP_PER_GROUP) + b_exp.reshape(N_GROUPS, EXP_PER_GROUP)
    el = jnp.take_along_axis(el, g_sel[:, None, None], axis=1)[:, 0]
    top_p, top_e = lax.top_k(jax.nn.softmax(el, axis=-1), TOP_K_IN_GROUP)
    top_p = top_p / jnp.sum(top_p, axis=-1, keepdims=True)
    weights = (p_g[:, None] * top_p).reshape(-1)
    expert = (g_sel[:, None] * EXP_PER_GROUP + top_e).reshape(-1).astype(jnp.int32)
    token = jnp.repeat(jnp.arange(n_tok, dtype=jnp.int32), TOP_K_IN_GROUP)
    n_slots = n_tok * TOP_K_IN_GROUP
    order = jnp.argsort(expert)
    s_exp, s_tok, s_w = expert[order], token[order], weights[order]
    counts = jnp.bincount(expert, length=N_EXPERTS)
    starts = jnp.cumsum(counts) - counts
    pcounts = (counts + MOE_BLK - 1) // MOE_BLK * MOE_BLK
    pends = jnp.cumsum(pcounts)
    pstarts = pends - pcounts
    dest = pstarts[s_exp] + (jnp.arange(n_slots, dtype=jnp.int32) - starts[s_exp])
    n_blocks = -(-n_slots // MOE_BLK) + N_EXPERTS
    n_pad = n_blocks * MOE_BLK
    buf_tok = jnp.zeros((n_pad,), jnp.int32).at[dest].set(s_tok)
    buf_w = jnp.zeros((n_pad,), jnp.float32).at[dest].set(s_w)
    blk_exp = jnp.minimum(jnp.searchsorted(pends, jnp.arange(n_blocks, dtype=jnp.int32) * MOE_BLK, side='right'),
                          N_EXPERTS - 1)

    def expert_block(args):
        e, tok = args
        xb = xt[tok]
        return (jax.nn.silu(xb @ w1[e]) * (xb @ w3[e])) @ w2[e]

    yb = lax.map(expert_block, (blk_exp, buf_tok.reshape(n_blocks, MOE_BLK))).reshape(n_pad, D_MODEL)
    y = jnp.zeros((n_tok, D_MODEL), jnp.float32).at[buf_tok].add(yb.astype(jnp.float32) * buf_w[:, None])
    return y.reshape(h.shape).astype(h.dtype)


def setup_inputs(seed: int = 0) -> dict:
    key = jax.random.key(seed)
    kit = iter(jax.random.split(key, 64))
    f32 = jnp.float32
    L = DEPTH

    def nrm(shape, scale):
        return jax.random.normal(next(kit), shape, f32) * scale

    def gain(shape):
        return 1.0 + 0.02 * jax.random.normal(next(kit), shape, f32)

    x = nrm((BATCH, SEQ, D_MODEL), 1.0)
    mem = nrm((BATCH, MEM_LEN, D_MODEL), 1.0)
    rel_bias = nrm((NUM_BUCKETS, NSA_HEADS), 0.5)
    norm_mix = gain((L, D_MODEL))
    w_in = nrm((L, D_MODEL, PROJ_COLS), D_MODEL ** -0.5)
    rg_conv_w = nrm((L, CONV_W, 1, RG_WIDTH), CONV_W ** -0.5)
    rg_conv_b = nrm((L, RG_WIDTH), 0.02)
    rg_w_r = nrm((L, RG_BLOCKS, RG_BLOCK, RG_BLOCK), RG_BLOCK ** -0.5)
    rg_b_r = nrm((L, RG_WIDTH), 0.02)
    rg_w_i = nrm((L, RG_BLOCKS, RG_BLOCK, RG_BLOCK), RG_BLOCK ** -0.5)
    rg_b_i = nrm((L, RG_WIDTH), 0.02)
    a0 = jax.random.uniform(next(kit), (L, RG_WIDTH), f32, minval=0.9, maxval=0.999) ** (1.0 / RG_C)
    rg_lambda = jnp.log(a0) - jnp.log1p(-a0)
    nsa_g_q = gain((L, HEAD_DIM))
    nsa_g_kc = gain((L, HEAD_DIM))
    nsa_g_ks = gain((L, HEAD_DIM))
    nsa_g_kw = gain((L, HEAD_DIM))
    cmp_pos_k = nrm((L, CMP_L, HEAD_DIM), 0.02)
    cmp_pos_v = nrm((L, CMP_L, HEAD_DIM), 0.02)
    cmp_k_w1 = nrm((L, CMP_L * HEAD_DIM, CMP_HIDDEN), (CMP_L * HEAD_DIM) ** -0.5)
    cmp_k_w2 = nrm((L, CMP_HIDDEN, HEAD_DIM), CMP_HIDDEN ** -0.5)
    cmp_v_w1 = nrm((L, CMP_L * HEAD_DIM, CMP_HIDDEN), (CMP_L * HEAD_DIM) ** -0.5)
    cmp_v_w2 = nrm((L, CMP_HIDDEN, HEAD_DIM), CMP_HIDDEN ** -0.5)
    out_g_rg = gain((L, RG_WIDTH))
    out_g_nsa = gain((L, NSA_WIDTH))
    w_out = nrm((L, MIX_WIDTH, D_MODEL), MIX_WIDTH ** -0.5)
    norm_x = gain((L, D_MODEL))
    norm_mem = gain((L, D_MODEL))
    xa_w_q = nrm((L, D_MODEL, D_MODEL), D_MODEL ** -0.5)
    xa_w_kv = nrm((L, D_MODEL, 2 * D_MODEL), D_MODEL ** -0.5)
    xa_w_o = nrm((L, D_MODEL, D_MODEL), D_MODEL ** -0.5)
    xa_g_q = gain((L, X_HEAD_DIM))
    xa_g_k = gain((L, X_HEAD_DIM))
    norm_moe = gain((L, D_MODEL))
    router_g_w = nrm((L, D_MODEL, N_GROUPS), D_MODEL ** -0.5)
    router_g_b = nrm((L, N_GROUPS), 0.01)
    router_e_w = nrm((L, D_MODEL, N_EXPERTS), D_MODEL ** -0.5)
    router_e_b = nrm((L, N_EXPERTS), 0.01)
    exp_w1 = nrm((L, N_EXPERTS, D_MODEL, D_EXPERT), D_MODEL ** -0.5)
    exp_w3 = nrm((L, N_EXPERTS, D_MODEL, D_EXPERT), D_MODEL ** -0.5)
    exp_w2 = nrm((L, N_EXPERTS, D_EXPERT, D_MODEL), D_EXPERT ** -0.5)
    return {'x': x, 'mem': mem, 'rel_bias': rel_bias, 'norm_mix': norm_mix, 'w_in': w_in,
            'rg_conv_w': rg_conv_w, 'rg_conv_b': rg_conv_b, 'rg_w_r': rg_w_r, 'rg_b_r': rg_b_r,
            'rg_w_i': rg_w_i, 'rg_b_i': rg_b_i, 'rg_lambda': rg_lambda,
            'nsa_g_q': nsa_g_q, 'nsa_g_kc': nsa_g_kc, 'nsa_g_ks': nsa_g_ks, 'nsa_g_kw': nsa_g_kw,
            'cmp_pos_k': cmp_pos_k, 'cmp_pos_v': cmp_pos_v, 'cmp_k_w1': cmp_k_w1, 'cmp_k_w2': cmp_k_w2,
            'cmp_v_w1': cmp_v_w1, 'cmp_v_w2': cmp_v_w2, 'out_g_rg': out_g_rg, 'out_g_nsa': out_g_nsa,
            'w_out': w_out, 'norm_x': norm_x, 'norm_mem': norm_mem, 'xa_w_q': xa_w_q, 'xa_w_kv': xa_w_kv,
            'xa_w_o': xa_w_o, 'xa_g_q': xa_g_q, 'xa_g_k': xa_g_k, 'norm_moe': norm_moe,
            'router_g_w': router_g_w, 'router_g_b': router_g_b, 'router_e_w': router_e_w,
            'router_e_b': router_e_b, 'exp_w1': exp_w1, 'exp_w3': exp_w3, 'exp_w2': exp_w2}


def reference(x, mem, rel_bias, norm_mix, w_in, rg_conv_w, rg_conv_b, rg_w_r, rg_b_r, rg_w_i, rg_b_i,
              rg_lambda, nsa_g_q, nsa_g_kc, nsa_g_ks, nsa_g_kw, cmp_pos_k, cmp_pos_v, cmp_k_w1, cmp_k_w2,
              cmp_v_w1, cmp_v_w2, out_g_rg, out_g_nsa, w_out, norm_x, norm_mem, xa_w_q, xa_w_kv, xa_w_o,
              xa_g_q, xa_g_k, norm_moe, router_g_w, router_g_b, router_e_w, router_e_b,
              exp_w1, exp_w3, exp_w2):
    bsz, seq, _ = x.shape
    offsets = np.cumsum(PROJ_SIZES)[:-1].tolist()
    h = x
    for l in range(DEPTH):
        proj = rmsnorm(h, norm_mix[l]) @ w_in[l]
        u, gate, q, kc, vc, ks, vs, kw, vw, gl = jnp.split(proj, offsets, axis=-1)
        y_rg = rglru_group(u, gate, rg_conv_w[l], rg_conv_b[l], rg_w_r[l], rg_b_r[l],
                           rg_w_i[l], rg_b_i[l], rg_lambda[l])
        kvs = lambda t: t.reshape(bsz, seq, NSA_KV, HEAD_DIM)
        y_nsa = nsa_group(q.reshape(bsz, seq, NSA_HEADS, HEAD_DIM), kvs(kc), kvs(vc), kvs(ks), kvs(vs),
                          kvs(kw), kvs(vw), gl.reshape(bsz, seq, NSA_HEADS, 3), rel_bias,
                          nsa_g_q[l], nsa_g_kc[l], nsa_g_ks[l], nsa_g_kw[l], cmp_pos_k[l], cmp_pos_v[l],
                          cmp_k_w1[l], cmp_k_w2[l], cmp_v_w1[l], cmp_v_w2[l])
        mixed = jnp.concatenate([rmsnorm(y_rg, out_g_rg[l]), rmsnorm(y_nsa, out_g_nsa[l])], axis=-1)
        h = h + mixed @ w_out[l]
        h = h + memory_cross_attention(h, mem, norm_x[l], norm_mem[l], xa_w_q[l], xa_w_kv[l], xa_w_o[l],
                                       xa_g_q[l], xa_g_k[l])
        h = h + hierarchical_moe(h, norm_moe[l], router_g_w[l], router_g_b[l], router_e_w[l], router_e_b[l],
                                 exp_w1[l], exp_w3[l], exp_w2[l])
    return h
```

```python
import functools
import math

import numpy as np
import jax
import jax.numpy as jnp
from jax import lax
from jax.experimental import pallas as pl
from jax.experimental.pallas import tpu as pltpu

F32 = jnp.float32
BF16 = jnp.bfloat16

D_MODEL = 1024
RG_WIDTH = 512
RG_BLOCKS = 8
RG_BLOCK = 64
CONV_W = 4
RG_C = 8.0
NSA_WIDTH = 512
NSA_HEADS = 8
HEAD_DIM = 64
NSA_KV = 2
NSA_HPG = 4
KV_W = 128
CMP_L = 32
CMP_STRIDE = 16
CMP_HIDDEN = 256
SEL_L = 64
N_SEL = 8
WINDOW = 512
NUM_BUCKETS = 32
MAX_DIST = 128
X_HEADS = 4
X_HEAD_DIM = 256
N_GROUPS = 4
EXP_PER_GROUP = 8
N_EXPERTS = 32
D_EXPERT = 512
EPS = 1e-6
NEG_INF = -1e30
MASKED_BELOW = -1e29
SEL_FORCE = 1e9
LANES = 128

TQ = 64
NEAR = WINDOW + TQ
FAR_TK = 512
QROWS = NSA_HPG * TQ

TM_PROJ = 512
TM_MID = 256
TM_DEST = 512
TM_DMA = 512
MOE_TB = 256
TM_COMB = 512
RG_CHUNK = 256
VMEM_LIMIT = 56 * 1024 * 1024


def _cparams(n_axes):
    return pltpu.CompilerParams(dimension_semantics=("arbitrary",) * n_axes,
                                vmem_limit_bytes=VMEM_LIMIT)


def _dot(a, b):
    return jnp.dot(a, b, preferred_element_type=F32)


def _dot_nt(a, b):
    return lax.dot_general(a, b, (((1,), (1,)), ((), ())), preferred_element_type=F32)


def _gelu_tanh(x):
    return 0.5 * x * (1.0 + jnp.tanh(math.sqrt(2.0 / math.pi) * (x + 0.044715 * (x * x * x))))


def _rms(x, g):
    return x * lax.rsqrt(jnp.mean(x * x, axis=-1, keepdims=True) + EPS) * g


def _group_rms(x, ones_blk, g):
    ms = _dot((x * x).astype(BF16), ones_blk)
    return x * lax.rsqrt(ms + EPS) * g


def _inproj_kernel(x_ref, g_ref, wrg_ref, wq_ref, wkv_ref, wgl_ref, gq_ref, gks_ref, gkw_ref, ones_ref,
                   u_ref, gate_ref, q_ref, kc_ref, vc_ref, ks_ref, vs_ref, kw_ref, vw_ref, gates_ref):
    xb = _rms(x_ref[...], g_ref[...]).astype(BF16)
    rg = _dot(xb, wrg_ref[...])
    u_ref[...] = rg[:, :RG_WIDTH].astype(BF16)
    gate_ref[...] = rg[:, RG_WIDTH:].astype(BF16)
    q = _dot(xb, wq_ref[...])
    q_ref[...] = _group_rms(q, ones_ref[...], gq_ref[...]).astype(BF16)
    kv = _dot(xb, wkv_ref[...])
    ones_kv = ones_ref[:KV_W, :KV_W]
    kc_ref[...] = kv[:, 0 * KV_W:1 * KV_W].astype(BF16)
    vc_ref[...] = kv[:, 1 * KV_W:2 * KV_W].astype(BF16)
    ks_ref[...] = _group_rms(kv[:, 2 * KV_W:3 * KV_W], ones_kv, gks_ref[...]).astype(BF16)
    vs_ref[...] = kv[:, 3 * KV_W:4 * KV_W].astype(BF16)
    kw_ref[...] = _group_rms(kv[:, 4 * KV_W:5 * KV_W], ones_kv, gkw_ref[...]).astype(BF16)
    vw_ref[...] = kv[:, 5 * KV_W:6 * KV_W].astype(BF16)
    gates_ref[...] = jax.nn.sigmoid(_dot(xb, wgl_ref[...]))


def _inproj(x2, g, wrg, wq, wkv, wgl, gq, gks, gkw, ones_blk):
    n_tok = x2.shape[0]
    tm = min(TM_PROJ, n_tok)
    full = lambda a: pl.BlockSpec(a.shape, lambda i: (0,) * a.ndim)
    row = lambda w: pl.BlockSpec((tm, w), lambda i: (i, 0))
    outs = [(RG_WIDTH, BF16), (RG_WIDTH, BF16), (NSA_WIDTH, BF16)] + [(KV_W, BF16)] * 6 + [(LANES, F32)]
    return pl.pallas_call(
        _inproj_kernel,
        grid=(n_tok // tm,),
        in_specs=[row(D_MODEL)] + [full(a) for a in (g, wrg, wq, wkv, wgl, gq, gks, gkw, ones_blk)],
        out_specs=[row(w) for w, _ in outs],
        out_shape=[jax.ShapeDtypeStruct((n_tok, w), dt) for w, dt in outs],
        compiler_params=_cparams(1),
    )(x2, g, wrg, wq, wkv, wgl, gq, gks, gkw, ones_blk)


def _rglru_kernel(u_ref, gate_ref, cw_ref, cb_ref, wg_ref, bg_ref, lam_ref, og_ref, y_ref, upad, a_s, h_s):
    seq = u_ref.shape[1]
    upad[0:8, :] = jnp.zeros((8, RG_WIDTH), F32)
    upad[8:8 + seq, :] = u_ref[0].astype(F32)
    neg_lam = -lam_ref[...]
    softplus = jnp.maximum(neg_lam, 0.0) + jnp.log(1.0 + jnp.exp(-jnp.abs(neg_lam)))
    ch = min(RG_CHUNK, seq)
    for c in range(seq // ch):
        r0 = c * ch
        uc = cb_ref[...]
        for k in range(CONV_W):
            off = 8 + r0 - (CONV_W - 1) + k
            uc = uc + cw_ref[k:k + 1, :] * upad[off:off + ch, :]
        gt = _dot(uc.astype(BF16), wg_ref[...]) + bg_ref[...]
        r = jax.nn.sigmoid(gt[:, :RG_WIDTH])
        ig = jax.nn.sigmoid(gt[:, RG_WIDTH:])
        log_a = (-RG_C) * r * softplus
        a = jnp.exp(log_a)
        a_s[r0:r0 + ch, :] = a
        h_s[r0:r0 + ch, :] = jnp.sqrt(1.0 - a * a) * ig * uc

    def step(t, h):
        h = a_s[pl.ds(t, 1), :] * h + h_s[pl.ds(t, 1), :]
        h_s[pl.ds(t, 1), :] = h
        return h

    lax.fori_loop(0, seq, step, jnp.zeros((1, RG_WIDTH), F32), unroll=8)

    for c in range(seq // ch):
        r0 = c * ch
        y = _gelu_tanh(gate_ref[0, r0:r0 + ch, :].astype(F32)) * h_s[r0:r0 + ch, :]
        y_ref[0, r0:r0 + ch, :] = _rms(y, og_ref[...]).astype(BF16)


def _rglru(u3, gate3, cw, cb, wg, bg, lam, og):
    bsz, seq, _ = u3.shape
    full = lambda a: pl.BlockSpec(a.shape, lambda b: (0,) * a.ndim)
    blk = pl.BlockSpec((1, seq, RG_WIDTH), lambda b: (b, 0, 0))
    return pl.pallas_call(
        _rglru_kernel,
        grid=(bsz,),
        in_specs=[blk, blk] + [full(a) for a in (cw, cb, wg, bg, lam, og)],
        out_specs=blk,
        out_shape=jax.ShapeDtypeStruct((bsz, seq, RG_WIDTH), BF16),
        scratch_shapes=[pltpu.VMEM((seq + 8, RG_WIDTH), F32), pltpu.VMEM((seq, RG_WIDTH), F32),
                        pltpu.VMEM((seq, RG_WIDTH), F32)],
        compiler_params=_cparams(1),
    )(u3, gate3, cw, cb, wg, bg, lam, og)


def _compress_kernel(kx_ref, vx_ref, w1k_ref, w2k_ref, pk_ref, w1v_ref, w2v_ref, pv_ref, gk_ref, ones_ref,
                     ko_ref, vo_ref):
    n_chunk = kx_ref.shape[1]
    half = NSA_KV * CMP_HIDDEN

    def mlp(x_ref, w1_ref, w2_ref, p_ref):
        ab = _dot(x_ref[0], w1_ref[...])
        pos = _dot(p_ref[...], w1_ref[...])
        hid = ab[:, :half] + pltpu.roll(ab[:, half:], n_chunk - 1, 0) + (pos[0:1, :half] + pos[1:2, half:])
        return _dot(_gelu_tanh(hid).astype(BF16), w2_ref[...])

    kc = mlp(kx_ref, w1k_ref, w2k_ref, pk_ref)
    ko_ref[0] = _group_rms(kc, ones_ref[...], gk_ref[...]).astype(BF16)
    vo_ref[0] = mlp(vx_ref, w1v_ref, w2v_ref, pv_ref).astype(BF16)


def _compress(kx, vx, w1k, w2k, pk, w1v, w2v, pv, gk, ones_kv):
    bsz, n_chunk, width = kx.shape
    full = lambda a: pl.BlockSpec(a.shape, lambda b: (0,) * a.ndim)
    xin = pl.BlockSpec((1, n_chunk, width), lambda b: (b, 0, 0))
    out = pl.BlockSpec((1, n_chunk, KV_W), lambda b: (b, 0, 0))
    return pl.pallas_call(
        _compress_kernel,
        grid=(bsz,),
        in_specs=[xin, xin] + [full(a) for a in (w1k, w2k, pk, w1v, w2v, pv, gk, ones_kv)],
        out_specs=[out, out],
        out_shape=[jax.ShapeDtypeStruct((bsz, n_chunk, KV_W), BF16)] * 2,
        compiler_params=_cparams(1),
    )(kx, vx, w1k, w2k, pk, w1v, w2v, pv, gk, ones_kv)


def _nsa_kernel(q_ref, gates_ref, kcmp_ref, vcmp_ref, ksp_ref, vsp_ref, kwp_ref, vwp_ref, et_ref, ov_ref,
                bc_ref, bw_ref, bs_ref, bf_ref, og_ref, y_ref):
    i = pl.program_id(1)
    t0 = pl.multiple_of(i * TQ, TQ)
    lane = lax.broadcasted_iota(jnp.int32, (TQ, LANES), 1)
    lo_half = lane < HEAD_DIM
    q_slabs = [q_ref[0, :, p * LANES:(p + 1) * LANES] for p in range(NSA_HPG)]
    gates = gates_ref[0]
    kcmp = kcmp_ref[0]
    vcmp = vcmp_ref[0]
    near_j = lax.broadcasted_iota(jnp.int32, (1, NEAR), 1)
    near_ok = near_j >= WINDOW - t0
    kw_near = kwp_ref[0, pl.ds(t0, NEAR), :]
    vw_near = vwp_ref[0, pl.ds(t0, NEAR), :]
    ks_near = ksp_ref[0, pl.ds(t0, NEAR), :]
    vs_near = vsp_ref[0, pl.ds(t0, NEAR), :]
    et_near = et_ref[pl.ds(t0, NEAR), :]
    far_limit = t0 - WINDOW
    n_far = (jnp.maximum(far_limit, 0) + FAR_TK - 1) // FAR_TK
    blk_f = lane.astype(F32)

    def stack4(x):
        return jnp.concatenate([x] * NSA_HPG, axis=0)

    group_out = []
    for g in range(NSA_KV):
        half = lo_half if g == 0 else jnp.logical_not(lo_half)
        q4 = jnp.concatenate([jnp.where(half, qs, jnp.zeros_like(qs)) for qs in q_slabs], axis=0)

        bc = bc_ref[0, g]
        lc = _dot_nt(q4, kcmp) + bc
        mc = jnp.max(lc, axis=-1, keepdims=True)
        ec = jnp.where(bc > MASKED_BELOW, jnp.exp(lc - mc), 0.0)
        sc = jnp.sum(ec, axis=-1, keepdims=True)
        pc = ec / jnp.where(sc > 0.0, sc, 1.0)
        o_c = _dot(pc.astype(BF16), vcmp)

        pcs = pc[0:TQ] + pc[TQ:2 * TQ] + pc[2 * TQ:3 * TQ] + pc[3 * TQ:4 * TQ]
        pcs_hi = pcs.astype(BF16)
        pcs_lo = (pcs - pcs_hi.astype(F32)).astype(BF16)
        imp = _dot(pcs_hi, ov_ref[...]) + _dot(pcs_lo, ov_ref[...])
        forced = (lane == 0) | (lane == i) | (lane == i - 1)
        score = jnp.where(forced, SEL_FORCE, jnp.where(lane > i, -3e38, imp))
        sel = jnp.zeros((TQ, LANES), F32)
        for _ in range(N_SEL):
            best = jnp.max(score, axis=-1, keepdims=True)
            first = jnp.min(jnp.where(score == best, blk_f, 1e9), axis=-1, keepdims=True)
            pick = blk_f == first
            sel = jnp.where(pick, 1.0, sel)
            score = jnp.where(pick, -3e38, score)
        sel_b = sel.astype(BF16)

        lw = jnp.where(near_ok, _dot_nt(q4, kw_near) + bw_ref[g], NEG_INF)
        ew = jnp.exp(lw - jnp.max(lw, axis=-1, keepdims=True))
        o_w = _dot(ew.astype(BF16), vw_near) / jnp.sum(ew, axis=-1, keepdims=True)

        sel_near = stack4(_dot_nt(sel_b, et_near)) > 0.5
        ls = jnp.where(sel_near & near_ok, _dot_nt(q4, ks_near) + bs_ref[g], NEG_INF)
        m1 = jnp.max(ls, axis=-1, keepdims=True)
        e1 = jnp.exp(ls - m1)
        l1 = jnp.sum(e1, axis=-1, keepdims=True)
        acc1 = _dot(e1.astype(BF16), vs_near)
        bfar = bf_ref[g]

        def far_step(kf, carry):
            m, l, acc = carry
            base = pl.multiple_of(WINDOW + kf * FAR_TK, FAR_TK)
            kt = ksp_ref[0, pl.ds(base, FAR_TK), :]
            vt = vsp_ref[0, pl.ds(base, FAR_TK), :]
            pos = kf * FAR_TK + lax.broadcasted_iota(jnp.int32, (1, FAR_TK), 1)
            keep = (stack4(_dot_nt(sel_b, et_ref[pl.ds(base, FAR_TK), :])) > 0.5) & (pos < far_limit)
            lf = jnp.where(keep, _dot_nt(q4, kt) + bfar, NEG_INF)
            m_new = jnp.maximum(m, jnp.max(lf, axis=-1, keepdims=True))
            alpha = jnp.exp(m - m_new)
            e = jnp.exp(lf - m_new)
            l_new = alpha * l + jnp.sum(e, axis=-1, keepdims=True)
            return m_new, l_new, alpha * acc + _dot(e.astype(BF16), vt)

        _, l_s, acc_s = lax.fori_loop(0, n_far, far_step, (m1, l1, acc1))
        o_s = acc_s / l_s

        def gate_col(j):
            cols = [gates[:, (g * NSA_HPG + p) * 3 + j:(g * NSA_HPG + p) * 3 + j + 1] for p in range(NSA_HPG)]
            return jnp.concatenate(cols, axis=0)

        group_out.append(gate_col(0) * o_c + gate_col(1) * o_s + gate_col(2) * o_w)

    slabs = [jnp.where(lo_half, group_out[0][p * TQ:(p + 1) * TQ], group_out[1][p * TQ:(p + 1) * TQ])
             for p in range(NSA_HPG)]
    y_ref[0] = _rms(jnp.concatenate(slabs, axis=-1), og_ref[...]).astype(BF16)


def _nsa(q3, gates3, kcmp, vcmp, ksp, vsp, kwp, vwp, et, ov, bias_c, bias_w, bias_s, bias_far, og):
    bsz, seq, _ = q3.shape
    n_chunk = kcmp.shape[1]
    full = lambda a: pl.BlockSpec(a.shape, lambda b, i: (0,) * a.ndim)
    per_b = lambda a: pl.BlockSpec((1,) + a.shape[1:], lambda b, i: (b,) + (0,) * (a.ndim - 1))
    return pl.pallas_call(
        _nsa_kernel,
        grid=(bsz, seq // TQ),
        in_specs=[pl.BlockSpec((1, TQ, NSA_WIDTH), lambda b, i: (b, i, 0)),
                  pl.BlockSpec((1, TQ, LANES), lambda b, i: (b, i, 0)),
                  per_b(kcmp), per_b(vcmp), per_b(ksp), per_b(vsp), per_b(kwp), per_b(vwp),
                  full(et), full(ov),
                  pl.BlockSpec((1, NSA_KV, QROWS, n_chunk), lambda b, i: (i, 0, 0, 0)),
                  full(bias_w), full(bias_s), full(bias_far), full(og)],
        out_specs=pl.BlockSpec((1, TQ, NSA_WIDTH), lambda b, i: (b, i, 0)),
        out_shape=jax.ShapeDtypeStruct((bsz, seq, NSA_WIDTH), BF16),
        compiler_params=_cparams(2),
    )(q3, gates3, kcmp, vcmp, ksp, vsp, kwp, vwp, et, ov, bias_c, bias_w, bias_s, bias_far, og)


def _memkv_kernel(mem_ref, g_ref, wkv_ref, gk_ref, k_ref, v_ref):
    mn = _rms(mem_ref[0], g_ref[...]).astype(BF16)
    kv = _dot(mn, wkv_ref[...])
    for h in range(X_HEADS):
        sl = slice(h * X_HEAD_DIM, (h + 1) * X_HEAD_DIM)
        k_ref[0, :, sl] = _rms(kv[:, sl], gk_ref[...]).astype(BF16)
    v_ref[0] = kv[:, D_MODEL:].astype(BF16)


def _memkv(mem, g, wkv, gk):
    bsz, mlen, _ = mem.shape
    full = lambda a: pl.BlockSpec(a.shape, lambda b: (0,) * a.ndim)
    blk = pl.BlockSpec((1, mlen, D_MODEL), lambda b: (b, 0, 0))
    return pl.pallas_call(
        _memkv_kernel,
        grid=(bsz,),
        in_specs=[blk, full(g), full(wkv), full(gk)],
        out_specs=[blk, blk],
        out_shape=[jax.ShapeDtypeStruct((bsz, mlen, D_MODEL), BF16)] * 2,
        compiler_params=_cparams(1),
    )(mem, g, wkv, gk)


def _mid_kernel(x_ref, yrg_ref, ynsa_ref, woa_ref, wob_ref, gx_ref, wq_ref, gq_ref, k_ref, v_ref, wo_ref,
                gm_ref, wr_ref, br_ref, h_ref, xt_ref, rw_ref, ri_ref, cnt_ref):
    h1 = x_ref[0] + _dot(yrg_ref[0], woa_ref[...]) + _dot(ynsa_ref[0], wob_ref[...])

    q = _dot(_rms(h1, gx_ref[...]).astype(BF16), wq_ref[...])
    heads = []
    for h in range(X_HEADS):
        sl = slice(h * X_HEAD_DIM, (h + 1) * X_HEAD_DIM)
        qh = _rms(q[:, sl], gq_ref[...]).astype(BF16)
        lg = _dot_nt(qh, k_ref[0, :, sl])
        e = jnp.exp(lg - jnp.max(lg, axis=-1, keepdims=True))
        heads.append(_dot(e.astype(BF16), v_ref[0, :, sl]) / jnp.sum(e, axis=-1, keepdims=True))
    h2 = h1 + _dot(jnp.concatenate(heads, axis=-1).astype(BF16), wo_ref[...])
    h_ref[0] = h2

    xt = _rms(h2, gm_ref[...])
    xt_ref[0] = xt
    lg = jnp.dot(xt, wr_ref[...], preferred_element_type=F32, precision=lax.Precision.HIGHEST) + br_ref[...]
    lane = lax.broadcasted_iota(jnp.int32, lg.shape, 1)
    lane_f = lane.astype(F32)
    first_of = lambda hit: jnp.min(jnp.where(hit, lane_f, 1e9), axis=-1, keepdims=True)
    glog = jnp.where(lane < N_GROUPS, lg, -3e38)
    gmax = jnp.max(glog, axis=-1, keepdims=True)
    gsel = first_of(glog == gmax)
    p_g = 1.0 / jnp.sum(jnp.exp(glog - gmax), axis=-1, keepdims=True)
    lo = N_GROUPS + EXP_PER_GROUP * gsel
    el = jnp.where((lane_f >= lo) & (lane_f < lo + EXP_PER_GROUP), lg, -3e38)
    m_a = jnp.max(el, axis=-1, keepdims=True)
    i_a = first_of(el == m_a)
    el2 = jnp.where(lane_f == i_a, -3e38, el)
    m_b = jnp.max(el2, axis=-1, keepdims=True)
    i_b = first_of(el2 == m_b)
    r = jnp.exp(m_b - m_a)
    w_a = p_g / (1.0 + r)
    w_b = p_g * r / (1.0 + r)
    e_a = i_a - N_GROUPS
    e_b = i_b - N_GROUPS
    rw_ref[0] = jnp.where(lane == 0, w_a, jnp.where(lane == 1, w_b, 0.0))
    ri_ref[0] = jnp.where(lane == 0, e_a, jnp.where(lane == 1, e_b, 0.0)).astype(jnp.int32)

    @pl.when((pl.program_id(0) == 0) & (pl.program_id(1) == 0))
    def _():
        cnt_ref[...] = jnp.zeros_like(cnt_ref)

    hot = jnp.where((lane_f == e_a) | (lane_f == e_b), 1.0, 0.0)
    cnt_ref[...] += jnp.sum(hot, axis=0, keepdims=True)


def _mid(x, yrg, ynsa, woa, wob, gx, wq, gq, kx, vx, wo, gm, wr, br):
    bsz, seq, _ = x.shape
    tm = min(TM_MID, seq)
    mlen = kx.shape[1]
    full = lambda a: pl.BlockSpec(a.shape, lambda b, i: (0,) * a.ndim)
    tok = lambda w: pl.BlockSpec((1, tm, w), lambda b, i: (b, i, 0))
    memb = pl.BlockSpec((1, mlen, D_MODEL), lambda b, i: (b, 0, 0))
    return pl.pallas_call(
        _mid_kernel,
        grid=(bsz, seq // tm),
        in_specs=[tok(D_MODEL), tok(RG_WIDTH), tok(NSA_WIDTH), full(woa), full(wob), full(gx), full(wq), full(gq),
                  memb, memb, full(wo), full(gm), full(wr), full(br)],
        out_specs=[tok(D_MODEL), tok(D_MODEL), tok(LANES), tok(LANES), pl.BlockSpec((1, LANES), lambda b, i: (0, 0))],
        out_shape=[jax.ShapeDtypeStruct((bsz, seq, D_MODEL), F32), jax.ShapeDtypeStruct((bsz, seq, D_MODEL), F32),
                   jax.ShapeDtypeStruct((bsz, seq, LANES), F32), jax.ShapeDtypeStruct((bsz, seq, LANES), jnp.int32),
                   jax.ShapeDtypeStruct((1, LANES), F32)],
        compiler_params=_cparams(2),
    )(x, yrg, ynsa, woa, wob, gx, wq, gq, kx, vx, wo, gm, wr, br)


def _dest_kernel(ri_ref, pstart_ref, dest_ref, run_ref):
    @pl.when(pl.program_id(0) == 0)
    def _():
        run_ref[...] = jnp.zeros_like(run_ref)

    ri = ri_ref[...]
    tm = ri.shape[0]
    lane = lax.broadcasted_iota(jnp.int32, ri.shape, 1)
    e_a = ri[:, 0:1]
    e_b = ri[:, 1:2]
    hot_a = lane == e_a
    hot_b = lane == e_b
    hot = jnp.where(hot_a | hot_b, 1.0, 0.0)
    row = lax.broadcasted_iota(jnp.int32, (tm, tm), 0)
    col = lax.broadcasted_iota(jnp.int32, (tm, tm), 1)
    earlier = jnp.where(col < row, 1.0, 0.0).astype(BF16)
    base = _dot(earlier, hot.astype(BF16)) + run_ref[...] + pstart_ref[...]
    d_a = jnp.sum(jnp.where(hot_a, base, 0.0), axis=-1, keepdims=True)
    d_b = jnp.sum(jnp.where(hot_b, base, 0.0), axis=-1, keepdims=True)
    dest_ref[...] = jnp.where(lane == 0, d_a, jnp.where(lane == 1, d_b, 0.0)).astype(jnp.int32)
    run_ref[...] += jnp.sum(hot, axis=0, keepdims=True)


def _dest(ri2, pstart):
    n_tok = ri2.shape[0]
    tm = min(TM_DEST, n_tok)
    return pl.pallas_call(
        _dest_kernel,
        grid=(n_tok // tm,),
        in_specs=[pl.BlockSpec((tm, LANES), lambda i: (i, 0)), pl.BlockSpec((1, LANES), lambda i: (0, 0))],
        out_specs=pl.BlockSpec((tm, LANES), lambda i: (i, 0)),
        out_shape=jax.ShapeDtypeStruct((n_tok, LANES), jnp.int32),
        scratch_shapes=[pltpu.VMEM((1, LANES), F32)],
        compiler_params=_cparams(1),
    )(ri2, pstart)


def _row_copy(src_ref, src_row, dst_ref, dst_row, sem):
    return pltpu.make_async_copy(src_ref.at[src_row], dst_ref.at[dst_row], sem)


def _dispatch_kernel(da_ref, db_ref, xt_ref, buf_ref, xs_ref, sem):
    del buf_ref
    tm = da_ref.shape[2]
    t0 = pl.program_id(0) * tm

    def issue(t, c):
        _row_copy(xt_ref, t0 + t, xs_ref, da_ref[0, 0, t], sem).start()
        _row_copy(xt_ref, t0 + t, xs_ref, db_ref[0, 0, t], sem).start()
        return c

    lax.fori_loop(0, tm, issue, 0)

    def drain(t, c):
        _row_copy(xt_ref, 0, xs_ref, 0, sem).wait()
        _row_copy(xt_ref, 0, xs_ref, 0, sem).wait()
        return c

    lax.fori_loop(0, tm, drain, 0)


def _dispatch(da, db, xt_rows, buf):
    n_tiles, _, tm = da.shape
    smem = pl.BlockSpec((1, 1, tm), lambda i: (i, 0, 0), memory_space=pltpu.SMEM)
    hbm = pl.BlockSpec(memory_space=pl.ANY)
    return pl.pallas_call(
        _dispatch_kernel,
        grid=(n_tiles,),
        in_specs=[smem, smem, hbm, hbm],
        out_specs=hbm,
        out_shape=jax.ShapeDtypeStruct(buf.shape, buf.dtype),
        scratch_shapes=[pltpu.SemaphoreType.DMA(())],
        input_output_aliases={3: 0},
        compiler_params=pltpu.CompilerParams(dimension_semantics=("arbitrary",), has_side_effects=True),
    )(da, db, xt_rows, buf)


def _collect_kernel(da_ref, db_ref, ys_ref, ya_ref, yb_ref, sem):
    tm = da_ref.shape[2]
    t0 = pl.program_id(0) * tm

    def issue(t, c):
        _row_copy(ys_ref, da_ref[0, 0, t], ya_ref, t0 + t, sem).start()
        _row_copy(ys_ref, db_ref[0, 0, t], yb_ref, t0 + t, sem).start()
        return c

    lax.fori_loop(0, tm, issue, 0)

    def drain(t, c):
        _row_copy(ys_ref, 0, ya_ref, 0, sem).wait()
        _row_copy(ys_ref, 0, yb_ref, 0, sem).wait()
        return c

    lax.fori_loop(0, tm, drain, 0)


def _collect(da, db, ys_rows, n_tok):
    n_tiles, _, tm = da.shape
    smem = pl.BlockSpec((1, 1, tm), lambda i: (i, 0, 0), memory_space=pltpu.SMEM)
    hbm = pl.BlockSpec(memory_space=pl.ANY)
    shape = jax.ShapeDtypeStruct((n_tok,) + ys_rows.shape[1:], ys_rows.dtype)
    return pl.pallas_call(
        _collect_kernel,
        grid=(n_tiles,),
        in_specs=[smem, smem, hbm],
        out_specs=[hbm, hbm],
        out_shape=[shape, shape],
        scratch_shapes=[pltpu.SemaphoreType.DMA(())],
        compiler_params=pltpu.CompilerParams(dimension_semantics=("arbitrary",), has_side_effects=True),
    )(da, db, ys_rows)


def _ffn_kernel(bexp_ref, nused_ref, xs_ref, w1_ref, w3_ref, w2_ref, ys_ref):
    del bexp_ref
    j = pl.program_id(0)

    @pl.when(j < nused_ref[0])
    def _():
        xb = xs_ref[...].astype(BF16)
        a = _dot(xb, w1_ref[0])
        h = a * jax.nn.sigmoid(a) * _dot(xb, w3_ref[0])
        ys_ref[...] = _dot(h.astype(BF16), w2_ref[0])

    @pl.when(j >= nused_ref[0])
    def _():
        ys_ref[...] = jnp.zeros_like(ys_ref)


def _ffn(blk_exp, n_used, xs, w1, w3, w2):
    n_pad = xs.shape[0]
    n_blocks = n_pad // MOE_TB
    grid_spec = pltpu.PrefetchScalarGridSpec(
        num_scalar_prefetch=2,
        grid=(n_blocks,),
        in_specs=[pl.BlockSpec((MOE_TB, D_MODEL), lambda j, be, nu: (j, 0)),
                  pl.BlockSpec((1, D_MODEL, D_EXPERT), lambda j, be, nu: (be[j], 0, 0)),
                  pl.BlockSpec((1, D_MODEL, D_EXPERT), lambda j, be, nu: (be[j], 0, 0)),
                  pl.BlockSpec((1, D_EXPERT, D_MODEL), lambda j, be, nu: (be[j], 0, 0))],
        out_specs=pl.BlockSpec((MOE_TB, D_MODEL), lambda j, be, nu: (j, 0)),
    )
    return pl.pallas_call(
        _ffn_kernel,
        grid_spec=grid_spec,
        out_shape=jax.ShapeDtypeStruct((n_pad, D_MODEL), F32),
        compiler_params=_cparams(1),
    )(blk_exp, n_used, xs, w1, w3, w2)


def _combine_kernel(h_ref, ya_ref, yb_ref, rw_ref, o_ref):
    rw = rw_ref[...]
    o_ref[...] = h_ref[...] + rw[:, 0:1] * ya_ref[...] + rw[:, 1:2] * yb_ref[...]


def _combine(h2, ya, yb, rw):
    n_tok = h2.shape[0]
    tm = min(TM_COMB, n_tok)
    row = lambda w: pl.BlockSpec((tm, w), lambda i: (i, 0))
    return pl.pallas_call(
        _combine_kernel,
        grid=(n_tok // tm,),
        in_specs=[row(D_MODEL), row(D_MODEL), row(D_MODEL), row(LANES)],
        out_specs=row(D_MODEL),
        out_shape=jax.ShapeDtypeStruct((n_tok, D_MODEL), F32),
        compiler_params=_cparams(1),
    )(h2, ya, yb, rw)


def _rel_bucket_np(dist):
    n = np.maximum(dist, 0)
    max_exact = NUM_BUCKETS // 2
    nf = np.maximum(n, 1).astype(np.float32)
    large = max_exact + (np.log(nf / max_exact) / math.log(MAX_DIST / max_exact)
                         * (NUM_BUCKETS - max_exact)).astype(np.int32)
    large = np.minimum(large, NUM_BUCKETS - 1)
    return np.where(n < max_exact, n, large).astype(np.int32)


def _bias_tables(rel_bias, seq):
    n_chunk = seq // CMP_STRIDE
    n_tiles = seq // TQ
    heads = np.arange(NSA_KV)[:, None] * NSA_HPG + np.repeat(np.arange(NSA_HPG), TQ)[None, :]
    qi = np.tile(np.arange(TQ), NSA_HPG)

    def lookup(bucket, valid):
        b = np.broadcast_to(bucket[..., None, :, :], bucket.shape[:-2] + (NSA_KV,) + bucket.shape[-2:])
        h = np.broadcast_to(heads[:, :, None], b.shape)
        vals = rel_bias[b, h]
        return jnp.where(np.broadcast_to(valid[..., None, :, :], b.shape), vals, NEG_INF).astype(F32)

    j = np.arange(NEAR)
    dw = qi[:, None] + WINDOW - j[None, :]
    bias_w = lookup(_rel_bucket_np(dw), (dw >= 0) & (dw < WINDOW))
    bias_s = lookup(_rel_bucket_np(dw), dw >= 0)
    bias_far = rel_bias[NUM_BUCKETS - 1][heads][:, :, None].astype(F32)
    t = (np.arange(n_tiles) * TQ)[:, None] + qi[None, :]
    c = np.arange(n_chunk)
    center = c * CMP_STRIDE + CMP_L // 2
    end = c * CMP_STRIDE + CMP_L - 1
    valid_c = (end[None, None, :] <= t[:, :, None]) & (c < n_chunk - 1)[None, None, :]
    bias_c = lookup(_rel_bucket_np(t[:, :, None] - center[None, None, :]), valid_c)
    return bias_c, bias_w, bias_s, bias_far


def _selection_tables(seq):
    n_chunk = seq // CMP_STRIDE
    c = np.arange(n_chunk)
    n = np.arange(LANES)
    start = c * CMP_STRIDE
    overlap = ((start[:, None] <= n[None, :] * SEL_L + SEL_L - 1) & (start[:, None] + CMP_L - 1 >= n[None, :] * SEL_L)
               & (c < n_chunk - 1)[:, None] & (n < seq // SEL_L)[None, :])
    pos = np.arange(seq + WINDOW) - WINDOW
    expand = (pos[:, None] >= 0) & (pos[:, None] // SEL_L == n[None, :])
    return jnp.asarray(overlap, BF16), jnp.asarray(expand, BF16)


def _block_ones(width, group):
    idx = np.arange(width) // group
    return jnp.asarray((idx[:, None] == idx[None, :]) / group, BF16)


def _block_diag(w):
    nb, n, m = w.shape
    eye = jnp.eye(nb, dtype=w.dtype)
    return jnp.einsum('hij,hg->higj', w, eye).reshape(nb * n, nb * m)


def _compress_weights(w1, w2, pos):
    half_l = CMP_L // 2
    parts = []
    for half in range(2):
        wh = w1[half * half_l * HEAD_DIM:(half + 1) * half_l * HEAD_DIM].reshape(half_l, HEAD_DIM, CMP_HIDDEN)
        z = jnp.zeros_like(wh)
        for g in range(NSA_KV):
            grp = [wh if gg == g else z for gg in range(NSA_KV)]
            parts.append(jnp.stack(grp, axis=1).reshape(half_l * KV_W, CMP_HIDDEN))
    w1cat = jnp.concatenate(parts, axis=1).astype(BF16)
    w2bd = _block_diag(jnp.stack([w2] * NSA_KV)).astype(BF16)
    prow = [jnp.tile(pos[half * half_l:(half + 1) * half_l][:, None, :], (1, NSA_KV, 1)).reshape(-1)
            for half in range(2)]
    pmat = jnp.zeros((8, half_l * KV_W), F32).at[0].set(prow[0]).at[1].set(prow[1]).astype(BF16)
    return w1cat, w2bd, pmat


def kernel(x, mem, rel_bias, norm_mix, w_in, rg_conv_w, rg_conv_b, rg_w_r, rg_b_r, rg_w_i, rg_b_i, rg_lambda, nsa_g_q, nsa_g_kc, nsa_g_ks, nsa_g_kw, cmp_pos_k, cmp_pos_v, cmp_k_w1, cmp_k_w2, cmp_v_w1, cmp_v_w2, out_g_rg, out_g_nsa, w_out, norm_x, norm_mem, xa_w_q, xa_w_kv, xa_w_o, xa_g_q, xa_g_k, norm_moe, router_g_w, router_g_b, router_e_w, router_e_b, exp_w1, exp_w3, exp_w2):
    bsz, seq, _ = x.shape
    n_tok = bsz * seq
    assert seq % FAR_TK == 0 and seq // SEL_L <= LANES and norm_mix.shape[0] == 1
    l = 0
    row = lambda v: v.reshape(1, -1).astype(F32)

    perm = np.array([(half * NSA_HPG + p) * HEAD_DIM + d
                     for p in range(NSA_HPG) for half in range(NSA_KV) for d in range(HEAD_DIM)])
    offs = np.cumsum([0, RG_WIDTH, RG_WIDTH, NSA_WIDTH] + [KV_W] * 6)
    w = w_in[l]
    wrg = w[:, :offs[2]].astype(BF16)
    wq = w[:, offs[2]:offs[3]][:, perm].astype(BF16)
    wkv = w[:, offs[3]:offs[9]].astype(BF16)
    wgl = jnp.pad(w[:, offs[9]:], ((0, 0), (0, LANES - 3 * NSA_HEADS))).astype(BF16)
    ones64 = _block_ones(NSA_WIDTH, HEAD_DIM)
    gq = row(jnp.tile(nsa_g_q[l], NSA_HEADS) * HEAD_DIM ** -0.5)
    u, gate, q, kc, vc, ks, vs, kw, vw, gates = _inproj(
        x.reshape(n_tok, D_MODEL), row(norm_mix[l]), wrg, wq, wkv, wgl, gq,
        row(jnp.tile(nsa_g_ks[l], NSA_KV)), row(jnp.tile(nsa_g_kw[l], NSA_KV)), ones64)

    wg = jnp.concatenate([_block_diag(rg_w_r[l]), _block_diag(rg_w_i[l])], axis=1).astype(BF16)
    bg = jnp.concatenate([rg_b_r[l], rg_b_i[l]]).reshape(1, -1)
    y_rg = _rglru(u.reshape(bsz, seq, RG_WIDTH), gate.reshape(bsz, seq, RG_WIDTH),
                  rg_conv_w[l].reshape(CONV_W, RG_WIDTH), row(rg_conv_b[l]), wg, bg, row(rg_lambda[l]),
                  row(out_g_rg[l]))

    n_chunk = seq // CMP_STRIDE
    w1k, w2k, pk = _compress_weights(cmp_k_w1[l], cmp_k_w2[l], cmp_pos_k[l])
    w1v, w2v, pv = _compress_weights(cmp_v_w1[l], cmp_v_w2[l], cmp_pos_v[l])
    kcmp, vcmp = _compress(kc.reshape(bsz, n_chunk, CMP_STRIDE * KV_W), vc.reshape(bsz, n_chunk, CMP_STRIDE * KV_W),
                           w1k, w2k, pk, w1v, w2v, pv, row(jnp.tile(nsa_g_kc[l], NSA_KV)),
                           ones64[:KV_W, :KV_W])
    padw = lambda t: jnp.pad(t.reshape(bsz, seq, KV_W), ((0, 0), (WINDOW, 0), (0, 0)))
    bias_c, bias_w, bias_s, bias_far = _bias_tables(rel_bias, seq)
    overlap, expand = _selection_tables(seq)
    y_nsa = _nsa(q.reshape(bsz, seq, NSA_WIDTH), gates.reshape(bsz, seq, LANES), kcmp, vcmp,
                 padw(ks), padw(vs), padw(kw), padw(vw), expand, overlap, bias_c, bias_w, bias_s, bias_far,
                 row(out_g_nsa[l][perm]))

    kx, vx = _memkv(mem, row(norm_mem[l]), xa_w_kv[l].astype(BF16), row(xa_g_k[l]))
    wo_mix = w_out[l]
    wr = jnp.pad(jnp.concatenate([router_g_w[l], router_e_w[l]], axis=1),
                 ((0, 0), (0, LANES - N_GROUPS - N_EXPERTS)))
    br = jnp.pad(jnp.concatenate([router_g_b[l], router_e_b[l]]), (0, LANES - N_GROUPS - N_EXPERTS)).reshape(1, -1)
    h2, xt, rw, ri, counts = _mid(
        x, y_rg, y_nsa, wo_mix[:RG_WIDTH].astype(BF16), wo_mix[RG_WIDTH:][perm].astype(BF16), row(norm_x[l]),
        xa_w_q[l].astype(BF16), row(xa_g_q[l] * X_HEAD_DIM ** -0.5), kx, vx, xa_w_o[l].astype(BF16),
        row(norm_moe[l]), wr, br)

    n_slots = 2 * n_tok
    n_blocks = n_slots // MOE_TB + N_EXPERTS
    n_pad = n_blocks * MOE_TB
    cnt = counts[0, :N_EXPERTS].astype(jnp.int32)
    pcnt = (cnt + MOE_TB - 1) // MOE_TB * MOE_TB
    pends = jnp.cumsum(pcnt)
    pstart = jnp.pad((pends - pcnt).astype(F32), (0, LANES - N_EXPERTS)).reshape(1, LANES)
    blk_exp = jnp.minimum(jnp.searchsorted(pends, jnp.arange(n_blocks, dtype=jnp.int32) * MOE_TB, side='right'),
                          N_EXPERTS - 1).astype(jnp.int32)
    n_used = (pends[-1:] // MOE_TB).astype(jnp.int32)
    dest = _dest(ri.reshape(n_tok, LANES), pstart)
    tmd = min(TM_DMA, n_tok)
    da = dest[:, 0].reshape(n_tok // tmd, 1, tmd)
    db = dest[:, 1].reshape(n_tok // tmd, 1, tmd)
    rows = (D_MODEL // LANES, LANES)
    xs = _dispatch(da, db, xt.reshape((n_tok,) + rows), jnp.zeros((n_pad,) + rows, F32))
    ys = _ffn(blk_exp, n_used, xs.reshape(n_pad, D_MODEL),
              exp_w1[l].astype(BF16), exp_w3[l].astype(BF16), exp_w2[l].astype(BF16))
    ya, yb = _collect(da, db, ys.reshape((n_pad,) + rows), n_tok)
    out = _combine(h2.reshape(n_tok, D_MODEL), ya.reshape(n_tok, D_MODEL), yb.reshape(n_tok, D_MODEL),
                   rw.reshape(n_tok, LANES))
    return out.reshape(bsz, seq, D_MODEL)
```

```python
import functools
import math

import numpy as np
import jax
import jax.numpy as jnp
from jax import lax
from jax.experimental import pallas as pl
from jax.experimental.pallas import tpu as pltpu

F32 = jnp.float32
BF16 = jnp.bfloat16

D_MODEL = 1024
RG_WIDTH = 512
RG_BLOCKS = 8
RG_BLOCK = 64
CONV_W = 4
RG_C = 8.0
NSA_WIDTH = 512
NSA_HEADS = 8
HEAD_DIM = 64
NSA_KV = 2
NSA_HPG = 4
KV_W = 128
CMP_L = 32
CMP_STRIDE = 16
CMP_HIDDEN = 256
SEL_L = 64
N_SEL = 8
WINDOW = 512
NUM_BUCKETS = 32
MAX_DIST = 128
X_HEADS = 4
X_HEAD_DIM = 256
N_GROUPS = 4
EXP_PER_GROUP = 8
N_EXPERTS = 32
D_EXPERT = 512
EPS = 1e-6
NEG_INF = -1e30
MASKED_BELOW = -1e29
SEL_FORCE = 1e9
LANES = 128

TQ = 64
NEAR = WINDOW + TQ
FAR_TK = 512
QROWS = NSA_HPG * TQ

TM_PROJ = 512
TM_MID = 256
TM_DEST = 512
TM_DMA = 512
MOE_TB = 256
ROW_TILE = D_MODEL // LANES
RG_CHUNK = 256
VMEM_LIMIT = 56 * 1024 * 1024


def _cparams(n_axes):
    return pltpu.CompilerParams(dimension_semantics=("arbitrary",) * n_axes,
                                vmem_limit_bytes=VMEM_LIMIT)


def _dot(a, b):
    return jnp.dot(a, b, preferred_element_type=F32)


def _dot_nt(a, b):
    return lax.dot_general(a, b, (((1,), (1,)), ((), ())), preferred_element_type=F32)


def _gelu_tanh(x):
    return 0.5 * x * (1.0 + jnp.tanh(math.sqrt(2.0 / math.pi) * (x + 0.044715 * (x * x * x))))


def _rms(x, g):
    return x * lax.rsqrt(jnp.mean(x * x, axis=-1, keepdims=True) + EPS) * g


def _group_rms(x, ones_blk, g):
    ms = _dot((x * x).astype(BF16), ones_blk)
    return x * lax.rsqrt(ms + EPS) * g


def _inproj_kernel(x_ref, g_ref, wrg_ref, wq_ref, wkv_ref, wgl_ref, gq_ref, gks_ref, gkw_ref, ones_ref,
                   u_ref, gate_ref, q_ref, kc_ref, vc_ref, ks_ref, vs_ref, kw_ref, vw_ref, gates_ref):
    xb = _rms(x_ref[...], g_ref[...]).astype(BF16)
    rg = _dot(xb, wrg_ref[...])
    u_ref[...] = rg[:, :RG_WIDTH].astype(BF16)
    gate_ref[...] = rg[:, RG_WIDTH:].astype(BF16)
    q = _dot(xb, wq_ref[...])
    q_ref[...] = _group_rms(q, ones_ref[...], gq_ref[...]).astype(BF16)
    kv = _dot(xb, wkv_ref[...])
    ones_kv = ones_ref[:KV_W, :KV_W]
    kc_ref[...] = kv[:, 0 * KV_W:1 * KV_W].astype(BF16)
    vc_ref[...] = kv[:, 1 * KV_W:2 * KV_W].astype(BF16)
    ks_ref[...] = _group_rms(kv[:, 2 * KV_W:3 * KV_W], ones_kv, gks_ref[...]).astype(BF16)
    vs_ref[...] = kv[:, 3 * KV_W:4 * KV_W].astype(BF16)
    kw_ref[...] = _group_rms(kv[:, 4 * KV_W:5 * KV_W], ones_kv, gkw_ref[...]).astype(BF16)
    vw_ref[...] = kv[:, 5 * KV_W:6 * KV_W].astype(BF16)
    gates_ref[...] = jax.nn.sigmoid(_dot(xb, wgl_ref[...]))


def _inproj(x2, g, wrg, wq, wkv, wgl, gq, gks, gkw, ones_blk):
    n_tok = x2.shape[0]
    tm = min(TM_PROJ, n_tok)
    full = lambda a: pl.BlockSpec(a.shape, lambda i: (0,) * a.ndim)
    row = lambda w: pl.BlockSpec((tm, w), lambda i: (i, 0))
    outs = [(RG_WIDTH, BF16), (RG_WIDTH, BF16), (NSA_WIDTH, BF16)] + [(KV_W, BF16)] * 6 + [(LANES, F32)]
    return pl.pallas_call(
        _inproj_kernel,
        grid=(n_tok // tm,),
        in_specs=[row(D_MODEL)] + [full(a) for a in (g, wrg, wq, wkv, wgl, gq, gks, gkw, ones_blk)],
        out_specs=[row(w) for w, _ in outs],
        out_shape=[jax.ShapeDtypeStruct((n_tok, w), dt) for w, dt in outs],
        compiler_params=_cparams(1),
    )(x2, g, wrg, wq, wkv, wgl, gq, gks, gkw, ones_blk)


def _rglru_kernel(u_ref, gate_ref, cw_ref, cb_ref, wg_ref, bg_ref, lam_ref, og_ref, y_ref, upad, a_s, h_s):
    seq = u_ref.shape[1]
    upad[0:8, :] = jnp.zeros((8, RG_WIDTH), F32)
    upad[8:8 + seq, :] = u_ref[0].astype(F32)
    neg_lam = -lam_ref[...]
    softplus = jnp.maximum(neg_lam, 0.0) + jnp.log(1.0 + jnp.exp(-jnp.abs(neg_lam)))
    ch = min(RG_CHUNK, seq)
    for c in range(seq // ch):
        r0 = c * ch
        uc = cb_ref[...]
        for k in range(CONV_W):
            off = 8 + r0 - (CONV_W - 1) + k
            uc = uc + cw_ref[k:k + 1, :] * upad[off:off + ch, :]
        gt = _dot(uc.astype(BF16), wg_ref[...]) + bg_ref[...]
        r = jax.nn.sigmoid(gt[:, :RG_WIDTH])
        ig = jax.nn.sigmoid(gt[:, RG_WIDTH:])
        log_a = (-RG_C) * r * softplus
        a = jnp.exp(log_a)
        a_s[r0:r0 + ch, :] = a
        h_s[r0:r0 + ch, :] = jnp.sqrt(1.0 - a * a) * ig * uc

    def step(t, h):
        h = a_s[pl.ds(t, 1), :] * h + h_s[pl.ds(t, 1), :]
        h_s[pl.ds(t, 1), :] = h
        return h

    lax.fori_loop(0, seq, step, jnp.zeros((1, RG_WIDTH), F32), unroll=8)

    for c in range(seq // ch):
        r0 = c * ch
        y = _gelu_tanh(gate_ref[0, r0:r0 + ch, :].astype(F32)) * h_s[r0:r0 + ch, :]
        y_ref[0, r0:r0 + ch, :] = _rms(y, og_ref[...]).astype(BF16)


def _rglru(u3, gate3, cw, cb, wg, bg, lam, og):
    bsz, seq, _ = u3.shape
    full = lambda a: pl.BlockSpec(a.shape, lambda b: (0,) * a.ndim)
    blk = pl.BlockSpec((1, seq, RG_WIDTH), lambda b: (b, 0, 0))
    return pl.pallas_call(
        _rglru_kernel,
        grid=(bsz,),
        in_specs=[blk, blk] + [full(a) for a in (cw, cb, wg, bg, lam, og)],
        out_specs=blk,
        out_shape=jax.ShapeDtypeStruct((bsz, seq, RG_WIDTH), BF16),
        scratch_shapes=[pltpu.VMEM((seq + 8, RG_WIDTH), F32), pltpu.VMEM((seq, RG_WIDTH), F32),
                        pltpu.VMEM((seq, RG_WIDTH), F32)],
        compiler_params=_cparams(1),
    )(u3, gate3, cw, cb, wg, bg, lam, og)


def _compress_kernel(kx_ref, vx_ref, w1k_ref, w2k_ref, pk_ref, w1v_ref, w2v_ref, pv_ref, gk_ref, ones_ref,
                     ko_ref, vo_ref):
    n_chunk = kx_ref.shape[1]
    half = NSA_KV * CMP_HIDDEN

    def mlp(x_ref, w1_ref, w2_ref, p_ref):
        ab = _dot(x_ref[0], w1_ref[...])
        pos = _dot(p_ref[...], w1_ref[...])
        hid = ab[:, :half] + pltpu.roll(ab[:, half:], n_chunk - 1, 0) + (pos[0:1, :half] + pos[1:2, half:])
        return _dot(_gelu_tanh(hid).astype(BF16), w2_ref[...])

    kc = mlp(kx_ref, w1k_ref, w2k_ref, pk_ref)
    ko_ref[0] = _group_rms(kc, ones_ref[...], gk_ref[...]).astype(BF16)
    vo_ref[0] = mlp(vx_ref, w1v_ref, w2v_ref, pv_ref).astype(BF16)


def _compress(kx, vx, w1k, w2k, pk, w1v, w2v, pv, gk, ones_kv):
    bsz, n_chunk, width = kx.shape
    full = lambda a: pl.BlockSpec(a.shape, lambda b: (0,) * a.ndim)
    xin = pl.BlockSpec((1, n_chunk, width), lambda b: (b, 0, 0))
    out = pl.BlockSpec((1, n_chunk, KV_W), lambda b: (b, 0, 0))
    return pl.pallas_call(
        _compress_kernel,
        grid=(bsz,),
        in_specs=[xin, xin] + [full(a) for a in (w1k, w2k, pk, w1v, w2v, pv, gk, ones_kv)],
        out_specs=[out, out],
        out_shape=[jax.ShapeDtypeStruct((bsz, n_chunk, KV_W), BF16)] * 2,
        compiler_params=_cparams(1),
    )(kx, vx, w1k, w2k, pk, w1v, w2v, pv, gk, ones_kv)


def _nsa_kernel(q_ref, gates_ref, kcmp_ref, vcmp_ref, ksp_ref, vsp_ref, kwp_ref, vwp_ref, et_ref, ov_ref,
                bc_ref, bw_ref, bs_ref, bf_ref, og_ref, y_ref):
    i = pl.program_id(1)
    t0 = pl.multiple_of(i * TQ, TQ)
    lane = lax.broadcasted_iota(jnp.int32, (TQ, LANES), 1)
    lo_half = lane < HEAD_DIM
    q_slabs = [q_ref[0, :, p * LANES:(p + 1) * LANES] for p in range(NSA_HPG)]
    gates = gates_ref[0]
    kcmp = kcmp_ref[0]
    vcmp = vcmp_ref[0]
    near_j = lax.broadcasted_iota(jnp.int32, (1, NEAR), 1)
    near_ok = near_j >= WINDOW - t0
    kw_near = kwp_ref[0, pl.ds(t0, NEAR), :]
    vw_near = vwp_ref[0, pl.ds(t0, NEAR), :]
    ks_near = ksp_ref[0, pl.ds(t0, NEAR), :]
    vs_near = vsp_ref[0, pl.ds(t0, NEAR), :]
    et_near = et_ref[pl.ds(t0, NEAR), :]
    far_limit = t0 - WINDOW
    n_far = (jnp.maximum(far_limit, 0) + FAR_TK - 1) // FAR_TK
    blk_f = lane.astype(F32)

    def stack4(x):
        return jnp.concatenate([x] * NSA_HPG, axis=0)

    group_out = []
    for g in range(NSA_KV):
        half = lo_half if g == 0 else jnp.logical_not(lo_half)
        q4 = jnp.concatenate([jnp.where(half, qs, jnp.zeros_like(qs)) for qs in q_slabs], axis=0)

        bc = bc_ref[0, g]
        lc = _dot_nt(q4, kcmp) + bc
        mc = jnp.max(lc, axis=-1, keepdims=True)
        ec = jnp.where(bc > MASKED_BELOW, jnp.exp(lc - mc), 0.0)
        sc = jnp.sum(ec, axis=-1, keepdims=True)
        pc = ec / jnp.where(sc > 0.0, sc, 1.0)
        o_c = _dot(pc.astype(BF16), vcmp)

        pcs = pc[0:TQ] + pc[TQ:2 * TQ] + pc[2 * TQ:3 * TQ] + pc[3 * TQ:4 * TQ]
        pcs_hi = pcs.astype(BF16)
        pcs_lo = (pcs - pcs_hi.astype(F32)).astype(BF16)
        imp = _dot(pcs_hi, ov_ref[...]) + _dot(pcs_lo, ov_ref[...])
        forced = (lane == 0) | (lane == i) | (lane == i - 1)
        score = jnp.where(forced, SEL_FORCE, jnp.where(lane > i, -3e38, imp))
        sel = jnp.zeros((TQ, LANES), F32)
        for _ in range(N_SEL):
            best = jnp.max(score, axis=-1, keepdims=True)
            first = jnp.min(jnp.where(score == best, blk_f, 1e9), axis=-1, keepdims=True)
            pick = blk_f == first
            sel = jnp.where(pick, 1.0, sel)
            score = jnp.where(pick, -3e38, score)
        sel_b = sel.astype(BF16)

        lw = jnp.where(near_ok, _dot_nt(q4, kw_near) + bw_ref[g], NEG_INF)
        ew = jnp.exp(lw - jnp.max(lw, axis=-1, keepdims=True))
        o_w = _dot(ew.astype(BF16), vw_near) / jnp.sum(ew, axis=-1, keepdims=True)

        sel_near = stack4(_dot_nt(sel_b, et_near)) > 0.5
        ls = jnp.where(sel_near & near_ok, _dot_nt(q4, ks_near) + bs_ref[g], NEG_INF)
        m1 = jnp.max(ls, axis=-1, keepdims=True)
        e1 = jnp.exp(ls - m1)
        l1 = jnp.sum(e1, axis=-1, keepdims=True)
        acc1 = _dot(e1.astype(BF16), vs_near)
        bfar = bf_ref[g]

        def far_step(kf, carry):
            m, l, acc = carry
            base = pl.multiple_of(WINDOW + kf * FAR_TK, FAR_TK)
            kt = ksp_ref[0, pl.ds(base, FAR_TK), :]
            vt = vsp_ref[0, pl.ds(base, FAR_TK), :]
            pos = kf * FAR_TK + lax.broadcasted_iota(jnp.int32, (1, FAR_TK), 1)
            keep = (stack4(_dot_nt(sel_b, et_ref[pl.ds(base, FAR_TK), :])) > 0.5) & (pos < far_limit)
            lf = jnp.where(keep, _dot_nt(q4, kt) + bfar, NEG_INF)
            m_new = jnp.maximum(m, jnp.max(lf, axis=-1, keepdims=True))
            alpha = jnp.exp(m - m_new)
            e = jnp.exp(lf - m_new)
            l_new = alpha * l + jnp.sum(e, axis=-1, keepdims=True)
            return m_new, l_new, alpha * acc + _dot(e.astype(BF16), vt)

        _, l_s, acc_s = lax.fori_loop(0, n_far, far_step, (m1, l1, acc1))
        o_s = acc_s / l_s

        def gate_col(j):
            cols = [gates[:, (g * NSA_HPG + p) * 3 + j:(g * NSA_HPG + p) * 3 + j + 1] for p in range(NSA_HPG)]
            return jnp.concatenate(cols, axis=0)

        group_out.append(gate_col(0) * o_c + gate_col(1) * o_s + gate_col(2) * o_w)

    slabs = [jnp.where(lo_half, group_out[0][p * TQ:(p + 1) * TQ], group_out[1][p * TQ:(p + 1) * TQ])
             for p in range(NSA_HPG)]
    y_ref[0] = _rms(jnp.concatenate(slabs, axis=-1), og_ref[...]).astype(BF16)


def _nsa(q3, gates3, kcmp, vcmp, ksp, vsp, kwp, vwp, et, ov, bias_c, bias_w, bias_s, bias_far, og):
    bsz, seq, _ = q3.shape
    n_chunk = kcmp.shape[1]
    full = lambda a: pl.BlockSpec(a.shape, lambda b, i: (0,) * a.ndim)
    per_b = lambda a: pl.BlockSpec((1,) + a.shape[1:], lambda b, i: (b,) + (0,) * (a.ndim - 1))
    return pl.pallas_call(
        _nsa_kernel,
        grid=(bsz, seq // TQ),
        in_specs=[pl.BlockSpec((1, TQ, NSA_WIDTH), lambda b, i: (b, i, 0)),
                  pl.BlockSpec((1, TQ, LANES), lambda b, i: (b, i, 0)),
                  per_b(kcmp), per_b(vcmp), per_b(ksp), per_b(vsp), per_b(kwp), per_b(vwp),
                  full(et), full(ov),
                  pl.BlockSpec((1, NSA_KV, QROWS, n_chunk), lambda b, i: (i, 0, 0, 0)),
                  full(bias_w), full(bias_s), full(bias_far), full(og)],
        out_specs=pl.BlockSpec((1, TQ, NSA_WIDTH), lambda b, i: (b, i, 0)),
        out_shape=jax.ShapeDtypeStruct((bsz, seq, NSA_WIDTH), BF16),
        compiler_params=_cparams(2),
    )(q3, gates3, kcmp, vcmp, ksp, vsp, kwp, vwp, et, ov, bias_c, bias_w, bias_s, bias_far, og)


def _memkv_kernel(mem_ref, g_ref, wkv_ref, gk_ref, k_ref, v_ref):
    mn = _rms(mem_ref[0], g_ref[...]).astype(BF16)
    kv = _dot(mn, wkv_ref[...])
    for h in range(X_HEADS):
        sl = slice(h * X_HEAD_DIM, (h + 1) * X_HEAD_DIM)
        k_ref[0, :, sl] = _rms(kv[:, sl], gk_ref[...]).astype(BF16)
    v_ref[0] = kv[:, D_MODEL:].astype(BF16)


def _memkv(mem, g, wkv, gk):
    bsz, mlen, _ = mem.shape
    full = lambda a: pl.BlockSpec(a.shape, lambda b: (0,) * a.ndim)
    blk = pl.BlockSpec((1, mlen, D_MODEL), lambda b: (b, 0, 0))
    return pl.pallas_call(
        _memkv_kernel,
        grid=(bsz,),
        in_specs=[blk, full(g), full(wkv), full(gk)],
        out_specs=[blk, blk],
        out_shape=[jax.ShapeDtypeStruct((bsz, mlen, D_MODEL), BF16)] * 2,
        compiler_params=_cparams(1),
    )(mem, g, wkv, gk)


def _mid_kernel(x_ref, yrg_ref, ynsa_ref, woa_ref, wob_ref, gx_ref, wq_ref, gq_ref, k_ref, v_ref, wo_ref,
                gm_ref, wr_ref, br_ref, h_ref, xt_ref, rw_ref, ri_ref, cnt_ref):
    h1 = x_ref[0] + _dot(yrg_ref[0], woa_ref[...]) + _dot(ynsa_ref[0], wob_ref[...])

    q = _dot(_rms(h1, gx_ref[...]).astype(BF16), wq_ref[...])
    heads = []
    for h in range(X_HEADS):
        sl = slice(h * X_HEAD_DIM, (h + 1) * X_HEAD_DIM)
        qh = _rms(q[:, sl], gq_ref[...]).astype(BF16)
        lg = _dot_nt(qh, k_ref[0, :, sl])
        e = jnp.exp(lg - jnp.max(lg, axis=-1, keepdims=True))
        heads.append(_dot(e.astype(BF16), v_ref[0, :, sl]) / jnp.sum(e, axis=-1, keepdims=True))
    h2 = h1 + _dot(jnp.concatenate(heads, axis=-1).astype(BF16), wo_ref[...])
    h_ref[0] = h2

    xt = _rms(h2, gm_ref[...])
    _store_row_tiles(xt_ref, xt)
    lg = jnp.dot(xt, wr_ref[...], preferred_element_type=F32, precision=lax.Precision.HIGHEST) + br_ref[...]
    lane = lax.broadcasted_iota(jnp.int32, lg.shape, 1)
    lane_f = lane.astype(F32)
    first_of = lambda hit: jnp.min(jnp.where(hit, lane_f, 1e9), axis=-1, keepdims=True)
    glog = jnp.where(lane < N_GROUPS, lg, -3e38)
    gmax = jnp.max(glog, axis=-1, keepdims=True)
    gsel = first_of(glog == gmax)
    p_g = 1.0 / jnp.sum(jnp.exp(glog - gmax), axis=-1, keepdims=True)
    lo = N_GROUPS + EXP_PER_GROUP * gsel
    el = jnp.where((lane_f >= lo) & (lane_f < lo + EXP_PER_GROUP), lg, -3e38)
    m_a = jnp.max(el, axis=-1, keepdims=True)
    i_a = first_of(el == m_a)
    el2 = jnp.where(lane_f == i_a, -3e38, el)
    m_b = jnp.max(el2, axis=-1, keepdims=True)
    i_b = first_of(el2 == m_b)
    r = jnp.exp(m_b - m_a)
    w_a = p_g / (1.0 + r)
    w_b = p_g * r / (1.0 + r)
    e_a = i_a - N_GROUPS
    e_b = i_b - N_GROUPS
    rw_ref[0] = jnp.where(lane == 0, w_a, jnp.where(lane == 1, w_b, 0.0))
    ri_ref[0] = jnp.where(lane == 0, e_a, jnp.where(lane == 1, e_b, 0.0)).astype(jnp.int32)

    @pl.when((pl.program_id(0) == 0) & (pl.program_id(1) == 0))
    def _():
        cnt_ref[...] = jnp.zeros_like(cnt_ref)

    hot = jnp.where((lane_f == e_a) | (lane_f == e_b), 1.0, 0.0)
    cnt_ref[...] += jnp.sum(hot, axis=0, keepdims=True)


def _mid(x, yrg, ynsa, woa, wob, gx, wq, gq, kx, vx, wo, gm, wr, br):
    bsz, seq, _ = x.shape
    tm = min(TM_MID, seq)
    mlen = kx.shape[1]
    n_i = seq // tm
    full = lambda a: pl.BlockSpec(a.shape, lambda b, i: (0,) * a.ndim)
    tok = lambda w: pl.BlockSpec((1, tm, w), lambda b, i: (b, i, 0))
    memb = pl.BlockSpec((1, mlen, D_MODEL), lambda b, i: (b, 0, 0))
    xt_spec = pl.BlockSpec((tm * ROW_TILE, LANES), lambda b, i: (b * n_i + i, 0))
    return pl.pallas_call(
        _mid_kernel,
        grid=(bsz, seq // tm),
        in_specs=[tok(D_MODEL), tok(RG_WIDTH), tok(NSA_WIDTH), full(woa), full(wob), full(gx), full(wq), full(gq),
                  memb, memb, full(wo), full(gm), full(wr), full(br)],
        out_specs=[tok(D_MODEL), xt_spec, tok(LANES), tok(LANES), pl.BlockSpec((1, LANES), lambda b, i: (0, 0))],
        out_shape=[jax.ShapeDtypeStruct((bsz, seq, D_MODEL), F32),
                   jax.ShapeDtypeStruct((bsz * seq * ROW_TILE, LANES), F32),
                   jax.ShapeDtypeStruct((bsz, seq, LANES), F32), jax.ShapeDtypeStruct((bsz, seq, LANES), jnp.int32),
                   jax.ShapeDtypeStruct((1, LANES), F32)],
        compiler_params=_cparams(2),
    )(x, yrg, ynsa, woa, wob, gx, wq, gq, kx, vx, wo, gm, wr, br)


def _dest_kernel(ri_ref, pstart_ref, dest_ref, run_ref):
    @pl.when(pl.program_id(0) == 0)
    def _():
        run_ref[...] = jnp.zeros_like(run_ref)

    ri = ri_ref[...]
    tm = ri.shape[0]
    lane = lax.broadcasted_iota(jnp.int32, ri.shape, 1)
    e_a = ri[:, 0:1]
    e_b = ri[:, 1:2]
    hot_a = lane == e_a
    hot_b = lane == e_b
    hot = jnp.where(hot_a | hot_b, 1.0, 0.0)
    row = lax.broadcasted_iota(jnp.int32, (tm, tm), 0)
    col = lax.broadcasted_iota(jnp.int32, (tm, tm), 1)
    earlier = jnp.where(col < row, 1.0, 0.0).astype(BF16)
    base = _dot(earlier, hot.astype(BF16)) + run_ref[...] + pstart_ref[...]
    d_a = jnp.sum(jnp.where(hot_a, base, 0.0), axis=-1, keepdims=True)
    d_b = jnp.sum(jnp.where(hot_b, base, 0.0), axis=-1, keepdims=True)
    dest_ref[...] = jnp.where(lane == 0, d_a, jnp.where(lane == 1, d_b, 0.0)).astype(jnp.int32)
    run_ref[...] += jnp.sum(hot, axis=0, keepdims=True)


def _dest(ri2, pstart):
    n_tok = ri2.shape[0]
    tm = min(TM_DEST, n_tok)
    return pl.pallas_call(
        _dest_kernel,
        grid=(n_tok // tm,),
        in_specs=[pl.BlockSpec((tm, LANES), lambda i: (i, 0)), pl.BlockSpec((1, LANES), lambda i: (0, 0))],
        out_specs=pl.BlockSpec((tm, LANES), lambda i: (i, 0)),
        out_shape=jax.ShapeDtypeStruct((n_tok, LANES), jnp.int32),
        scratch_shapes=[pltpu.VMEM((1, LANES), F32)],
        compiler_params=_cparams(1),
    )(ri2, pstart)


def _store_row_tiles(ref, val):
    n = val.shape[0]
    for c in range(ROW_TILE):
        ref[pl.ds(c, n, stride=ROW_TILE), :] = val[:, c * LANES:(c + 1) * LANES]


def _load_row_tiles(ref, n):
    return [ref[pl.ds(c, n, stride=ROW_TILE), :] for c in range(ROW_TILE)]


def _token_rows(ref, t):
    return ref.at[pl.ds(pl.multiple_of(t * ROW_TILE, ROW_TILE), ROW_TILE), :]


def _dispatch_kernel(da_ref, db_ref, xt_ref, buf_ref, xs_ref, sem):
    del buf_ref
    tm = da_ref.shape[2]

    def issue(t, c):
        pltpu.make_async_copy(_token_rows(xt_ref, t), _token_rows(xs_ref, da_ref[0, 0, t]), sem).start()
        pltpu.make_async_copy(_token_rows(xt_ref, t), _token_rows(xs_ref, db_ref[0, 0, t]), sem).start()
        return c

    lax.fori_loop(0, tm, issue, 0)

    def drain(t, c):
        pltpu.make_async_copy(_token_rows(xt_ref, 0), _token_rows(xs_ref, 0), sem).wait()
        pltpu.make_async_copy(_token_rows(xt_ref, 0), _token_rows(xs_ref, 0), sem).wait()
        return c

    lax.fori_loop(0, tm, drain, 0)


def _dispatch(da, db, xt_rows, buf):
    n_tiles, _, tm = da.shape
    smem = pl.BlockSpec((1, 1, tm), lambda i: (i, 0, 0), memory_space=pltpu.SMEM)
    hbm = pl.BlockSpec(memory_space=pl.ANY)
    return pl.pallas_call(
        _dispatch_kernel,
        grid=(n_tiles,),
        in_specs=[smem, smem, pl.BlockSpec((tm * ROW_TILE, LANES), lambda i: (i, 0)), hbm],
        out_specs=hbm,
        out_shape=jax.ShapeDtypeStruct(buf.shape, buf.dtype),
        scratch_shapes=[pltpu.SemaphoreType.DMA(())],
        input_output_aliases={3: 0},
        compiler_params=pltpu.CompilerParams(dimension_semantics=("arbitrary",), has_side_effects=True,
                                             vmem_limit_bytes=VMEM_LIMIT),
    )(da, db, xt_rows, buf)


def _ffn_kernel(bexp_ref, nused_ref, xs_ref, w1_ref, w3_ref, w2_ref, ys_ref):
    del bexp_ref
    j = pl.program_id(0)

    @pl.when(j < nused_ref[0])
    def _():
        xb = jnp.concatenate(_load_row_tiles(xs_ref, MOE_TB), axis=-1).astype(BF16)
        a = _dot(xb, w1_ref[0])
        h = a * jax.nn.sigmoid(a) * _dot(xb, w3_ref[0])
        _store_row_tiles(ys_ref, _dot(h.astype(BF16), w2_ref[0]))

    @pl.when(j >= nused_ref[0])
    def _():
        ys_ref[...] = jnp.zeros_like(ys_ref)


def _ffn(blk_exp, n_used, xs, w1, w3, w2):
    n_blocks = xs.shape[0] // (MOE_TB * ROW_TILE)
    rows = pl.BlockSpec((MOE_TB * ROW_TILE, LANES), lambda j, be, nu: (j, 0))
    grid_spec = pltpu.PrefetchScalarGridSpec(
        num_scalar_prefetch=2,
        grid=(n_blocks,),
        in_specs=[rows,
                  pl.BlockSpec((1, D_MODEL, D_EXPERT), lambda j, be, nu: (be[j], 0, 0)),
                  pl.BlockSpec((1, D_MODEL, D_EXPERT), lambda j, be, nu: (be[j], 0, 0)),
                  pl.BlockSpec((1, D_EXPERT, D_MODEL), lambda j, be, nu: (be[j], 0, 0))],
        out_specs=rows,
    )
    return pl.pallas_call(
        _ffn_kernel,
        grid_spec=grid_spec,
        out_shape=jax.ShapeDtypeStruct(xs.shape, F32),
        compiler_params=_cparams(1),
    )(blk_exp, n_used, xs, w1, w3, w2)


def _combine_kernel(da_ref, db_ref, h_ref, rw_ref, ys_ref, o_ref, ya, yb, sem):
    tm = da_ref.shape[2]

    def issue(t, c):
        pltpu.make_async_copy(_token_rows(ys_ref, da_ref[0, 0, t]), _token_rows(ya, t), sem).start()
        pltpu.make_async_copy(_token_rows(ys_ref, db_ref[0, 0, t]), _token_rows(yb, t), sem).start()
        return c

    lax.fori_loop(0, tm, issue, 0)

    def drain(t, c):
        pltpu.make_async_copy(_token_rows(ys_ref, 0), _token_rows(ya, 0), sem).wait()
        pltpu.make_async_copy(_token_rows(ys_ref, 0), _token_rows(yb, 0), sem).wait()
        return c

    lax.fori_loop(0, tm, drain, 0)
    rw = rw_ref[...]
    mix = [rw[:, 0:1] * a + rw[:, 1:2] * b for a, b in zip(_load_row_tiles(ya, tm), _load_row_tiles(yb, tm))]
    o_ref[...] = h_ref[...] + jnp.concatenate(mix, axis=-1)


def _combine(da, db, h2, rw, ys):
    n_tiles, _, tm = da.shape
    n_tok = h2.shape[0]
    smem = pl.BlockSpec((1, 1, tm), lambda i: (i, 0, 0), memory_space=pltpu.SMEM)
    row = lambda w: pl.BlockSpec((tm, w), lambda i: (i, 0))
    return pl.pallas_call(
        _combine_kernel,
        grid=(n_tiles,),
        in_specs=[smem, smem, row(D_MODEL), row(LANES), pl.BlockSpec(memory_space=pl.ANY)],
        out_specs=row(D_MODEL),
        out_shape=jax.ShapeDtypeStruct((n_tok, D_MODEL), F32),
        scratch_shapes=[pltpu.VMEM((tm * ROW_TILE, LANES), F32), pltpu.VMEM((tm * ROW_TILE, LANES), F32),
                        pltpu.SemaphoreType.DMA(())],
        compiler_params=_cparams(1),
    )(da, db, h2, rw, ys)


def _rel_bucket_np(dist):
    n = np.maximum(dist, 0)
    max_exact = NUM_BUCKETS // 2
    nf = np.maximum(n, 1).astype(np.float32)
    large = max_exact + (np.log(nf / max_exact) / math.log(MAX_DIST / max_exact)
                         * (NUM_BUCKETS - max_exact)).astype(np.int32)
    large = np.minimum(large, NUM_BUCKETS - 1)
    return np.where(n < max_exact, n, large).astype(np.int32)


def _toeplitz(vec, rows):
    width = vec.shape[-1] - 1
    flat = jnp.tile(vec, (1,) * (vec.ndim - 1) + (rows,))[..., :rows * width]
    return flat.reshape(vec.shape[:-1] + (rows, width))


def _bias_tables(rel_bias, seq):
    n_chunk = seq // CMP_STRIDE
    n_tiles = seq // TQ
    table = rel_bias.T.astype(F32)

    wide = NEAR + TQ
    k = np.arange(wide + 1)
    dw = np.where(k < NEAR, WINDOW - k, WINDOW + wide + 1 - k)
    used = (k < NEAR) | (k > wide + 1 - TQ)
    vals = table[:, _rel_bucket_np(dw)]

    def near_tile(valid):
        t = _toeplitz(jnp.where(valid[None, :], vals, NEG_INF), TQ)[:, :, :NEAR]
        return t.reshape(NSA_KV, QROWS, NEAR)

    bias_w = near_tile(used & (dw >= 0) & (dw < WINDOW))
    bias_s = near_tile(used & (dw >= 0))
    bias_far = table[:, NUM_BUCKETS - 1].reshape(NSA_KV, NSA_HPG, 1)
    bias_far = jnp.broadcast_to(bias_far, (NSA_KV, NSA_HPG, TQ)).reshape(NSA_KV, QROWS, 1)

    r = np.arange(CMP_STRIDE)[:, None]
    k = np.arange(2 * n_chunk + 1)[None, :]
    lag = 2 * n_chunk + 1 - k
    valid = (k > n_chunk + 1) & (CMP_STRIDE * lag + r >= CMP_L - 1)
    vals = table[:, _rel_bucket_np(CMP_STRIDE * lag + r - CMP_L // 2)]
    full = _toeplitz(jnp.where(valid[None], vals, NEG_INF), n_chunk)[..., :n_chunk]
    full = jnp.where(np.arange(n_chunk) < n_chunk - 1, full, NEG_INF)
    a4 = TQ // CMP_STRIDE
    full = full.reshape(NSA_KV, NSA_HPG, CMP_STRIDE, n_tiles, a4, n_chunk).transpose(3, 0, 1, 4, 2, 5)
    bias_c = full.reshape(n_tiles, NSA_KV, QROWS, n_chunk)
    return bias_c, bias_w, bias_s, bias_far


def _selection_tables(seq):
    n_chunk = seq // CMP_STRIDE
    c = np.arange(n_chunk)
    n = np.arange(LANES)
    start = c * CMP_STRIDE
    overlap = ((start[:, None] <= n[None, :] * SEL_L + SEL_L - 1) & (start[:, None] + CMP_L - 1 >= n[None, :] * SEL_L)
               & (c < n_chunk - 1)[:, None] & (n < seq // SEL_L)[None, :])
    pos = np.arange(seq + WINDOW) - WINDOW
    expand = (pos[:, None] >= 0) & (pos[:, None] // SEL_L == n[None, :])
    return jnp.asarray(overlap, BF16), jnp.asarray(expand, BF16)


def _block_ones(width, group):
    idx = np.arange(width) // group
    return jnp.asarray((idx[:, None] == idx[None, :]) / group, BF16)


def _block_diag(w):
    nb, n, m = w.shape
    eye = jnp.eye(nb, dtype=w.dtype)
    return jnp.einsum('hij,hg->higj', w, eye).reshape(nb * n, nb * m)


def _compress_weights(w1, w2, pos):
    half_l = CMP_L // 2
    parts = []
    for half in range(2):
        wh = w1[half * half_l * HEAD_DIM:(half + 1) * half_l * HEAD_DIM].reshape(half_l, HEAD_DIM, CMP_HIDDEN)
        z = jnp.zeros_like(wh)
        for g in range(NSA_KV):
            grp = [wh if gg == g else z for gg in range(NSA_KV)]
            parts.append(jnp.stack(grp, axis=1).reshape(half_l * KV_W, CMP_HIDDEN))
    w1cat = jnp.concatenate(parts, axis=1).astype(BF16)
    w2bd = _block_diag(jnp.stack([w2] * NSA_KV)).astype(BF16)
    prow = [jnp.tile(pos[half * half_l:(half + 1) * half_l][:, None, :], (1, NSA_KV, 1)).reshape(-1)
            for half in range(2)]
    pmat = jnp.zeros((8, half_l * KV_W), F32).at[0].set(prow[0]).at[1].set(prow[1]).astype(BF16)
    return w1cat, w2bd, pmat


def kernel(x, mem, rel_bias, norm_mix, w_in, rg_conv_w, rg_conv_b, rg_w_r, rg_b_r, rg_w_i, rg_b_i, rg_lambda, nsa_g_q, nsa_g_kc, nsa_g_ks, nsa_g_kw, cmp_pos_k, cmp_pos_v, cmp_k_w1, cmp_k_w2, cmp_v_w1, cmp_v_w2, out_g_rg, out_g_nsa, w_out, norm_x, norm_mem, xa_w_q, xa_w_kv, xa_w_o, xa_g_q, xa_g_k, norm_moe, router_g_w, router_g_b, router_e_w, router_e_b, exp_w1, exp_w3, exp_w2):
    bsz, seq, _ = x.shape
    n_tok = bsz * seq
    assert seq % FAR_TK == 0 and seq // SEL_L <= LANES and norm_mix.shape[0] == 1
    l = 0
    row = lambda v: v.reshape(1, -1).astype(F32)

    perm = np.array([(half * NSA_HPG + p) * HEAD_DIM + d
                     for p in range(NSA_HPG) for half in range(NSA_KV) for d in range(HEAD_DIM)])
    offs = np.cumsum([0, RG_WIDTH, RG_WIDTH, NSA_WIDTH] + [KV_W] * 6)
    w = w_in[l]
    wrg = w[:, :offs[2]].astype(BF16)
    wq = w[:, offs[2]:offs[3]][:, perm].astype(BF16)
    wkv = w[:, offs[3]:offs[9]].astype(BF16)
    wgl = jnp.pad(w[:, offs[9]:], ((0, 0), (0, LANES - 3 * NSA_HEADS))).astype(BF16)
    ones64 = _block_ones(NSA_WIDTH, HEAD_DIM)
    gq = row(jnp.tile(nsa_g_q[l], NSA_HEADS) * HEAD_DIM ** -0.5)
    u, gate, q, kc, vc, ks, vs, kw, vw, gates = _inproj(
        x.reshape(n_tok, D_MODEL), row(norm_mix[l]), wrg, wq, wkv, wgl, gq,
        row(jnp.tile(nsa_g_ks[l], NSA_KV)), row(jnp.tile(nsa_g_kw[l], NSA_KV)), ones64)

    wg = jnp.concatenate([_block_diag(rg_w_r[l]), _block_diag(rg_w_i[l])], axis=1).astype(BF16)
    bg = jnp.concatenate([rg_b_r[l], rg_b_i[l]]).reshape(1, -1)
    y_rg = _rglru(u.reshape(bsz, seq, RG_WIDTH), gate.reshape(bsz, seq, RG_WIDTH),
                  rg_conv_w[l].reshape(CONV_W, RG_WIDTH), row(rg_conv_b[l]), wg, bg, row(rg_lambda[l]),
                  row(out_g_rg[l]))

    n_chunk = seq // CMP_STRIDE
    w1k, w2k, pk = _compress_weights(cmp_k_w1[l], cmp_k_w2[l], cmp_pos_k[l])
    w1v, w2v, pv = _compress_weights(cmp_v_w1[l], cmp_v_w2[l], cmp_pos_v[l])
    kcmp, vcmp = _compress(kc.reshape(bsz, n_chunk, CMP_STRIDE * KV_W), vc.reshape(bsz, n_chunk, CMP_STRIDE * KV_W),
                           w1k, w2k, pk, w1v, w2v, pv, row(jnp.tile(nsa_g_kc[l], NSA_KV)),
                           ones64[:KV_W, :KV_W])
    padw = lambda t: jnp.pad(t.reshape(bsz, seq, KV_W), ((0, 0), (WINDOW, 0), (0, 0)))
    bias_c, bias_w, bias_s, bias_far = _bias_tables(rel_bias, seq)
    overlap, expand = _selection_tables(seq)
    y_nsa = _nsa(q.reshape(bsz, seq, NSA_WIDTH), gates.reshape(bsz, seq, LANES), kcmp, vcmp,
                 padw(ks), padw(vs), padw(kw), padw(vw), expand, overlap, bias_c, bias_w, bias_s, bias_far,
                 row(out_g_nsa[l][perm]))

    kx, vx = _memkv(mem, row(norm_mem[l]), xa_w_kv[l].astype(BF16), row(xa_g_k[l]))
    wo_mix = w_out[l]
    wr = jnp.pad(jnp.concatenate([router_g_w[l], router_e_w[l]], axis=1),
                 ((0, 0), (0, LANES - N_GROUPS - N_EXPERTS)))
    br = jnp.pad(jnp.concatenate([router_g_b[l], router_e_b[l]]), (0, LANES - N_GROUPS - N_EXPERTS)).reshape(1, -1)
    h2, xt, rw, ri, counts = _mid(
        x, y_rg, y_nsa, wo_mix[:RG_WIDTH].astype(BF16), wo_mix[RG_WIDTH:][perm].astype(BF16), row(norm_x[l]),
        xa_w_q[l].astype(BF16), row(xa_g_q[l] * X_HEAD_DIM ** -0.5), kx, vx, xa_w_o[l].astype(BF16),
        row(norm_moe[l]), wr, br)

    n_slots = 2 * n_tok
    n_blocks = n_slots // MOE_TB + N_EXPERTS
    n_pad = n_blocks * MOE_TB
    cnt = counts[0, :N_EXPERTS].astype(jnp.int32)
    pcnt = (cnt + MOE_TB - 1) // MOE_TB * MOE_TB
    pends = jnp.cumsum(pcnt)
    pstart = jnp.pad((pends - pcnt).astype(F32), (0, LANES - N_EXPERTS)).reshape(1, LANES)
    blk_exp = jnp.minimum(jnp.sum(pends[None, :] <= jnp.arange(n_blocks, dtype=jnp.int32)[:, None] * MOE_TB, axis=1),
                          N_EXPERTS - 1).astype(jnp.int32)
    n_used = (pends[-1:] // MOE_TB).astype(jnp.int32)
    dest = _dest(ri.reshape(n_tok, LANES), pstart)
    tmd = min(TM_DMA, n_tok)
    da = dest[:, 0].reshape(n_tok // tmd, 1, tmd)
    db = dest[:, 1].reshape(n_tok // tmd, 1, tmd)
    xs = _dispatch(da, db, xt, jnp.zeros((n_pad * ROW_TILE, LANES), F32))
    ys = _ffn(blk_exp, n_used, xs, exp_w1[l].astype(BF16), exp_w3[l].astype(BF16), exp_w2[l].astype(BF16))
    out = _combine(da, db, h2.reshape(n_tok, D_MODEL), rw.reshape(n_tok, LANES), ys)
    return out.reshape(bsz, seq, D_MODEL)
```

```python
import functools
import math

import numpy as np
import jax
import jax.numpy as jnp
from jax import lax
from jax.experimental import pallas as pl
from jax.experimental.pallas import tpu as pltpu

F32 = jnp.float32
BF16 = jnp.bfloat16

D_MODEL = 1024
RG_WIDTH = 512
RG_BLOCKS = 8
RG_BLOCK = 64
CONV_W = 4
RG_C = 8.0
NSA_WIDTH = 512
NSA_HEADS = 8
HEAD_DIM = 64
NSA_KV = 2
NSA_HPG = 4
KV_W = 128
CMP_L = 32
CMP_STRIDE = 16
CMP_HIDDEN = 256
SEL_L = 64
N_SEL = 8
WINDOW = 512
NUM_BUCKETS = 32
MAX_DIST = 128
X_HEADS = 4
X_HEAD_DIM = 256
N_GROUPS = 4
EXP_PER_GROUP = 8
N_EXPERTS = 32
D_EXPERT = 512
EPS = 1e-6
NEG_INF = -1e30
MASKED_BELOW = -1e29
SEL_FORCE = 1e9
LANES = 128

TQ = 64
NEAR = WINDOW + TQ
FAR_TK = 512
QROWS2 = NSA_HEADS * TQ
UNSEL_PENALTY = 2.0 ** 100

TM_PROJ = 512
TM_MID = 256
TM_DEST = 512
TM_DMA = 512
MOE_TB = 256
ROW_TILE = D_MODEL // LANES
RG_CHUNK = 256
VMEM_LIMIT = 56 * 1024 * 1024


def _cparams(n_axes):
    return pltpu.CompilerParams(dimension_semantics=("arbitrary",) * n_axes,
                                vmem_limit_bytes=VMEM_LIMIT)


def _dot(a, b):
    return jnp.dot(a, b, preferred_element_type=F32)


def _dot_nt(a, b):
    return lax.dot_general(a, b, (((1,), (1,)), ((), ())), preferred_element_type=F32)


def _gelu_tanh(x):
    return 0.5 * x * (1.0 + jnp.tanh(math.sqrt(2.0 / math.pi) * (x + 0.044715 * (x * x * x))))


def _rms(x, g):
    return x * lax.rsqrt(jnp.mean(x * x, axis=-1, keepdims=True) + EPS) * g


def _group_rms(x, ones_blk, g):
    ms = _dot((x * x).astype(BF16), ones_blk)
    return x * lax.rsqrt(ms + EPS) * g


def _inproj_kernel(x_ref, g_ref, wrg_ref, wq_ref, wkv_ref, wgl_ref, gq_ref, gks_ref, gkw_ref, ones_ref,
                   u_ref, gate_ref, q_ref, kc_ref, vc_ref, ks_ref, vs_ref, kw_ref, vw_ref, gates_ref):
    xb = _rms(x_ref[...], g_ref[...]).astype(BF16)
    rg = _dot(xb, wrg_ref[...])
    u_ref[...] = rg[:, :RG_WIDTH].astype(BF16)
    gate_ref[...] = rg[:, RG_WIDTH:].astype(BF16)
    q = _dot(xb, wq_ref[...])
    q_ref[...] = _group_rms(q, ones_ref[...], gq_ref[...]).astype(BF16)
    kv = _dot(xb, wkv_ref[...])
    ones_kv = ones_ref[:KV_W, :KV_W]
    kc_ref[...] = kv[:, 0 * KV_W:1 * KV_W].astype(BF16)
    vc_ref[...] = kv[:, 1 * KV_W:2 * KV_W].astype(BF16)
    ks_ref[...] = _group_rms(kv[:, 2 * KV_W:3 * KV_W], ones_kv, gks_ref[...]).astype(BF16)
    vs_ref[...] = kv[:, 3 * KV_W:4 * KV_W].astype(BF16)
    kw_ref[...] = _group_rms(kv[:, 4 * KV_W:5 * KV_W], ones_kv, gkw_ref[...]).astype(BF16)
    vw_ref[...] = kv[:, 5 * KV_W:6 * KV_W].astype(BF16)
    gates_ref[...] = jax.nn.sigmoid(_dot(xb, wgl_ref[...]))


def _inproj(x2, g, wrg, wq, wkv, wgl, gq, gks, gkw, ones_blk):
    n_tok = x2.shape[0]
    tm = min(TM_PROJ, n_tok)
    full = lambda a: pl.BlockSpec(a.shape, lambda i: (0,) * a.ndim)
    row = lambda w: pl.BlockSpec((tm, w), lambda i: (i, 0))
    outs = [(RG_WIDTH, BF16), (RG_WIDTH, BF16), (NSA_WIDTH, BF16)] + [(KV_W, BF16)] * 6 + [(LANES, F32)]
    return pl.pallas_call(
        _inproj_kernel,
        grid=(n_tok // tm,),
        in_specs=[row(D_MODEL)] + [full(a) for a in (g, wrg, wq, wkv, wgl, gq, gks, gkw, ones_blk)],
        out_specs=[row(w) for w, _ in outs],
        out_shape=[jax.ShapeDtypeStruct((n_tok, w), dt) for w, dt in outs],
        compiler_params=_cparams(1),
    )(x2, g, wrg, wq, wkv, wgl, gq, gks, gkw, ones_blk)


def _rglru_kernel(u_ref, gate_ref, cw_ref, cb_ref, wg_ref, bg_ref, lam_ref, og_ref, y_ref, upad, a_s, h_s):
    seq = u_ref.shape[1]
    upad[0:8, :] = jnp.zeros((8, RG_WIDTH), F32)
    upad[8:8 + seq, :] = u_ref[0].astype(F32)
    neg_lam = -lam_ref[...]
    softplus = jnp.maximum(neg_lam, 0.0) + jnp.log(1.0 + jnp.exp(-jnp.abs(neg_lam)))
    ch = min(RG_CHUNK, seq)
    for c in range(seq // ch):
        r0 = c * ch
        uc = cb_ref[...]
        for k in range(CONV_W):
            off = 8 + r0 - (CONV_W - 1) + k
            uc = uc + cw_ref[k:k + 1, :] * upad[off:off + ch, :]
        gt = _dot(uc.astype(BF16), wg_ref[...]) + bg_ref[...]
        r = jax.nn.sigmoid(gt[:, :RG_WIDTH])
        ig = jax.nn.sigmoid(gt[:, RG_WIDTH:])
        log_a = (-RG_C) * r * softplus
        a = jnp.exp(log_a)
        a_s[r0:r0 + ch, :] = a
        h_s[r0:r0 + ch, :] = jnp.sqrt(1.0 - a * a) * ig * uc

    def step(t, h):
        h = a_s[pl.ds(t, 1), :] * h + h_s[pl.ds(t, 1), :]
        h_s[pl.ds(t, 1), :] = h
        return h

    lax.fori_loop(0, seq, step, jnp.zeros((1, RG_WIDTH), F32), unroll=8)

    for c in range(seq // ch):
        r0 = c * ch
        y = _gelu_tanh(gate_ref[0, r0:r0 + ch, :].astype(F32)) * h_s[r0:r0 + ch, :]
        y_ref[0, r0:r0 + ch, :] = _rms(y, og_ref[...]).astype(BF16)


def _rglru(u3, gate3, cw, cb, wg, bg, lam, og):
    bsz, seq, _ = u3.shape
    full = lambda a: pl.BlockSpec(a.shape, lambda b: (0,) * a.ndim)
    blk = pl.BlockSpec((1, seq, RG_WIDTH), lambda b: (b, 0, 0))
    return pl.pallas_call(
        _rglru_kernel,
        grid=(bsz,),
        in_specs=[blk, blk] + [full(a) for a in (cw, cb, wg, bg, lam, og)],
        out_specs=blk,
        out_shape=jax.ShapeDtypeStruct((bsz, seq, RG_WIDTH), BF16),
        scratch_shapes=[pltpu.VMEM((seq + 8, RG_WIDTH), F32), pltpu.VMEM((seq, RG_WIDTH), F32),
                        pltpu.VMEM((seq, RG_WIDTH), F32)],
        compiler_params=_cparams(1),
    )(u3, gate3, cw, cb, wg, bg, lam, og)


def _compress_kernel(kx_ref, vx_ref, w1k_ref, w2k_ref, pk_ref, w1v_ref, w2v_ref, pv_ref, gk_ref, ones_ref,
                     ko_ref, vo_ref):
    n_chunk = kx_ref.shape[1]
    half = NSA_KV * CMP_HIDDEN

    def mlp(x_ref, w1_ref, w2_ref, p_ref):
        ab = _dot(x_ref[0], w1_ref[...])
        pos = _dot(p_ref[...], w1_ref[...])
        hid = ab[:, :half] + pltpu.roll(ab[:, half:], n_chunk - 1, 0) + (pos[0:1, :half] + pos[1:2, half:])
        return _dot(_gelu_tanh(hid).astype(BF16), w2_ref[...])

    kc = mlp(kx_ref, w1k_ref, w2k_ref, pk_ref)
    ko_ref[0] = _group_rms(kc, ones_ref[...], gk_ref[...]).astype(BF16)
    vo_ref[0] = mlp(vx_ref, w1v_ref, w2v_ref, pv_ref).astype(BF16)


def _compress(kx, vx, w1k, w2k, pk, w1v, w2v, pv, gk, ones_kv):
    bsz, n_chunk, width = kx.shape
    full = lambda a: pl.BlockSpec(a.shape, lambda b: (0,) * a.ndim)
    xin = pl.BlockSpec((1, n_chunk, width), lambda b: (b, 0, 0))
    out = pl.BlockSpec((1, n_chunk, KV_W), lambda b: (b, 0, 0))
    return pl.pallas_call(
        _compress_kernel,
        grid=(bsz,),
        in_specs=[xin, xin] + [full(a) for a in (w1k, w2k, pk, w1v, w2v, pv, gk, ones_kv)],
        out_specs=[out, out],
        out_shape=[jax.ShapeDtypeStruct((bsz, n_chunk, KV_W), BF16)] * 2,
        compiler_params=_cparams(1),
    )(kx, vx, w1k, w2k, pk, w1v, w2v, pv, gk, ones_kv)


def _nsa_kernel(q_ref, gates_ref, kcmp_ref, vcmp_ref, ksx_ref, vsp_ref, kwp_ref, vwp_ref, ovt_ref,
                bc_ref, bw_ref, bs_ref, bf_ref, og_ref, y_ref):
    i = pl.program_id(1)
    t0 = pl.multiple_of(i * TQ, TQ)
    n_blk = ovt_ref.shape[0]
    lane = lax.broadcasted_iota(jnp.int32, (TQ, LANES), 1)
    lo_half = lane < HEAD_DIM
    pieces = []
    for p in range(NSA_HPG):
        qs = q_ref[0, :, p * LANES:(p + 1) * LANES]
        zero = jnp.zeros_like(qs)
        pieces += [jnp.where(lo_half, qs, zero), jnp.where(lo_half, zero, qs)]
    q8 = jnp.concatenate(pieces, axis=0)

    bc = bc_ref[0]
    lc = _dot_nt(q8, kcmp_ref[0]) + bc
    ec = jnp.where(bc > MASKED_BELOW, jnp.exp(lc - jnp.max(lc, axis=-1, keepdims=True)), 0.0)
    sc = jnp.sum(ec, axis=-1, keepdims=True)
    pc = ec / jnp.where(sc > 0.0, sc, 1.0)
    o_c = _dot(pc.astype(BF16), vcmp_ref[0])

    blocks = [pc[r * TQ:(r + 1) * TQ] for r in range(NSA_HPG * NSA_KV)]
    pcs = jnp.concatenate([sum(blocks[g::NSA_KV]) for g in range(NSA_KV)], axis=0)
    pcs_hi = pcs.astype(BF16)
    pcs_lo = (pcs - pcs_hi.astype(F32)).astype(BF16)
    imp = _dot_nt(ovt_ref[...], pcs_hi) + _dot_nt(ovt_ref[...], pcs_lo)
    blk = lax.broadcasted_iota(jnp.int32, imp.shape, 0)
    forced = (blk == 0) | (blk == i) | (blk == i - 1)
    score = jnp.where(forced, SEL_FORCE, jnp.where(blk > i, -3e38, imp))
    rank = jnp.zeros(imp.shape, F32)
    for m in range(n_blk):
        row = score[m:m + 1, :]
        rank = rank + jnp.where(blk > m, jnp.where(row >= score, 1.0, 0.0), jnp.where(row > score, 1.0, 0.0))
    unsel = jnp.where(rank < N_SEL, 0.0, 1.0)
    unsel_far = jnp.where(blk >= i - WINDOW // SEL_L, 1.0, unsel)
    pad = jnp.zeros((LANES - 2 * n_blk, imp.shape[1]), F32)
    u_t = jnp.concatenate([unsel, unsel_far, pad], axis=0).T
    u_lane = lax.broadcasted_iota(jnp.int32, u_t.shape, 1)
    u_near = jnp.where(u_lane < n_blk, u_t, 0.0).astype(BF16)
    u_far = jnp.where(u_lane >= n_blk, u_t, 0.0).astype(BF16)
    qx_near = jnp.concatenate([q8, jnp.concatenate([u_near] * NSA_HPG, axis=0)], axis=1)
    qx_far = jnp.concatenate([q8, jnp.concatenate([u_far] * NSA_HPG, axis=0)], axis=1)

    lw = _dot_nt(q8, kwp_ref[0, pl.ds(t0, NEAR), :]) + bw_ref[0]
    ew = jnp.exp(lw - jnp.max(lw, axis=-1, keepdims=True))
    o_w = _dot(ew.astype(BF16), vwp_ref[0, pl.ds(t0, NEAR), :]) / jnp.sum(ew, axis=-1, keepdims=True)

    ls = _dot_nt(qx_near, ksx_ref[0, pl.ds(t0, NEAR), :]) + bs_ref[0]
    m1 = jnp.max(ls, axis=-1, keepdims=True)
    e1 = jnp.exp(ls - m1)
    l1 = jnp.sum(e1, axis=-1, keepdims=True)
    acc1 = _dot(e1.astype(BF16), vsp_ref[0, pl.ds(t0, NEAR), :])
    bfar = bf_ref[...]

    def far_step(kf, carry):
        m, l, acc = carry
        base = pl.multiple_of(WINDOW + kf * FAR_TK, FAR_TK)
        lf = _dot_nt(qx_far, ksx_ref[0, pl.ds(base, FAR_TK), :]) + bfar
        m_new = jnp.maximum(m, jnp.max(lf, axis=-1, keepdims=True))
        alpha = jnp.exp(m - m_new)
        e = jnp.exp(lf - m_new)
        l_new = alpha * l + jnp.sum(e, axis=-1, keepdims=True)
        return m_new, l_new, alpha * acc + _dot(e.astype(BF16), vsp_ref[0, pl.ds(base, FAR_TK), :])

    n_far = (jnp.maximum(t0 - WINDOW, 0) + FAR_TK - 1) // FAR_TK
    _, l_s, acc_s = lax.fori_loop(0, n_far, far_step, (m1, l1, acc1))
    o_s = acc_s / l_s

    gates = gates_ref[0]

    def gate_col(j):
        cols = [gates[:, (g * NSA_HPG + p) * 3 + j:(g * NSA_HPG + p) * 3 + j + 1]
                for p in range(NSA_HPG) for g in range(NSA_KV)]
        return jnp.concatenate(cols, axis=0)

    out = gate_col(0) * o_c + gate_col(1) * o_s + gate_col(2) * o_w
    slabs = [jnp.where(lo_half, out[(2 * p) * TQ:(2 * p + 1) * TQ], out[(2 * p + 1) * TQ:(2 * p + 2) * TQ])
             for p in range(NSA_HPG)]
    y_ref[0] = _rms(jnp.concatenate(slabs, axis=-1), og_ref[...]).astype(BF16)


def _nsa(q3, gates3, kcmp, vcmp, ksx, vsp, kwp, vwp, ovt, bias_c, bias_w, bias_s, bias_far, og):
    bsz, seq, _ = q3.shape
    n_chunk = kcmp.shape[1]
    n_var = bias_w.shape[0] - 1
    full = lambda a: pl.BlockSpec(a.shape, lambda b, i: (0,) * a.ndim)
    per_b = lambda a: pl.BlockSpec((1,) + a.shape[1:], lambda b, i: (b,) + (0,) * (a.ndim - 1))
    near = pl.BlockSpec((1, QROWS2, NEAR), lambda b, i: (jnp.minimum(i, n_var), 0, 0))
    return pl.pallas_call(
        _nsa_kernel,
        grid=(bsz, seq // TQ),
        in_specs=[pl.BlockSpec((1, TQ, NSA_WIDTH), lambda b, i: (b, i, 0)),
                  pl.BlockSpec((1, TQ, LANES), lambda b, i: (b, i, 0)),
                  per_b(kcmp), per_b(vcmp), per_b(ksx), per_b(vsp), per_b(kwp), per_b(vwp),
                  full(ovt),
                  pl.BlockSpec((1, QROWS2, n_chunk), lambda b, i: (i, 0, 0)),
                  near, near, full(bias_far), full(og)],
        out_specs=pl.BlockSpec((1, TQ, NSA_WIDTH), lambda b, i: (b, i, 0)),
        out_shape=jax.ShapeDtypeStruct((bsz, seq, NSA_WIDTH), BF16),
        compiler_params=_cparams(2),
    )(q3, gates3, kcmp, vcmp, ksx, vsp, kwp, vwp, ovt, bias_c, bias_w, bias_s, bias_far, og)


def _memkv_kernel(mem_ref, g_ref, wkv_ref, gk_ref, k_ref, v_ref):
    mn = _rms(mem_ref[0], g_ref[...]).astype(BF16)
    kv = _dot(mn, wkv_ref[...])
    for h in range(X_HEADS):
        sl = slice(h * X_HEAD_DIM, (h + 1) * X_HEAD_DIM)
        k_ref[0, :, sl] = _rms(kv[:, sl], gk_ref[...]).astype(BF16)
    v_ref[0] = kv[:, D_MODEL:].astype(BF16)


def _memkv(mem, g, wkv, gk):
    bsz, mlen, _ = mem.shape
    full = lambda a: pl.BlockSpec(a.shape, lambda b: (0,) * a.ndim)
    blk = pl.BlockSpec((1, mlen, D_MODEL), lambda b: (b, 0, 0))
    return pl.pallas_call(
        _memkv_kernel,
        grid=(bsz,),
        in_specs=[blk, full(g), full(wkv), full(gk)],
        out_specs=[blk, blk],
        out_shape=[jax.ShapeDtypeStruct((bsz, mlen, D_MODEL), BF16)] * 2,
        compiler_params=_cparams(1),
    )(mem, g, wkv, gk)


def _mid_kernel(x_ref, yrg_ref, ynsa_ref, woa_ref, wob_ref, gx_ref, wq_ref, gq_ref, k_ref, v_ref, wo_ref,
                gm_ref, wr_ref, br_ref, h_ref, xt_ref, rw_ref, ri_ref, cnt_ref):
    h1 = x_ref[0] + _dot(yrg_ref[0], woa_ref[...]) + _dot(ynsa_ref[0], wob_ref[...])

    q = _dot(_rms(h1, gx_ref[...]).astype(BF16), wq_ref[...])
    heads = []
    for h in range(X_HEADS):
        sl = slice(h * X_HEAD_DIM, (h + 1) * X_HEAD_DIM)
        qh = _rms(q[:, sl], gq_ref[...]).astype(BF16)
        lg = _dot_nt(qh, k_ref[0, :, sl])
        e = jnp.exp(lg - jnp.max(lg, axis=-1, keepdims=True))
        heads.append(_dot(e.astype(BF16), v_ref[0, :, sl]) / jnp.sum(e, axis=-1, keepdims=True))
    h2 = h1 + _dot(jnp.concatenate(heads, axis=-1).astype(BF16), wo_ref[...])
    h_ref[0] = h2

    xt = _rms(h2, gm_ref[...])
    _store_row_tiles(xt_ref, xt)
    lg = jnp.dot(xt, wr_ref[...], preferred_element_type=F32, precision=lax.Precision.HIGHEST) + br_ref[...]
    lane = lax.broadcasted_iota(jnp.int32, lg.shape, 1)
    lane_f = lane.astype(F32)
    first_of = lambda hit: jnp.min(jnp.where(hit, lane_f, 1e9), axis=-1, keepdims=True)
    glog = jnp.where(lane < N_GROUPS, lg, -3e38)
    gmax = jnp.max(glog, axis=-1, keepdims=True)
    gsel = first_of(glog == gmax)
    p_g = 1.0 / jnp.sum(jnp.exp(glog - gmax), axis=-1, keepdims=True)
    lo = N_GROUPS + EXP_PER_GROUP * gsel
    el = jnp.where((lane_f >= lo) & (lane_f < lo + EXP_PER_GROUP), lg, -3e38)
    m_a = jnp.max(el, axis=-1, keepdims=True)
    i_a = first_of(el == m_a)
    el2 = jnp.where(lane_f == i_a, -3e38, el)
    m_b = jnp.max(el2, axis=-1, keepdims=True)
    i_b = first_of(el2 == m_b)
    r = jnp.exp(m_b - m_a)
    w_a = p_g / (1.0 + r)
    w_b = p_g * r / (1.0 + r)
    e_a = i_a - N_GROUPS
    e_b = i_b - N_GROUPS
    rw_ref[0] = jnp.where(lane == 0, w_a, jnp.where(lane == 1, w_b, 0.0))
    ri_ref[0] = jnp.where(lane == 0, e_a, jnp.where(lane == 1, e_b, 0.0)).astype(jnp.int32)

    @pl.when((pl.program_id(0) == 0) & (pl.program_id(1) == 0))
    def _():
        cnt_ref[...] = jnp.zeros_like(cnt_ref)

    hot = jnp.where((lane_f == e_a) | (lane_f == e_b), 1.0, 0.0)
    cnt_ref[...] += jnp.sum(hot, axis=0, keepdims=True)


def _mid(x, yrg, ynsa, woa, wob, gx, wq, gq, kx, vx, wo, gm, wr, br):
    bsz, seq, _ = x.shape
    tm = min(TM_MID, seq)
    mlen = kx.shape[1]
    n_i = seq // tm
    full = lambda a: pl.BlockSpec(a.shape, lambda b, i: (0,) * a.ndim)
    tok = lambda w: pl.BlockSpec((1, tm, w), lambda b, i: (b, i, 0))
    memb = pl.BlockSpec((1, mlen, D_MODEL), lambda b, i: (b, 0, 0))
    xt_spec = pl.BlockSpec((tm * ROW_TILE, LANES), lambda b, i: (b * n_i + i, 0))
    return pl.pallas_call(
        _mid_kernel,
        grid=(bsz, seq // tm),
        in_specs=[tok(D_MODEL), tok(RG_WIDTH), tok(NSA_WIDTH), full(woa), full(wob), full(gx), full(wq), full(gq),
                  memb, memb, full(wo), full(gm), full(wr), full(br)],
        out_specs=[tok(D_MODEL), xt_spec, tok(LANES), tok(LANES), pl.BlockSpec((1, LANES), lambda b, i: (0, 0))],
        out_shape=[jax.ShapeDtypeStruct((bsz, seq, D_MODEL), F32),
                   jax.ShapeDtypeStruct((bsz * seq * ROW_TILE, LANES), F32),
                   jax.ShapeDtypeStruct((bsz, seq, LANES), F32), jax.ShapeDtypeStruct((bsz, seq, LANES), jnp.int32),
                   jax.ShapeDtypeStruct((1, LANES), F32)],
        compiler_params=_cparams(2),
    )(x, yrg, ynsa, woa, wob, gx, wq, gq, kx, vx, wo, gm, wr, br)


def _dest_kernel(ri_ref, pstart_ref, dest_ref, run_ref):
    @pl.when(pl.program_id(0) == 0)
    def _():
        run_ref[...] = jnp.zeros_like(run_ref)

    ri = ri_ref[...]
    tm = ri.shape[0]
    lane = lax.broadcasted_iota(jnp.int32, ri.shape, 1)
    e_a = ri[:, 0:1]
    e_b = ri[:, 1:2]
    hot_a = lane == e_a
    hot_b = lane == e_b
    hot = jnp.where(hot_a | hot_b, 1.0, 0.0)
    row = lax.broadcasted_iota(jnp.int32, (tm, tm), 0)
    col = lax.broadcasted_iota(jnp.int32, (tm, tm), 1)
    earlier = jnp.where(col < row, 1.0, 0.0).astype(BF16)
    base = _dot(earlier, hot.astype(BF16)) + run_ref[...] + pstart_ref[...]
    d_a = jnp.sum(jnp.where(hot_a, base, 0.0), axis=-1, keepdims=True)
    d_b = jnp.sum(jnp.where(hot_b, base, 0.0), axis=-1, keepdims=True)
    dest_ref[...] = jnp.where(lane == 0, d_a, jnp.where(lane == 1, d_b, 0.0)).astype(jnp.int32)
    run_ref[...] += jnp.sum(hot, axis=0, keepdims=True)


def _dest(ri2, pstart):
    n_tok = ri2.shape[0]
    tm = min(TM_DEST, n_tok)
    return pl.pallas_call(
        _dest_kernel,
        grid=(n_tok // tm,),
        in_specs=[pl.BlockSpec((tm, LANES), lambda i: (i, 0)), pl.BlockSpec((1, LANES), lambda i: (0, 0))],
        out_specs=pl.BlockSpec((tm, LANES), lambda i: (i, 0)),
        out_shape=jax.ShapeDtypeStruct((n_tok, LANES), jnp.int32),
        scratch_shapes=[pltpu.VMEM((1, LANES), F32)],
        compiler_params=_cparams(1),
    )(ri2, pstart)


def _store_row_tiles(ref, val):
    n = val.shape[0]
    for c in range(ROW_TILE):
        ref[pl.ds(c, n, stride=ROW_TILE), :] = val[:, c * LANES:(c + 1) * LANES]


def _load_row_tiles(ref, n):
    return [ref[pl.ds(c, n, stride=ROW_TILE), :] for c in range(ROW_TILE)]


def _token_rows(ref, t):
    return ref.at[pl.ds(pl.multiple_of(t * ROW_TILE, ROW_TILE), ROW_TILE), :]


def _dispatch_kernel(da_ref, db_ref, xt_ref, buf_ref, xs_ref, sem):
    del buf_ref
    tm = da_ref.shape[2]

    def issue(t, c):
        pltpu.make_async_copy(_token_rows(xt_ref, t), _token_rows(xs_ref, da_ref[0, 0, t]), sem).start()
        pltpu.make_async_copy(_token_rows(xt_ref, t), _token_rows(xs_ref, db_ref[0, 0, t]), sem).start()
        return c

    lax.fori_loop(0, tm, issue, 0)

    def drain(t, c):
        pltpu.make_async_copy(_token_rows(xt_ref, 0), _token_rows(xs_ref, 0), sem).wait()
        pltpu.make_async_copy(_token_rows(xt_ref, 0), _token_rows(xs_ref, 0), sem).wait()
        return c

    lax.fori_loop(0, tm, drain, 0)


def _dispatch(da, db, xt_rows, buf):
    n_tiles, _, tm = da.shape
    smem = pl.BlockSpec((1, 1, tm), lambda i: (i, 0, 0), memory_space=pltpu.SMEM)
    hbm = pl.BlockSpec(memory_space=pl.ANY)
    return pl.pallas_call(
        _dispatch_kernel,
        grid=(n_tiles,),
        in_specs=[smem, smem, pl.BlockSpec((tm * ROW_TILE, LANES), lambda i: (i, 0)), hbm],
        out_specs=hbm,
        out_shape=jax.ShapeDtypeStruct(buf.shape, buf.dtype),
        scratch_shapes=[pltpu.SemaphoreType.DMA(())],
        input_output_aliases={3: 0},
        compiler_params=pltpu.CompilerParams(dimension_semantics=("arbitrary",), has_side_effects=True,
                                             vmem_limit_bytes=VMEM_LIMIT),
    )(da, db, xt_rows, buf)


def _ffn_kernel(bexp_ref, nused_ref, xs_ref, w1_ref, w3_ref, w2_ref, ys_ref):
    del bexp_ref
    j = pl.program_id(0)

    @pl.when(j < nused_ref[0])
    def _():
        xb = jnp.concatenate(_load_row_tiles(xs_ref, MOE_TB), axis=-1).astype(BF16)
        a = _dot(xb, w1_ref[0])
        h = a * jax.nn.sigmoid(a) * _dot(xb, w3_ref[0])
        _store_row_tiles(ys_ref, _dot(h.astype(BF16), w2_ref[0]))

    @pl.when(j >= nused_ref[0])
    def _():
        ys_ref[...] = jnp.zeros_like(ys_ref)


def _ffn(blk_exp, n_used, xs, w1, w3, w2):
    n_blocks = xs.shape[0] // (MOE_TB * ROW_TILE)
    rows = pl.BlockSpec((MOE_TB * ROW_TILE, LANES), lambda j, be, nu: (j, 0))
    grid_spec = pltpu.PrefetchScalarGridSpec(
        num_scalar_prefetch=2,
        grid=(n_blocks,),
        in_specs=[rows,
                  pl.BlockSpec((1, D_MODEL, D_EXPERT), lambda j, be, nu: (be[j], 0, 0)),
                  pl.BlockSpec((1, D_MODEL, D_EXPERT), lambda j, be, nu: (be[j], 0, 0)),
                  pl.BlockSpec((1, D_EXPERT, D_MODEL), lambda j, be, nu: (be[j], 0, 0))],
        out_specs=rows,
    )
    return pl.pallas_call(
        _ffn_kernel,
        grid_spec=grid_spec,
        out_shape=jax.ShapeDtypeStruct(xs.shape, F32),
        compiler_params=_cparams(1),
    )(blk_exp, n_used, xs, w1, w3, w2)


def _combine_kernel(da_ref, db_ref, h_ref, rw_ref, ys_ref, o_ref, ya, yb, sem):
    tm = da_ref.shape[2]

    def issue(t, c):
        pltpu.make_async_copy(_token_rows(ys_ref, da_ref[0, 0, t]), _token_rows(ya, t), sem).start()
        pltpu.make_async_copy(_token_rows(ys_ref, db_ref[0, 0, t]), _token_rows(yb, t), sem).start()
        return c

    lax.fori_loop(0, tm, issue, 0)

    def drain(t, c):
        pltpu.make_async_copy(_token_rows(ys_ref, 0), _token_rows(ya, 0), sem).wait()
        pltpu.make_async_copy(_token_rows(ys_ref, 0), _token_rows(yb, 0), sem).wait()
        return c

    lax.fori_loop(0, tm, drain, 0)
    rw = rw_ref[...]
    mix = [rw[:, 0:1] * a + rw[:, 1:2] * b for a, b in zip(_load_row_tiles(ya, tm), _load_row_tiles(yb, tm))]
    o_ref[...] = h_ref[...] + jnp.concatenate(mix, axis=-1)


def _combine(da, db, h2, rw, ys):
    n_tiles, _, tm = da.shape
    n_tok = h2.shape[0]
    smem = pl.BlockSpec((1, 1, tm), lambda i: (i, 0, 0), memory_space=pltpu.SMEM)
    row = lambda w: pl.BlockSpec((tm, w), lambda i: (i, 0))
    return pl.pallas_call(
        _combine_kernel,
        grid=(n_tiles,),
        in_specs=[smem, smem, row(D_MODEL), row(LANES), pl.BlockSpec(memory_space=pl.ANY)],
        out_specs=row(D_MODEL),
        out_shape=jax.ShapeDtypeStruct((n_tok, D_MODEL), F32),
        scratch_shapes=[pltpu.VMEM((tm * ROW_TILE, LANES), F32), pltpu.VMEM((tm * ROW_TILE, LANES), F32),
                        pltpu.SemaphoreType.DMA(())],
        compiler_params=_cparams(1),
    )(da, db, h2, rw, ys)


def _rel_bucket_np(dist):
    n = np.maximum(dist, 0)
    max_exact = NUM_BUCKETS // 2
    nf = np.maximum(n, 1).astype(np.float32)
    large = max_exact + (np.log(nf / max_exact) / math.log(MAX_DIST / max_exact)
                         * (NUM_BUCKETS - max_exact)).astype(np.int32)
    large = np.minimum(large, NUM_BUCKETS - 1)
    return np.where(n < max_exact, n, large).astype(np.int32)


def _toeplitz(vec, rows):
    width = vec.shape[-1] - 1
    flat = jnp.tile(vec, (1,) * (vec.ndim - 1) + (rows,))[..., :rows * width]
    return flat.reshape(vec.shape[:-1] + (rows, width))


def _bias_tables(rel_bias, seq):
    n_chunk = seq // CMP_STRIDE
    n_tiles = seq // TQ
    table = rel_bias.T.astype(F32)

    wide = NEAR + TQ
    k = np.arange(wide + 1)
    dw = np.where(k < NEAR, WINDOW - k, WINDOW + wide + 1 - k)
    used = (k < NEAR) | (k > wide + 1 - TQ)
    vals = table[:, _rel_bucket_np(dw)]

    n_var = WINDOW // TQ
    first_key = WINDOW - TQ * np.arange(n_var + 1)[:, None, None]
    in_seq = np.arange(NEAR)[None, None, :] >= first_key

    def near_tile(valid):
        t = _toeplitz(jnp.where(valid[None, :], vals, NEG_INF), TQ)[:, :, :NEAR]
        t = t.reshape(NSA_KV, NSA_HPG, TQ, NEAR).transpose(1, 0, 2, 3).reshape(1, QROWS2, NEAR)
        return jnp.where(in_seq, t, NEG_INF)

    bias_w = near_tile(used & (dw >= 0) & (dw < WINDOW))
    bias_s = near_tile(used & (dw >= 0))
    bias_far = table[:, NUM_BUCKETS - 1].reshape(NSA_KV, NSA_HPG, 1).transpose(1, 0, 2)
    bias_far = jnp.broadcast_to(bias_far, (NSA_HPG, NSA_KV, TQ)).reshape(QROWS2, 1)

    r = np.arange(CMP_STRIDE)[:, None]
    k = np.arange(2 * n_chunk + 1)[None, :]
    lag = 2 * n_chunk + 1 - k
    valid = (k > n_chunk + 1) & (CMP_STRIDE * lag + r >= CMP_L - 1)
    vals = table[:, _rel_bucket_np(CMP_STRIDE * lag + r - CMP_L // 2)]
    full = _toeplitz(jnp.where(valid[None], vals, NEG_INF), n_chunk)[..., :n_chunk]
    full = jnp.where(np.arange(n_chunk) < n_chunk - 1, full, NEG_INF)
    a4 = TQ // CMP_STRIDE
    full = full.reshape(NSA_KV, NSA_HPG, CMP_STRIDE, n_tiles, a4, n_chunk).transpose(3, 1, 0, 4, 2, 5)
    bias_c = full.reshape(n_tiles, QROWS2, n_chunk)
    return bias_c, bias_w, bias_s, bias_far


def _selection_tables(seq):
    n_chunk = seq // CMP_STRIDE
    n_blk = seq // SEL_L
    c = np.arange(n_chunk)
    n = np.arange(n_blk)
    start = c * CMP_STRIDE
    overlap_t = ((start[None, :] <= n[:, None] * SEL_L + SEL_L - 1) & (start[None, :] + CMP_L - 1 >= n[:, None] * SEL_L)
                 & (c < n_chunk - 1)[None, :])
    pos = np.arange(seq + WINDOW) - WINDOW
    lane_blk = np.arange(LANES) % n_blk
    hit = (pos[:, None] >= 0) & (pos[:, None] // SEL_L == lane_blk[None, :]) & (np.arange(LANES) < 2 * n_blk)[None, :]
    return jnp.asarray(overlap_t, BF16), jnp.asarray(np.where(hit, -UNSEL_PENALTY, 0.0), BF16)


def _block_ones(width, group):
    idx = np.arange(width) // group
    return jnp.asarray((idx[:, None] == idx[None, :]) / group, BF16)


def _block_diag(w):
    nb, n, m = w.shape
    eye = jnp.eye(nb, dtype=w.dtype)
    return jnp.einsum('hij,hg->higj', w, eye).reshape(nb * n, nb * m)


def _compress_weights(w1, w2, pos):
    half_l = CMP_L // 2
    parts = []
    for half in range(2):
        wh = w1[half * half_l * HEAD_DIM:(half + 1) * half_l * HEAD_DIM].reshape(half_l, HEAD_DIM, CMP_HIDDEN)
        z = jnp.zeros_like(wh)
        for g in range(NSA_KV):
            grp = [wh if gg == g else z for gg in range(NSA_KV)]
            parts.append(jnp.stack(grp, axis=1).reshape(half_l * KV_W, CMP_HIDDEN))
    w1cat = jnp.concatenate(parts, axis=1).astype(BF16)
    w2bd = _block_diag(jnp.stack([w2] * NSA_KV)).astype(BF16)
    prow = [jnp.tile(pos[half * half_l:(half + 1) * half_l][:, None, :], (1, NSA_KV, 1)).reshape(-1)
            for half in range(2)]
    pmat = jnp.zeros((8, half_l * KV_W), F32).at[0].set(prow[0]).at[1].set(prow[1]).astype(BF16)
    return w1cat, w2bd, pmat


def kernel(x, mem, rel_bias, norm_mix, w_in, rg_conv_w, rg_conv_b, rg_w_r, rg_b_r, rg_w_i, rg_b_i, rg_lambda, nsa_g_q, nsa_g_kc, nsa_g_ks, nsa_g_kw, cmp_pos_k, cmp_pos_v, cmp_k_w1, cmp_k_w2, cmp_v_w1, cmp_v_w2, out_g_rg, out_g_nsa, w_out, norm_x, norm_mem, xa_w_q, xa_w_kv, xa_w_o, xa_g_q, xa_g_k, norm_moe, router_g_w, router_g_b, router_e_w, router_e_b, exp_w1, exp_w3, exp_w2):
    bsz, seq, _ = x.shape
    n_tok = bsz * seq
    assert seq % FAR_TK == 0 and 2 * (seq // SEL_L) <= LANES and norm_mix.shape[0] == 1
    l = 0
    row = lambda v: v.reshape(1, -1).astype(F32)

    perm = np.array([(half * NSA_HPG + p) * HEAD_DIM + d
                     for p in range(NSA_HPG) for half in range(NSA_KV) for d in range(HEAD_DIM)])
    offs = np.cumsum([0, RG_WIDTH, RG_WIDTH, NSA_WIDTH] + [KV_W] * 6)
    w = w_in[l]
    wrg = w[:, :offs[2]].astype(BF16)
    wq = w[:, offs[2]:offs[3]][:, perm].astype(BF16)
    wkv = w[:, offs[3]:offs[9]].astype(BF16)
    wgl = jnp.pad(w[:, offs[9]:], ((0, 0), (0, LANES - 3 * NSA_HEADS))).astype(BF16)
    ones64 = _block_ones(NSA_WIDTH, HEAD_DIM)
    gq = row(jnp.tile(nsa_g_q[l], NSA_HEADS) * HEAD_DIM ** -0.5)
    u, gate, q, kc, vc, ks, vs, kw, vw, gates = _inproj(
        x.reshape(n_tok, D_MODEL), row(norm_mix[l]), wrg, wq, wkv, wgl, gq,
        row(jnp.tile(nsa_g_ks[l], NSA_KV)), row(jnp.tile(nsa_g_kw[l], NSA_KV)), ones64)

    wg = jnp.concatenate([_block_diag(rg_w_r[l]), _block_diag(rg_w_i[l])], axis=1).astype(BF16)
    bg = jnp.concatenate([rg_b_r[l], rg_b_i[l]]).reshape(1, -1)
    y_rg = _rglru(u.reshape(bsz, seq, RG_WIDTH), gate.reshape(bsz, seq, RG_WIDTH),
                  rg_conv_w[l].reshape(CONV_W, RG_WIDTH), row(rg_conv_b[l]), wg, bg, row(rg_lambda[l]),
                  row(out_g_rg[l]))

    n_chunk = seq // CMP_STRIDE
    w1k, w2k, pk = _compress_weights(cmp_k_w1[l], cmp_k_w2[l], cmp_pos_k[l])
    w1v, w2v, pv = _compress_weights(cmp_v_w1[l], cmp_v_w2[l], cmp_pos_v[l])
    kcmp, vcmp = _compress(kc.reshape(bsz, n_chunk, CMP_STRIDE * KV_W), vc.reshape(bsz, n_chunk, CMP_STRIDE * KV_W),
                           w1k, w2k, pk, w1v, w2v, pv, row(jnp.tile(nsa_g_kc[l], NSA_KV)),
                           ones64[:KV_W, :KV_W])
    padw = lambda t: jnp.pad(t.reshape(bsz, seq, KV_W), ((0, 0), (WINDOW, 0), (0, 0)))
    bias_c, bias_w, bias_s, bias_far = _bias_tables(rel_bias, seq)
    overlap_t, penalty = _selection_tables(seq)
    ksx = jnp.concatenate([padw(ks), jnp.broadcast_to(penalty, (bsz,) + penalty.shape)], axis=-1)
    y_nsa = _nsa(q.reshape(bsz, seq, NSA_WIDTH), gates.reshape(bsz, seq, LANES), kcmp, vcmp,
                 ksx, padw(vs), padw(kw), padw(vw), overlap_t, bias_c, bias_w, bias_s, bias_far,
                 row(out_g_nsa[l][perm]))

    kx, vx = _memkv(mem, row(norm_mem[l]), xa_w_kv[l].astype(BF16), row(xa_g_k[l]))
    wo_mix = w_out[l]
    wr = jnp.pad(jnp.concatenate([router_g_w[l], router_e_w[l]], axis=1),
                 ((0, 0), (0, LANES - N_GROUPS - N_EXPERTS)))
    br = jnp.pad(jnp.concatenate([router_g_b[l], router_e_b[l]]), (0, LANES - N_GROUPS - N_EXPERTS)).reshape(1, -1)
    h2, xt, rw, ri, counts = _mid(
        x, y_rg, y_nsa, wo_mix[:RG_WIDTH].astype(BF16), wo_mix[RG_WIDTH:][perm].astype(BF16), row(norm_x[l]),
        xa_w_q[l].astype(BF16), row(xa_g_q[l] * X_HEAD_DIM ** -0.5), kx, vx, xa_w_o[l].astype(BF16),
        row(norm_moe[l]), wr, br)

    n_slots = 2 * n_tok
    n_blocks = n_slots // MOE_TB + N_EXPERTS
    n_pad = n_blocks * MOE_TB
    cnt = counts[0, :N_EXPERTS].astype(jnp.int32)
    pcnt = (cnt + MOE_TB - 1) // MOE_TB * MOE_TB
    pends = jnp.cumsum(pcnt)
    pstart = jnp.pad((pends - pcnt).astype(F32), (0, LANES - N_EXPERTS)).reshape(1, LANES)
    blk_exp = jnp.minimum(jnp.sum(pends[None, :] <= jnp.arange(n_blocks, dtype=jnp.int32)[:, None] * MOE_TB, axis=1),
                          N_EXPERTS - 1).astype(jnp.int32)
    n_used = (pends[-1:] // MOE_TB).astype(jnp.int32)
    dest = _dest(ri.reshape(n_tok, LANES), pstart)
    tmd = min(TM_DMA, n_tok)
    da = dest[:, 0].reshape(n_tok // tmd, 1, tmd)
    db = dest[:, 1].reshape(n_tok // tmd, 1, tmd)
    xs = _dispatch(da, db, xt, jnp.zeros((n_pad * ROW_TILE, LANES), F32))
    ys = _ffn(blk_exp, n_used, xs, exp_w1[l].astype(BF16), exp_w3[l].astype(BF16), exp_w2[l].astype(BF16))
    out = _combine(da, db, h2.reshape(n_tok, D_MODEL), rw.reshape(n_tok, LANES), ys)
    return out.reshape(bsz, seq, D_MODEL)
```

```python
import functools
import math

import numpy as np
import jax
import jax.numpy as jnp
from jax import lax
from jax.experimental import pallas as pl
from jax.experimental.pallas import tpu as pltpu

F32 = jnp.float32
BF16 = jnp.bfloat16

D_MODEL = 1024
RG_WIDTH = 512
RG_BLOCKS = 8
RG_BLOCK = 64
CONV_W = 4
RG_C = 8.0
NSA_WIDTH = 512
NSA_HEADS = 8
HEAD_DIM = 64
NSA_KV = 2
NSA_HPG = 4
KV_W = 128
CMP_L = 32
CMP_STRIDE = 16
CMP_HIDDEN = 256
SEL_L = 64
N_SEL = 8
WINDOW = 512
NUM_BUCKETS = 32
MAX_DIST = 128
X_HEADS = 4
X_HEAD_DIM = 256
N_GROUPS = 4
EXP_PER_GROUP = 8
N_EXPERTS = 32
D_EXPERT = 512
EPS = 1e-6
NEG_INF = -1e30
MASKED_BELOW = -1e29
SEL_FORCE = 1e9
LANES = 128

TQ = 64
NEAR = WINDOW + TQ
FAR_TK = 512
QROWS2 = NSA_HEADS * TQ
UNSEL_PENALTY = 2.0 ** 100

TM_PROJ = 512
TM_MID = 512
TM_DEST = 512
TM_DMA = 512
DMA_UNROLL = 8
MOE_TB = 512
ROW_TILE = D_MODEL // LANES
RG_CHUNK = 256
VMEM_LIMIT = 56 * 1024 * 1024


def _cparams(n_axes):
    return pltpu.CompilerParams(dimension_semantics=("arbitrary",) * n_axes,
                                vmem_limit_bytes=VMEM_LIMIT)


def _dot(a, b):
    return jnp.dot(a, b, preferred_element_type=F32)


def _dot_nt(a, b):
    return lax.dot_general(a, b, (((1,), (1,)), ((), ())), preferred_element_type=F32)


def _gelu_tanh(x):
    return 0.5 * x * (1.0 + jnp.tanh(math.sqrt(2.0 / math.pi) * (x + 0.044715 * (x * x * x))))


def _rms(x, g):
    return x * lax.rsqrt(jnp.mean(x * x, axis=-1, keepdims=True) + EPS) * g


def _group_rms(x, ones_blk, g):
    ms = _dot((x * x).astype(BF16), ones_blk)
    return x * lax.rsqrt(ms + EPS) * g


def _inproj_kernel(x_ref, g_ref, wrg_ref, wq_ref, wkv_ref, wgl_ref, gq_ref, gks_ref, gkw_ref, ones_ref,
                   u_ref, gate_ref, q_ref, kc_ref, vc_ref, ks_ref, vs_ref, kw_ref, vw_ref, gates_ref):
    xb = _rms(x_ref[...], g_ref[...]).astype(BF16)
    rg = _dot(xb, wrg_ref[...])
    u_ref[...] = rg[:, :RG_WIDTH].astype(BF16)
    gate_ref[...] = rg[:, RG_WIDTH:].astype(BF16)
    q = _dot(xb, wq_ref[...])
    q_ref[...] = _group_rms(q, ones_ref[...], gq_ref[...]).astype(BF16)
    kv = _dot(xb, wkv_ref[...])
    ones_kv = ones_ref[:KV_W, :KV_W]
    kc_ref[...] = kv[:, 0 * KV_W:1 * KV_W].astype(BF16)
    vc_ref[...] = kv[:, 1 * KV_W:2 * KV_W].astype(BF16)
    ks_ref[...] = _group_rms(kv[:, 2 * KV_W:3 * KV_W], ones_kv, gks_ref[...]).astype(BF16)
    vs_ref[...] = kv[:, 3 * KV_W:4 * KV_W].astype(BF16)
    kw_ref[...] = _group_rms(kv[:, 4 * KV_W:5 * KV_W], ones_kv, gkw_ref[...]).astype(BF16)
    vw_ref[...] = kv[:, 5 * KV_W:6 * KV_W].astype(BF16)
    gates_ref[...] = jax.nn.sigmoid(_dot(xb, wgl_ref[...]))


def _inproj(x2, g, wrg, wq, wkv, wgl, gq, gks, gkw, ones_blk):
    n_tok = x2.shape[0]
    tm = min(TM_PROJ, n_tok)
    full = lambda a: pl.BlockSpec(a.shape, lambda i: (0,) * a.ndim)
    row = lambda w: pl.BlockSpec((tm, w), lambda i: (i, 0))
    outs = [(RG_WIDTH, BF16), (RG_WIDTH, BF16), (NSA_WIDTH, BF16)] + [(KV_W, BF16)] * 6 + [(LANES, F32)]
    return pl.pallas_call(
        _inproj_kernel,
        grid=(n_tok // tm,),
        in_specs=[row(D_MODEL)] + [full(a) for a in (g, wrg, wq, wkv, wgl, gq, gks, gkw, ones_blk)],
        out_specs=[row(w) for w, _ in outs],
        out_shape=[jax.ShapeDtypeStruct((n_tok, w), dt) for w, dt in outs],
        compiler_params=_cparams(1),
    )(x2, g, wrg, wq, wkv, wgl, gq, gks, gkw, ones_blk)


def _rglru_kernel(u_ref, gate_ref, cw_ref, cb_ref, wg_ref, bg_ref, lam_ref, og_ref, y_ref, upad, a_s, h_s):
    seq = u_ref.shape[1]
    upad[0:8, :] = jnp.zeros((8, RG_WIDTH), F32)
    upad[8:8 + seq, :] = u_ref[0].astype(F32)
    neg_lam = -lam_ref[...]
    softplus = jnp.maximum(neg_lam, 0.0) + jnp.log(1.0 + jnp.exp(-jnp.abs(neg_lam)))
    ch = min(RG_CHUNK, seq)
    for c in range(seq // ch):
        r0 = c * ch
        uc = cb_ref[...]
        for k in range(CONV_W):
            off = 8 + r0 - (CONV_W - 1) + k
            uc = uc + cw_ref[k:k + 1, :] * upad[off:off + ch, :]
        gt = _dot(uc.astype(BF16), wg_ref[...]) + bg_ref[...]
        r = jax.nn.sigmoid(gt[:, :RG_WIDTH])
        ig = jax.nn.sigmoid(gt[:, RG_WIDTH:])
        log_a = (-RG_C) * r * softplus
        a = jnp.exp(log_a)
        a_s[r0:r0 + ch, :] = a
        h_s[r0:r0 + ch, :] = jnp.sqrt(1.0 - a * a) * ig * uc

    def step(t, h):
        h = a_s[pl.ds(t, 1), :] * h + h_s[pl.ds(t, 1), :]
        h_s[pl.ds(t, 1), :] = h
        return h

    lax.fori_loop(0, seq, step, jnp.zeros((1, RG_WIDTH), F32), unroll=8)

    for c in range(seq // ch):
        r0 = c * ch
        y = _gelu_tanh(gate_ref[0, r0:r0 + ch, :].astype(F32)) * h_s[r0:r0 + ch, :]
        y_ref[0, r0:r0 + ch, :] = _rms(y, og_ref[...]).astype(BF16)


def _rglru(u3, gate3, cw, cb, wg, bg, lam, og):
    bsz, seq, _ = u3.shape
    full = lambda a: pl.BlockSpec(a.shape, lambda b: (0,) * a.ndim)
    blk = pl.BlockSpec((1, seq, RG_WIDTH), lambda b: (b, 0, 0))
    return pl.pallas_call(
        _rglru_kernel,
        grid=(bsz,),
        in_specs=[blk, blk] + [full(a) for a in (cw, cb, wg, bg, lam, og)],
        out_specs=blk,
        out_shape=jax.ShapeDtypeStruct((bsz, seq, RG_WIDTH), BF16),
        scratch_shapes=[pltpu.VMEM((seq + 8, RG_WIDTH), F32), pltpu.VMEM((seq, RG_WIDTH), F32),
                        pltpu.VMEM((seq, RG_WIDTH), F32)],
        compiler_params=_cparams(1),
    )(u3, gate3, cw, cb, wg, bg, lam, og)


def _compress_kernel(kx_ref, vx_ref, w1k_ref, w2k_ref, pk_ref, w1v_ref, w2v_ref, pv_ref, gk_ref, ones_ref,
                     ko_ref, vo_ref):
    n_chunk = kx_ref.shape[1]
    half = NSA_KV * CMP_HIDDEN

    def mlp(x_ref, w1_ref, w2_ref, p_ref):
        ab = _dot(x_ref[0], w1_ref[...])
        pos = _dot(p_ref[...], w1_ref[...])
        hid = ab[:, :half] + pltpu.roll(ab[:, half:], n_chunk - 1, 0) + (pos[0:1, :half] + pos[1:2, half:])
        return _dot(_gelu_tanh(hid).astype(BF16), w2_ref[...])

    kc = mlp(kx_ref, w1k_ref, w2k_ref, pk_ref)
    ko_ref[0] = _group_rms(kc, ones_ref[...], gk_ref[...]).astype(BF16)
    vo_ref[0] = mlp(vx_ref, w1v_ref, w2v_ref, pv_ref).astype(BF16)


def _compress(kx, vx, w1k, w2k, pk, w1v, w2v, pv, gk, ones_kv):
    bsz, n_chunk, width = kx.shape
    full = lambda a: pl.BlockSpec(a.shape, lambda b: (0,) * a.ndim)
    xin = pl.BlockSpec((1, n_chunk, width), lambda b: (b, 0, 0))
    out = pl.BlockSpec((1, n_chunk, KV_W), lambda b: (b, 0, 0))
    return pl.pallas_call(
        _compress_kernel,
        grid=(bsz,),
        in_specs=[xin, xin] + [full(a) for a in (w1k, w2k, pk, w1v, w2v, pv, gk, ones_kv)],
        out_specs=[out, out],
        out_shape=[jax.ShapeDtypeStruct((bsz, n_chunk, KV_W), BF16)] * 2,
        compiler_params=_cparams(1),
    )(kx, vx, w1k, w2k, pk, w1v, w2v, pv, gk, ones_kv)


def _nsa_kernel(q_ref, gates_ref, kcmp_ref, vcmp_ref, ksx_ref, vsp_ref, kwp_ref, vwp_ref, ovt_ref,
                bc_ref, bw_ref, bs_ref, bf_ref, og_ref, y_ref):
    i = pl.program_id(1)
    t0 = pl.multiple_of(i * TQ, TQ)
    n_blk = ovt_ref.shape[0]
    lane = lax.broadcasted_iota(jnp.int32, (TQ, LANES), 1)
    lo_half = lane < HEAD_DIM
    pieces = []
    for p in range(NSA_HPG):
        qs = q_ref[0, :, p * LANES:(p + 1) * LANES]
        zero = jnp.zeros_like(qs)
        pieces += [jnp.where(lo_half, qs, zero), jnp.where(lo_half, zero, qs)]
    q8 = jnp.concatenate(pieces, axis=0)

    bc = bc_ref[0]
    lc = _dot_nt(q8, kcmp_ref[0]) + bc
    ec = jnp.where(bc > MASKED_BELOW, jnp.exp(lc - jnp.max(lc, axis=-1, keepdims=True)), 0.0)
    sc = jnp.sum(ec, axis=-1, keepdims=True)
    pc = ec / jnp.where(sc > 0.0, sc, 1.0)
    o_c = _dot(pc.astype(BF16), vcmp_ref[0])

    blocks = [pc[r * TQ:(r + 1) * TQ] for r in range(NSA_HPG * NSA_KV)]
    pcs = jnp.concatenate([sum(blocks[g::NSA_KV]) for g in range(NSA_KV)], axis=0)
    pcs_hi = pcs.astype(BF16)
    pcs_lo = (pcs - pcs_hi.astype(F32)).astype(BF16)
    imp = _dot_nt(ovt_ref[...], pcs_hi) + _dot_nt(ovt_ref[...], pcs_lo)
    blk = lax.broadcasted_iota(jnp.int32, imp.shape, 0)
    forced = (blk == 0) | (blk == i) | (blk == i - 1)
    score = jnp.where(forced, SEL_FORCE, jnp.where(blk > i, -3e38, imp))
    rank = jnp.zeros(imp.shape, F32)
    for m in range(n_blk):
        row = score[m:m + 1, :]
        rank = rank + jnp.where(blk > m, jnp.where(row >= score, 1.0, 0.0), jnp.where(row > score, 1.0, 0.0))
    unsel = jnp.where(rank < N_SEL, 0.0, 1.0)
    unsel_far = jnp.where(blk >= i - WINDOW // SEL_L, 1.0, unsel)
    pad = jnp.zeros((LANES - 2 * n_blk, imp.shape[1]), F32)
    u_t = jnp.concatenate([unsel, unsel_far, pad], axis=0).T
    u_lane = lax.broadcasted_iota(jnp.int32, u_t.shape, 1)
    u_near = jnp.where(u_lane < n_blk, u_t, 0.0).astype(BF16)
    u_far = jnp.where(u_lane >= n_blk, u_t, 0.0).astype(BF16)
    qx_near = jnp.concatenate([q8, jnp.concatenate([u_near] * NSA_HPG, axis=0)], axis=1)
    qx_far = jnp.concatenate([q8, jnp.concatenate([u_far] * NSA_HPG, axis=0)], axis=1)

    lw = _dot_nt(q8, kwp_ref[0, pl.ds(t0, NEAR), :]) + bw_ref[0]
    ew = jnp.exp(lw - jnp.max(lw, axis=-1, keepdims=True))
    o_w = _dot(ew.astype(BF16), vwp_ref[0, pl.ds(t0, NEAR), :]) / jnp.sum(ew, axis=-1, keepdims=True)

    ls = _dot_nt(qx_near, ksx_ref[0, pl.ds(t0, NEAR), :]) + bs_ref[0]
    m1 = jnp.max(ls, axis=-1, keepdims=True)
    e1 = jnp.exp(ls - m1)
    l1 = jnp.sum(e1, axis=-1, keepdims=True)
    acc1 = _dot(e1.astype(BF16), vsp_ref[0, pl.ds(t0, NEAR), :])
    bfar = bf_ref[...]

    def far_step(kf, carry):
        m, l, acc = carry
        base = pl.multiple_of(WINDOW + kf * FAR_TK, FAR_TK)
        lf = _dot_nt(qx_far, ksx_ref[0, pl.ds(base, FAR_TK), :]) + bfar
        m_new = jnp.maximum(m, jnp.max(lf, axis=-1, keepdims=True))
        alpha = jnp.exp(m - m_new)
        e = jnp.exp(lf - m_new)
        l_new = alpha * l + jnp.sum(e, axis=-1, keepdims=True)
        return m_new, l_new, alpha * acc + _dot(e.astype(BF16), vsp_ref[0, pl.ds(base, FAR_TK), :])

    n_far = (jnp.maximum(t0 - WINDOW, 0) + FAR_TK - 1) // FAR_TK
    _, l_s, acc_s = lax.fori_loop(0, n_far, far_step, (m1, l1, acc1))
    o_s = acc_s / l_s

    gates = gates_ref[0]

    def gate_col(j):
        cols = [gates[:, (g * NSA_HPG + p) * 3 + j:(g * NSA_HPG + p) * 3 + j + 1]
                for p in range(NSA_HPG) for g in range(NSA_KV)]
        return jnp.concatenate(cols, axis=0)

    out = gate_col(0) * o_c + gate_col(1) * o_s + gate_col(2) * o_w
    slabs = [jnp.where(lo_half, out[(2 * p) * TQ:(2 * p + 1) * TQ], out[(2 * p + 1) * TQ:(2 * p + 2) * TQ])
             for p in range(NSA_HPG)]
    y_ref[0] = _rms(jnp.concatenate(slabs, axis=-1), og_ref[...]).astype(BF16)


def _nsa(q3, gates3, kcmp, vcmp, ksx, vsp, kwp, vwp, ovt, bias_c, bias_w, bias_s, bias_far, og):
    bsz, seq, _ = q3.shape
    n_chunk = kcmp.shape[1]
    n_var = bias_w.shape[0] - 1
    full = lambda a: pl.BlockSpec(a.shape, lambda b, i: (0,) * a.ndim)
    per_b = lambda a: pl.BlockSpec((1,) + a.shape[1:], lambda b, i: (b,) + (0,) * (a.ndim - 1))
    near = pl.BlockSpec((1, QROWS2, NEAR), lambda b, i: (jnp.minimum(i, n_var), 0, 0))
    return pl.pallas_call(
        _nsa_kernel,
        grid=(bsz, seq // TQ),
        in_specs=[pl.BlockSpec((1, TQ, NSA_WIDTH), lambda b, i: (b, i, 0)),
                  pl.BlockSpec((1, TQ, LANES), lambda b, i: (b, i, 0)),
                  per_b(kcmp), per_b(vcmp), per_b(ksx), per_b(vsp), per_b(kwp), per_b(vwp),
                  full(ovt),
                  pl.BlockSpec((1, QROWS2, n_chunk), lambda b, i: (i, 0, 0)),
                  near, near, full(bias_far), full(og)],
        out_specs=pl.BlockSpec((1, TQ, NSA_WIDTH), lambda b, i: (b, i, 0)),
        out_shape=jax.ShapeDtypeStruct((bsz, seq, NSA_WIDTH), BF16),
        compiler_params=_cparams(2),
    )(q3, gates3, kcmp, vcmp, ksx, vsp, kwp, vwp, ovt, bias_c, bias_w, bias_s, bias_far, og)


def _memkv_kernel(mem_ref, g_ref, wkv_ref, gk_ref, k_ref, v_ref):
    mn = _rms(mem_ref[0], g_ref[...]).astype(BF16)
    kv = _dot(mn, wkv_ref[...])
    for h in range(X_HEADS):
        sl = slice(h * X_HEAD_DIM, (h + 1) * X_HEAD_DIM)
        k_ref[0, :, sl] = _rms(kv[:, sl], gk_ref[...]).astype(BF16)
    v_ref[0] = kv[:, D_MODEL:].astype(BF16)


def _memkv(mem, g, wkv, gk):
    bsz, mlen, _ = mem.shape
    full = lambda a: pl.BlockSpec(a.shape, lambda b: (0,) * a.ndim)
    blk = pl.BlockSpec((1, mlen, D_MODEL), lambda b: (b, 0, 0))
    return pl.pallas_call(
        _memkv_kernel,
        grid=(bsz,),
        in_specs=[blk, full(g), full(wkv), full(gk)],
        out_specs=[blk, blk],
        out_shape=[jax.ShapeDtypeStruct((bsz, mlen, D_MODEL), BF16)] * 2,
        compiler_params=_cparams(1),
    )(mem, g, wkv, gk)


def _mid_kernel(x_ref, yrg_ref, ynsa_ref, woa_ref, wob_ref, gx_ref, wq_ref, gq_ref, k_ref, v_ref, wo_ref,
                gm_ref, wrh_ref, wrl_ref, br_ref, h_ref, xt_ref, rw_ref, ri_ref, cnt_ref):
    h1 = x_ref[0] + _dot(yrg_ref[0], woa_ref[...]) + _dot(ynsa_ref[0], wob_ref[...])

    q = _dot(_rms(h1, gx_ref[...]).astype(BF16), wq_ref[...])
    heads = []
    for h in range(X_HEADS):
        sl = slice(h * X_HEAD_DIM, (h + 1) * X_HEAD_DIM)
        qh = _rms(q[:, sl], gq_ref[...]).astype(BF16)
        lg = _dot_nt(qh, k_ref[0, :, sl])
        e = jnp.exp(lg - jnp.max(lg, axis=-1, keepdims=True))
        heads.append(_dot(e.astype(BF16), v_ref[0, :, sl]) / jnp.sum(e, axis=-1, keepdims=True))
    h2 = h1 + _dot(jnp.concatenate(heads, axis=-1).astype(BF16), wo_ref[...])
    h_ref[0] = h2

    xt = _rms(h2, gm_ref[...])
    _store_row_tiles(xt_ref, xt)
    xt_hi = xt.astype(BF16)
    xt_lo = (xt - xt_hi.astype(F32)).astype(BF16)
    lg = _dot(xt_hi, wrh_ref[...]) + _dot(xt_lo, wrh_ref[...]) + _dot(xt_hi, wrl_ref[...]) + br_ref[...]
    lane = lax.broadcasted_iota(jnp.int32, lg.shape, 1)
    lane_f = lane.astype(F32)
    first_of = lambda hit: jnp.min(jnp.where(hit, lane_f, 1e9), axis=-1, keepdims=True)
    glog = jnp.where(lane < N_GROUPS, lg, -3e38)
    gmax = jnp.max(glog, axis=-1, keepdims=True)
    gsel = first_of(glog == gmax)
    p_g = 1.0 / jnp.sum(jnp.exp(glog - gmax), axis=-1, keepdims=True)
    lo = N_GROUPS + EXP_PER_GROUP * gsel
    el = jnp.where((lane_f >= lo) & (lane_f < lo + EXP_PER_GROUP), lg, -3e38)
    m_a = jnp.max(el, axis=-1, keepdims=True)
    i_a = first_of(el == m_a)
    el2 = jnp.where(lane_f == i_a, -3e38, el)
    m_b = jnp.max(el2, axis=-1, keepdims=True)
    i_b = first_of(el2 == m_b)
    r = jnp.exp(m_b - m_a)
    w_a = p_g / (1.0 + r)
    w_b = p_g * r / (1.0 + r)
    e_a = i_a - N_GROUPS
    e_b = i_b - N_GROUPS
    rw_ref[0] = jnp.where(lane == 0, w_a, jnp.where(lane == 1, w_b, 0.0))
    ri_ref[0] = jnp.where(lane == 0, e_a, jnp.where(lane == 1, e_b, 0.0)).astype(jnp.int32)

    @pl.when((pl.program_id(0) == 0) & (pl.program_id(1) == 0))
    def _():
        cnt_ref[...] = jnp.zeros_like(cnt_ref)

    hot = jnp.where((lane_f == e_a) | (lane_f == e_b), 1.0, 0.0)
    cnt_ref[...] += jnp.sum(hot, axis=0, keepdims=True)


def _mid(x, yrg, ynsa, woa, wob, gx, wq, gq, kx, vx, wo, gm, wrh, wrl, br):
    bsz, seq, _ = x.shape
    tm = min(TM_MID, seq)
    mlen = kx.shape[1]
    n_i = seq // tm
    full = lambda a: pl.BlockSpec(a.shape, lambda b, i: (0,) * a.ndim)
    tok = lambda w: pl.BlockSpec((1, tm, w), lambda b, i: (b, i, 0))
    memb = pl.BlockSpec((1, mlen, D_MODEL), lambda b, i: (b, 0, 0))
    xt_spec = pl.BlockSpec((tm * ROW_TILE, LANES), lambda b, i: (b * n_i + i, 0))
    return pl.pallas_call(
        _mid_kernel,
        grid=(bsz, seq // tm),
        in_specs=[tok(D_MODEL), tok(RG_WIDTH), tok(NSA_WIDTH), full(woa), full(wob), full(gx), full(wq), full(gq),
                  memb, memb, full(wo), full(gm), full(wrh), full(wrl), full(br)],
        out_specs=[tok(D_MODEL), xt_spec, tok(LANES), tok(LANES), pl.BlockSpec((1, LANES), lambda b, i: (0, 0))],
        out_shape=[jax.ShapeDtypeStruct((bsz, seq, D_MODEL), F32),
                   jax.ShapeDtypeStruct((bsz * seq * ROW_TILE, LANES), F32),
                   jax.ShapeDtypeStruct((bsz, seq, LANES), F32), jax.ShapeDtypeStruct((bsz, seq, LANES), jnp.int32),
                   jax.ShapeDtypeStruct((1, LANES), F32)],
        compiler_params=_cparams(2),
    )(x, yrg, ynsa, woa, wob, gx, wq, gq, kx, vx, wo, gm, wrh, wrl, br)


def _dest_kernel(ri_ref, pstart_ref, dest_ref, run_ref):
    @pl.when(pl.program_id(0) == 0)
    def _():
        run_ref[...] = jnp.zeros_like(run_ref)

    ri = ri_ref[...]
    tm = ri.shape[0]
    lane = lax.broadcasted_iota(jnp.int32, ri.shape, 1)
    e_a = ri[:, 0:1]
    e_b = ri[:, 1:2]
    hot_a = lane == e_a
    hot_b = lane == e_b
    hot = jnp.where(hot_a | hot_b, 1.0, 0.0)
    row = lax.broadcasted_iota(jnp.int32, (tm, tm), 0)
    col = lax.broadcasted_iota(jnp.int32, (tm, tm), 1)
    earlier = jnp.where(col < row, 1.0, 0.0).astype(BF16)
    base = _dot(earlier, hot.astype(BF16)) + run_ref[...] + pstart_ref[...]
    d_a = jnp.sum(jnp.where(hot_a, base, 0.0), axis=-1, keepdims=True)
    d_b = jnp.sum(jnp.where(hot_b, base, 0.0), axis=-1, keepdims=True)
    dest_ref[...] = jnp.where(lane == 0, d_a, jnp.where(lane == 1, d_b, 0.0)).astype(jnp.int32)
    run_ref[...] += jnp.sum(hot, axis=0, keepdims=True)


def _dest(ri2, pstart):
    n_tok = ri2.shape[0]
    tm = min(TM_DEST, n_tok)
    return pl.pallas_call(
        _dest_kernel,
        grid=(n_tok // tm,),
        in_specs=[pl.BlockSpec((tm, LANES), lambda i: (i, 0)), pl.BlockSpec((1, LANES), lambda i: (0, 0))],
        out_specs=pl.BlockSpec((tm, LANES), lambda i: (i, 0)),
        out_shape=jax.ShapeDtypeStruct((n_tok, LANES), jnp.int32),
        scratch_shapes=[pltpu.VMEM((1, LANES), F32)],
        compiler_params=_cparams(1),
    )(ri2, pstart)


def _store_row_tiles(ref, val):
    n = val.shape[0]
    for c in range(ROW_TILE):
        ref[pl.ds(c, n, stride=ROW_TILE), :] = val[:, c * LANES:(c + 1) * LANES]


def _load_row_tiles(ref, n):
    return [ref[pl.ds(c, n, stride=ROW_TILE), :] for c in range(ROW_TILE)]


def _token_rows(ref, t):
    return ref.at[pl.ds(pl.multiple_of(t * ROW_TILE, ROW_TILE), ROW_TILE), :]


def _dispatch_kernel(da_ref, db_ref, xt_ref, buf_ref, xs_ref, sem):
    del buf_ref
    tm = da_ref.shape[2]

    def issue(t, c):
        pltpu.make_async_copy(_token_rows(xt_ref, t), _token_rows(xs_ref, da_ref[0, 0, t]), sem).start(priority=0)
        pltpu.make_async_copy(_token_rows(xt_ref, t), _token_rows(xs_ref, db_ref[0, 0, t]), sem).start(priority=1)
        return c

    lax.fori_loop(0, tm, issue, 0, unroll=DMA_UNROLL)

    def drain(t, c):
        pltpu.make_async_copy(_token_rows(xt_ref, 0), _token_rows(xs_ref, 0), sem).wait()
        pltpu.make_async_copy(_token_rows(xt_ref, 0), _token_rows(xs_ref, 0), sem).wait()
        return c

    lax.fori_loop(0, tm, drain, 0, unroll=DMA_UNROLL)


def _dispatch(da, db, xt_rows, buf):
    n_tiles, _, tm = da.shape
    smem = pl.BlockSpec((1, 1, tm), lambda i: (i, 0, 0), memory_space=pltpu.SMEM)
    hbm = pl.BlockSpec(memory_space=pl.ANY)
    return pl.pallas_call(
        _dispatch_kernel,
        grid=(n_tiles,),
        in_specs=[smem, smem, pl.BlockSpec((tm * ROW_TILE, LANES), lambda i: (i, 0)), hbm],
        out_specs=hbm,
        out_shape=jax.ShapeDtypeStruct(buf.shape, buf.dtype),
        scratch_shapes=[pltpu.SemaphoreType.DMA(())],
        input_output_aliases={3: 0},
        compiler_params=pltpu.CompilerParams(dimension_semantics=("arbitrary",), has_side_effects=True,
                                             vmem_limit_bytes=VMEM_LIMIT),
    )(da, db, xt_rows, buf)


def _ffn_kernel(bexp_ref, nused_ref, xs_ref, w1_ref, w3_ref, w2_ref, ys_ref):
    del bexp_ref
    j = pl.program_id(0)

    @pl.when(j < nused_ref[0])
    def _():
        xb = jnp.concatenate(_load_row_tiles(xs_ref, MOE_TB), axis=-1).astype(BF16)
        a = _dot(xb, w1_ref[0])
        h = a * jax.nn.sigmoid(a) * _dot(xb, w3_ref[0])
        _store_row_tiles(ys_ref, _dot(h.astype(BF16), w2_ref[0]))

    @pl.when(j >= nused_ref[0])
    def _():
        ys_ref[...] = jnp.zeros_like(ys_ref)


def _ffn(blk_exp, n_used, xs, w1, w3, w2):
    n_blocks = xs.shape[0] // (MOE_TB * ROW_TILE)
    rows = pl.BlockSpec((MOE_TB * ROW_TILE, LANES), lambda j, be, nu: (j, 0))
    grid_spec = pltpu.PrefetchScalarGridSpec(
        num_scalar_prefetch=2,
        grid=(n_blocks,),
        in_specs=[rows,
                  pl.BlockSpec((1, D_MODEL, D_EXPERT), lambda j, be, nu: (be[j], 0, 0)),
                  pl.BlockSpec((1, D_MODEL, D_EXPERT), lambda j, be, nu: (be[j], 0, 0)),
                  pl.BlockSpec((1, D_EXPERT, D_MODEL), lambda j, be, nu: (be[j], 0, 0))],
        out_specs=rows,
    )
    return pl.pallas_call(
        _ffn_kernel,
        grid_spec=grid_spec,
        out_shape=jax.ShapeDtypeStruct(xs.shape, F32),
        compiler_params=_cparams(1),
    )(blk_exp, n_used, xs, w1, w3, w2)


def _combine_kernel(da_ref, db_ref, h_ref, rw_ref, ys_ref, o_ref, ya, yb, sem):
    tm = da_ref.shape[2]

    def issue(t, c):
        pltpu.make_async_copy(_token_rows(ys_ref, da_ref[0, 0, t]), _token_rows(ya, t), sem).start(priority=0)
        pltpu.make_async_copy(_token_rows(ys_ref, db_ref[0, 0, t]), _token_rows(yb, t), sem).start(priority=1)
        return c

    lax.fori_loop(0, tm, issue, 0, unroll=DMA_UNROLL)

    def drain(t, c):
        pltpu.make_async_copy(_token_rows(ys_ref, 0), _token_rows(ya, 0), sem).wait()
        pltpu.make_async_copy(_token_rows(ys_ref, 0), _token_rows(yb, 0), sem).wait()
        return c

    lax.fori_loop(0, tm, drain, 0, unroll=DMA_UNROLL)
    rw = rw_ref[...]
    mix = [rw[:, 0:1] * a + rw[:, 1:2] * b for a, b in zip(_load_row_tiles(ya, tm), _load_row_tiles(yb, tm))]
    o_ref[...] = h_ref[...] + jnp.concatenate(mix, axis=-1)


def _combine(da, db, h2, rw, ys):
    n_tiles, _, tm = da.shape
    n_tok = h2.shape[0]
    smem = pl.BlockSpec((1, 1, tm), lambda i: (i, 0, 0), memory_space=pltpu.SMEM)
    row = lambda w: pl.BlockSpec((tm, w), lambda i: (i, 0))
    return pl.pallas_call(
        _combine_kernel,
        grid=(n_tiles,),
        in_specs=[smem, smem, row(D_MODEL), row(LANES), pl.BlockSpec(memory_space=pl.ANY)],
        out_specs=row(D_MODEL),
        out_shape=jax.ShapeDtypeStruct((n_tok, D_MODEL), F32),
        scratch_shapes=[pltpu.VMEM((tm * ROW_TILE, LANES), F32), pltpu.VMEM((tm * ROW_TILE, LANES), F32),
                        pltpu.SemaphoreType.DMA(())],
        compiler_params=_cparams(1),
    )(da, db, h2, rw, ys)


def _rel_bucket_np(dist):
    n = np.maximum(dist, 0)
    max_exact = NUM_BUCKETS // 2
    nf = np.maximum(n, 1).astype(np.float32)
    large = max_exact + (np.log(nf / max_exact) / math.log(MAX_DIST / max_exact)
                         * (NUM_BUCKETS - max_exact)).astype(np.int32)
    large = np.minimum(large, NUM_BUCKETS - 1)
    return np.where(n < max_exact, n, large).astype(np.int32)


def _toeplitz(vec, rows):
    width = vec.shape[-1] - 1
    flat = jnp.tile(vec, (1,) * (vec.ndim - 1) + (rows,))[..., :rows * width]
    return flat.reshape(vec.shape[:-1] + (rows, width))


def _bias_tables(rel_bias, seq):
    n_chunk = seq // CMP_STRIDE
    n_tiles = seq // TQ
    table = rel_bias.T.astype(F32)

    wide = NEAR + TQ
    k = np.arange(wide + 1)
    dw = np.where(k < NEAR, WINDOW - k, WINDOW + wide + 1 - k)
    used = (k < NEAR) | (k > wide + 1 - TQ)
    vals = table[:, _rel_bucket_np(dw)]

    n_var = WINDOW // TQ
    first_key = WINDOW - TQ * np.arange(n_var + 1)[:, None, None]
    in_seq = np.arange(NEAR)[None, None, :] >= first_key

    def near_tile(valid):
        t = _toeplitz(jnp.where(valid[None, :], vals, NEG_INF), TQ)[:, :, :NEAR]
        t = t.reshape(NSA_KV, NSA_HPG, TQ, NEAR).transpose(1, 0, 2, 3).reshape(1, QROWS2, NEAR)
        return jnp.where(in_seq, t, NEG_INF)

    bias_w = near_tile(used & (dw >= 0) & (dw < WINDOW))
    bias_s = near_tile(used & (dw >= 0))
    bias_far = table[:, NUM_BUCKETS - 1].reshape(NSA_KV, NSA_HPG, 1).transpose(1, 0, 2)
    bias_far = jnp.broadcast_to(bias_far, (NSA_HPG, NSA_KV, TQ)).reshape(QROWS2, 1)

    r = np.arange(CMP_STRIDE)[:, None]
    k = np.arange(2 * n_chunk + 1)[None, :]
    lag = 2 * n_chunk + 1 - k
    valid = (k > n_chunk + 1) & (CMP_STRIDE * lag + r >= CMP_L - 1)
    vals = table[:, _rel_bucket_np(CMP_STRIDE * lag + r - CMP_L // 2)]
    full = _toeplitz(jnp.where(valid[None], vals, NEG_INF), n_chunk)[..., :n_chunk]
    full = jnp.where(np.arange(n_chunk) < n_chunk - 1, full, NEG_INF)
    a4 = TQ // CMP_STRIDE
    full = full.reshape(NSA_KV, NSA_HPG, CMP_STRIDE, n_tiles, a4, n_chunk).transpose(3, 1, 0, 4, 2, 5)
    bias_c = full.reshape(n_tiles, QROWS2, n_chunk)
    return bias_c, bias_w, bias_s, bias_far


def _selection_tables(seq):
    n_chunk = seq // CMP_STRIDE
    n_blk = seq // SEL_L
    c = np.arange(n_chunk)
    n = np.arange(n_blk)
    start = c * CMP_STRIDE
    overlap_t = ((start[None, :] <= n[:, None] * SEL_L + SEL_L - 1) & (start[None, :] + CMP_L - 1 >= n[:, None] * SEL_L)
                 & (c < n_chunk - 1)[None, :])
    pos = np.arange(seq + WINDOW) - WINDOW
    lane_blk = np.arange(LANES) % n_blk
    hit = (pos[:, None] >= 0) & (pos[:, None] // SEL_L == lane_blk[None, :]) & (np.arange(LANES) < 2 * n_blk)[None, :]
    return jnp.asarray(overlap_t, BF16), jnp.asarray(np.where(hit, -UNSEL_PENALTY, 0.0), BF16)


def _block_ones(width, group):
    idx = np.arange(width) // group
    return jnp.asarray((idx[:, None] == idx[None, :]) / group, BF16)


def _block_diag(w):
    nb, n, m = w.shape
    eye = jnp.eye(nb, dtype=w.dtype)
    return jnp.einsum('hij,hg->higj', w, eye).reshape(nb * n, nb * m)


def _compress_weights(w1, w2, pos):
    half_l = CMP_L // 2
    parts = []
    for half in range(2):
        wh = w1[half * half_l * HEAD_DIM:(half + 1) * half_l * HEAD_DIM].reshape(half_l, HEAD_DIM, CMP_HIDDEN)
        z = jnp.zeros_like(wh)
        for g in range(NSA_KV):
            grp = [wh if gg == g else z for gg in range(NSA_KV)]
            parts.append(jnp.stack(grp, axis=1).reshape(half_l * KV_W, CMP_HIDDEN))
    w1cat = jnp.concatenate(parts, axis=1).astype(BF16)
    w2bd = _block_diag(jnp.stack([w2] * NSA_KV)).astype(BF16)
    prow = [jnp.tile(pos[half * half_l:(half + 1) * half_l][:, None, :], (1, NSA_KV, 1)).reshape(-1)
            for half in range(2)]
    pmat = jnp.zeros((8, half_l * KV_W), F32).at[0].set(prow[0]).at[1].set(prow[1]).astype(BF16)
    return w1cat, w2bd, pmat


def kernel(x, mem, rel_bias, norm_mix, w_in, rg_conv_w, rg_conv_b, rg_w_r, rg_b_r, rg_w_i, rg_b_i, rg_lambda, nsa_g_q, nsa_g_kc, nsa_g_ks, nsa_g_kw, cmp_pos_k, cmp_pos_v, cmp_k_w1, cmp_k_w2, cmp_v_w1, cmp_v_w2, out_g_rg, out_g_nsa, w_out, norm_x, norm_mem, xa_w_q, xa_w_kv, xa_w_o, xa_g_q, xa_g_k, norm_moe, router_g_w, router_g_b, router_e_w, router_e_b, exp_w1, exp_w3, exp_w2):
    bsz, seq, _ = x.shape
    n_tok = bsz * seq
    assert seq % FAR_TK == 0 and 2 * (seq // SEL_L) <= LANES and norm_mix.shape[0] == 1
    l = 0
    row = lambda v: v.reshape(1, -1).astype(F32)

    perm = np.array([(half * NSA_HPG + p) * HEAD_DIM + d
                     for p in range(NSA_HPG) for half in range(NSA_KV) for d in range(HEAD_DIM)])
    offs = np.cumsum([0, RG_WIDTH, RG_WIDTH, NSA_WIDTH] + [KV_W] * 6)
    w = w_in[l]
    wrg = w[:, :offs[2]].astype(BF16)
    wq = w[:, offs[2]:offs[3]][:, perm].astype(BF16)
    wkv = w[:, offs[3]:offs[9]].astype(BF16)
    wgl = jnp.pad(w[:, offs[9]:], ((0, 0), (0, LANES - 3 * NSA_HEADS))).astype(BF16)
    ones64 = _block_ones(NSA_WIDTH, HEAD_DIM)
    gq = row(jnp.tile(nsa_g_q[l], NSA_HEADS) * HEAD_DIM ** -0.5)
    u, gate, q, kc, vc, ks, vs, kw, vw, gates = _inproj(
        x.reshape(n_tok, D_MODEL), row(norm_mix[l]), wrg, wq, wkv, wgl, gq,
        row(jnp.tile(nsa_g_ks[l], NSA_KV)), row(jnp.tile(nsa_g_kw[l], NSA_KV)), ones64)

    wg = jnp.concatenate([_block_diag(rg_w_r[l]), _block_diag(rg_w_i[l])], axis=1).astype(BF16)
    bg = jnp.concatenate([rg_b_r[l], rg_b_i[l]]).reshape(1, -1)
    y_rg = _rglru(u.reshape(bsz, seq, RG_WIDTH), gate.reshape(bsz, seq, RG_WIDTH),
                  rg_conv_w[l].reshape(CONV_W, RG_WIDTH), row(rg_conv_b[l]), wg, bg, row(rg_lambda[l]),
                  row(out_g_rg[l]))

    n_chunk = seq // CMP_STRIDE
    w1k, w2k, pk = _compress_weights(cmp_k_w1[l], cmp_k_w2[l], cmp_pos_k[l])
    w1v, w2v, pv = _compress_weights(cmp_v_w1[l], cmp_v_w2[l], cmp_pos_v[l])
    kcmp, vcmp = _compress(kc.reshape(bsz, n_chunk, CMP_STRIDE * KV_W), vc.reshape(bsz, n_chunk, CMP_STRIDE * KV_W),
                           w1k, w2k, pk, w1v, w2v, pv, row(jnp.tile(nsa_g_kc[l], NSA_KV)),
                           ones64[:KV_W, :KV_W])
    padw = lambda t: jnp.pad(t.reshape(bsz, seq, KV_W), ((0, 0), (WINDOW, 0), (0, 0)))
    bias_c, bias_w, bias_s, bias_far = _bias_tables(rel_bias, seq)
    overlap_t, penalty = _selection_tables(seq)
    ksx = jnp.concatenate([padw(ks), jnp.broadcast_to(penalty, (bsz,) + penalty.shape)], axis=-1)
    y_nsa = _nsa(q.reshape(bsz, seq, NSA_WIDTH), gates.reshape(bsz, seq, LANES), kcmp, vcmp,
                 ksx, padw(vs), padw(kw), padw(vw), overlap_t, bias_c, bias_w, bias_s, bias_far,
                 row(out_g_nsa[l][perm]))

    kx, vx = _memkv(mem, row(norm_mem[l]), xa_w_kv[l].astype(BF16), row(xa_g_k[l]))
    wo_mix = w_out[l]
    wr = jnp.pad(jnp.concatenate([router_g_w[l], router_e_w[l]], axis=1),
                 ((0, 0), (0, LANES - N_GROUPS - N_EXPERTS)))
    wr_hi = wr.astype(BF16)
    br = jnp.pad(jnp.concatenate([router_g_b[l], router_e_b[l]]), (0, LANES - N_GROUPS - N_EXPERTS)).reshape(1, -1)
    h2, xt, rw, ri, counts = _mid(
        x, y_rg, y_nsa, wo_mix[:RG_WIDTH].astype(BF16), wo_mix[RG_WIDTH:][perm].astype(BF16), row(norm_x[l]),
        xa_w_q[l].astype(BF16), row(xa_g_q[l] * X_HEAD_DIM ** -0.5), kx, vx, xa_w_o[l].astype(BF16),
        row(norm_moe[l]), wr_hi, (wr - wr_hi.astype(F32)).astype(BF16), br)

    n_slots = 2 * n_tok
    n_blocks = n_slots // MOE_TB + N_EXPERTS
    n_pad = n_blocks * MOE_TB
    cnt = counts[0, :N_EXPERTS].astype(jnp.int32)
    pcnt = (cnt + MOE_TB - 1) // MOE_TB * MOE_TB
    pends = jnp.cumsum(pcnt)
    pstart = jnp.pad((pends - pcnt).astype(F32), (0, LANES - N_EXPERTS)).reshape(1, LANES)
    blk_exp = jnp.minimum(jnp.sum(pends[None, :] <= jnp.arange(n_blocks, dtype=jnp.int32)[:, None] * MOE_TB, axis=1),
                          N_EXPERTS - 1).astype(jnp.int32)
    n_used = (pends[-1:] // MOE_TB).astype(jnp.int32)
    dest = _dest(ri.reshape(n_tok, LANES), pstart)
    tmd = min(TM_DMA, n_tok)
    da = dest[:, 0].reshape(n_tok // tmd, 1, tmd)
    db = dest[:, 1].reshape(n_tok // tmd, 1, tmd)
    xs = _dispatch(da, db, xt, jnp.zeros((n_pad * ROW_TILE, LANES), F32))
    ys = _ffn(blk_exp, n_used, xs, exp_w1[l].astype(BF16), exp_w3[l].astype(BF16), exp_w2[l].astype(BF16))
    out = _combine(da, db, h2.reshape(n_tok, D_MODEL), rw.reshape(n_tok, LANES), ys)
    return out.reshape(bsz, seq, D_MODEL)
```

```python
import functools
import math

import numpy as np
import jax
import jax.numpy as jnp
from jax import lax
from jax.experimental import pallas as pl
from jax.experimental.pallas import tpu as pltpu

F32 = jnp.float32
BF16 = jnp.bfloat16

D_MODEL = 1024
RG_WIDTH = 512
RG_BLOCKS = 8
RG_BLOCK = 64
CONV_W = 4
RG_C = 8.0
NSA_WIDTH = 512
NSA_HEADS = 8
HEAD_DIM = 64
NSA_KV = 2
NSA_HPG = 4
KV_W = 128
CMP_L = 32
CMP_STRIDE = 16
CMP_HIDDEN = 256
SEL_L = 64
N_SEL = 8
WINDOW = 512
NUM_BUCKETS = 32
MAX_DIST = 128
X_HEADS = 4
X_HEAD_DIM = 256
N_GROUPS = 4
EXP_PER_GROUP = 8
N_EXPERTS = 32
D_EXPERT = 512
EPS = 1e-6
NEG_INF = -1e30
MASKED_BELOW = -1e29
SEL_FORCE = 1e9
LANES = 128

TQ = 64
NEAR = WINDOW + TQ
FAR_TK = 512
NEARP = 640
SM_ROWS = 64
QROWS2 = NSA_HEADS * TQ
UNSEL_PENALTY = 2.0 ** 100

TM_PROJ = 512
TM_MID = 512
TM_DEST = 512
TM_DMA = 512
DMA_UNROLL = 8
MOE_TB = 512
ROW_TILE = D_MODEL // LANES
RG_CHUNK = 256
VMEM_LIMIT = 56 * 1024 * 1024


def _cparams(n_axes):
    return pltpu.CompilerParams(dimension_semantics=("arbitrary",) * n_axes,
                                vmem_limit_bytes=VMEM_LIMIT)


def _dot(a, b):
    return jnp.dot(a, b, preferred_element_type=F32)


def _dot_nt(a, b):
    return lax.dot_general(a, b, (((1,), (1,)), ((), ())), preferred_element_type=F32)


def _gelu_tanh(x):
    return 0.5 * x * (1.0 + jnp.tanh(math.sqrt(2.0 / math.pi) * (x + 0.044715 * (x * x * x))))


def _rms(x, g):
    return x * lax.rsqrt(jnp.mean(x * x, axis=-1, keepdims=True) + EPS) * g


def _group_rms(x, ones_blk, g):
    ms = _dot((x * x).astype(BF16), ones_blk)
    return x * lax.rsqrt(ms + EPS) * g


def _inproj_kernel(x_ref, g_ref, wrg_ref, wq_ref, wkv_ref, wgl_ref, gq_ref, gks_ref, gkw_ref, ones_ref,
                   u_ref, gate_ref, q_ref, kc_ref, vc_ref, ks_ref, vs_ref, kw_ref, vw_ref, gates_ref):
    xb = _rms(x_ref[...], g_ref[...]).astype(BF16)
    rg = _dot(xb, wrg_ref[...])
    u_ref[...] = rg[:, :RG_WIDTH].astype(BF16)
    gate_ref[...] = rg[:, RG_WIDTH:].astype(BF16)
    q = _dot(xb, wq_ref[...])
    q_ref[...] = _group_rms(q, ones_ref[...], gq_ref[...]).astype(BF16)
    kv = _dot(xb, wkv_ref[...])
    ones_kv = ones_ref[:KV_W, :KV_W]
    kc_ref[...] = kv[:, 0 * KV_W:1 * KV_W].astype(BF16)
    vc_ref[...] = kv[:, 1 * KV_W:2 * KV_W].astype(BF16)
    ks_ref[...] = _group_rms(kv[:, 2 * KV_W:3 * KV_W], ones_kv, gks_ref[...]).astype(BF16)
    vs_ref[...] = kv[:, 3 * KV_W:4 * KV_W].astype(BF16)
    kw_ref[...] = _group_rms(kv[:, 4 * KV_W:5 * KV_W], ones_kv, gkw_ref[...]).astype(BF16)
    vw_ref[...] = kv[:, 5 * KV_W:6 * KV_W].astype(BF16)
    gates_ref[...] = jax.nn.sigmoid(_dot(xb, wgl_ref[...]))


def _inproj(x2, g, wrg, wq, wkv, wgl, gq, gks, gkw, ones_blk):
    n_tok = x2.shape[0]
    tm = min(TM_PROJ, n_tok)
    full = lambda a: pl.BlockSpec(a.shape, lambda i: (0,) * a.ndim)
    row = lambda w: pl.BlockSpec((tm, w), lambda i: (i, 0))
    outs = [(RG_WIDTH, BF16), (RG_WIDTH, BF16), (NSA_WIDTH, BF16)] + [(KV_W, BF16)] * 6 + [(LANES, F32)]
    return pl.pallas_call(
        _inproj_kernel,
        grid=(n_tok // tm,),
        in_specs=[row(D_MODEL)] + [full(a) for a in (g, wrg, wq, wkv, wgl, gq, gks, gkw, ones_blk)],
        out_specs=[row(w) for w, _ in outs],
        out_shape=[jax.ShapeDtypeStruct((n_tok, w), dt) for w, dt in outs],
        compiler_params=_cparams(1),
    )(x2, g, wrg, wq, wkv, wgl, gq, gks, gkw, ones_blk)


def _rglru_kernel(u_ref, gate_ref, cw_ref, cb_ref, wg_ref, bg_ref, lam_ref, og_ref, y_ref, upad, a_s, h_s):
    seq = u_ref.shape[1]
    upad[0:8, :] = jnp.zeros((8, RG_WIDTH), F32)
    upad[8:8 + seq, :] = u_ref[0].astype(F32)
    neg_lam = -lam_ref[...]
    softplus = jnp.maximum(neg_lam, 0.0) + jnp.log(1.0 + jnp.exp(-jnp.abs(neg_lam)))
    ch = min(RG_CHUNK, seq)
    for c in range(seq // ch):
        r0 = c * ch
        uc = cb_ref[...]
        for k in range(CONV_W):
            off = 8 + r0 - (CONV_W - 1) + k
            uc = uc + cw_ref[k:k + 1, :] * upad[off:off + ch, :]
        gt = _dot(uc.astype(BF16), wg_ref[...]) + bg_ref[...]
        r = jax.nn.sigmoid(gt[:, :RG_WIDTH])
        ig = jax.nn.sigmoid(gt[:, RG_WIDTH:])
        log_a = (-RG_C) * r * softplus
        a = jnp.exp(log_a)
        a_s[r0:r0 + ch, :] = a
        h_s[r0:r0 + ch, :] = jnp.sqrt(1.0 - a * a) * ig * uc

    def step(t, h):
        h = a_s[pl.ds(t, 1), :] * h + h_s[pl.ds(t, 1), :]
        h_s[pl.ds(t, 1), :] = h
        return h

    lax.fori_loop(0, seq, step, jnp.zeros((1, RG_WIDTH), F32), unroll=8)

    for c in range(seq // ch):
        r0 = c * ch
        y = _gelu_tanh(gate_ref[0, r0:r0 + ch, :].astype(F32)) * h_s[r0:r0 + ch, :]
        y_ref[0, r0:r0 + ch, :] = _rms(y, og_ref[...]).astype(BF16)


def _rglru(u3, gate3, cw, cb, wg, bg, lam, og):
    bsz, seq, _ = u3.shape
    full = lambda a: pl.BlockSpec(a.shape, lambda b: (0,) * a.ndim)
    blk = pl.BlockSpec((1, seq, RG_WIDTH), lambda b: (b, 0, 0))
    return pl.pallas_call(
        _rglru_kernel,
        grid=(bsz,),
        in_specs=[blk, blk] + [full(a) for a in (cw, cb, wg, bg, lam, og)],
        out_specs=blk,
        out_shape=jax.ShapeDtypeStruct((bsz, seq, RG_WIDTH), BF16),
        scratch_shapes=[pltpu.VMEM((seq + 8, RG_WIDTH), F32), pltpu.VMEM((seq, RG_WIDTH), F32),
                        pltpu.VMEM((seq, RG_WIDTH), F32)],
        compiler_params=_cparams(1),
    )(u3, gate3, cw, cb, wg, bg, lam, og)


def _compress_kernel(kx_ref, vx_ref, w1k_ref, w2k_ref, pk_ref, w1v_ref, w2v_ref, pv_ref, gk_ref, ones_ref,
                     ko_ref, vo_ref):
    n_chunk = kx_ref.shape[1]
    half = NSA_KV * CMP_HIDDEN

    def mlp(x_ref, w1_ref, w2_ref, p_ref):
        ab = _dot(x_ref[0], w1_ref[...])
        pos = _dot(p_ref[...], w1_ref[...])
        hid = ab[:, :half] + pltpu.roll(ab[:, half:], n_chunk - 1, 0) + (pos[0:1, :half] + pos[1:2, half:])
        return _dot(_gelu_tanh(hid).astype(BF16), w2_ref[...])

    kc = mlp(kx_ref, w1k_ref, w2k_ref, pk_ref)
    ko_ref[0] = _group_rms(kc, ones_ref[...], gk_ref[...]).astype(BF16)
    vo_ref[0] = mlp(vx_ref, w1v_ref, w2v_ref, pv_ref).astype(BF16)


def _compress(kx, vx, w1k, w2k, pk, w1v, w2v, pv, gk, ones_kv):
    bsz, n_chunk, width = kx.shape
    full = lambda a: pl.BlockSpec(a.shape, lambda b: (0,) * a.ndim)
    xin = pl.BlockSpec((1, n_chunk, width), lambda b: (b, 0, 0))
    out = pl.BlockSpec((1, n_chunk, KV_W), lambda b: (b, 0, 0))
    return pl.pallas_call(
        _compress_kernel,
        grid=(bsz,),
        in_specs=[xin, xin] + [full(a) for a in (w1k, w2k, pk, w1v, w2v, pv, gk, ones_kv)],
        out_specs=[out, out],
        out_shape=[jax.ShapeDtypeStruct((bsz, n_chunk, KV_W), BF16)] * 2,
        compiler_params=_cparams(1),
    )(kx, vx, w1k, w2k, pk, w1v, w2v, pv, gk, ones_kv)


def _nsa_kernel(q_ref, gates_ref, kcmp_ref, vcmp_ref, ksx_ref, vsp_ref, kwp_ref, vwp_ref, ovt_ref,
                bc_ref, bw_ref, bs_ref, bf_ref, og_ref, y_ref,
                sw_ref, pw_ref, ss_ref, ps_ref, qxn_ref, qxf_ref, part_ref, mb_ref, mrun_ref, lrun_ref, acc_ref):
    i = pl.program_id(1)
    t0 = pl.multiple_of(i * TQ, TQ)
    n_blk = ovt_ref.shape[0]
    lane = lax.broadcasted_iota(jnp.int32, (TQ, LANES), 1)
    lo_half = lane < HEAD_DIM
    pieces = []
    for p in range(NSA_HPG):
        qs = q_ref[0, :, p * LANES:(p + 1) * LANES]
        zero = jnp.zeros_like(qs)
        pieces += [jnp.where(lo_half, qs, zero), jnp.where(lo_half, zero, qs)]
    q8 = jnp.concatenate(pieces, axis=0)

    bc = bc_ref[0]
    lc = _dot_nt(q8, kcmp_ref[0]) + bc
    ec = jnp.where(bc > MASKED_BELOW, jnp.exp(lc - jnp.max(lc, axis=-1, keepdims=True)), 0.0)
    sc = jnp.sum(ec, axis=-1, keepdims=True)
    pc = ec / jnp.where(sc > 0.0, sc, 1.0)
    o_c = _dot(pc.astype(BF16), vcmp_ref[0])

    blocks = [pc[r * TQ:(r + 1) * TQ] for r in range(NSA_HPG * NSA_KV)]
    pcs = jnp.concatenate([sum(blocks[g::NSA_KV]) for g in range(NSA_KV)], axis=0)
    pcs_hi = pcs.astype(BF16)
    pcs_lo = (pcs - pcs_hi.astype(F32)).astype(BF16)
    imp = _dot_nt(ovt_ref[...], pcs_hi) + _dot_nt(ovt_ref[...], pcs_lo)
    blk = lax.broadcasted_iota(jnp.int32, imp.shape, 0)
    forced = (blk == 0) | (blk == i) | (blk == i - 1)
    score = jnp.where(forced, SEL_FORCE, jnp.where(blk > i, -3e38, imp))
    rank = jnp.zeros(imp.shape, F32)
    for m in range(n_blk):
        row = score[m:m + 1, :]
        rank = rank + jnp.where(blk > m, jnp.where(row >= score, 1.0, 0.0), jnp.where(row > score, 1.0, 0.0))
    unsel = jnp.where(rank < N_SEL, 0.0, 1.0)
    unsel_far = jnp.where(blk >= i - WINDOW // SEL_L, 1.0, unsel)
    pad = jnp.zeros((LANES - 2 * n_blk, imp.shape[1]), F32)
    u_t = jnp.concatenate([unsel, unsel_far, pad], axis=0).T
    u_lane = lax.broadcasted_iota(jnp.int32, u_t.shape, 1)
    u_near = jnp.where(u_lane < n_blk, u_t, 0.0).astype(BF16)
    u_far = jnp.where(u_lane >= n_blk, u_t, 0.0).astype(BF16)
    qxn_ref[...] = jnp.concatenate([q8, jnp.concatenate([u_near] * NSA_HPG, axis=0)], axis=1)
    qxf_ref[...] = jnp.concatenate([q8, jnp.concatenate([u_far] * NSA_HPG, axis=0)], axis=1)

    def softmax_tiles(s_ref, p_ref, n_lt, bias_tile, m_floor_ref=None):
        def rows_of(r):
            return pl.ds(pl.multiple_of(r * SM_ROWS, SM_ROWS), SM_ROWS)

        def pass_max(r, c):
            rows = rows_of(r)
            part = None
            for t in range(n_lt):
                cols = slice(t * LANES, (t + 1) * LANES)
                l = s_ref[rows, cols] + bias_tile(rows, cols)
                s_ref[rows, cols] = l
                part = l if part is None else jnp.maximum(part, l)
            part_ref[rows, :] = part
            return c

        lax.fori_loop(0, QROWS2 // SM_ROWS, pass_max, 0)
        m = jnp.broadcast_to(jnp.max(part_ref[...], axis=-1, keepdims=True), (QROWS2, LANES))
        if m_floor_ref is not None:
            m = jnp.maximum(m, m_floor_ref[...])
        mb_ref[...] = m

        def pass_exp(r, c):
            rows = rows_of(r)
            m_rows = mb_ref[rows, :]
            part = None
            for t in range(n_lt):
                cols = slice(t * LANES, (t + 1) * LANES)
                e = jnp.exp(s_ref[rows, cols] - m_rows)
                p_ref[rows, cols] = e.astype(BF16)
                part = e if part is None else part + e
            part_ref[rows, :] = part
            return c

        lax.fori_loop(0, QROWS2 // SM_ROWS, pass_exp, 0)
        return m, jnp.broadcast_to(jnp.sum(part_ref[...], axis=-1, keepdims=True), (QROWS2, LANES))

    near_lt = NEARP // LANES
    far_lt = FAR_TK // LANES

    sw_ref[...] = _dot_nt(qxn_ref[:, :LANES], kwp_ref[0, pl.ds(t0, NEARP), :])
    _, l_w = softmax_tiles(sw_ref, pw_ref, near_lt, lambda rows, cols: bw_ref[0, rows, cols])
    o_w = _dot(pw_ref[...], vwp_ref[0, pl.ds(t0, NEARP), :]) / l_w

    ss_ref[...] = _dot_nt(qxn_ref[...], ksx_ref[0, pl.ds(t0, NEARP), :])
    m_near, l_near = softmax_tiles(ss_ref, ps_ref, near_lt, lambda rows, cols: bs_ref[0, rows, cols])
    mrun_ref[...] = m_near
    lrun_ref[...] = l_near
    acc_ref[...] = _dot(ps_ref[...], vsp_ref[0, pl.ds(t0, NEARP), :])

    def far_step(kf, c):
        base = pl.multiple_of(WINDOW + kf * FAR_TK, FAR_TK)
        ss_ref[:, :FAR_TK] = _dot_nt(qxf_ref[...], ksx_ref[0, pl.ds(base, FAR_TK), :])
        m_new, l_tile = softmax_tiles(ss_ref, ps_ref, far_lt, lambda rows, cols: bf_ref[rows, :],
                                      m_floor_ref=mrun_ref)
        alpha = jnp.exp(mrun_ref[...] - m_new)
        mrun_ref[...] = m_new
        lrun_ref[...] = alpha * lrun_ref[...] + l_tile
        acc_ref[...] = alpha * acc_ref[...] + _dot(ps_ref[:, :FAR_TK], vsp_ref[0, pl.ds(base, FAR_TK), :])
        return c

    n_far = (jnp.maximum(t0 - WINDOW, 0) + FAR_TK - 1) // FAR_TK
    lax.fori_loop(0, n_far, far_step, 0)
    o_s = acc_ref[...] / lrun_ref[...]

    gates = gates_ref[0]

    def gate_col(j):
        cols = [gates[:, (g * NSA_HPG + p) * 3 + j:(g * NSA_HPG + p) * 3 + j + 1]
                for p in range(NSA_HPG) for g in range(NSA_KV)]
        return jnp.concatenate(cols, axis=0)

    out = gate_col(0) * o_c + gate_col(1) * o_s + gate_col(2) * o_w
    slabs = [jnp.where(lo_half, out[(2 * p) * TQ:(2 * p + 1) * TQ], out[(2 * p + 1) * TQ:(2 * p + 2) * TQ])
             for p in range(NSA_HPG)]
    y_ref[0] = _rms(jnp.concatenate(slabs, axis=-1), og_ref[...]).astype(BF16)


def _nsa(q3, gates3, kcmp, vcmp, ksx, vsp, kwp, vwp, ovt, bias_c, bias_w, bias_s, bias_far, og):
    bsz, seq, _ = q3.shape
    n_chunk = kcmp.shape[1]
    n_var = bias_w.shape[0] - 1
    full = lambda a: pl.BlockSpec(a.shape, lambda b, i: (0,) * a.ndim)
    per_b = lambda a: pl.BlockSpec((1,) + a.shape[1:], lambda b, i: (b,) + (0,) * (a.ndim - 1))
    near = pl.BlockSpec((1, QROWS2, NEARP), lambda b, i: (jnp.minimum(i, n_var), 0, 0))
    stat = pltpu.VMEM((QROWS2, LANES), F32)
    return pl.pallas_call(
        _nsa_kernel,
        grid=(bsz, seq // TQ),
        in_specs=[pl.BlockSpec((1, TQ, NSA_WIDTH), lambda b, i: (b, i, 0)),
                  pl.BlockSpec((1, TQ, LANES), lambda b, i: (b, i, 0)),
                  per_b(kcmp), per_b(vcmp), per_b(ksx), per_b(vsp), per_b(kwp), per_b(vwp),
                  full(ovt),
                  pl.BlockSpec((1, QROWS2, n_chunk), lambda b, i: (i, 0, 0)),
                  near, near, full(bias_far), full(og)],
        out_specs=pl.BlockSpec((1, TQ, NSA_WIDTH), lambda b, i: (b, i, 0)),
        out_shape=jax.ShapeDtypeStruct((bsz, seq, NSA_WIDTH), BF16),
        scratch_shapes=[pltpu.VMEM((QROWS2, NEARP), F32), pltpu.VMEM((QROWS2, NEARP), BF16),
                        pltpu.VMEM((QROWS2, NEARP), F32), pltpu.VMEM((QROWS2, NEARP), BF16),
                        pltpu.VMEM((QROWS2, 2 * LANES), BF16), pltpu.VMEM((QROWS2, 2 * LANES), BF16),
                        stat, stat, stat, stat, stat],
        compiler_params=_cparams(2),
    )(q3, gates3, kcmp, vcmp, ksx, vsp, kwp, vwp, ovt, bias_c, bias_w, bias_s, bias_far, og)


def _memkv_kernel(mem_ref, g_ref, wkv_ref, gk_ref, k_ref, v_ref):
    mn = _rms(mem_ref[0], g_ref[...]).astype(BF16)
    kv = _dot(mn, wkv_ref[...])
    for h in range(X_HEADS):
        sl = slice(h * X_HEAD_DIM, (h + 1) * X_HEAD_DIM)
        k_ref[0, :, sl] = _rms(kv[:, sl], gk_ref[...]).astype(BF16)
    v_ref[0] = kv[:, D_MODEL:].astype(BF16)


def _memkv(mem, g, wkv, gk):
    bsz, mlen, _ = mem.shape
    full = lambda a: pl.BlockSpec(a.shape, lambda b: (0,) * a.ndim)
    blk = pl.BlockSpec((1, mlen, D_MODEL), lambda b: (b, 0, 0))
    return pl.pallas_call(
        _memkv_kernel,
        grid=(bsz,),
        in_specs=[blk, full(g), full(wkv), full(gk)],
        out_specs=[blk, blk],
        out_shape=[jax.ShapeDtypeStruct((bsz, mlen, D_MODEL), BF16)] * 2,
        compiler_params=_cparams(1),
    )(mem, g, wkv, gk)


def _mid_kernel(x_ref, yrg_ref, ynsa_ref, woa_ref, wob_ref, gx_ref, wq_ref, gq_ref, k_ref, v_ref, wo_ref,
                gm_ref, wrh_ref, wrl_ref, br_ref, h_ref, xt_ref, rw_ref, ri_ref, cnt_ref):
    h1 = x_ref[0] + _dot(yrg_ref[0], woa_ref[...]) + _dot(ynsa_ref[0], wob_ref[...])

    q = _dot(_rms(h1, gx_ref[...]).astype(BF16), wq_ref[...])
    heads = []
    for h in range(X_HEADS):
        sl = slice(h * X_HEAD_DIM, (h + 1) * X_HEAD_DIM)
        qh = _rms(q[:, sl], gq_ref[...]).astype(BF16)
        lg = _dot_nt(qh, k_ref[0, :, sl])
        e = jnp.exp(lg - jnp.max(lg, axis=-1, keepdims=True))
        heads.append(_dot(e.astype(BF16), v_ref[0, :, sl]) / jnp.sum(e, axis=-1, keepdims=True))
    h2 = h1 + _dot(jnp.concatenate(heads, axis=-1).astype(BF16), wo_ref[...])
    h_ref[0] = h2

    xt = _rms(h2, gm_ref[...])
    _store_row_tiles(xt_ref, xt)
    xt_hi = xt.astype(BF16)
    xt_lo = (xt - xt_hi.astype(F32)).astype(BF16)
    lg = _dot(xt_hi, wrh_ref[...]) + _dot(xt_lo, wrh_ref[...]) + _dot(xt_hi, wrl_ref[...]) + br_ref[...]
    lane = lax.broadcasted_iota(jnp.int32, lg.shape, 1)
    lane_f = lane.astype(F32)
    first_of = lambda hit: jnp.min(jnp.where(hit, lane_f, 1e9), axis=-1, keepdims=True)
    glog = jnp.where(lane < N_GROUPS, lg, -3e38)
    gmax = jnp.max(glog, axis=-1, keepdims=True)
    gsel = first_of(glog == gmax)
    p_g = 1.0 / jnp.sum(jnp.exp(glog - gmax), axis=-1, keepdims=True)
    lo = N_GROUPS + EXP_PER_GROUP * gsel
    el = jnp.where((lane_f >= lo) & (lane_f < lo + EXP_PER_GROUP), lg, -3e38)
    m_a = jnp.max(el, axis=-1, keepdims=True)
    i_a = first_of(el == m_a)
    el2 = jnp.where(lane_f == i_a, -3e38, el)
    m_b = jnp.max(el2, axis=-1, keepdims=True)
    i_b = first_of(el2 == m_b)
    r = jnp.exp(m_b - m_a)
    w_a = p_g / (1.0 + r)
    w_b = p_g * r / (1.0 + r)
    e_a = i_a - N_GROUPS
    e_b = i_b - N_GROUPS
    rw_ref[0] = jnp.where(lane == 0, w_a, jnp.where(lane == 1, w_b, 0.0))
    ri_ref[0] = jnp.where(lane == 0, e_a, jnp.where(lane == 1, e_b, 0.0)).astype(jnp.int32)

    @pl.when((pl.program_id(0) == 0) & (pl.program_id(1) == 0))
    def _():
        cnt_ref[...] = jnp.zeros_like(cnt_ref)

    hot = jnp.where((lane_f == e_a) | (lane_f == e_b), 1.0, 0.0)
    cnt_ref[...] += jnp.sum(hot, axis=0, keepdims=True)


def _mid(x, yrg, ynsa, woa, wob, gx, wq, gq, kx, vx, wo, gm, wrh, wrl, br):
    bsz, seq, _ = x.shape
    tm = min(TM_MID, seq)
    mlen = kx.shape[1]
    n_i = seq // tm
    full = lambda a: pl.BlockSpec(a.shape, lambda b, i: (0,) * a.ndim)
    tok = lambda w: pl.BlockSpec((1, tm, w), lambda b, i: (b, i, 0))
    memb = pl.BlockSpec((1, mlen, D_MODEL), lambda b, i: (b, 0, 0))
    xt_spec = pl.BlockSpec((tm * ROW_TILE, LANES), lambda b, i: (b * n_i + i, 0))
    return pl.pallas_call(
        _mid_kernel,
        grid=(bsz, seq // tm),
        in_specs=[tok(D_MODEL), tok(RG_WIDTH), tok(NSA_WIDTH), full(woa), full(wob), full(gx), full(wq), full(gq),
                  memb, memb, full(wo), full(gm), full(wrh), full(wrl), full(br)],
        out_specs=[tok(D_MODEL), xt_spec, tok(LANES), tok(LANES), pl.BlockSpec((1, LANES), lambda b, i: (0, 0))],
        out_shape=[jax.ShapeDtypeStruct((bsz, seq, D_MODEL), F32),
                   jax.ShapeDtypeStruct((bsz * seq * ROW_TILE, LANES), F32),
                   jax.ShapeDtypeStruct((bsz, seq, LANES), F32), jax.ShapeDtypeStruct((bsz, seq, LANES), jnp.int32),
                   jax.ShapeDtypeStruct((1, LANES), F32)],
        compiler_params=_cparams(2),
    )(x, yrg, ynsa, woa, wob, gx, wq, gq, kx, vx, wo, gm, wrh, wrl, br)


def _dest_kernel(ri_ref, pstart_ref, dest_ref, run_ref):
    @pl.when(pl.program_id(0) == 0)
    def _():
        run_ref[...] = jnp.zeros_like(run_ref)

    ri = ri_ref[...]
    tm = ri.shape[0]
    lane = lax.broadcasted_iota(jnp.int32, ri.shape, 1)
    e_a = ri[:, 0:1]
    e_b = ri[:, 1:2]
    hot_a = lane == e_a
    hot_b = lane == e_b
    hot = jnp.where(hot_a | hot_b, 1.0, 0.0)
    row = lax.broadcasted_iota(jnp.int32, (tm, tm), 0)
    col = lax.broadcasted_iota(jnp.int32, (tm, tm), 1)
    earlier = jnp.where(col < row, 1.0, 0.0).astype(BF16)
    base = _dot(earlier, hot.astype(BF16)) + run_ref[...] + pstart_ref[...]
    d_a = jnp.sum(jnp.where(hot_a, base, 0.0), axis=-1, keepdims=True)
    d_b = jnp.sum(jnp.where(hot_b, base, 0.0), axis=-1, keepdims=True)
    dest_ref[...] = jnp.where(lane == 0, d_a, jnp.where(lane == 1, d_b, 0.0)).astype(jnp.int32)
    run_ref[...] += jnp.sum(hot, axis=0, keepdims=True)


def _dest(ri2, pstart):
    n_tok = ri2.shape[0]
    tm = min(TM_DEST, n_tok)
    return pl.pallas_call(
        _dest_kernel,
        grid=(n_tok // tm,),
        in_specs=[pl.BlockSpec((tm, LANES), lambda i: (i, 0)), pl.BlockSpec((1, LANES), lambda i: (0, 0))],
        out_specs=pl.BlockSpec((tm, LANES), lambda i: (i, 0)),
        out_shape=jax.ShapeDtypeStruct((n_tok, LANES), jnp.int32),
        scratch_shapes=[pltpu.VMEM((1, LANES), F32)],
        compiler_params=_cparams(1),
    )(ri2, pstart)


def _store_row_tiles(ref, val):
    n = val.shape[0]
    for c in range(ROW_TILE):
        ref[pl.ds(c, n, stride=ROW_TILE), :] = val[:, c * LANES:(c + 1) * LANES]


def _load_row_tiles(ref, n):
    return [ref[pl.ds(c, n, stride=ROW_TILE), :] for c in range(ROW_TILE)]


def _token_rows(ref, t):
    return ref.at[pl.ds(pl.multiple_of(t * ROW_TILE, ROW_TILE), ROW_TILE), :]


def _dispatch_kernel(da_ref, db_ref, xt_ref, buf_ref, xs_ref, sem):
    del buf_ref
    tm = da_ref.shape[2]

    def issue(t, c):
        pltpu.make_async_copy(_token_rows(xt_ref, t), _token_rows(xs_ref, da_ref[0, 0, t]), sem).start(priority=0)
        pltpu.make_async_copy(_token_rows(xt_ref, t), _token_rows(xs_ref, db_ref[0, 0, t]), sem).start(priority=1)
        return c

    lax.fori_loop(0, tm, issue, 0, unroll=DMA_UNROLL)

    def drain(t, c):
        pltpu.make_async_copy(_token_rows(xt_ref, 0), _token_rows(xs_ref, 0), sem).wait()
        pltpu.make_async_copy(_token_rows(xt_ref, 0), _token_rows(xs_ref, 0), sem).wait()
        return c

    lax.fori_loop(0, tm, drain, 0, unroll=DMA_UNROLL)


def _dispatch(da, db, xt_rows, buf):
    n_tiles, _, tm = da.shape
    smem = pl.BlockSpec((1, 1, tm), lambda i: (i, 0, 0), memory_space=pltpu.SMEM)
    hbm = pl.BlockSpec(memory_space=pl.ANY)
    return pl.pallas_call(
        _dispatch_kernel,
        grid=(n_tiles,),
        in_specs=[smem, smem, pl.BlockSpec((tm * ROW_TILE, LANES), lambda i: (i, 0)), hbm],
        out_specs=hbm,
        out_shape=jax.ShapeDtypeStruct(buf.shape, buf.dtype),
        scratch_shapes=[pltpu.SemaphoreType.DMA(())],
        input_output_aliases={3: 0},
        compiler_params=pltpu.CompilerParams(dimension_semantics=("arbitrary",), has_side_effects=True,
                                             vmem_limit_bytes=VMEM_LIMIT),
    )(da, db, xt_rows, buf)


def _ffn_kernel(bexp_ref, nused_ref, xs_ref, w1_ref, w3_ref, w2_ref, ys_ref):
    del bexp_ref
    j = pl.program_id(0)

    @pl.when(j < nused_ref[0])
    def _():
        xb = jnp.concatenate(_load_row_tiles(xs_ref, MOE_TB), axis=-1).astype(BF16)
        a = _dot(xb, w1_ref[0])
        h = a * jax.nn.sigmoid(a) * _dot(xb, w3_ref[0])
        _store_row_tiles(ys_ref, _dot(h.astype(BF16), w2_ref[0]))

    @pl.when(j >= nused_ref[0])
    def _():
        ys_ref[...] = jnp.zeros_like(ys_ref)


def _ffn(blk_exp, n_used, xs, w1, w3, w2):
    n_blocks = xs.shape[0] // (MOE_TB * ROW_TILE)
    rows = pl.BlockSpec((MOE_TB * ROW_TILE, LANES), lambda j, be, nu: (j, 0))
    grid_spec = pltpu.PrefetchScalarGridSpec(
        num_scalar_prefetch=2,
        grid=(n_blocks,),
        in_specs=[rows,
                  pl.BlockSpec((1, D_MODEL, D_EXPERT), lambda j, be, nu: (be[j], 0, 0)),
                  pl.BlockSpec((1, D_MODEL, D_EXPERT), lambda j, be, nu: (be[j], 0, 0)),
                  pl.BlockSpec((1, D_EXPERT, D_MODEL), lambda j, be, nu: (be[j], 0, 0))],
        out_specs=rows,
    )
    return pl.pallas_call(
        _ffn_kernel,
        grid_spec=grid_spec,
        out_shape=jax.ShapeDtypeStruct(xs.shape, F32),
        compiler_params=_cparams(1),
    )(blk_exp, n_used, xs, w1, w3, w2)


def _combine_kernel(da_ref, db_ref, h_ref, rw_ref, ys_ref, o_ref, ya, yb, sem):
    tm = da_ref.shape[2]

    def issue(t, c):
        pltpu.make_async_copy(_token_rows(ys_ref, da_ref[0, 0, t]), _token_rows(ya, t), sem).start(priority=0)
        pltpu.make_async_copy(_token_rows(ys_ref, db_ref[0, 0, t]), _token_rows(yb, t), sem).start(priority=1)
        return c

    lax.fori_loop(0, tm, issue, 0, unroll=DMA_UNROLL)

    def drain(t, c):
        pltpu.make_async_copy(_token_rows(ys_ref, 0), _token_rows(ya, 0), sem).wait()
        pltpu.make_async_copy(_token_rows(ys_ref, 0), _token_rows(yb, 0), sem).wait()
        return c

    lax.fori_loop(0, tm, drain, 0, unroll=DMA_UNROLL)
    rw = rw_ref[...]
    mix = [rw[:, 0:1] * a + rw[:, 1:2] * b for a, b in zip(_load_row_tiles(ya, tm), _load_row_tiles(yb, tm))]
    o_ref[...] = h_ref[...] + jnp.concatenate(mix, axis=-1)


def _combine(da, db, h2, rw, ys):
    n_tiles, _, tm = da.shape
    n_tok = h2.shape[0]
    smem = pl.BlockSpec((1, 1, tm), lambda i: (i, 0, 0), memory_space=pltpu.SMEM)
    row = lambda w: pl.BlockSpec((tm, w), lambda i: (i, 0))
    return pl.pallas_call(
        _combine_kernel,
        grid=(n_tiles,),
        in_specs=[smem, smem, row(D_MODEL), row(LANES), pl.BlockSpec(memory_space=pl.ANY)],
        out_specs=row(D_MODEL),
        out_shape=jax.ShapeDtypeStruct((n_tok, D_MODEL), F32),
        scratch_shapes=[pltpu.VMEM((tm * ROW_TILE, LANES), F32), pltpu.VMEM((tm * ROW_TILE, LANES), F32),
                        pltpu.SemaphoreType.DMA(())],
        compiler_params=_cparams(1),
    )(da, db, h2, rw, ys)


def _rel_bucket_np(dist):
    n = np.maximum(dist, 0)
    max_exact = NUM_BUCKETS // 2
    nf = np.maximum(n, 1).astype(np.float32)
    large = max_exact + (np.log(nf / max_exact) / math.log(MAX_DIST / max_exact)
                         * (NUM_BUCKETS - max_exact)).astype(np.int32)
    large = np.minimum(large, NUM_BUCKETS - 1)
    return np.where(n < max_exact, n, large).astype(np.int32)


def _toeplitz(vec, rows):
    width = vec.shape[-1] - 1
    flat = jnp.tile(vec, (1,) * (vec.ndim - 1) + (rows,))[..., :rows * width]
    return flat.reshape(vec.shape[:-1] + (rows, width))


def _bias_tables(rel_bias, seq):
    n_chunk = seq // CMP_STRIDE
    n_tiles = seq // TQ
    table = rel_bias.T.astype(F32)

    wide = NEAR + TQ
    k = np.arange(wide + 1)
    dw = np.where(k < NEAR, WINDOW - k, WINDOW + wide + 1 - k)
    used = (k < NEAR) | (k > wide + 1 - TQ)
    vals = table[:, _rel_bucket_np(dw)]

    n_var = WINDOW // TQ
    first_key = WINDOW - TQ * np.arange(n_var + 1)[:, None, None]
    col = np.arange(NEARP)[None, None, :]
    in_seq = (col >= first_key) & (col < NEAR)

    def near_tile(valid):
        t = _toeplitz(jnp.where(valid[None, :], vals, NEG_INF), TQ)[:, :, :NEAR]
        t = jnp.pad(t, ((0, 0), (0, 0), (0, NEARP - NEAR)))
        t = t.reshape(NSA_KV, NSA_HPG, TQ, NEARP).transpose(1, 0, 2, 3).reshape(1, QROWS2, NEARP)
        return jnp.where(in_seq, t, NEG_INF)

    bias_w = near_tile(used & (dw >= 0) & (dw < WINDOW))
    bias_s = near_tile(used & (dw >= 0))
    bias_far = table[:, NUM_BUCKETS - 1].reshape(NSA_KV, NSA_HPG, 1, 1).transpose(1, 0, 2, 3)
    bias_far = jnp.broadcast_to(bias_far, (NSA_HPG, NSA_KV, TQ, LANES)).reshape(QROWS2, LANES)

    r = np.arange(CMP_STRIDE)[:, None]
    k = np.arange(2 * n_chunk + 1)[None, :]
    lag = 2 * n_chunk + 1 - k
    valid = (k > n_chunk + 1) & (CMP_STRIDE * lag + r >= CMP_L - 1)
    vals = table[:, _rel_bucket_np(CMP_STRIDE * lag + r - CMP_L // 2)]
    full = _toeplitz(jnp.where(valid[None], vals, NEG_INF), n_chunk)[..., :n_chunk]
    full = jnp.where(np.arange(n_chunk) < n_chunk - 1, full, NEG_INF)
    a4 = TQ // CMP_STRIDE
    full = full.reshape(NSA_KV, NSA_HPG, CMP_STRIDE, n_tiles, a4, n_chunk).transpose(3, 1, 0, 4, 2, 5)
    bias_c = full.reshape(n_tiles, QROWS2, n_chunk)
    return bias_c, bias_w, bias_s, bias_far


def _selection_tables(seq):
    n_chunk = seq // CMP_STRIDE
    n_blk = seq // SEL_L
    c = np.arange(n_chunk)
    n = np.arange(n_blk)
    start = c * CMP_STRIDE
    overlap_t = ((start[None, :] <= n[:, None] * SEL_L + SEL_L - 1) & (start[None, :] + CMP_L - 1 >= n[:, None] * SEL_L)
                 & (c < n_chunk - 1)[None, :])
    pos = np.arange(seq + WINDOW + NEARP - NEAR) - WINDOW
    lane_blk = np.arange(LANES) % n_blk
    hit = (pos[:, None] >= 0) & (pos[:, None] // SEL_L == lane_blk[None, :]) & (np.arange(LANES) < 2 * n_blk)[None, :]
    return jnp.asarray(overlap_t, BF16), jnp.asarray(np.where(hit, -UNSEL_PENALTY, 0.0), BF16)


def _block_ones(width, group):
    idx = np.arange(width) // group
    return jnp.asarray((idx[:, None] == idx[None, :]) / group, BF16)


def _block_diag(w):
    nb, n, m = w.shape
    eye = jnp.eye(nb, dtype=w.dtype)
    return jnp.einsum('hij,hg->higj', w, eye).reshape(nb * n, nb * m)


def _compress_weights(w1, w2, pos):
    half_l = CMP_L // 2
    parts = []
    for half in range(2):
        wh = w1[half * half_l * HEAD_DIM:(half + 1) * half_l * HEAD_DIM].reshape(half_l, HEAD_DIM, CMP_HIDDEN)
        z = jnp.zeros_like(wh)
        for g in range(NSA_KV):
            grp = [wh if gg == g else z for gg in range(NSA_KV)]
            parts.append(jnp.stack(grp, axis=1).reshape(half_l * KV_W, CMP_HIDDEN))
    w1cat = jnp.concatenate(parts, axis=1).astype(BF16)
    w2bd = _block_diag(jnp.stack([w2] * NSA_KV)).astype(BF16)
    prow = [jnp.tile(pos[half * half_l:(half + 1) * half_l][:, None, :], (1, NSA_KV, 1)).reshape(-1)
            for half in range(2)]
    pmat = jnp.zeros((8, half_l * KV_W), F32).at[0].set(prow[0]).at[1].set(prow[1]).astype(BF16)
    return w1cat, w2bd, pmat


def kernel(x, mem, rel_bias, norm_mix, w_in, rg_conv_w, rg_conv_b, rg_w_r, rg_b_r, rg_w_i, rg_b_i, rg_lambda, nsa_g_q, nsa_g_kc, nsa_g_ks, nsa_g_kw, cmp_pos_k, cmp_pos_v, cmp_k_w1, cmp_k_w2, cmp_v_w1, cmp_v_w2, out_g_rg, out_g_nsa, w_out, norm_x, norm_mem, xa_w_q, xa_w_kv, xa_w_o, xa_g_q, xa_g_k, norm_moe, router_g_w, router_g_b, router_e_w, router_e_b, exp_w1, exp_w3, exp_w2):
    bsz, seq, _ = x.shape
    n_tok = bsz * seq
    assert seq % FAR_TK == 0 and 2 * (seq // SEL_L) <= LANES and norm_mix.shape[0] == 1
    l = 0
    row = lambda v: v.reshape(1, -1).astype(F32)

    perm = np.array([(half * NSA_HPG + p) * HEAD_DIM + d
                     for p in range(NSA_HPG) for half in range(NSA_KV) for d in range(HEAD_DIM)])
    offs = np.cumsum([0, RG_WIDTH, RG_WIDTH, NSA_WIDTH] + [KV_W] * 6)
    w = w_in[l]
    wrg = w[:, :offs[2]].astype(BF16)
    wq = w[:, offs[2]:offs[3]][:, perm].astype(BF16)
    wkv = w[:, offs[3]:offs[9]].astype(BF16)
    wgl = jnp.pad(w[:, offs[9]:], ((0, 0), (0, LANES - 3 * NSA_HEADS))).astype(BF16)
    ones64 = _block_ones(NSA_WIDTH, HEAD_DIM)
    gq = row(jnp.tile(nsa_g_q[l], NSA_HEADS) * HEAD_DIM ** -0.5)
    u, gate, q, kc, vc, ks, vs, kw, vw, gates = _inproj(
        x.reshape(n_tok, D_MODEL), row(norm_mix[l]), wrg, wq, wkv, wgl, gq,
        row(jnp.tile(nsa_g_ks[l], NSA_KV)), row(jnp.tile(nsa_g_kw[l], NSA_KV)), ones64)

    wg = jnp.concatenate([_block_diag(rg_w_r[l]), _block_diag(rg_w_i[l])], axis=1).astype(BF16)
    bg = jnp.concatenate([rg_b_r[l], rg_b_i[l]]).reshape(1, -1)
    y_rg = _rglru(u.reshape(bsz, seq, RG_WIDTH), gate.reshape(bsz, seq, RG_WIDTH),
                  rg_conv_w[l].reshape(CONV_W, RG_WIDTH), row(rg_conv_b[l]), wg, bg, row(rg_lambda[l]),
                  row(out_g_rg[l]))

    n_chunk = seq // CMP_STRIDE
    w1k, w2k, pk = _compress_weights(cmp_k_w1[l], cmp_k_w2[l], cmp_pos_k[l])
    w1v, w2v, pv = _compress_weights(cmp_v_w1[l], cmp_v_w2[l], cmp_pos_v[l])
    kcmp, vcmp = _compress(kc.reshape(bsz, n_chunk, CMP_STRIDE * KV_W), vc.reshape(bsz, n_chunk, CMP_STRIDE * KV_W),
                           w1k, w2k, pk, w1v, w2v, pv, row(jnp.tile(nsa_g_kc[l], NSA_KV)),
                           ones64[:KV_W, :KV_W])
    padw = lambda t: jnp.pad(t.reshape(bsz, seq, KV_W), ((0, 0), (WINDOW, NEARP - NEAR), (0, 0)))
    bias_c, bias_w, bias_s, bias_far = _bias_tables(rel_bias, seq)
    overlap_t, penalty = _selection_tables(seq)
    ksx = jnp.concatenate([padw(ks), jnp.broadcast_to(penalty, (bsz,) + penalty.shape)], axis=-1)
    y_nsa = _nsa(q.reshape(bsz, seq, NSA_WIDTH), gates.reshape(bsz, seq, LANES), kcmp, vcmp,
                 ksx, padw(vs), padw(kw), padw(vw), overlap_t, bias_c, bias_w, bias_s, bias_far,
                 row(out_g_nsa[l][perm]))

    kx, vx = _memkv(mem, row(norm_mem[l]), xa_w_kv[l].astype(BF16), row(xa_g_k[l]))
    wo_mix = w_out[l]
    wr = jnp.pad(jnp.concatenate([router_g_w[l], router_e_w[l]], axis=1),
                 ((0, 0), (0, LANES - N_GROUPS - N_EXPERTS)))
    wr_hi = wr.astype(BF16)
    br = jnp.pad(jnp.concatenate([router_g_b[l], router_e_b[l]]), (0, LANES - N_GROUPS - N_EXPERTS)).reshape(1, -1)
    h2, xt, rw, ri, counts = _mid(
        x, y_rg, y_nsa, wo_mix[:RG_WIDTH].astype(BF16), wo_mix[RG_WIDTH:][perm].astype(BF16), row(norm_x[l]),
        xa_w_q[l].astype(BF16), row(xa_g_q[l] * X_HEAD_DIM ** -0.5), kx, vx, xa_w_o[l].astype(BF16),
        row(norm_moe[l]), wr_hi, (wr - wr_hi.astype(F32)).astype(BF16), br)

    n_slots = 2 * n_tok
    n_blocks = n_slots // MOE_TB + N_EXPERTS
    n_pad = n_blocks * MOE_TB
    cnt = counts[0, :N_EXPERTS].astype(jnp.int32)
    pcnt = (cnt + MOE_TB - 1) // MOE_TB * MOE_TB
    pends = jnp.cumsum(pcnt)
    pstart = jnp.pad((pends - pcnt).astype(F32), (0, LANES - N_EXPERTS)).reshape(1, LANES)
    blk_exp = jnp.minimum(jnp.sum(pends[None, :] <= jnp.arange(n_blocks, dtype=jnp.int32)[:, None] * MOE_TB, axis=1),
                          N_EXPERTS - 1).astype(jnp.int32)
    n_used = (pends[-1:] // MOE_TB).astype(jnp.int32)
    dest = _dest(ri.reshape(n_tok, LANES), pstart)
    tmd = min(TM_DMA, n_tok)
    da = dest[:, 0].reshape(n_tok // tmd, 1, tmd)
    db = dest[:, 1].reshape(n_tok // tmd, 1, tmd)
    xs = _dispatch(da, db, xt, jnp.zeros((n_pad * ROW_TILE, LANES), F32))
    ys = _ffn(blk_exp, n_used, xs, exp_w1[l].astype(BF16), exp_w3[l].astype(BF16), exp_w2[l].astype(BF16))
    out = _combine(da, db, h2.reshape(n_tok, D_MODEL), rw.reshape(n_tok, LANES), ys)
    return out.reshape(bsz, seq, D_MODEL)
```

```python
import math

import numpy as np
import jax
import jax.numpy as jnp
from jax import lax
from jax.experimental import pallas as pl
from jax.experimental.pallas import tpu as pltpu

F32 = jnp.float32
BF16 = jnp.bfloat16

D_MODEL = 1024
RG_WIDTH = 512
RG_BLOCKS = 8
RG_BLOCK = 64
CONV_W = 4
RG_C = 8.0
NSA_WIDTH = 512
NSA_HEADS = 8
HEAD_DIM = 64
NSA_KV = 2
NSA_HPG = 4
KV_W = 128
CMP_L = 32
CMP_STRIDE = 16
CMP_HIDDEN = 256
SEL_L = 64
N_SEL = 8
WINDOW = 512
NUM_BUCKETS = 32
MAX_DIST = 128
X_HEADS = 4
X_HEAD_DIM = 256
N_GROUPS = 4
EXP_PER_GROUP = 8
N_EXPERTS = 32
D_EXPERT = 512
EPS = 1e-6
NEG_INF = -1e30
MASKED_BELOW = -1e29
SEL_FORCE = 1e9
LANES = 128

TQ = 64
NEAR = WINDOW + TQ
FAR_TK = 512
QROWS2 = NSA_HEADS * TQ
UNSEL_PENALTY = 2.0 ** 100

TM_PROJ = 1024
TM_MID = 512
TM_DEST = 512
TM_DMA = 512
DMA_UNROLL = 8
MOE_TB = 512
ROW_TILE = D_MODEL // LANES
RG_CHUNK = 256
SCAN_ROWS = 8
SCAN_UNROLL = 4
VMEM_LIMIT = 56 * 1024 * 1024


def _cparams(n_axes):
    return pltpu.CompilerParams(dimension_semantics=("arbitrary",) * n_axes,
                                vmem_limit_bytes=VMEM_LIMIT)


def _dot(a, b):
    return jnp.dot(a, b, preferred_element_type=F32)


def _dot_nt(a, b):
    return lax.dot_general(a, b, (((1,), (1,)), ((), ())), preferred_element_type=F32)


def _gelu_tanh(x):
    return 0.5 * x * (1.0 + jnp.tanh(math.sqrt(2.0 / math.pi) * (x + 0.044715 * (x * x * x))))


def _sigmoid(x):
    return 0.5 * jnp.tanh(0.5 * x) + 0.5


def _rms(x, g):
    return x * lax.rsqrt(jnp.mean(x * x, axis=-1, keepdims=True) + EPS) * g


def _group_rms(x, ones_blk, g):
    ms = _dot((x * x).astype(BF16), ones_blk)
    return x * lax.rsqrt(ms + EPS) * g


def _inproj_kernel(x_ref, g_ref, wrg_ref, wq_ref, wkv_ref, wgl_ref, gq_ref, gks_ref, gkw_ref, ones_ref,
                   u_ref, gate_ref, q_ref, kc_ref, vc_ref, ks_ref, vs_ref, kw_ref, vw_ref, gates_ref):
    xb = _rms(x_ref[...], g_ref[...]).astype(BF16)
    rg = _dot(xb, wrg_ref[...])
    u_ref[...] = rg[:, :RG_WIDTH].astype(BF16)
    gate_ref[...] = rg[:, RG_WIDTH:].astype(BF16)
    q = _dot(xb, wq_ref[...])
    q_ref[...] = _group_rms(q, ones_ref[...], gq_ref[...]).astype(BF16)
    kv = _dot(xb, wkv_ref[...])
    ones_kv = ones_ref[:KV_W, :KV_W]
    kc_ref[...] = kv[:, 0 * KV_W:1 * KV_W].astype(BF16)
    vc_ref[...] = kv[:, 1 * KV_W:2 * KV_W].astype(BF16)
    ks_ref[...] = _group_rms(kv[:, 2 * KV_W:3 * KV_W], ones_kv, gks_ref[...]).astype(BF16)
    vs_ref[...] = kv[:, 3 * KV_W:4 * KV_W].astype(BF16)
    kw_ref[...] = _group_rms(kv[:, 4 * KV_W:5 * KV_W], ones_kv, gkw_ref[...]).astype(BF16)
    vw_ref[...] = kv[:, 5 * KV_W:6 * KV_W].astype(BF16)
    gates_ref[...] = _sigmoid(_dot(xb, wgl_ref[...]))


def _inproj(x2, g, wrg, wq, wkv, wgl, gq, gks, gkw, ones_blk):
    n_tok = x2.shape[0]
    tm = min(TM_PROJ, n_tok)
    full = lambda a: pl.BlockSpec(a.shape, lambda i: (0,) * a.ndim)
    row = lambda w: pl.BlockSpec((tm, w), lambda i: (i, 0))
    outs = [(RG_WIDTH, BF16), (RG_WIDTH, BF16), (NSA_WIDTH, BF16)] + [(KV_W, BF16)] * 6 + [(LANES, F32)]
    return pl.pallas_call(
        _inproj_kernel,
        grid=(n_tok // tm,),
        in_specs=[row(D_MODEL)] + [full(a) for a in (g, wrg, wq, wkv, wgl, gq, gks, gkw, ones_blk)],
        out_specs=[row(w) for w, _ in outs],
        out_shape=[jax.ShapeDtypeStruct((n_tok, w), dt) for w, dt in outs],
        compiler_params=_cparams(1),
    )(x2, g, wrg, wq, wkv, wgl, gq, gks, gkw, ones_blk)


def _rglru_kernel(u_ref, gate_ref, cw_ref, cb_ref, wg_ref, bg_ref, lam_ref, og_ref, y_ref, upad, a_s, h_s):
    seq = u_ref.shape[1]
    upad[0:8, :] = jnp.zeros((8, RG_WIDTH), F32)
    upad[8:8 + seq, :] = u_ref[0].astype(F32)
    neg_lam = -lam_ref[...]
    softplus = jnp.maximum(neg_lam, 0.0) + jnp.log(1.0 + jnp.exp(-jnp.abs(neg_lam)))
    ch = min(RG_CHUNK, seq)
    for c in range(seq // ch):
        r0 = c * ch
        uc = cb_ref[...]
        for k in range(CONV_W):
            off = 8 + r0 - (CONV_W - 1) + k
            uc = uc + cw_ref[k:k + 1, :] * upad[off:off + ch, :]
        gt = _dot(uc.astype(BF16), wg_ref[...]) + bg_ref[...]
        a = jnp.exp((-RG_C) * _sigmoid(gt[:, :RG_WIDTH]) * softplus)
        a_s[r0:r0 + ch, :] = a
        s = 1.0 - a * a
        h_s[r0:r0 + ch, :] = s * lax.rsqrt(jnp.maximum(s, 1e-30)) * _sigmoid(gt[:, RG_WIDTH:]) * uc

    row = lax.broadcasted_iota(jnp.int32, (SCAN_ROWS, RG_WIDTH), 0)

    def block(j, h_prev):
        rows = pl.ds(pl.multiple_of(j * SCAN_ROWS, SCAN_ROWS), SCAN_ROWS)
        a = a_s[rows, :]
        b = h_s[rows, :]
        k = 1
        while k < SCAN_ROWS:
            keep = row >= k
            b = jnp.where(keep, a * pltpu.roll(b, k, 0) + b, b)
            a = jnp.where(keep, a * pltpu.roll(a, k, 0), a)
            k *= 2
        h = a * h_prev + b
        h_s[rows, :] = h
        return h[SCAN_ROWS - 1:SCAN_ROWS, :]

    lax.fori_loop(0, seq // SCAN_ROWS, block, jnp.zeros((1, RG_WIDTH), F32), unroll=SCAN_UNROLL)

    for c in range(seq // ch):
        r0 = c * ch
        y = _gelu_tanh(gate_ref[0, r0:r0 + ch, :].astype(F32)) * h_s[r0:r0 + ch, :]
        y_ref[0, r0:r0 + ch, :] = _rms(y, og_ref[...]).astype(BF16)


def _rglru(u3, gate3, cw, cb, wg, bg, lam, og):
    bsz, seq, _ = u3.shape
    full = lambda a: pl.BlockSpec(a.shape, lambda b: (0,) * a.ndim)
    blk = pl.BlockSpec((1, seq, RG_WIDTH), lambda b: (b, 0, 0))
    return pl.pallas_call(
        _rglru_kernel,
        grid=(bsz,),
        in_specs=[blk, blk] + [full(a) for a in (cw, cb, wg, bg, lam, og)],
        out_specs=blk,
        out_shape=jax.ShapeDtypeStruct((bsz, seq, RG_WIDTH), BF16),
        scratch_shapes=[pltpu.VMEM((seq + 8, RG_WIDTH), F32), pltpu.VMEM((seq, RG_WIDTH), F32),
                        pltpu.VMEM((seq, RG_WIDTH), F32)],
        compiler_params=_cparams(1),
    )(u3, gate3, cw, cb, wg, bg, lam, og)


def _compress_kernel(kx_ref, vx_ref, w1k_ref, w2k_ref, pk_ref, w1v_ref, w2v_ref, pv_ref, gk_ref, ones_ref,
                     ko_ref, vo_ref):
    n_chunk = kx_ref.shape[1]
    half = NSA_KV * CMP_HIDDEN

    def mlp(x_ref, w1_ref, w2_ref, p_ref):
        ab = _dot(x_ref[0], w1_ref[...])
        pos = _dot(p_ref[...], w1_ref[...])
        hid = ab[:, :half] + pltpu.roll(ab[:, half:], n_chunk - 1, 0) + (pos[0:1, :half] + pos[1:2, half:])
        return _dot(_gelu_tanh(hid).astype(BF16), w2_ref[...])

    kc = mlp(kx_ref, w1k_ref, w2k_ref, pk_ref)
    ko_ref[0] = _group_rms(kc, ones_ref[...], gk_ref[...]).astype(BF16)
    vo_ref[0] = mlp(vx_ref, w1v_ref, w2v_ref, pv_ref).astype(BF16)


def _compress(kx, vx, w1k, w2k, pk, w1v, w2v, pv, gk, ones_kv):
    bsz, n_chunk, width = kx.shape
    full = lambda a: pl.BlockSpec(a.shape, lambda b: (0,) * a.ndim)
    xin = pl.BlockSpec((1, n_chunk, width), lambda b: (b, 0, 0))
    out = pl.BlockSpec((1, n_chunk, KV_W), lambda b: (b, 0, 0))
    return pl.pallas_call(
        _compress_kernel,
        grid=(bsz,),
        in_specs=[xin, xin] + [full(a) for a in (w1k, w2k, pk, w1v, w2v, pv, gk, ones_kv)],
        out_specs=[out, out],
        out_shape=[jax.ShapeDtypeStruct((bsz, n_chunk, KV_W), BF16)] * 2,
        compiler_params=_cparams(1),
    )(kx, vx, w1k, w2k, pk, w1v, w2v, pv, gk, ones_kv)


def _nsa_kernel(q_ref, gates_ref, kcmp_ref, vcmp_ref, ksx_ref, vsp_ref, kwp_ref, vwp_ref, ovt_ref,
                bc_ref, bw_ref, bs_ref, bf_ref, og_ref, y_ref):
    i = pl.program_id(1)
    t0 = pl.multiple_of(i * TQ, TQ)
    n_blk = ovt_ref.shape[0]
    lane = lax.broadcasted_iota(jnp.int32, (TQ, LANES), 1)
    lo_half = lane < HEAD_DIM
    pieces = []
    for p in range(NSA_HPG):
        qs = q_ref[0, :, p * LANES:(p + 1) * LANES]
        zero = jnp.zeros_like(qs)
        pieces += [jnp.where(lo_half, qs, zero), jnp.where(lo_half, zero, qs)]
    q8 = jnp.concatenate(pieces, axis=0)

    bc = bc_ref[0]
    lc = _dot_nt(q8, kcmp_ref[0]) + bc
    ec = jnp.where(bc > MASKED_BELOW, jnp.exp(lc - jnp.max(lc, axis=-1, keepdims=True)), 0.0)
    sc = jnp.sum(ec, axis=-1, keepdims=True)
    pc = ec / jnp.where(sc > 0.0, sc, 1.0)
    o_c = _dot(pc.astype(BF16), vcmp_ref[0])

    blocks = [pc[r * TQ:(r + 1) * TQ] for r in range(NSA_HPG * NSA_KV)]
    pcs = jnp.concatenate([sum(blocks[g::NSA_KV]) for g in range(NSA_KV)], axis=0)
    pcs_hi = pcs.astype(BF16)
    pcs_lo = (pcs - pcs_hi.astype(F32)).astype(BF16)
    imp = _dot_nt(ovt_ref[...], pcs_hi) + _dot_nt(ovt_ref[...], pcs_lo)
    blk = lax.broadcasted_iota(jnp.int32, imp.shape, 0)
    forced = (blk == 0) | (blk == i) | (blk == i - 1)
    score = jnp.where(forced, SEL_FORCE, jnp.where(blk > i, -3e38, imp))
    rank = jnp.zeros(imp.shape, F32)
    for m in range(n_blk):
        row = score[m:m + 1, :]
        rank = rank + jnp.where(blk > m, jnp.where(row >= score, 1.0, 0.0), jnp.where(row > score, 1.0, 0.0))
    unsel = jnp.where(rank < N_SEL, 0.0, 1.0)
    unsel_far = jnp.where(blk >= i - WINDOW // SEL_L, 1.0, unsel)
    pad = jnp.zeros((LANES - 2 * n_blk, imp.shape[1]), F32)
    u_t = jnp.concatenate([unsel, unsel_far, pad], axis=0).T
    u_lane = lax.broadcasted_iota(jnp.int32, u_t.shape, 1)
    u_near = jnp.where(u_lane < n_blk, u_t, 0.0).astype(BF16)
    u_far = jnp.where(u_lane >= n_blk, u_t, 0.0).astype(BF16)
    qx_near = jnp.concatenate([q8, jnp.concatenate([u_near] * NSA_HPG, axis=0)], axis=1)
    qx_far = jnp.concatenate([q8, jnp.concatenate([u_far] * NSA_HPG, axis=0)], axis=1)

    lw = _dot_nt(q8, kwp_ref[0, pl.ds(t0, NEAR), :]) + bw_ref[0]
    ew = jnp.exp(lw - jnp.max(lw, axis=-1, keepdims=True))
    o_w = _dot(ew.astype(BF16), vwp_ref[0, pl.ds(t0, NEAR), :]) / jnp.sum(ew, axis=-1, keepdims=True)

    ls = _dot_nt(qx_near, ksx_ref[0, pl.ds(t0, NEAR), :]) + bs_ref[0]
    m1 = jnp.max(ls, axis=-1, keepdims=True)
    e1 = jnp.exp(ls - m1)
    l1 = jnp.sum(e1, axis=-1, keepdims=True)
    acc1 = _dot(e1.astype(BF16), vsp_ref[0, pl.ds(t0, NEAR), :])
    bfar = bf_ref[...]

    def far_step(kf, carry):
        m, l, acc = carry
        base = pl.multiple_of(WINDOW + kf * FAR_TK, FAR_TK)
        lf = _dot_nt(qx_far, ksx_ref[0, pl.ds(base, FAR_TK), :]) + bfar
        m_new = jnp.maximum(m, jnp.max(lf, axis=-1, keepdims=True))
        alpha = jnp.exp(m - m_new)
        e = jnp.exp(lf - m_new)
        l_new = alpha * l + jnp.sum(e, axis=-1, keepdims=True)
        return m_new, l_new, alpha * acc + _dot(e.astype(BF16), vsp_ref[0, pl.ds(base, FAR_TK), :])

    n_far = (jnp.maximum(t0 - WINDOW, 0) + FAR_TK - 1) // FAR_TK
    _, l_s, acc_s = lax.fori_loop(0, n_far, far_step, (m1, l1, acc1))
    o_s = acc_s / l_s

    gates = gates_ref[0]

    def gate_col(j):
        cols = [gates[:, (g * NSA_HPG + p) * 3 + j:(g * NSA_HPG + p) * 3 + j + 1]
                for p in range(NSA_HPG) for g in range(NSA_KV)]
        return jnp.concatenate(cols, axis=0)

    out = gate_col(0) * o_c + gate_col(1) * o_s + gate_col(2) * o_w
    slabs = [jnp.where(lo_half, out[(2 * p) * TQ:(2 * p + 1) * TQ], out[(2 * p + 1) * TQ:(2 * p + 2) * TQ])
             for p in range(NSA_HPG)]
    y_ref[0] = _rms(jnp.concatenate(slabs, axis=-1), og_ref[...]).astype(BF16)


def _nsa(q3, gates3, kcmp, vcmp, ksx, vsp, kwp, vwp, ovt, bias_c, bias_w, bias_s, bias_far, og):
    bsz, seq, _ = q3.shape
    n_chunk = kcmp.shape[1]
    n_var = bias_w.shape[0] - 1
    full = lambda a: pl.BlockSpec(a.shape, lambda b, i: (0,) * a.ndim)
    per_b = lambda a: pl.BlockSpec((1,) + a.shape[1:], lambda b, i: (b,) + (0,) * (a.ndim - 1))
    near = pl.BlockSpec((1, QROWS2, NEAR), lambda b, i: (jnp.minimum(i, n_var), 0, 0))
    return pl.pallas_call(
        _nsa_kernel,
        grid=(bsz, seq // TQ),
        in_specs=[pl.BlockSpec((1, TQ, NSA_WIDTH), lambda b, i: (b, i, 0)),
                  pl.BlockSpec((1, TQ, LANES), lambda b, i: (b, i, 0)),
                  per_b(kcmp), per_b(vcmp), per_b(ksx), per_b(vsp), per_b(kwp), per_b(vwp),
                  full(ovt),
                  pl.BlockSpec((1, QROWS2, n_chunk), lambda b, i: (i, 0, 0)),
                  near, near, full(bias_far), full(og)],
        out_specs=pl.BlockSpec((1, TQ, NSA_WIDTH), lambda b, i: (b, i, 0)),
        out_shape=jax.ShapeDtypeStruct((bsz, seq, NSA_WIDTH), BF16),
        compiler_params=_cparams(2),
    )(q3, gates3, kcmp, vcmp, ksx, vsp, kwp, vwp, ovt, bias_c, bias_w, bias_s, bias_far, og)


def _memkv_kernel(mem_ref, g_ref, wkv_ref, gk_ref, k_ref, v_ref):
    mn = _rms(mem_ref[0], g_ref[...]).astype(BF16)
    kv = _dot(mn, wkv_ref[...])
    for h in range(X_HEADS):
        sl = slice(h * X_HEAD_DIM, (h + 1) * X_HEAD_DIM)
        k_ref[0, :, sl] = _rms(kv[:, sl], gk_ref[...]).astype(BF16)
    v_ref[0] = kv[:, D_MODEL:].astype(BF16)


def _memkv(mem, g, wkv, gk):
    bsz, mlen, _ = mem.shape
    full = lambda a: pl.BlockSpec(a.shape, lambda b: (0,) * a.ndim)
    blk = pl.BlockSpec((1, mlen, D_MODEL), lambda b: (b, 0, 0))
    return pl.pallas_call(
        _memkv_kernel,
        grid=(bsz,),
        in_specs=[blk, full(g), full(wkv), full(gk)],
        out_specs=[blk, blk],
        out_shape=[jax.ShapeDtypeStruct((bsz, mlen, D_MODEL), BF16)] * 2,
        compiler_params=_cparams(1),
    )(mem, g, wkv, gk)


def _mid_kernel(x_ref, yrg_ref, ynsa_ref, woa_ref, wob_ref, gx_ref, wq_ref, gq_ref, k_ref, v_ref, wo_ref,
                gm_ref, wrh_ref, wrl_ref, br_ref, h_ref, xt_ref, rw_ref, ri_ref, cnt_ref):
    h1 = x_ref[0] + _dot(yrg_ref[0], woa_ref[...]) + _dot(ynsa_ref[0], wob_ref[...])

    q = _dot(_rms(h1, gx_ref[...]).astype(BF16), wq_ref[...])
    heads = []
    for h in range(X_HEADS):
        sl = slice(h * X_HEAD_DIM, (h + 1) * X_HEAD_DIM)
        qh = _rms(q[:, sl], gq_ref[...]).astype(BF16)
        lg = _dot_nt(qh, k_ref[0, :, sl])
        e = jnp.exp(lg - jnp.max(lg, axis=-1, keepdims=True))
        heads.append(_dot(e.astype(BF16), v_ref[0, :, sl]) / jnp.sum(e, axis=-1, keepdims=True))
    h2 = h1 + _dot(jnp.concatenate(heads, axis=-1).astype(BF16), wo_ref[...])
    h_ref[0] = h2

    xt = _rms(h2, gm_ref[...])
    _store_row_tiles(xt_ref, xt)
    xt_hi = xt.astype(BF16)
    xt_lo = (xt - xt_hi.astype(F32)).astype(BF16)
    lg = _dot(xt_hi, wrh_ref[...]) + _dot(xt_lo, wrh_ref[...]) + _dot(xt_hi, wrl_ref[...]) + br_ref[...]
    lane = lax.broadcasted_iota(jnp.int32, lg.shape, 1)
    lane_f = lane.astype(F32)
    first_of = lambda hit: jnp.min(jnp.where(hit, lane_f, 1e9), axis=-1, keepdims=True)
    glog = jnp.where(lane < N_GROUPS, lg, -3e38)
    gmax = jnp.max(glog, axis=-1, keepdims=True)
    gsel = first_of(glog == gmax)
    p_g = 1.0 / jnp.sum(jnp.exp(glog - gmax), axis=-1, keepdims=True)
    lo = N_GROUPS + EXP_PER_GROUP * gsel
    el = jnp.where((lane_f >= lo) & (lane_f < lo + EXP_PER_GROUP), lg, -3e38)
    m_a = jnp.max(el, axis=-1, keepdims=True)
    i_a = first_of(el == m_a)
    el2 = jnp.where(lane_f == i_a, -3e38, el)
    m_b = jnp.max(el2, axis=-1, keepdims=True)
    i_b = first_of(el2 == m_b)
    r = jnp.exp(m_b - m_a)
    w_a = p_g / (1.0 + r)
    w_b = p_g * r / (1.0 + r)
    e_a = i_a - N_GROUPS
    e_b = i_b - N_GROUPS
    rw_ref[0] = jnp.where(lane == 0, w_a, jnp.where(lane == 1, w_b, 0.0))
    ri_ref[0] = jnp.where(lane == 0, e_a, jnp.where(lane == 1, e_b, 0.0)).astype(jnp.int32)

    @pl.when((pl.program_id(0) == 0) & (pl.program_id(1) == 0))
    def _():
        cnt_ref[...] = jnp.zeros_like(cnt_ref)

    hot = jnp.where((lane_f == e_a) | (lane_f == e_b), 1.0, 0.0)
    cnt_ref[...] += jnp.sum(hot, axis=0, keepdims=True)


def _mid(x, yrg, ynsa, woa, wob, gx, wq, gq, kx, vx, wo, gm, wrh, wrl, br):
    bsz, seq, _ = x.shape
    tm = min(TM_MID, seq)
    mlen = kx.shape[1]
    n_i = seq // tm
    full = lambda a: pl.BlockSpec(a.shape, lambda b, i: (0,) * a.ndim)
    tok = lambda w: pl.BlockSpec((1, tm, w), lambda b, i: (b, i, 0))
    memb = pl.BlockSpec((1, mlen, D_MODEL), lambda b, i: (b, 0, 0))
    xt_spec = pl.BlockSpec((tm * ROW_TILE, LANES), lambda b, i: (b * n_i + i, 0))
    return pl.pallas_call(
        _mid_kernel,
        grid=(bsz, seq // tm),
        in_specs=[tok(D_MODEL), tok(RG_WIDTH), tok(NSA_WIDTH), full(woa), full(wob), full(gx), full(wq), full(gq),
                  memb, memb, full(wo), full(gm), full(wrh), full(wrl), full(br)],
        out_specs=[tok(D_MODEL), xt_spec, tok(LANES), tok(LANES), pl.BlockSpec((1, LANES), lambda b, i: (0, 0))],
        out_shape=[jax.ShapeDtypeStruct((bsz, seq, D_MODEL), F32),
                   jax.ShapeDtypeStruct((bsz * seq * ROW_TILE, LANES), F32),
                   jax.ShapeDtypeStruct((bsz, seq, LANES), F32), jax.ShapeDtypeStruct((bsz, seq, LANES), jnp.int32),
                   jax.ShapeDtypeStruct((1, LANES), F32)],
        compiler_params=_cparams(2),
    )(x, yrg, ynsa, woa, wob, gx, wq, gq, kx, vx, wo, gm, wrh, wrl, br)


def _dest_kernel(ri_ref, pstart_ref, dest_ref, run_ref):
    @pl.when(pl.program_id(0) == 0)
    def _():
        run_ref[...] = jnp.zeros_like(run_ref)

    ri = ri_ref[...]
    tm = ri.shape[0]
    lane = lax.broadcasted_iota(jnp.int32, ri.shape, 1)
    e_a = ri[:, 0:1]
    e_b = ri[:, 1:2]
    hot_a = lane == e_a
    hot_b = lane == e_b
    hot = jnp.where(hot_a | hot_b, 1.0, 0.0)
    row = lax.broadcasted_iota(jnp.int32, (tm, tm), 0)
    col = lax.broadcasted_iota(jnp.int32, (tm, tm), 1)
    earlier = jnp.where(col < row, 1.0, 0.0).astype(BF16)
    base = _dot(earlier, hot.astype(BF16)) + run_ref[...] + pstart_ref[...]
    d_a = jnp.sum(jnp.where(hot_a, base, 0.0), axis=-1, keepdims=True)
    d_b = jnp.sum(jnp.where(hot_b, base, 0.0), axis=-1, keepdims=True)
    dest_ref[...] = jnp.where(lane == 0, d_a, jnp.where(lane == 1, d_b, 0.0)).astype(jnp.int32)
    run_ref[...] += jnp.sum(hot, axis=0, keepdims=True)


def _dest(ri2, pstart):
    n_tok = ri2.shape[0]
    tm = min(TM_DEST, n_tok)
    return pl.pallas_call(
        _dest_kernel,
        grid=(n_tok // tm,),
        in_specs=[pl.BlockSpec((tm, LANES), lambda i: (i, 0)), pl.BlockSpec((1, LANES), lambda i: (0, 0))],
        out_specs=pl.BlockSpec((tm, LANES), lambda i: (i, 0)),
        out_shape=jax.ShapeDtypeStruct((n_tok, LANES), jnp.int32),
        scratch_shapes=[pltpu.VMEM((1, LANES), F32)],
        compiler_params=_cparams(1),
    )(ri2, pstart)


def _store_row_tiles(ref, val):
    n = val.shape[0]
    for c in range(ROW_TILE):
        ref[pl.ds(c, n, stride=ROW_TILE), :] = val[:, c * LANES:(c + 1) * LANES]


def _load_row_tiles(ref, n):
    return [ref[pl.ds(c, n, stride=ROW_TILE), :] for c in range(ROW_TILE)]


def _token_rows(ref, t):
    return ref.at[pl.ds(pl.multiple_of(t * ROW_TILE, ROW_TILE), ROW_TILE), :]


def _dispatch_kernel(da_ref, db_ref, xt_ref, buf_ref, xs_ref, sem):
    del buf_ref
    tm = da_ref.shape[2]

    def issue(t, c):
        pltpu.make_async_copy(_token_rows(xt_ref, t), _token_rows(xs_ref, da_ref[0, 0, t]), sem).start(priority=0)
        pltpu.make_async_copy(_token_rows(xt_ref, t), _token_rows(xs_ref, db_ref[0, 0, t]), sem).start(priority=1)
        return c

    lax.fori_loop(0, tm, issue, 0, unroll=DMA_UNROLL)

    def drain(t, c):
        pltpu.make_async_copy(_token_rows(xt_ref, 0), _token_rows(xs_ref, 0), sem).wait()
        pltpu.make_async_copy(_token_rows(xt_ref, 0), _token_rows(xs_ref, 0), sem).wait()
        return c

    lax.fori_loop(0, tm, drain, 0, unroll=DMA_UNROLL)


def _dispatch(da, db, xt_rows, buf):
    n_tiles, _, tm = da.shape
    smem = pl.BlockSpec((1, 1, tm), lambda i: (i, 0, 0), memory_space=pltpu.SMEM)
    hbm = pl.BlockSpec(memory_space=pl.ANY)
    return pl.pallas_call(
        _dispatch_kernel,
        grid=(n_tiles,),
        in_specs=[smem, smem, pl.BlockSpec((tm * ROW_TILE, LANES), lambda i: (i, 0)), hbm],
        out_specs=hbm,
        out_shape=jax.ShapeDtypeStruct(buf.shape, buf.dtype),
        scratch_shapes=[pltpu.SemaphoreType.DMA(())],
        input_output_aliases={3: 0},
        compiler_params=pltpu.CompilerParams(dimension_semantics=("arbitrary",), has_side_effects=True,
                                             vmem_limit_bytes=VMEM_LIMIT),
    )(da, db, xt_rows, buf)


def _ffn_kernel(bexp_ref, nused_ref, xs_ref, w1_ref, w3_ref, w2_ref, ys_ref):
    del bexp_ref
    j = pl.program_id(0)

    @pl.when(j < nused_ref[0])
    def _():
        xb = jnp.concatenate(_load_row_tiles(xs_ref, MOE_TB), axis=-1).astype(BF16)
        a = _dot(xb, w1_ref[0])
        h = a * _sigmoid(a) * _dot(xb, w3_ref[0])
        _store_row_tiles(ys_ref, _dot(h.astype(BF16), w2_ref[0]))

    @pl.when(j >= nused_ref[0])
    def _():
        ys_ref[...] = jnp.zeros_like(ys_ref)


def _ffn(blk_exp, n_used, xs, w1, w3, w2):
    n_blocks = xs.shape[0] // (MOE_TB * ROW_TILE)
    rows = pl.BlockSpec((MOE_TB * ROW_TILE, LANES), lambda j, be, nu: (j, 0))
    grid_spec = pltpu.PrefetchScalarGridSpec(
        num_scalar_prefetch=2,
        grid=(n_blocks,),
        in_specs=[rows,
                  pl.BlockSpec((1, D_MODEL, D_EXPERT), lambda j, be, nu: (be[j], 0, 0)),
                  pl.BlockSpec((1, D_MODEL, D_EXPERT), lambda j, be, nu: (be[j], 0, 0)),
                  pl.BlockSpec((1, D_EXPERT, D_MODEL), lambda j, be, nu: (be[j], 0, 0))],
        out_specs=rows,
    )
    return pl.pallas_call(
        _ffn_kernel,
        grid_spec=grid_spec,
        out_shape=jax.ShapeDtypeStruct(xs.shape, F32),
        compiler_params=_cparams(1),
    )(blk_exp, n_used, xs, w1, w3, w2)


def _combine_kernel(da_ref, db_ref, h_ref, rw_ref, ys_ref, o_ref, ya, yb, sem):
    tm = da_ref.shape[2]

    def issue(t, c):
        pltpu.make_async_copy(_token_rows(ys_ref, da_ref[0, 0, t]), _token_rows(ya, t), sem).start(priority=0)
        pltpu.make_async_copy(_token_rows(ys_ref, db_ref[0, 0, t]), _token_rows(yb, t), sem).start(priority=1)
        return c

    lax.fori_loop(0, tm, issue, 0, unroll=DMA_UNROLL)

    def drain(t, c):
        pltpu.make_async_copy(_token_rows(ys_ref, 0), _token_rows(ya, 0), sem).wait()
        pltpu.make_async_copy(_token_rows(ys_ref, 0), _token_rows(yb, 0), sem).wait()
        return c

    lax.fori_loop(0, tm, drain, 0, unroll=DMA_UNROLL)
    rw = rw_ref[...]
    mix = [rw[:, 0:1] * a + rw[:, 1:2] * b for a, b in zip(_load_row_tiles(ya, tm), _load_row_tiles(yb, tm))]
    o_ref[...] = h_ref[...] + jnp.concatenate(mix, axis=-1)


def _combine(da, db, h2, rw, ys):
    n_tiles, _, tm = da.shape
    n_tok = h2.shape[0]
    smem = pl.BlockSpec((1, 1, tm), lambda i: (i, 0, 0), memory_space=pltpu.SMEM)
    row = lambda w: pl.BlockSpec((tm, w), lambda i: (i, 0))
    return pl.pallas_call(
        _combine_kernel,
        grid=(n_tiles,),
        in_specs=[smem, smem, row(D_MODEL), row(LANES), pl.BlockSpec(memory_space=pl.ANY)],
        out_specs=row(D_MODEL),
        out_shape=jax.ShapeDtypeStruct((n_tok, D_MODEL), F32),
        scratch_shapes=[pltpu.VMEM((tm * ROW_TILE, LANES), F32), pltpu.VMEM((tm * ROW_TILE, LANES), F32),
                        pltpu.SemaphoreType.DMA(())],
        compiler_params=_cparams(1),
    )(da, db, h2, rw, ys)


def _rel_bucket_np(dist):
    n = np.maximum(dist, 0)
    max_exact = NUM_BUCKETS // 2
    nf = np.maximum(n, 1).astype(np.float32)
    large = max_exact + (np.log(nf / max_exact) / math.log(MAX_DIST / max_exact)
                         * (NUM_BUCKETS - max_exact)).astype(np.int32)
    large = np.minimum(large, NUM_BUCKETS - 1)
    return np.where(n < max_exact, n, large).astype(np.int32)


def _toeplitz(vec, rows):
    width = vec.shape[-1] - 1
    flat = jnp.tile(vec, (1,) * (vec.ndim - 1) + (rows,))[..., :rows * width]
    return flat.reshape(vec.shape[:-1] + (rows, width))


def _bias_tables(rel_bias, seq):
    n_chunk = seq // CMP_STRIDE
    n_tiles = seq // TQ
    table = rel_bias.T.astype(F32)

    wide = NEAR + TQ
    k = np.arange(wide + 1)
    dw = np.where(k < NEAR, WINDOW - k, WINDOW + wide + 1 - k)
    used = (k < NEAR) | (k > wide + 1 - TQ)
    vals = table[:, _rel_bucket_np(dw)]

    n_var = WINDOW // TQ
    first_key = WINDOW - TQ * np.arange(n_var + 1)[:, None, None]
    in_seq = np.arange(NEAR)[None, None, :] >= first_key

    def near_tile(valid):
        t = _toeplitz(jnp.where(valid[None, :], vals, NEG_INF), TQ)[:, :, :NEAR]
        t = t.reshape(NSA_KV, NSA_HPG, TQ, NEAR).transpose(1, 0, 2, 3).reshape(1, QROWS2, NEAR)
        return jnp.where(in_seq, t, NEG_INF)

    bias_w = near_tile(used & (dw >= 0) & (dw < WINDOW))
    bias_s = near_tile(used & (dw >= 0))
    bias_far = table[:, NUM_BUCKETS - 1].reshape(NSA_KV, NSA_HPG, 1).transpose(1, 0, 2)
    bias_far = jnp.broadcast_to(bias_far, (NSA_HPG, NSA_KV, TQ)).reshape(QROWS2, 1)

    r = np.arange(CMP_STRIDE)[:, None]
    k = np.arange(2 * n_chunk + 1)[None, :]
    lag = 2 * n_chunk + 1 - k
    valid = (k > n_chunk + 1) & (CMP_STRIDE * lag + r >= CMP_L - 1)
    vals = table[:, _rel_bucket_np(CMP_STRIDE * lag + r - CMP_L // 2)]
    full = _toeplitz(jnp.where(valid[None], vals, NEG_INF), n_chunk)[..., :n_chunk]
    full = jnp.where(np.arange(n_chunk) < n_chunk - 1, full, NEG_INF)
    a4 = TQ // CMP_STRIDE
    full = full.reshape(NSA_KV, NSA_HPG, CMP_STRIDE, n_tiles, a4, n_chunk).transpose(3, 1, 0, 4, 2, 5)
    bias_c = full.reshape(n_tiles, QROWS2, n_chunk)
    return bias_c, bias_w, bias_s, bias_far


def _selection_tables(seq):
    n_chunk = seq // CMP_STRIDE
    n_blk = seq // SEL_L
    c = np.arange(n_chunk)
    n = np.arange(n_blk)
    start = c * CMP_STRIDE
    overlap_t = ((start[None, :] <= n[:, None] * SEL_L + SEL_L - 1) & (start[None, :] + CMP_L - 1 >= n[:, None] * SEL_L)
                 & (c < n_chunk - 1)[None, :])
    pos = np.arange(seq + WINDOW) - WINDOW
    lane_blk = np.arange(LANES) % n_blk
    hit = (pos[:, None] >= 0) & (pos[:, None] // SEL_L == lane_blk[None, :]) & (np.arange(LANES) < 2 * n_blk)[None, :]
    return jnp.asarray(overlap_t, BF16), jnp.asarray(np.where(hit, -UNSEL_PENALTY, 0.0), BF16)


def _block_ones(width, group):
    idx = np.arange(width) // group
    return jnp.asarray((idx[:, None] == idx[None, :]) / group, BF16)


def _block_diag(w):
    nb, n, m = w.shape
    eye = jnp.eye(nb, dtype=w.dtype)
    return jnp.einsum('hij,hg->higj', w, eye).reshape(nb * n, nb * m)


def _compress_weights(w1, w2, pos):
    half_l = CMP_L // 2
    parts = []
    for half in range(2):
        wh = w1[half * half_l * HEAD_DIM:(half + 1) * half_l * HEAD_DIM].reshape(half_l, HEAD_DIM, CMP_HIDDEN)
        z = jnp.zeros_like(wh)
        for g in range(NSA_KV):
            grp = [wh if gg == g else z for gg in range(NSA_KV)]
            parts.append(jnp.stack(grp, axis=1).reshape(half_l * KV_W, CMP_HIDDEN))
    w1cat = jnp.concatenate(parts, axis=1).astype(BF16)
    w2bd = _block_diag(jnp.stack([w2] * NSA_KV)).astype(BF16)
    prow = [jnp.tile(pos[half * half_l:(half + 1) * half_l][:, None, :], (1, NSA_KV, 1)).reshape(-1)
            for half in range(2)]
    pmat = jnp.zeros((8, half_l * KV_W), F32).at[0].set(prow[0]).at[1].set(prow[1]).astype(BF16)
    return w1cat, w2bd, pmat


def kernel(x, mem, rel_bias, norm_mix, w_in, rg_conv_w, rg_conv_b, rg_w_r, rg_b_r, rg_w_i, rg_b_i, rg_lambda, nsa_g_q, nsa_g_kc, nsa_g_ks, nsa_g_kw, cmp_pos_k, cmp_pos_v, cmp_k_w1, cmp_k_w2, cmp_v_w1, cmp_v_w2, out_g_rg, out_g_nsa, w_out, norm_x, norm_mem, xa_w_q, xa_w_kv, xa_w_o, xa_g_q, xa_g_k, norm_moe, router_g_w, router_g_b, router_e_w, router_e_b, exp_w1, exp_w3, exp_w2):
    bsz, seq, _ = x.shape
    n_tok = bsz * seq
    assert seq % FAR_TK == 0 and 2 * (seq // SEL_L) <= LANES and norm_mix.shape[0] == 1
    l = 0
    row = lambda v: v.reshape(1, -1).astype(F32)

    perm = np.array([(half * NSA_HPG + p) * HEAD_DIM + d
                     for p in range(NSA_HPG) for half in range(NSA_KV) for d in range(HEAD_DIM)])
    offs = np.cumsum([0, RG_WIDTH, RG_WIDTH, NSA_WIDTH] + [KV_W] * 6)
    w = w_in[l]
    wrg = w[:, :offs[2]].astype(BF16)
    wq = w[:, offs[2]:offs[3]][:, perm].astype(BF16)
    wkv = w[:, offs[3]:offs[9]].astype(BF16)
    wgl = jnp.pad(w[:, offs[9]:], ((0, 0), (0, LANES - 3 * NSA_HEADS))).astype(BF16)
    ones64 = _block_ones(NSA_WIDTH, HEAD_DIM)
    gq = row(jnp.tile(nsa_g_q[l], NSA_HEADS) * HEAD_DIM ** -0.5)
    u, gate, q, kc, vc, ks, vs, kw, vw, gates = _inproj(
        x.reshape(n_tok, D_MODEL), row(norm_mix[l]), wrg, wq, wkv, wgl, gq,
        row(jnp.tile(nsa_g_ks[l], NSA_KV)), row(jnp.tile(nsa_g_kw[l], NSA_KV)), ones64)

    wg = jnp.concatenate([_block_diag(rg_w_r[l]), _block_diag(rg_w_i[l])], axis=1).astype(BF16)
    bg = jnp.concatenate([rg_b_r[l], rg_b_i[l]]).reshape(1, -1)
    y_rg = _rglru(u.reshape(bsz, seq, RG_WIDTH), gate.reshape(bsz, seq, RG_WIDTH),
                  rg_conv_w[l].reshape(CONV_W, RG_WIDTH), row(rg_conv_b[l]), wg, bg, row(rg_lambda[l]),
                  row(out_g_rg[l]))

    n_chunk = seq // CMP_STRIDE
    w1k, w2k, pk = _compress_weights(cmp_k_w1[l], cmp_k_w2[l], cmp_pos_k[l])
    w1v, w2v, pv = _compress_weights(cmp_v_w1[l], cmp_v_w2[l], cmp_pos_v[l])
    kcmp, vcmp = _compress(kc.reshape(bsz, n_chunk, CMP_STRIDE * KV_W), vc.reshape(bsz, n_chunk, CMP_STRIDE * KV_W),
                           w1k, w2k, pk, w1v, w2v, pv, row(jnp.tile(nsa_g_kc[l], NSA_KV)),
                           ones64[:KV_W, :KV_W])
    padw = lambda t: jnp.pad(t.reshape(bsz, seq, KV_W), ((0, 0), (WINDOW, 0), (0, 0)))
    bias_c, bias_w, bias_s, bias_far = _bias_tables(rel_bias, seq)
    overlap_t, penalty = _selection_tables(seq)
    ksx = jnp.concatenate([padw(ks), jnp.broadcast_to(penalty, (bsz,) + penalty.shape)], axis=-1)
    y_nsa = _nsa(q.reshape(bsz, seq, NSA_WIDTH), gates.reshape(bsz, seq, LANES), kcmp, vcmp,
                 ksx, padw(vs), padw(kw), padw(vw), overlap_t, bias_c, bias_w, bias_s, bias_far,
                 row(out_g_nsa[l][perm]))

    kx, vx = _memkv(mem, row(norm_mem[l]), xa_w_kv[l].astype(BF16), row(xa_g_k[l]))
    wo_mix = w_out[l]
    wr = jnp.pad(jnp.concatenate([router_g_w[l], router_e_w[l]], axis=1),
                 ((0, 0), (0, LANES - N_GROUPS - N_EXPERTS)))
    wr_hi = wr.astype(BF16)
    br = jnp.pad(jnp.concatenate([router_g_b[l], router_e_b[l]]), (0, LANES - N_GROUPS - N_EXPERTS)).reshape(1, -1)
    h2, xt, rw, ri, counts = _mid(
        x, y_rg, y_nsa, wo_mix[:RG_WIDTH].astype(BF16), wo_mix[RG_WIDTH:][perm].astype(BF16), row(norm_x[l]),
        xa_w_q[l].astype(BF16), row(xa_g_q[l] * X_HEAD_DIM ** -0.5), kx, vx, xa_w_o[l].astype(BF16),
        row(norm_moe[l]), wr_hi, (wr - wr_hi.astype(F32)).astype(BF16), br)

    n_slots = 2 * n_tok
    n_blocks = n_slots // MOE_TB + N_EXPERTS
    n_pad = n_blocks * MOE_TB
    cnt = counts[0, :N_EXPERTS].astype(jnp.int32)
    pcnt = (cnt + MOE_TB - 1) // MOE_TB * MOE_TB
    pends = jnp.cumsum(pcnt)
    pstart = jnp.pad((pends - pcnt).astype(F32), (0, LANES - N_EXPERTS)).reshape(1, LANES)
    blk_exp = jnp.minimum(jnp.sum(pends[None, :] <= jnp.arange(n_blocks, dtype=jnp.int32)[:, None] * MOE_TB, axis=1),
                          N_EXPERTS - 1).astype(jnp.int32)
    n_used = (pends[-1:] // MOE_TB).astype(jnp.int32)
    dest = _dest(ri.reshape(n_tok, LANES), pstart)
    tmd = min(TM_DMA, n_tok)
    da = dest[:, 0].reshape(n_tok // tmd, 1, tmd)
    db = dest[:, 1].reshape(n_tok // tmd, 1, tmd)
    xs = _dispatch(da, db, xt, jnp.zeros((n_pad * ROW_TILE, LANES), F32))
    ys = _ffn(blk_exp, n_used, xs, exp_w1[l].astype(BF16), exp_w3[l].astype(BF16), exp_w2[l].astype(BF16))
    out = _combine(da, db, h2.reshape(n_tok, D_MODEL), rw.reshape(n_tok, LANES), ys)
    return out.reshape(bsz, seq, D_MODEL)
```

```python
import math

import numpy as np
import jax
import jax.numpy as jnp
from jax import lax
from jax.experimental import pallas as pl
from jax.experimental.pallas import tpu as pltpu

F32 = jnp.float32
BF16 = jnp.bfloat16

D_MODEL = 1024
RG_WIDTH = 512
RG_BLOCKS = 8
RG_BLOCK = 64
CONV_W = 4
RG_C = 8.0
NSA_WIDTH = 512
NSA_HEADS = 8
HEAD_DIM = 64
NSA_KV = 2
NSA_HPG = 4
KV_W = 128
CMP_L = 32
CMP_STRIDE = 16
CMP_HIDDEN = 256
SEL_L = 64
N_SEL = 8
WINDOW = 512
NUM_BUCKETS = 32
MAX_DIST = 128
X_HEADS = 4
X_HEAD_DIM = 256
N_GROUPS = 4
EXP_PER_GROUP = 8
N_EXPERTS = 32
D_EXPERT = 512
EPS = 1e-6
NEG_INF = -1e30
MASKED_BELOW = -1e29
SEL_FORCE = 1e9
LANES = 128

TQ = 64
NEAR = WINDOW + TQ
FAR_TK = 512
QROWS2 = NSA_HEADS * TQ
UNSEL_PENALTY = 2.0 ** 100
NSA_NB = 4

TM_PROJ = 1024
TM_MID = 512
TM_DEST = 512
TM_DMA = 512
DMA_UNROLL = 8
MOE_TB = 512
ROW_TILE = D_MODEL // LANES
RG_CHUNK = 256
SCAN_ROWS = 8
SCAN_UNROLL = 4
VMEM_LIMIT = 56 * 1024 * 1024


def _cparams(n_axes):
    return pltpu.CompilerParams(dimension_semantics=("arbitrary",) * n_axes,
                                vmem_limit_bytes=VMEM_LIMIT)


def _dot(a, b):
    return jnp.dot(a, b, preferred_element_type=F32)


def _dot_nt(a, b):
    return lax.dot_general(a, b, (((1,), (1,)), ((), ())), preferred_element_type=F32)


def _gelu_tanh(x):
    return 0.5 * x * (1.0 + jnp.tanh(math.sqrt(2.0 / math.pi) * (x + 0.044715 * (x * x * x))))


def _sigmoid(x):
    return 0.5 * jnp.tanh(0.5 * x) + 0.5


def _rms(x, g):
    return x * lax.rsqrt(jnp.mean(x * x, axis=-1, keepdims=True) + EPS) * g


def _group_rms(x, ones_blk, g):
    ms = _dot((x * x).astype(BF16), ones_blk)
    return x * lax.rsqrt(ms + EPS) * g


def _inproj_kernel(x_ref, g_ref, wrg_ref, wq_ref, wkv_ref, wgl_ref, gq_ref, gks_ref, gkw_ref, ones_ref,
                   u_ref, gate_ref, q_ref, kc_ref, vc_ref, ks_ref, vs_ref, kw_ref, vw_ref, gates_ref):
    xb = _rms(x_ref[...], g_ref[...]).astype(BF16)
    rg = _dot(xb, wrg_ref[...])
    u_ref[...] = rg[:, :RG_WIDTH].astype(BF16)
    gate_ref[...] = rg[:, RG_WIDTH:].astype(BF16)
    q = _dot(xb, wq_ref[...])
    q_ref[...] = _group_rms(q, ones_ref[...], gq_ref[...]).astype(BF16)
    kv = _dot(xb, wkv_ref[...])
    ones_kv = ones_ref[:KV_W, :KV_W]
    kc_ref[...] = kv[:, 0 * KV_W:1 * KV_W].astype(BF16)
    vc_ref[...] = kv[:, 1 * KV_W:2 * KV_W].astype(BF16)
    ks_ref[...] = _group_rms(kv[:, 2 * KV_W:3 * KV_W], ones_kv, gks_ref[...]).astype(BF16)
    vs_ref[...] = kv[:, 3 * KV_W:4 * KV_W].astype(BF16)
    kw_ref[...] = _group_rms(kv[:, 4 * KV_W:5 * KV_W], ones_kv, gkw_ref[...]).astype(BF16)
    vw_ref[...] = kv[:, 5 * KV_W:6 * KV_W].astype(BF16)
    gates_ref[...] = _sigmoid(_dot(xb, wgl_ref[...]))


def _inproj(x2, g, wrg, wq, wkv, wgl, gq, gks, gkw, ones_blk):
    n_tok = x2.shape[0]
    tm = min(TM_PROJ, n_tok)
    full = lambda a: pl.BlockSpec(a.shape, lambda i: (0,) * a.ndim)
    row = lambda w: pl.BlockSpec((tm, w), lambda i: (i, 0))
    outs = [(RG_WIDTH, BF16), (RG_WIDTH, BF16), (NSA_WIDTH, BF16)] + [(KV_W, BF16)] * 6 + [(LANES, F32)]
    return pl.pallas_call(
        _inproj_kernel,
        grid=(n_tok // tm,),
        in_specs=[row(D_MODEL)] + [full(a) for a in (g, wrg, wq, wkv, wgl, gq, gks, gkw, ones_blk)],
        out_specs=[row(w) for w, _ in outs],
        out_shape=[jax.ShapeDtypeStruct((n_tok, w), dt) for w, dt in outs],
        compiler_params=_cparams(1),
    )(x2, g, wrg, wq, wkv, wgl, gq, gks, gkw, ones_blk)


def _rglru_kernel(u_ref, gate_ref, cw_ref, cb_ref, wg_ref, bg_ref, lam_ref, og_ref, y_ref, upad, a_s, h_s):
    seq = u_ref.shape[1]
    upad[0:8, :] = jnp.zeros((8, RG_WIDTH), F32)
    upad[8:8 + seq, :] = u_ref[0].astype(F32)
    neg_lam = -lam_ref[...]
    softplus = jnp.maximum(neg_lam, 0.0) + jnp.log(1.0 + jnp.exp(-jnp.abs(neg_lam)))
    ch = min(RG_CHUNK, seq)
    for c in range(seq // ch):
        r0 = c * ch
        uc = cb_ref[...]
        for k in range(CONV_W):
            off = 8 + r0 - (CONV_W - 1) + k
            uc = uc + cw_ref[k:k + 1, :] * upad[off:off + ch, :]
        gt = _dot(uc.astype(BF16), wg_ref[...]) + bg_ref[...]
        a = jnp.exp((-RG_C) * _sigmoid(gt[:, :RG_WIDTH]) * softplus)
        a_s[r0:r0 + ch, :] = a
        s = 1.0 - a * a
        h_s[r0:r0 + ch, :] = s * lax.rsqrt(jnp.maximum(s, 1e-30)) * _sigmoid(gt[:, RG_WIDTH:]) * uc

    row = lax.broadcasted_iota(jnp.int32, (SCAN_ROWS, RG_WIDTH), 0)

    def block(j, h_prev):
        rows = pl.ds(pl.multiple_of(j * SCAN_ROWS, SCAN_ROWS), SCAN_ROWS)
        a = a_s[rows, :]
        b = h_s[rows, :]
        k = 1
        while k < SCAN_ROWS:
            keep = row >= k
            b = jnp.where(keep, a * pltpu.roll(b, k, 0) + b, b)
            a = jnp.where(keep, a * pltpu.roll(a, k, 0), a)
            k *= 2
        h = a * h_prev + b
        h_s[rows, :] = h
        return h[SCAN_ROWS - 1:SCAN_ROWS, :]

    lax.fori_loop(0, seq // SCAN_ROWS, block, jnp.zeros((1, RG_WIDTH), F32), unroll=SCAN_UNROLL)

    for c in range(seq // ch):
        r0 = c * ch
        y = _gelu_tanh(gate_ref[0, r0:r0 + ch, :].astype(F32)) * h_s[r0:r0 + ch, :]
        y_ref[0, r0:r0 + ch, :] = _rms(y, og_ref[...]).astype(BF16)


def _rglru(u3, gate3, cw, cb, wg, bg, lam, og):
    bsz, seq, _ = u3.shape
    full = lambda a: pl.BlockSpec(a.shape, lambda b: (0,) * a.ndim)
    blk = pl.BlockSpec((1, seq, RG_WIDTH), lambda b: (b, 0, 0))
    return pl.pallas_call(
        _rglru_kernel,
        grid=(bsz,),
        in_specs=[blk, blk] + [full(a) for a in (cw, cb, wg, bg, lam, og)],
        out_specs=blk,
        out_shape=jax.ShapeDtypeStruct((bsz, seq, RG_WIDTH), BF16),
        scratch_shapes=[pltpu.VMEM((seq + 8, RG_WIDTH), F32), pltpu.VMEM((seq, RG_WIDTH), F32),
                        pltpu.VMEM((seq, RG_WIDTH), F32)],
        compiler_params=_cparams(1),
    )(u3, gate3, cw, cb, wg, bg, lam, og)


def _compress_kernel(kx_ref, vx_ref, w1k_ref, w2k_ref, pk_ref, w1v_ref, w2v_ref, pv_ref, gk_ref, ones_ref,
                     ko_ref, vo_ref):
    n_chunk = kx_ref.shape[1]
    half = NSA_KV * CMP_HIDDEN

    def mlp(x_ref, w1_ref, w2_ref, p_ref):
        ab = _dot(x_ref[0], w1_ref[...])
        pos = _dot(p_ref[...], w1_ref[...])
        hid = ab[:, :half] + pltpu.roll(ab[:, half:], n_chunk - 1, 0) + (pos[0:1, :half] + pos[1:2, half:])
        return _dot(_gelu_tanh(hid).astype(BF16), w2_ref[...])

    kc = mlp(kx_ref, w1k_ref, w2k_ref, pk_ref)
    ko_ref[0] = _group_rms(kc, ones_ref[...], gk_ref[...]).astype(BF16)
    vo_ref[0] = mlp(vx_ref, w1v_ref, w2v_ref, pv_ref).astype(BF16)


def _compress(kx, vx, w1k, w2k, pk, w1v, w2v, pv, gk, ones_kv):
    bsz, n_chunk, width = kx.shape
    full = lambda a: pl.BlockSpec(a.shape, lambda b: (0,) * a.ndim)
    xin = pl.BlockSpec((1, n_chunk, width), lambda b: (b, 0, 0))
    out = pl.BlockSpec((1, n_chunk, KV_W), lambda b: (b, 0, 0))
    return pl.pallas_call(
        _compress_kernel,
        grid=(bsz,),
        in_specs=[xin, xin] + [full(a) for a in (w1k, w2k, pk, w1v, w2v, pv, gk, ones_kv)],
        out_specs=[out, out],
        out_shape=[jax.ShapeDtypeStruct((bsz, n_chunk, KV_W), BF16)] * 2,
        compiler_params=_cparams(1),
    )(kx, vx, w1k, w2k, pk, w1v, w2v, pv, gk, ones_kv)


def _nsa_kernel(q_ref, gates_ref, kcmp_ref, vcmp_ref, ksx_ref, vsp_ref, kwp_ref, vwp_ref, ovt_ref,
                bc_ref, bw_ref, bs_ref, bf_ref, og_ref, y_ref):
    i = pl.program_id(1)
    t0 = pl.multiple_of(i * TQ, TQ)
    n_blk = ovt_ref.shape[0]
    lane = lax.broadcasted_iota(jnp.int32, (TQ, LANES), 1)
    lo_half = lane < HEAD_DIM
    n_batch = q_ref.shape[0]

    def near_part(bb):
        pieces = []
        for p in range(NSA_HPG):
            qs = q_ref[bb, :, p * LANES:(p + 1) * LANES]
            zero = jnp.zeros_like(qs)
            pieces += [jnp.where(lo_half, qs, zero), jnp.where(lo_half, zero, qs)]
        q8 = jnp.concatenate(pieces, axis=0)

        bc = bc_ref[0]
        lc = _dot_nt(q8, kcmp_ref[bb]) + bc
        ec = jnp.where(bc > MASKED_BELOW, jnp.exp(lc - jnp.max(lc, axis=-1, keepdims=True)), 0.0)
        sc = jnp.sum(ec, axis=-1, keepdims=True)
        pc = ec / jnp.where(sc > 0.0, sc, 1.0)
        o_c = _dot(pc.astype(BF16), vcmp_ref[bb])

        blocks = [pc[r * TQ:(r + 1) * TQ] for r in range(NSA_HPG * NSA_KV)]
        pcs = jnp.concatenate([sum(blocks[g::NSA_KV]) for g in range(NSA_KV)], axis=0)
        pcs_hi = pcs.astype(BF16)
        pcs_lo = (pcs - pcs_hi.astype(F32)).astype(BF16)
        imp = _dot_nt(ovt_ref[...], pcs_hi) + _dot_nt(ovt_ref[...], pcs_lo)
        blk = lax.broadcasted_iota(jnp.int32, imp.shape, 0)
        forced = (blk == 0) | (blk == i) | (blk == i - 1)
        score = jnp.where(forced, SEL_FORCE, jnp.where(blk > i, -3e38, imp))
        rank = jnp.zeros(imp.shape, F32)
        for m in range(n_blk):
            row = score[m:m + 1, :]
            rank = rank + jnp.where(blk > m, jnp.where(row >= score, 1.0, 0.0), jnp.where(row > score, 1.0, 0.0))
        unsel = jnp.where(rank < N_SEL, 0.0, 1.0)
        unsel_far = jnp.where(blk >= i - WINDOW // SEL_L, 1.0, unsel)
        pad = jnp.zeros((LANES - 2 * n_blk, imp.shape[1]), F32)
        u_t = jnp.concatenate([unsel, unsel_far, pad], axis=0).T
        u_lane = lax.broadcasted_iota(jnp.int32, u_t.shape, 1)
        u_near = jnp.where(u_lane < n_blk, u_t, 0.0).astype(BF16)
        u_far = jnp.where(u_lane >= n_blk, u_t, 0.0).astype(BF16)
        qx_near = jnp.concatenate([q8, jnp.concatenate([u_near] * NSA_HPG, axis=0)], axis=1)
        qx_far = jnp.concatenate([q8, jnp.concatenate([u_far] * NSA_HPG, axis=0)], axis=1)

        lw = _dot_nt(q8, kwp_ref[bb, pl.ds(t0, NEAR), :]) + bw_ref[0]
        ew = jnp.exp(lw - jnp.max(lw, axis=-1, keepdims=True))
        o_w = _dot(ew.astype(BF16), vwp_ref[bb, pl.ds(t0, NEAR), :]) / jnp.sum(ew, axis=-1, keepdims=True)

        ls = _dot_nt(qx_near, ksx_ref[bb, pl.ds(t0, NEAR), :]) + bs_ref[0]
        m1 = jnp.max(ls, axis=-1, keepdims=True)
        e1 = jnp.exp(ls - m1)
        l1 = jnp.sum(e1, axis=-1, keepdims=True)
        acc1 = _dot(e1.astype(BF16), vsp_ref[bb, pl.ds(t0, NEAR), :])
        return o_c, o_w, qx_far, (m1, l1, acc1)

    near = [near_part(bb) for bb in range(n_batch)]
    bfar = bf_ref[...]

    def far_step(kf, carry):
        base = pl.multiple_of(WINDOW + kf * FAR_TK, FAR_TK)
        new = []
        for bb in range(n_batch):
            m, l, acc = carry[bb]
            lf = _dot_nt(near[bb][2], ksx_ref[bb, pl.ds(base, FAR_TK), :]) + bfar
            m_new = jnp.maximum(m, jnp.max(lf, axis=-1, keepdims=True))
            alpha = jnp.exp(m - m_new)
            e = jnp.exp(lf - m_new)
            l_new = alpha * l + jnp.sum(e, axis=-1, keepdims=True)
            new.append((m_new, l_new, alpha * acc + _dot(e.astype(BF16), vsp_ref[bb, pl.ds(base, FAR_TK), :])))
        return tuple(new)

    n_far = (jnp.maximum(t0 - WINDOW, 0) + FAR_TK - 1) // FAR_TK
    far = lax.fori_loop(0, n_far, far_step, tuple(part[3] for part in near))

    for bb in range(n_batch):
        o_c, o_w = near[bb][0], near[bb][1]
        _, l_s, acc_s = far[bb]
        o_s = acc_s / l_s
        gates = gates_ref[bb]

        def gate_col(j):
            cols = [gates[:, (g * NSA_HPG + p) * 3 + j:(g * NSA_HPG + p) * 3 + j + 1]
                    for p in range(NSA_HPG) for g in range(NSA_KV)]
            return jnp.concatenate(cols, axis=0)

        out = gate_col(0) * o_c + gate_col(1) * o_s + gate_col(2) * o_w
        slabs = [jnp.where(lo_half, out[(2 * p) * TQ:(2 * p + 1) * TQ], out[(2 * p + 1) * TQ:(2 * p + 2) * TQ])
                 for p in range(NSA_HPG)]
        y_ref[bb] = _rms(jnp.concatenate(slabs, axis=-1), og_ref[...]).astype(BF16)


def _nsa(q3, gates3, kcmp, vcmp, ksx, vsp, kwp, vwp, ovt, bias_c, bias_w, bias_s, bias_far, og):
    bsz, seq, _ = q3.shape
    n_chunk = kcmp.shape[1]
    n_var = bias_w.shape[0] - 1
    nb = NSA_NB if bsz % NSA_NB == 0 else 1
    full = lambda a: pl.BlockSpec(a.shape, lambda b, i: (0,) * a.ndim)
    per_b = lambda a: pl.BlockSpec((nb,) + a.shape[1:], lambda b, i: (b,) + (0,) * (a.ndim - 1))
    near = pl.BlockSpec((1, QROWS2, NEAR), lambda b, i: (jnp.minimum(i, n_var), 0, 0))
    return pl.pallas_call(
        _nsa_kernel,
        grid=(bsz // nb, seq // TQ),
        in_specs=[pl.BlockSpec((nb, TQ, NSA_WIDTH), lambda b, i: (b, i, 0)),
                  pl.BlockSpec((nb, TQ, LANES), lambda b, i: (b, i, 0)),
                  per_b(kcmp), per_b(vcmp), per_b(ksx), per_b(vsp), per_b(kwp), per_b(vwp),
                  full(ovt),
                  pl.BlockSpec((1, QROWS2, n_chunk), lambda b, i: (i, 0, 0)),
                  near, near, full(bias_far), full(og)],
        out_specs=pl.BlockSpec((nb, TQ, NSA_WIDTH), lambda b, i: (b, i, 0)),
        out_shape=jax.ShapeDtypeStruct((bsz, seq, NSA_WIDTH), BF16),
        compiler_params=_cparams(2),
    )(q3, gates3, kcmp, vcmp, ksx, vsp, kwp, vwp, ovt, bias_c, bias_w, bias_s, bias_far, og)


def _memkv_kernel(mem_ref, g_ref, wkv_ref, gk_ref, k_ref, v_ref):
    mn = _rms(mem_ref[0], g_ref[...]).astype(BF16)
    kv = _dot(mn, wkv_ref[...])
    for h in range(X_HEADS):
        sl = slice(h * X_HEAD_DIM, (h + 1) * X_HEAD_DIM)
        k_ref[0, :, sl] = _rms(kv[:, sl], gk_ref[...]).astype(BF16)
    v_ref[0] = kv[:, D_MODEL:].astype(BF16)


def _memkv(mem, g, wkv, gk):
    bsz, mlen, _ = mem.shape
    full = lambda a: pl.BlockSpec(a.shape, lambda b: (0,) * a.ndim)
    blk = pl.BlockSpec((1, mlen, D_MODEL), lambda b: (b, 0, 0))
    return pl.pallas_call(
        _memkv_kernel,
        grid=(bsz,),
        in_specs=[blk, full(g), full(wkv), full(gk)],
        out_specs=[blk, blk],
        out_shape=[jax.ShapeDtypeStruct((bsz, mlen, D_MODEL), BF16)] * 2,
        compiler_params=_cparams(1),
    )(mem, g, wkv, gk)


def _mid_kernel(x_ref, yrg_ref, ynsa_ref, woa_ref, wob_ref, gx_ref, wq_ref, gq_ref, k_ref, v_ref, wo_ref,
                gm_ref, wrh_ref, wrl_ref, br_ref, h_ref, xt_ref, rw_ref, ri_ref, cnt_ref):
    h1 = x_ref[0] + _dot(yrg_ref[0], woa_ref[...]) + _dot(ynsa_ref[0], wob_ref[...])

    q = _dot(_rms(h1, gx_ref[...]).astype(BF16), wq_ref[...])
    heads = []
    for h in range(X_HEADS):
        sl = slice(h * X_HEAD_DIM, (h + 1) * X_HEAD_DIM)
        qh = _rms(q[:, sl], gq_ref[...]).astype(BF16)
        lg = _dot_nt(qh, k_ref[0, :, sl])
        e = jnp.exp(lg - jnp.max(lg, axis=-1, keepdims=True))
        heads.append(_dot(e.astype(BF16), v_ref[0, :, sl]) / jnp.sum(e, axis=-1, keepdims=True))
    h2 = h1 + _dot(jnp.concatenate(heads, axis=-1).astype(BF16), wo_ref[...])
    h_ref[0] = h2

    xt = _rms(h2, gm_ref[...])
    _store_row_tiles(xt_ref, xt)
    xt_hi = xt.astype(BF16)
    xt_lo = (xt - xt_hi.astype(F32)).astype(BF16)
    lg = _dot(xt_hi, wrh_ref[...]) + _dot(xt_lo, wrh_ref[...]) + _dot(xt_hi, wrl_ref[...]) + br_ref[...]
    lane = lax.broadcasted_iota(jnp.int32, lg.shape, 1)
    lane_f = lane.astype(F32)
    first_of = lambda hit: jnp.min(jnp.where(hit, lane_f, 1e9), axis=-1, keepdims=True)
    glog = jnp.where(lane < N_GROUPS, lg, -3e38)
    gmax = jnp.max(glog, axis=-1, keepdims=True)
    gsel = first_of(glog == gmax)
    p_g = 1.0 / jnp.sum(jnp.exp(glog - gmax), axis=-1, keepdims=True)
    lo = N_GROUPS + EXP_PER_GROUP * gsel
    el = jnp.where((lane_f >= lo) & (lane_f < lo + EXP_PER_GROUP), lg, -3e38)
    m_a = jnp.max(el, axis=-1, keepdims=True)
    i_a = first_of(el == m_a)
    el2 = jnp.where(lane_f == i_a, -3e38, el)
    m_b = jnp.max(el2, axis=-1, keepdims=True)
    i_b = first_of(el2 == m_b)
    r = jnp.exp(m_b - m_a)
    w_a = p_g / (1.0 + r)
    w_b = p_g * r / (1.0 + r)
    e_a = i_a - N_GROUPS
    e_b = i_b - N_GROUPS
    rw_ref[0] = jnp.where(lane == 0, w_a, jnp.where(lane == 1, w_b, 0.0))
    ri_ref[0] = jnp.where(lane == 0, e_a, jnp.where(lane == 1, e_b, 0.0)).astype(jnp.int32)

    @pl.when((pl.program_id(0) == 0) & (pl.program_id(1) == 0))
    def _():
        cnt_ref[...] = jnp.zeros_like(cnt_ref)

    hot = jnp.where((lane_f == e_a) | (lane_f == e_b), 1.0, 0.0)
    cnt_ref[...] += jnp.sum(hot, axis=0, keepdims=True)


def _mid(x, yrg, ynsa, woa, wob, gx, wq, gq, kx, vx, wo, gm, wrh, wrl, br):
    bsz, seq, _ = x.shape
    tm = min(TM_MID, seq)
    mlen = kx.shape[1]
    n_i = seq // tm
    full = lambda a: pl.BlockSpec(a.shape, lambda b, i: (0,) * a.ndim)
    tok = lambda w: pl.BlockSpec((1, tm, w), lambda b, i: (b, i, 0))
    memb = pl.BlockSpec((1, mlen, D_MODEL), lambda b, i: (b, 0, 0))
    xt_spec = pl.BlockSpec((tm * ROW_TILE, LANES), lambda b, i: (b * n_i + i, 0))
    return pl.pallas_call(
        _mid_kernel,
        grid=(bsz, seq // tm),
        in_specs=[tok(D_MODEL), tok(RG_WIDTH), tok(NSA_WIDTH), full(woa), full(wob), full(gx), full(wq), full(gq),
                  memb, memb, full(wo), full(gm), full(wrh), full(wrl), full(br)],
        out_specs=[tok(D_MODEL), xt_spec, tok(LANES), tok(LANES), pl.BlockSpec((1, LANES), lambda b, i: (0, 0))],
        out_shape=[jax.ShapeDtypeStruct((bsz, seq, D_MODEL), F32),
                   jax.ShapeDtypeStruct((bsz * seq * ROW_TILE, LANES), F32),
                   jax.ShapeDtypeStruct((bsz, seq, LANES), F32), jax.ShapeDtypeStruct((bsz, seq, LANES), jnp.int32),
                   jax.ShapeDtypeStruct((1, LANES), F32)],
        compiler_params=_cparams(2),
    )(x, yrg, ynsa, woa, wob, gx, wq, gq, kx, vx, wo, gm, wrh, wrl, br)


def _dest_kernel(ri_ref, pstart_ref, dest_ref, run_ref):
    @pl.when(pl.program_id(0) == 0)
    def _():
        run_ref[...] = jnp.zeros_like(run_ref)

    ri = ri_ref[...]
    tm = ri.shape[0]
    lane = lax.broadcasted_iota(jnp.int32, ri.shape, 1)
    e_a = ri[:, 0:1]
    e_b = ri[:, 1:2]
    hot_a = lane == e_a
    hot_b = lane == e_b
    hot = jnp.where(hot_a | hot_b, 1.0, 0.0)
    row = lax.broadcasted_iota(jnp.int32, (tm, tm), 0)
    col = lax.broadcasted_iota(jnp.int32, (tm, tm), 1)
    earlier = jnp.where(col < row, 1.0, 0.0).astype(BF16)
    base = _dot(earlier, hot.astype(BF16)) + run_ref[...] + pstart_ref[...]
    d_a = jnp.sum(jnp.where(hot_a, base, 0.0), axis=-1, keepdims=True)
    d_b = jnp.sum(jnp.where(hot_b, base, 0.0), axis=-1, keepdims=True)
    dest_ref[...] = jnp.where(lane == 0, d_a, jnp.where(lane == 1, d_b, 0.0)).astype(jnp.int32)
    run_ref[...] += jnp.sum(hot, axis=0, keepdims=True)


def _dest(ri2, pstart):
    n_tok = ri2.shape[0]
    tm = min(TM_DEST, n_tok)
    return pl.pallas_call(
        _dest_kernel,
        grid=(n_tok // tm,),
        in_specs=[pl.BlockSpec((tm, LANES), lambda i: (i, 0)), pl.BlockSpec((1, LANES), lambda i: (0, 0))],
        out_specs=pl.BlockSpec((tm, LANES), lambda i: (i, 0)),
        out_shape=jax.ShapeDtypeStruct((n_tok, LANES), jnp.int32),
        scratch_shapes=[pltpu.VMEM((1, LANES), F32)],
        compiler_params=_cparams(1),
    )(ri2, pstart)


def _store_row_tiles(ref, val):
    n = val.shape[0]
    for c in range(ROW_TILE):
        ref[pl.ds(c, n, stride=ROW_TILE), :] = val[:, c * LANES:(c + 1) * LANES]


def _load_row_tiles(ref, n):
    return [ref[pl.ds(c, n, stride=ROW_TILE), :] for c in range(ROW_TILE)]


def _token_rows(ref, t):
    return ref.at[pl.ds(pl.multiple_of(t * ROW_TILE, ROW_TILE), ROW_TILE), :]


def _dispatch_kernel(da_ref, db_ref, xt_ref, buf_ref, xs_ref, sem):
    del buf_ref
    tm = da_ref.shape[2]

    def issue(t, c):
        pltpu.make_async_copy(_token_rows(xt_ref, t), _token_rows(xs_ref, da_ref[0, 0, t]), sem).start(priority=0)
        pltpu.make_async_copy(_token_rows(xt_ref, t), _token_rows(xs_ref, db_ref[0, 0, t]), sem).start(priority=1)
        return c

    lax.fori_loop(0, tm, issue, 0, unroll=DMA_UNROLL)

    def drain(t, c):
        pltpu.make_async_copy(_token_rows(xt_ref, 0), _token_rows(xs_ref, 0), sem).wait()
        pltpu.make_async_copy(_token_rows(xt_ref, 0), _token_rows(xs_ref, 0), sem).wait()
        return c

    lax.fori_loop(0, tm, drain, 0, unroll=DMA_UNROLL)


def _dispatch(da, db, xt_rows, buf):
    n_tiles, _, tm = da.shape
    smem = pl.BlockSpec((1, 1, tm), lambda i: (i, 0, 0), memory_space=pltpu.SMEM)
    hbm = pl.BlockSpec(memory_space=pl.ANY)
    return pl.pallas_call(
        _dispatch_kernel,
        grid=(n_tiles,),
        in_specs=[smem, smem, pl.BlockSpec((tm * ROW_TILE, LANES), lambda i: (i, 0)), hbm],
        out_specs=hbm,
        out_shape=jax.ShapeDtypeStruct(buf.shape, buf.dtype),
        scratch_shapes=[pltpu.SemaphoreType.DMA(())],
        input_output_aliases={3: 0},
        compiler_params=pltpu.CompilerParams(dimension_semantics=("arbitrary",), has_side_effects=True,
                                             vmem_limit_bytes=VMEM_LIMIT),
    )(da, db, xt_rows, buf)


def _ffn_kernel(bexp_ref, nused_ref, xs_ref, w1_ref, w3_ref, w2_ref, ys_ref):
    del bexp_ref
    j = pl.program_id(0)

    @pl.when(j < nused_ref[0])
    def _():
        xb = jnp.concatenate(_load_row_tiles(xs_ref, MOE_TB), axis=-1).astype(BF16)
        a = _dot(xb, w1_ref[0])
        h = a * _sigmoid(a) * _dot(xb, w3_ref[0])
        _store_row_tiles(ys_ref, _dot(h.astype(BF16), w2_ref[0]))

    @pl.when(j >= nused_ref[0])
    def _():
        ys_ref[...] = jnp.zeros_like(ys_ref)


def _ffn(blk_exp, n_used, xs, w1, w3, w2):
    n_blocks = xs.shape[0] // (MOE_TB * ROW_TILE)
    rows = pl.BlockSpec((MOE_TB * ROW_TILE, LANES), lambda j, be, nu: (j, 0))
    grid_spec = pltpu.PrefetchScalarGridSpec(
        num_scalar_prefetch=2,
        grid=(n_blocks,),
        in_specs=[rows,
                  pl.BlockSpec((1, D_MODEL, D_EXPERT), lambda j, be, nu: (be[j], 0, 0)),
                  pl.BlockSpec((1, D_MODEL, D_EXPERT), lambda j, be, nu: (be[j], 0, 0)),
                  pl.BlockSpec((1, D_EXPERT, D_MODEL), lambda j, be, nu: (be[j], 0, 0))],
        out_specs=rows,
    )
    return pl.pallas_call(
        _ffn_kernel,
        grid_spec=grid_spec,
        out_shape=jax.ShapeDtypeStruct(xs.shape, F32),
        compiler_params=_cparams(1),
    )(blk_exp, n_used, xs, w1, w3, w2)


def _combine_kernel(da_ref, db_ref, h_ref, rw_ref, ys_ref, o_ref, ya, yb, sem):
    tm = da_ref.shape[2]

    def issue(t, c):
        pltpu.make_async_copy(_token_rows(ys_ref, da_ref[0, 0, t]), _token_rows(ya, t), sem).start(priority=0)
        pltpu.make_async_copy(_token_rows(ys_ref, db_ref[0, 0, t]), _token_rows(yb, t), sem).start(priority=1)
        return c

    lax.fori_loop(0, tm, issue, 0, unroll=DMA_UNROLL)

    def drain(t, c):
        pltpu.make_async_copy(_token_rows(ys_ref, 0), _token_rows(ya, 0), sem).wait()
        pltpu.make_async_copy(_token_rows(ys_ref, 0), _token_rows(yb, 0), sem).wait()
        return c

    lax.fori_loop(0, tm, drain, 0, unroll=DMA_UNROLL)
    rw = rw_ref[...]
    mix = [rw[:, 0:1] * a + rw[:, 1:2] * b for a, b in zip(_load_row_tiles(ya, tm), _load_row_tiles(yb, tm))]
    o_ref[...] = h_ref[...] + jnp.concatenate(mix, axis=-1)


def _combine(da, db, h2, rw, ys):
    n_tiles, _, tm = da.shape
    n_tok = h2.shape[0]
    smem = pl.BlockSpec((1, 1, tm), lambda i: (i, 0, 0), memory_space=pltpu.SMEM)
    row = lambda w: pl.BlockSpec((tm, w), lambda i: (i, 0))
    return pl.pallas_call(
        _combine_kernel,
        grid=(n_tiles,),
        in_specs=[smem, smem, row(D_MODEL), row(LANES), pl.BlockSpec(memory_space=pl.ANY)],
        out_specs=row(D_MODEL),
        out_shape=jax.ShapeDtypeStruct((n_tok, D_MODEL), F32),
        scratch_shapes=[pltpu.VMEM((tm * ROW_TILE, LANES), F32), pltpu.VMEM((tm * ROW_TILE, LANES), F32),
                        pltpu.SemaphoreType.DMA(())],
        compiler_params=_cparams(1),
    )(da, db, h2, rw, ys)


def _rel_bucket_np(dist):
    n = np.maximum(dist, 0)
    max_exact = NUM_BUCKETS // 2
    nf = np.maximum(n, 1).astype(np.float32)
    large = max_exact + (np.log(nf / max_exact) / math.log(MAX_DIST / max_exact)
                         * (NUM_BUCKETS - max_exact)).astype(np.int32)
    large = np.minimum(large, NUM_BUCKETS - 1)
    return np.where(n < max_exact, n, large).astype(np.int32)


def _toeplitz(vec, rows):
    width = vec.shape[-1] - 1
    flat = jnp.tile(vec, (1,) * (vec.ndim - 1) + (rows,))[..., :rows * width]
    return flat.reshape(vec.shape[:-1] + (rows, width))


def _bias_tables(rel_bias, seq):
    n_chunk = seq // CMP_STRIDE
    n_tiles = seq // TQ
    table = rel_bias.T.astype(F32)

    wide = NEAR + TQ
    k = np.arange(wide + 1)
    dw = np.where(k < NEAR, WINDOW - k, WINDOW + wide + 1 - k)
    used = (k < NEAR) | (k > wide + 1 - TQ)
    vals = table[:, _rel_bucket_np(dw)]

    n_var = WINDOW // TQ
    first_key = WINDOW - TQ * np.arange(n_var + 1)[:, None, None]
    in_seq = np.arange(NEAR)[None, None, :] >= first_key

    def near_tile(valid):
        t = _toeplitz(jnp.where(valid[None, :], vals, NEG_INF), TQ)[:, :, :NEAR]
        t = t.reshape(NSA_KV, NSA_HPG, TQ, NEAR).transpose(1, 0, 2, 3).reshape(1, QROWS2, NEAR)
        return jnp.where(in_seq, t, NEG_INF)

    bias_w = near_tile(used & (dw >= 0) & (dw < WINDOW))
    bias_s = near_tile(used & (dw >= 0))
    bias_far = table[:, NUM_BUCKETS - 1].reshape(NSA_KV, NSA_HPG, 1).transpose(1, 0, 2)
    bias_far = jnp.broadcast_to(bias_far, (NSA_HPG, NSA_KV, TQ)).reshape(QROWS2, 1)

    r = np.arange(CMP_STRIDE)[:, None]
    k = np.arange(2 * n_chunk + 1)[None, :]
    lag = 2 * n_chunk + 1 - k
    valid = (k > n_chunk + 1) & (CMP_STRIDE * lag + r >= CMP_L - 1)
    vals = table[:, _rel_bucket_np(CMP_STRIDE * lag + r - CMP_L // 2)]
    full = _toeplitz(jnp.where(valid[None], vals, NEG_INF), n_chunk)[..., :n_chunk]
    full = jnp.where(np.arange(n_chunk) < n_chunk - 1, full, NEG_INF)
    a4 = TQ // CMP_STRIDE
    full = full.reshape(NSA_KV, NSA_HPG, CMP_STRIDE, n_tiles, a4, n_chunk).transpose(3, 1, 0, 4, 2, 5)
    bias_c = full.reshape(n_tiles, QROWS2, n_chunk)
    return bias_c, bias_w, bias_s, bias_far


def _selection_tables(seq):
    n_chunk = seq // CMP_STRIDE
    n_blk = seq // SEL_L
    c = np.arange(n_chunk)
    n = np.arange(n_blk)
    start = c * CMP_STRIDE
    overlap_t = ((start[None, :] <= n[:, None] * SEL_L + SEL_L - 1) & (start[None, :] + CMP_L - 1 >= n[:, None] * SEL_L)
                 & (c < n_chunk - 1)[None, :])
    pos = np.arange(seq + WINDOW) - WINDOW
    lane_blk = np.arange(LANES) % n_blk
    hit = (pos[:, None] >= 0) & (pos[:, None] // SEL_L == lane_blk[None, :]) & (np.arange(LANES) < 2 * n_blk)[None, :]
    return jnp.asarray(overlap_t, BF16), jnp.asarray(np.where(hit, -UNSEL_PENALTY, 0.0), BF16)


def _block_ones(width, group):
    idx = np.arange(width) // group
    return jnp.asarray((idx[:, None] == idx[None, :]) / group, BF16)


def _block_diag(w):
    nb, n, m = w.shape
    eye = jnp.eye(nb, dtype=w.dtype)
    return jnp.einsum('hij,hg->higj', w, eye).reshape(nb * n, nb * m)


def _compress_weights(w1, w2, pos):
    half_l = CMP_L // 2
    parts = []
    for half in range(2):
        wh = w1[half * half_l * HEAD_DIM:(half + 1) * half_l * HEAD_DIM].reshape(half_l, HEAD_DIM, CMP_HIDDEN)
        z = jnp.zeros_like(wh)
        for g in range(NSA_KV):
            grp = [wh if gg == g else z for gg in range(NSA_KV)]
            parts.append(jnp.stack(grp, axis=1).reshape(half_l * KV_W, CMP_HIDDEN))
    w1cat = jnp.concatenate(parts, axis=1).astype(BF16)
    w2bd = _block_diag(jnp.stack([w2] * NSA_KV)).astype(BF16)
    prow = [jnp.tile(pos[half * half_l:(half + 1) * half_l][:, None, :], (1, NSA_KV, 1)).reshape(-1)
            for half in range(2)]
    pmat = jnp.zeros((8, half_l * KV_W), F32).at[0].set(prow[0]).at[1].set(prow[1]).astype(BF16)
    return w1cat, w2bd, pmat


def kernel(x, mem, rel_bias, norm_mix, w_in, rg_conv_w, rg_conv_b, rg_w_r, rg_b_r, rg_w_i, rg_b_i, rg_lambda, nsa_g_q, nsa_g_kc, nsa_g_ks, nsa_g_kw, cmp_pos_k, cmp_pos_v, cmp_k_w1, cmp_k_w2, cmp_v_w1, cmp_v_w2, out_g_rg, out_g_nsa, w_out, norm_x, norm_mem, xa_w_q, xa_w_kv, xa_w_o, xa_g_q, xa_g_k, norm_moe, router_g_w, router_g_b, router_e_w, router_e_b, exp_w1, exp_w3, exp_w2):
    bsz, seq, _ = x.shape
    n_tok = bsz * seq
    assert seq % FAR_TK == 0 and 2 * (seq // SEL_L) <= LANES and norm_mix.shape[0] == 1
    l = 0
    row = lambda v: v.reshape(1, -1).astype(F32)

    perm = np.array([(half * NSA_HPG + p) * HEAD_DIM + d
                     for p in range(NSA_HPG) for half in range(NSA_KV) for d in range(HEAD_DIM)])
    offs = np.cumsum([0, RG_WIDTH, RG_WIDTH, NSA_WIDTH] + [KV_W] * 6)
    w = w_in[l]
    wrg = w[:, :offs[2]].astype(BF16)
    wq = w[:, offs[2]:offs[3]][:, perm].astype(BF16)
    wkv = w[:, offs[3]:offs[9]].astype(BF16)
    wgl = jnp.pad(w[:, offs[9]:], ((0, 0), (0, LANES - 3 * NSA_HEADS))).astype(BF16)
    ones64 = _block_ones(NSA_WIDTH, HEAD_DIM)
    gq = row(jnp.tile(nsa_g_q[l], NSA_HEADS) * HEAD_DIM ** -0.5)
    u, gate, q, kc, vc, ks, vs, kw, vw, gates = _inproj(
        x.reshape(n_tok, D_MODEL), row(norm_mix[l]), wrg, wq, wkv, wgl, gq,
        row(jnp.tile(nsa_g_ks[l], NSA_KV)), row(jnp.tile(nsa_g_kw[l], NSA_KV)), ones64)

    wg = jnp.concatenate([_block_diag(rg_w_r[l]), _block_diag(rg_w_i[l])], axis=1).astype(BF16)
    bg = jnp.concatenate([rg_b_r[l], rg_b_i[l]]).reshape(1, -1)
    y_rg = _rglru(u.reshape(bsz, seq, RG_WIDTH), gate.reshape(bsz, seq, RG_WIDTH),
                  rg_conv_w[l].reshape(CONV_W, RG_WIDTH), row(rg_conv_b[l]), wg, bg, row(rg_lambda[l]),
                  row(out_g_rg[l]))

    n_chunk = seq // CMP_STRIDE
    w1k, w2k, pk = _compress_weights(cmp_k_w1[l], cmp_k_w2[l], cmp_pos_k[l])
    w1v, w2v, pv = _compress_weights(cmp_v_w1[l], cmp_v_w2[l], cmp_pos_v[l])
    kcmp, vcmp = _compress(kc.reshape(bsz, n_chunk, CMP_STRIDE * KV_W), vc.reshape(bsz, n_chunk, CMP_STRIDE * KV_W),
                           w1k, w2k, pk, w1v, w2v, pv, row(jnp.tile(nsa_g_kc[l], NSA_KV)),
                           ones64[:KV_W, :KV_W])
    padw = lambda t: jnp.pad(t.reshape(bsz, seq, KV_W), ((0, 0), (WINDOW, 0), (0, 0)))
    bias_c, bias_w, bias_s, bias_far = _bias_tables(rel_bias, seq)
    overlap_t, penalty = _selection_tables(seq)
    ksx = jnp.concatenate([padw(ks), jnp.broadcast_to(penalty, (bsz,) + penalty.shape)], axis=-1)
    y_nsa = _nsa(q.reshape(bsz, seq, NSA_WIDTH), gates.reshape(bsz, seq, LANES), kcmp, vcmp,
                 ksx, padw(vs), padw(kw), padw(vw), overlap_t, bias_c, bias_w, bias_s, bias_far,
                 row(out_g_nsa[l][perm]))

    kx, vx = _memkv(mem, row(norm_mem[l]), xa_w_kv[l].astype(BF16), row(xa_g_k[l]))
    wo_mix = w_out[l]
    wr = jnp.pad(jnp.concatenate([router_g_w[l], router_e_w[l]], axis=1),
                 ((0, 0), (0, LANES - N_GROUPS - N_EXPERTS)))
    wr_hi = wr.astype(BF16)
    br = jnp.pad(jnp.concatenate([router_g_b[l], router_e_b[l]]), (0, LANES - N_GROUPS - N_EXPERTS)).reshape(1, -1)
    h2, xt, rw, ri, counts = _mid(
        x, y_rg, y_nsa, wo_mix[:RG_WIDTH].astype(BF16), wo_mix[RG_WIDTH:][perm].astype(BF16), row(norm_x[l]),
        xa_w_q[l].astype(BF16), row(xa_g_q[l] * X_HEAD_DIM ** -0.5), kx, vx, xa_w_o[l].astype(BF16),
        row(norm_moe[l]), wr_hi, (wr - wr_hi.astype(F32)).astype(BF16), br)

    n_slots = 2 * n_tok
    n_blocks = n_slots // MOE_TB + N_EXPERTS
    n_pad = n_blocks * MOE_TB
    cnt = counts[0, :N_EXPERTS].astype(jnp.int32)
    pcnt = (cnt + MOE_TB - 1) // MOE_TB * MOE_TB
    pends = jnp.cumsum(pcnt)
    pstart = jnp.pad((pends - pcnt).astype(F32), (0, LANES - N_EXPERTS)).reshape(1, LANES)
    blk_exp = jnp.minimum(jnp.sum(pends[None, :] <= jnp.arange(n_blocks, dtype=jnp.int32)[:, None] * MOE_TB, axis=1),
                          N_EXPERTS - 1).astype(jnp.int32)
    n_used = (pends[-1:] // MOE_TB).astype(jnp.int32)
    dest = _dest(ri.reshape(n_tok, LANES), pstart)
    tmd = min(TM_DMA, n_tok)
    da = dest[:, 0].reshape(n_tok // tmd, 1, tmd)
    db = dest[:, 1].reshape(n_tok // tmd, 1, tmd)
    xs = _dispatch(da, db, xt, jnp.zeros((n_pad * ROW_TILE, LANES), F32))
    ys = _ffn(blk_exp, n_used, xs, exp_w1[l].astype(BF16), exp_w3[l].astype(BF16), exp_w2[l].astype(BF16))
    out = _combine(da, db, h2.reshape(n_tok, D_MODEL), rw.reshape(n_tok, LANES), ys)
    return out.reshape(bsz, seq, D_MODEL)
```

```python
import math

import numpy as np
import jax
import jax.numpy as jnp
from jax import lax
from jax.experimental import pallas as pl
from jax.experimental.pallas import tpu as pltpu

F32 = jnp.float32
BF16 = jnp.bfloat16

D_MODEL = 1024
RG_WIDTH = 512
RG_BLOCKS = 8
RG_BLOCK = 64
CONV_W = 4
RG_C = 8.0
NSA_WIDTH = 512
NSA_HEADS = 8
HEAD_DIM = 64
NSA_KV = 2
NSA_HPG = 4
KV_W = 128
CMP_L = 32
CMP_STRIDE = 16
CMP_HIDDEN = 256
SEL_L = 64
N_SEL = 8
WINDOW = 512
NUM_BUCKETS = 32
MAX_DIST = 128
X_HEADS = 4
X_HEAD_DIM = 256
N_GROUPS = 4
EXP_PER_GROUP = 8
N_EXPERTS = 32
D_EXPERT = 512
EPS = 1e-6
NEG_INF = -1e30
MASKED_BELOW = -1e29
SEL_FORCE = 1e9
LANES = 128

TQ = 64
NEAR = WINDOW + TQ
FAR_TK = 512
QROWS2 = NSA_HEADS * TQ
UNSEL_PENALTY = 2.0 ** 100
NSA_NB = 4

TM_PROJ = 1024
TM_MID = 512
TM_DEST = 512
TM_DMA = 512
DMA_UNROLL = 8
MOE_TB = 512
ROW_TILE = D_MODEL // LANES
RG_CHUNK = 256
SCAN_ROWS = 8
SCAN_UNROLL = 4
VMEM_LIMIT = 56 * 1024 * 1024


def _cparams(n_axes):
    return pltpu.CompilerParams(dimension_semantics=("arbitrary",) * n_axes,
                                vmem_limit_bytes=VMEM_LIMIT)


def _dot(a, b):
    return jnp.dot(a, b, preferred_element_type=F32)


def _dot_nt(a, b):
    return lax.dot_general(a, b, (((1,), (1,)), ((), ())), preferred_element_type=F32)


def _gelu_tanh(x):
    c = math.sqrt(2.0 / math.pi)
    half = 0.5 * x
    return half + half * jnp.tanh(x * (c + (c * 0.044715) * (x * x)))


def _sigmoid(x):
    return 0.5 * jnp.tanh(0.5 * x) + 0.5


def _rms(x, g):
    return x * lax.rsqrt(jnp.mean(x * x, axis=-1, keepdims=True) + EPS) * g


def _group_rms(x, ones_blk, g):
    ms = _dot((x * x).astype(BF16), ones_blk)
    return x * lax.rsqrt(ms + EPS) * g


def _inproj_kernel(x_ref, g_ref, wrg_ref, wq_ref, wkv_ref, wgl_ref, gq_ref, gks_ref, gkw_ref, ones_ref,
                   u_ref, gate_ref, q_ref, kc_ref, vc_ref, ks_ref, vs_ref, kw_ref, vw_ref, gates_ref):
    xb = _rms(x_ref[...], g_ref[...]).astype(BF16)
    rg = _dot(xb, wrg_ref[...])
    u_ref[...] = rg[:, :RG_WIDTH].astype(BF16)
    gate_ref[...] = rg[:, RG_WIDTH:].astype(BF16)
    q = _dot(xb, wq_ref[...])
    q_ref[...] = _group_rms(q, ones_ref[...], gq_ref[...]).astype(BF16)
    kv = _dot(xb, wkv_ref[...])
    ones_kv = ones_ref[:KV_W, :KV_W]
    kc_ref[...] = kv[:, 0 * KV_W:1 * KV_W].astype(BF16)
    vc_ref[...] = kv[:, 1 * KV_W:2 * KV_W].astype(BF16)
    ks_ref[...] = _group_rms(kv[:, 2 * KV_W:3 * KV_W], ones_kv, gks_ref[...]).astype(BF16)
    vs_ref[...] = kv[:, 3 * KV_W:4 * KV_W].astype(BF16)
    kw_ref[...] = _group_rms(kv[:, 4 * KV_W:5 * KV_W], ones_kv, gkw_ref[...]).astype(BF16)
    vw_ref[...] = kv[:, 5 * KV_W:6 * KV_W].astype(BF16)
    gates_ref[...] = _sigmoid(_dot(xb, wgl_ref[...]))


def _inproj(x2, g, wrg, wq, wkv, wgl, gq, gks, gkw, ones_blk):
    n_tok = x2.shape[0]
    tm = min(TM_PROJ, n_tok)
    full = lambda a: pl.BlockSpec(a.shape, lambda i: (0,) * a.ndim)
    row = lambda w: pl.BlockSpec((tm, w), lambda i: (i, 0))
    outs = [(RG_WIDTH, BF16), (RG_WIDTH, BF16), (NSA_WIDTH, BF16)] + [(KV_W, BF16)] * 6 + [(LANES, F32)]
    return pl.pallas_call(
        _inproj_kernel,
        grid=(n_tok // tm,),
        in_specs=[row(D_MODEL)] + [full(a) for a in (g, wrg, wq, wkv, wgl, gq, gks, gkw, ones_blk)],
        out_specs=[row(w) for w, _ in outs],
        out_shape=[jax.ShapeDtypeStruct((n_tok, w), dt) for w, dt in outs],
        compiler_params=_cparams(1),
    )(x2, g, wrg, wq, wkv, wgl, gq, gks, gkw, ones_blk)


def _rglru_kernel(u_ref, gate_ref, cw_ref, cb_ref, wg_ref, bg_ref, lam_ref, og_ref, y_ref, upad, a_s, h_s):
    seq = u_ref.shape[1]
    upad[0:8, :] = jnp.zeros((8, RG_WIDTH), F32)
    upad[8:8 + seq, :] = u_ref[0].astype(F32)
    neg_lam = -lam_ref[...]
    softplus = jnp.maximum(neg_lam, 0.0) + jnp.log(1.0 + jnp.exp(-jnp.abs(neg_lam)))
    log_a_half = (-0.5 * RG_C) * softplus
    ch = min(RG_CHUNK, seq)
    for c in range(seq // ch):
        r0 = c * ch
        uc = cb_ref[...]
        for k in range(CONV_W):
            off = 8 + r0 - (CONV_W - 1) + k
            uc = uc + cw_ref[k:k + 1, :] * upad[off:off + ch, :]
        th = jnp.tanh(_dot(uc.astype(BF16), wg_ref[...]) + bg_ref[...])
        a = jnp.exp(log_a_half * th[:, :RG_WIDTH] + log_a_half)
        a_s[r0:r0 + ch, :] = a
        s = 1.0 - a * a
        h_s[r0:r0 + ch, :] = s * lax.rsqrt(jnp.maximum(s, 1e-30)) * (0.5 * th[:, RG_WIDTH:] + 0.5) * uc

    row = lax.broadcasted_iota(jnp.int32, (SCAN_ROWS, RG_WIDTH), 0)

    def block(j, h_prev):
        rows = pl.ds(pl.multiple_of(j * SCAN_ROWS, SCAN_ROWS), SCAN_ROWS)
        a = a_s[rows, :]
        b = h_s[rows, :]
        k = 1
        while k < SCAN_ROWS:
            keep = row >= k
            b = jnp.where(keep, a * pltpu.roll(b, k, 0) + b, b)
            a = jnp.where(keep, a * pltpu.roll(a, k, 0), a)
            k *= 2
        h = a * h_prev + b
        h_s[rows, :] = h
        return h[SCAN_ROWS - 1:SCAN_ROWS, :]

    lax.fori_loop(0, seq // SCAN_ROWS, block, jnp.zeros((1, RG_WIDTH), F32), unroll=SCAN_UNROLL)

    for c in range(seq // ch):
        r0 = c * ch
        y = _gelu_tanh(gate_ref[0, r0:r0 + ch, :].astype(F32)) * h_s[r0:r0 + ch, :]
        y_ref[0, r0:r0 + ch, :] = _rms(y, og_ref[...]).astype(BF16)


def _rglru(u3, gate3, cw, cb, wg, bg, lam, og):
    bsz, seq, _ = u3.shape
    full = lambda a: pl.BlockSpec(a.shape, lambda b: (0,) * a.ndim)
    blk = pl.BlockSpec((1, seq, RG_WIDTH), lambda b: (b, 0, 0))
    return pl.pallas_call(
        _rglru_kernel,
        grid=(bsz,),
        in_specs=[blk, blk] + [full(a) for a in (cw, cb, wg, bg, lam, og)],
        out_specs=blk,
        out_shape=jax.ShapeDtypeStruct((bsz, seq, RG_WIDTH), BF16),
        scratch_shapes=[pltpu.VMEM((seq + 8, RG_WIDTH), F32), pltpu.VMEM((seq, RG_WIDTH), F32),
                        pltpu.VMEM((seq, RG_WIDTH), F32)],
        compiler_params=_cparams(1),
    )(u3, gate3, cw, cb, wg, bg, lam, og)


def _compress_kernel(kx_ref, vx_ref, w1k_ref, w2k_ref, pk_ref, w1v_ref, w2v_ref, pv_ref, gk_ref, ones_ref,
                     ko_ref, vo_ref):
    n_chunk = kx_ref.shape[1]
    half = NSA_KV * CMP_HIDDEN

    def mlp(x_ref, w1_ref, w2_ref, p_ref):
        ab = _dot(x_ref[0], w1_ref[...])
        pos = _dot(p_ref[...], w1_ref[...])
        hid = ab[:, :half] + pltpu.roll(ab[:, half:], n_chunk - 1, 0) + (pos[0:1, :half] + pos[1:2, half:])
        return _dot(_gelu_tanh(hid).astype(BF16), w2_ref[...])

    kc = mlp(kx_ref, w1k_ref, w2k_ref, pk_ref)
    ko_ref[0] = _group_rms(kc, ones_ref[...], gk_ref[...]).astype(BF16)
    vo_ref[0] = mlp(vx_ref, w1v_ref, w2v_ref, pv_ref).astype(BF16)


def _compress(kx, vx, w1k, w2k, pk, w1v, w2v, pv, gk, ones_kv):
    bsz, n_chunk, width = kx.shape
    full = lambda a: pl.BlockSpec(a.shape, lambda b: (0,) * a.ndim)
    xin = pl.BlockSpec((1, n_chunk, width), lambda b: (b, 0, 0))
    out = pl.BlockSpec((1, n_chunk, KV_W), lambda b: (b, 0, 0))
    return pl.pallas_call(
        _compress_kernel,
        grid=(bsz,),
        in_specs=[xin, xin] + [full(a) for a in (w1k, w2k, pk, w1v, w2v, pv, gk, ones_kv)],
        out_specs=[out, out],
        out_shape=[jax.ShapeDtypeStruct((bsz, n_chunk, KV_W), BF16)] * 2,
        compiler_params=_cparams(1),
    )(kx, vx, w1k, w2k, pk, w1v, w2v, pv, gk, ones_kv)


def _nsa_kernel(q_ref, gates_ref, kcmp_ref, vcmp_ref, ksx_ref, vsp_ref, kwp_ref, vwp_ref, ovt_ref,
                bc_ref, bw_ref, bs_ref, bf_ref, og_ref, y_ref):
    i = pl.program_id(1)
    t0 = pl.multiple_of(i * TQ, TQ)
    n_blk = ovt_ref.shape[0]
    lane = lax.broadcasted_iota(jnp.int32, (TQ, LANES), 1)
    lo_half = lane < HEAD_DIM
    n_batch = q_ref.shape[0]

    def near_part(bb):
        pieces = []
        for p in range(NSA_HPG):
            qs = q_ref[bb, :, p * LANES:(p + 1) * LANES]
            zero = jnp.zeros_like(qs)
            pieces += [jnp.where(lo_half, qs, zero), jnp.where(lo_half, zero, qs)]
        q8 = jnp.concatenate(pieces, axis=0)

        bc = bc_ref[0]
        lc = _dot_nt(q8, kcmp_ref[bb]) + bc
        ec = jnp.where(bc > MASKED_BELOW, jnp.exp(lc - jnp.max(lc, axis=-1, keepdims=True)), 0.0)
        sc = jnp.sum(ec, axis=-1, keepdims=True)
        pc = ec / jnp.where(sc > 0.0, sc, 1.0)
        o_c = _dot(pc.astype(BF16), vcmp_ref[bb])

        blocks = [pc[r * TQ:(r + 1) * TQ] for r in range(NSA_HPG * NSA_KV)]
        pcs = jnp.concatenate([sum(blocks[g::NSA_KV]) for g in range(NSA_KV)], axis=0)
        pcs_hi = pcs.astype(BF16)
        pcs_lo = (pcs - pcs_hi.astype(F32)).astype(BF16)
        imp = _dot_nt(ovt_ref[...], pcs_hi) + _dot_nt(ovt_ref[...], pcs_lo)
        blk = lax.broadcasted_iota(jnp.int32, imp.shape, 0)
        forced = (blk == 0) | (blk == i) | (blk == i - 1)
        score = jnp.where(forced, SEL_FORCE, jnp.where(blk > i, -3e38, imp))
        rank = jnp.zeros(imp.shape, F32)
        for m in range(n_blk):
            row = score[m:m + 1, :]
            rank = rank + jnp.where(blk > m, jnp.where(row >= score, 1.0, 0.0), jnp.where(row > score, 1.0, 0.0))
        unsel = jnp.where(rank < N_SEL, 0.0, 1.0)
        unsel_far = jnp.where(blk >= i - WINDOW // SEL_L, 1.0, unsel)
        pad = jnp.zeros((LANES - 2 * n_blk, imp.shape[1]), F32)
        u_t = jnp.concatenate([unsel, unsel_far, pad], axis=0).T
        u_lane = lax.broadcasted_iota(jnp.int32, u_t.shape, 1)
        u_near = jnp.where(u_lane < n_blk, u_t, 0.0).astype(BF16)
        u_far = jnp.where(u_lane >= n_blk, u_t, 0.0).astype(BF16)
        qx_near = jnp.concatenate([q8, jnp.concatenate([u_near] * NSA_HPG, axis=0)], axis=1)
        qx_far = jnp.concatenate([q8, jnp.concatenate([u_far] * NSA_HPG, axis=0)], axis=1)

        lw = _dot_nt(q8, kwp_ref[bb, pl.ds(t0, NEAR), :]) + bw_ref[0]
        ew = jnp.exp(lw - jnp.max(lw, axis=-1, keepdims=True))
        o_w = _dot(ew.astype(BF16), vwp_ref[bb, pl.ds(t0, NEAR), :]) / jnp.sum(ew, axis=-1, keepdims=True)

        ls = _dot_nt(qx_near, ksx_ref[bb, pl.ds(t0, NEAR), :]) + bs_ref[0]
        m1 = jnp.max(ls, axis=-1, keepdims=True)
        e1 = jnp.exp(ls - m1)
        l1 = jnp.sum(e1, axis=-1, keepdims=True)
        acc1 = _dot(e1.astype(BF16), vsp_ref[bb, pl.ds(t0, NEAR), :])
        return o_c, o_w, qx_far, (m1, l1, acc1)

    near = [near_part(bb) for bb in range(n_batch)]
    bfar = bf_ref[...]

    def far_step(kf, carry):
        base = pl.multiple_of(WINDOW + kf * FAR_TK, FAR_TK)
        new = []
        for bb in range(n_batch):
            m, l, acc = carry[bb]
            lf = _dot_nt(near[bb][2], ksx_ref[bb, pl.ds(base, FAR_TK), :]) + bfar
            m_new = jnp.maximum(m, jnp.max(lf, axis=-1, keepdims=True))
            alpha = jnp.exp(m - m_new)
            e = jnp.exp(lf - m_new)
            l_new = alpha * l + jnp.sum(e, axis=-1, keepdims=True)
            new.append((m_new, l_new, alpha * acc + _dot(e.astype(BF16), vsp_ref[bb, pl.ds(base, FAR_TK), :])))
        return tuple(new)

    n_far = (jnp.maximum(t0 - WINDOW, 0) + FAR_TK - 1) // FAR_TK
    far = lax.fori_loop(0, n_far, far_step, tuple(part[3] for part in near))

    for bb in range(n_batch):
        o_c, o_w = near[bb][0], near[bb][1]
        _, l_s, acc_s = far[bb]
        o_s = acc_s / l_s
        gates = gates_ref[bb]

        def gate_col(j):
            cols = [gates[:, (g * NSA_HPG + p) * 3 + j:(g * NSA_HPG + p) * 3 + j + 1]
                    for p in range(NSA_HPG) for g in range(NSA_KV)]
            return jnp.concatenate(cols, axis=0)

        out = gate_col(0) * o_c + gate_col(1) * o_s + gate_col(2) * o_w
        slabs = [jnp.where(lo_half, out[(2 * p) * TQ:(2 * p + 1) * TQ], out[(2 * p + 1) * TQ:(2 * p + 2) * TQ])
                 for p in range(NSA_HPG)]
        y_ref[bb] = _rms(jnp.concatenate(slabs, axis=-1), og_ref[...]).astype(BF16)


def _nsa(q3, gates3, kcmp, vcmp, ksx, vsp, kwp, vwp, ovt, bias_c, bias_w, bias_s, bias_far, og):
    bsz, seq, _ = q3.shape
    n_chunk = kcmp.shape[1]
    n_var = bias_w.shape[0] - 1
    nb = NSA_NB if bsz % NSA_NB == 0 else 1
    full = lambda a: pl.BlockSpec(a.shape, lambda b, i: (0,) * a.ndim)
    per_b = lambda a: pl.BlockSpec((nb,) + a.shape[1:], lambda b, i: (b,) + (0,) * (a.ndim - 1))
    near = pl.BlockSpec((1, QROWS2, NEAR), lambda b, i: (jnp.minimum(i, n_var), 0, 0))
    return pl.pallas_call(
        _nsa_kernel,
        grid=(bsz // nb, seq // TQ),
        in_specs=[pl.BlockSpec((nb, TQ, NSA_WIDTH), lambda b, i: (b, i, 0)),
                  pl.BlockSpec((nb, TQ, LANES), lambda b, i: (b, i, 0)),
                  per_b(kcmp), per_b(vcmp), per_b(ksx), per_b(vsp), per_b(kwp), per_b(vwp),
                  full(ovt),
                  pl.BlockSpec((1, QROWS2, n_chunk), lambda b, i: (i, 0, 0)),
                  near, near, full(bias_far), full(og)],
        out_specs=pl.BlockSpec((nb, TQ, NSA_WIDTH), lambda b, i: (b, i, 0)),
        out_shape=jax.ShapeDtypeStruct((bsz, seq, NSA_WIDTH), BF16),
        compiler_params=_cparams(2),
    )(q3, gates3, kcmp, vcmp, ksx, vsp, kwp, vwp, ovt, bias_c, bias_w, bias_s, bias_far, og)


def _memkv_kernel(mem_ref, g_ref, wkv_ref, gk_ref, k_ref, v_ref):
    mn = _rms(mem_ref[0], g_ref[...]).astype(BF16)
    kv = _dot(mn, wkv_ref[...])
    for h in range(X_HEADS):
        sl = slice(h * X_HEAD_DIM, (h + 1) * X_HEAD_DIM)
        k_ref[0, :, sl] = _rms(kv[:, sl], gk_ref[...]).astype(BF16)
    v_ref[0] = kv[:, D_MODEL:].astype(BF16)


def _memkv(mem, g, wkv, gk):
    bsz, mlen, _ = mem.shape
    full = lambda a: pl.BlockSpec(a.shape, lambda b: (0,) * a.ndim)
    blk = pl.BlockSpec((1, mlen, D_MODEL), lambda b: (b, 0, 0))
    return pl.pallas_call(
        _memkv_kernel,
        grid=(bsz,),
        in_specs=[blk, full(g), full(wkv), full(gk)],
        out_specs=[blk, blk],
        out_shape=[jax.ShapeDtypeStruct((bsz, mlen, D_MODEL), BF16)] * 2,
        compiler_params=_cparams(1),
    )(mem, g, wkv, gk)


def _mid_kernel(x_ref, yrg_ref, ynsa_ref, woa_ref, wob_ref, gx_ref, wq_ref, gq_ref, k_ref, v_ref, wo_ref,
                gm_ref, wrh_ref, wrl_ref, br_ref, h_ref, xt_ref, rw_ref, ri_ref, cnt_ref):
    h1 = x_ref[0] + _dot(yrg_ref[0], woa_ref[...]) + _dot(ynsa_ref[0], wob_ref[...])

    q = _dot(_rms(h1, gx_ref[...]).astype(BF16), wq_ref[...])
    heads = []
    for h in range(X_HEADS):
        sl = slice(h * X_HEAD_DIM, (h + 1) * X_HEAD_DIM)
        qh = _rms(q[:, sl], gq_ref[...]).astype(BF16)
        lg = _dot_nt(qh, k_ref[0, :, sl])
        e = jnp.exp(lg - jnp.max(lg, axis=-1, keepdims=True))
        heads.append(_dot(e.astype(BF16), v_ref[0, :, sl]) / jnp.sum(e, axis=-1, keepdims=True))
    h2 = h1 + _dot(jnp.concatenate(heads, axis=-1).astype(BF16), wo_ref[...])
    h_ref[0] = h2

    xt = _rms(h2, gm_ref[...])
    _store_row_tiles(xt_ref, xt)
    xt_hi = xt.astype(BF16)
    xt_lo = (xt - xt_hi.astype(F32)).astype(BF16)
    lg = _dot(xt_hi, wrh_ref[...]) + _dot(xt_lo, wrh_ref[...]) + _dot(xt_hi, wrl_ref[...]) + br_ref[...]
    lane = lax.broadcasted_iota(jnp.int32, lg.shape, 1)
    lane_f = lane.astype(F32)
    first_of = lambda hit: jnp.min(jnp.where(hit, lane_f, 1e9), axis=-1, keepdims=True)
    glog = jnp.where(lane < N_GROUPS, lg, -3e38)
    gmax = jnp.max(glog, axis=-1, keepdims=True)
    gsel = first_of(glog == gmax)
    p_g = 1.0 / jnp.sum(jnp.exp(glog - gmax), axis=-1, keepdims=True)
    lo = N_GROUPS + EXP_PER_GROUP * gsel
    el = jnp.where((lane_f >= lo) & (lane_f < lo + EXP_PER_GROUP), lg, -3e38)
    m_a = jnp.max(el, axis=-1, keepdims=True)
    i_a = first_of(el == m_a)
    el2 = jnp.where(lane_f == i_a, -3e38, el)
    m_b = jnp.max(el2, axis=-1, keepdims=True)
    i_b = first_of(el2 == m_b)
    r = jnp.exp(m_b - m_a)
    w_a = p_g / (1.0 + r)
    w_b = p_g * r / (1.0 + r)
    e_a = i_a - N_GROUPS
    e_b = i_b - N_GROUPS
    rw_ref[0] = jnp.where(lane == 0, w_a, jnp.where(lane == 1, w_b, 0.0))
    ri_ref[0] = jnp.where(lane == 0, e_a, jnp.where(lane == 1, e_b, 0.0)).astype(jnp.int32)

    @pl.when((pl.program_id(0) == 0) & (pl.program_id(1) == 0))
    def _():
        cnt_ref[...] = jnp.zeros_like(cnt_ref)

    hot = jnp.where((lane_f == e_a) | (lane_f == e_b), 1.0, 0.0)
    cnt_ref[...] += jnp.sum(hot, axis=0, keepdims=True)


def _mid(x, yrg, ynsa, woa, wob, gx, wq, gq, kx, vx, wo, gm, wrh, wrl, br):
    bsz, seq, _ = x.shape
    tm = min(TM_MID, seq)
    mlen = kx.shape[1]
    n_i = seq // tm
    full = lambda a: pl.BlockSpec(a.shape, lambda b, i: (0,) * a.ndim)
    tok = lambda w: pl.BlockSpec((1, tm, w), lambda b, i: (b, i, 0))
    memb = pl.BlockSpec((1, mlen, D_MODEL), lambda b, i: (b, 0, 0))
    xt_spec = pl.BlockSpec((tm * ROW_TILE, LANES), lambda b, i: (b * n_i + i, 0))
    return pl.pallas_call(
        _mid_kernel,
        grid=(bsz, seq // tm),
        in_specs=[tok(D_MODEL), tok(RG_WIDTH), tok(NSA_WIDTH), full(woa), full(wob), full(gx), full(wq), full(gq),
                  memb, memb, full(wo), full(gm), full(wrh), full(wrl), full(br)],
        out_specs=[tok(D_MODEL), xt_spec, tok(LANES), tok(LANES), pl.BlockSpec((1, LANES), lambda b, i: (0, 0))],
        out_shape=[jax.ShapeDtypeStruct((bsz, seq, D_MODEL), F32),
                   jax.ShapeDtypeStruct((bsz * seq * ROW_TILE, LANES), F32),
                   jax.ShapeDtypeStruct((bsz, seq, LANES), F32), jax.ShapeDtypeStruct((bsz, seq, LANES), jnp.int32),
                   jax.ShapeDtypeStruct((1, LANES), F32)],
        compiler_params=_cparams(2),
    )(x, yrg, ynsa, woa, wob, gx, wq, gq, kx, vx, wo, gm, wrh, wrl, br)


def _dest_kernel(ri_ref, pstart_ref, dest_ref, run_ref):
    @pl.when(pl.program_id(0) == 0)
    def _():
        run_ref[...] = jnp.zeros_like(run_ref)

    ri = ri_ref[...]
    tm = ri.shape[0]
    lane = lax.broadcasted_iota(jnp.int32, ri.shape, 1)
    e_a = ri[:, 0:1]
    e_b = ri[:, 1:2]
    hot_a = lane == e_a
    hot_b = lane == e_b
    hot = jnp.where(hot_a | hot_b, 1.0, 0.0)
    row = lax.broadcasted_iota(jnp.int32, (tm, tm), 0)
    col = lax.broadcasted_iota(jnp.int32, (tm, tm), 1)
    earlier = jnp.where(col < row, 1.0, 0.0).astype(BF16)
    base = _dot(earlier, hot.astype(BF16)) + run_ref[...] + pstart_ref[...]
    d_a = jnp.sum(jnp.where(hot_a, base, 0.0), axis=-1, keepdims=True)
    d_b = jnp.sum(jnp.where(hot_b, base, 0.0), axis=-1, keepdims=True)
    dest_ref[...] = jnp.where(lane == 0, d_a, jnp.where(lane == 1, d_b, 0.0)).astype(jnp.int32)
    run_ref[...] += jnp.sum(hot, axis=0, keepdims=True)


def _dest(ri2, pstart):
    n_tok = ri2.shape[0]
    tm = min(TM_DEST, n_tok)
    return pl.pallas_call(
        _dest_kernel,
        grid=(n_tok // tm,),
        in_specs=[pl.BlockSpec((tm, LANES), lambda i: (i, 0)), pl.BlockSpec((1, LANES), lambda i: (0, 0))],
        out_specs=pl.BlockSpec((tm, LANES), lambda i: (i, 0)),
        out_shape=jax.ShapeDtypeStruct((n_tok, LANES), jnp.int32),
        scratch_shapes=[pltpu.VMEM((1, LANES), F32)],
        compiler_params=_cparams(1),
    )(ri2, pstart)


def _store_row_tiles(ref, val):
    n = val.shape[0]
    for c in range(ROW_TILE):
        ref[pl.ds(c, n, stride=ROW_TILE), :] = val[:, c * LANES:(c + 1) * LANES]


def _load_row_tiles(ref, n):
    return [ref[pl.ds(c, n, stride=ROW_TILE), :] for c in range(ROW_TILE)]


def _token_rows(ref, t):
    return ref.at[pl.ds(pl.multiple_of(t * ROW_TILE, ROW_TILE), ROW_TILE), :]


def _dispatch_kernel(cnt_ref, pstart_ref, da_ref, db_ref, xt_ref, xs_ref, zrow, sem, zsem):
    tm = da_ref.shape[2]

    @pl.when(pl.program_id(0) == 0)
    def _():
        zrow[...] = jnp.zeros_like(zrow)

        def per_expert(e, c):
            used = cnt_ref[e]
            padded = (used + MOE_TB - 1) // MOE_TB * MOE_TB
            base = pstart_ref[e]

            def fill(r, c2):
                pltpu.make_async_copy(zrow, _token_rows(xs_ref, base + r), zsem).start()
                return c2

            def fill_done(r, c2):
                pltpu.make_async_copy(zrow, _token_rows(xs_ref, 0), zsem).wait()
                return c2

            lax.fori_loop(used, padded, fill, 0)
            lax.fori_loop(used, padded, fill_done, 0)
            return c

        lax.fori_loop(0, N_EXPERTS, per_expert, 0)

        last = N_EXPERTS - 1
        first_unused = (pstart_ref[last] + (cnt_ref[last] + MOE_TB - 1) // MOE_TB * MOE_TB) // MOE_TB

        def per_block(j, c):
            def fill(r, c2):
                pltpu.make_async_copy(zrow, _token_rows(xs_ref, j * MOE_TB + r), zsem).start()
                return c2

            def fill_done(r, c2):
                pltpu.make_async_copy(zrow, _token_rows(xs_ref, 0), zsem).wait()
                return c2

            lax.fori_loop(0, MOE_TB, fill, 0, unroll=DMA_UNROLL)
            lax.fori_loop(0, MOE_TB, fill_done, 0, unroll=DMA_UNROLL)
            return c

        lax.fori_loop(first_unused, xs_ref.shape[0] // (ROW_TILE * MOE_TB), per_block, 0)

    def issue(t, c):
        pltpu.make_async_copy(_token_rows(xt_ref, t), _token_rows(xs_ref, da_ref[0, 0, t]), sem).start(priority=0)
        pltpu.make_async_copy(_token_rows(xt_ref, t), _token_rows(xs_ref, db_ref[0, 0, t]), sem).start(priority=1)
        return c

    lax.fori_loop(0, tm, issue, 0, unroll=DMA_UNROLL)

    def drain(t, c):
        pltpu.make_async_copy(_token_rows(xt_ref, 0), _token_rows(xs_ref, 0), sem).wait()
        pltpu.make_async_copy(_token_rows(xt_ref, 0), _token_rows(xs_ref, 0), sem).wait()
        return c

    lax.fori_loop(0, tm, drain, 0, unroll=DMA_UNROLL)


def _dispatch(cnt, pstart, da, db, xt_rows, n_pad):
    n_tiles, _, tm = da.shape
    smem = pl.BlockSpec((1, 1, tm), lambda i, c, p: (i, 0, 0), memory_space=pltpu.SMEM)
    grid_spec = pltpu.PrefetchScalarGridSpec(
        num_scalar_prefetch=2,
        grid=(n_tiles,),
        in_specs=[smem, smem, pl.BlockSpec((tm * ROW_TILE, LANES), lambda i, c, p: (i, 0))],
        out_specs=pl.BlockSpec(memory_space=pl.ANY),
        scratch_shapes=[pltpu.VMEM((ROW_TILE, LANES), F32), pltpu.SemaphoreType.DMA(()),
                        pltpu.SemaphoreType.DMA(())],
    )
    return pl.pallas_call(
        _dispatch_kernel,
        grid_spec=grid_spec,
        out_shape=jax.ShapeDtypeStruct((n_pad * ROW_TILE, LANES), F32),
        compiler_params=pltpu.CompilerParams(dimension_semantics=("arbitrary",), has_side_effects=True,
                                             vmem_limit_bytes=VMEM_LIMIT),
    )(cnt, pstart, da, db, xt_rows)


def _ffn_kernel(bexp_ref, nused_ref, xs_ref, w1_ref, w3_ref, w2_ref, ys_ref):
    del bexp_ref
    j = pl.program_id(0)

    @pl.when(j < nused_ref[0])
    def _():
        xb = jnp.concatenate(_load_row_tiles(xs_ref, MOE_TB), axis=-1).astype(BF16)
        a = _dot(xb, w1_ref[0])
        h = a * _sigmoid(a) * _dot(xb, w3_ref[0])
        _store_row_tiles(ys_ref, _dot(h.astype(BF16), w2_ref[0]))

    @pl.when(j >= nused_ref[0])
    def _():
        ys_ref[...] = jnp.zeros_like(ys_ref)


def _ffn(blk_exp, n_used, xs, w1, w3, w2):
    n_blocks = xs.shape[0] // (MOE_TB * ROW_TILE)
    rows = pl.BlockSpec((MOE_TB * ROW_TILE, LANES), lambda j, be, nu: (j, 0))
    used_rows = pl.BlockSpec((MOE_TB * ROW_TILE, LANES), lambda j, be, nu: (jnp.minimum(j, nu[0] - 1), 0))
    grid_spec = pltpu.PrefetchScalarGridSpec(
        num_scalar_prefetch=2,
        grid=(n_blocks,),
        in_specs=[used_rows,
                  pl.BlockSpec((1, D_MODEL, D_EXPERT), lambda j, be, nu: (be[j], 0, 0)),
                  pl.BlockSpec((1, D_MODEL, D_EXPERT), lambda j, be, nu: (be[j], 0, 0)),
                  pl.BlockSpec((1, D_EXPERT, D_MODEL), lambda j, be, nu: (be[j], 0, 0))],
        out_specs=rows,
    )
    return pl.pallas_call(
        _ffn_kernel,
        grid_spec=grid_spec,
        out_shape=jax.ShapeDtypeStruct(xs.shape, F32),
        compiler_params=_cparams(1),
    )(blk_exp, n_used, xs, w1, w3, w2)


def _combine_kernel(da_ref, db_ref, da_next_ref, db_next_ref, h_ref, rw_ref, ys_ref, o_ref, ya, yb, sems):
    tm = da_ref.shape[2]
    i = pl.program_id(0)
    slot = i % 2

    def start_gather(a_ref, b_ref, s):
        def issue(t, c):
            pltpu.make_async_copy(_token_rows(ys_ref, a_ref[0, 0, t]), _token_rows(ya.at[s], t),
                                  sems.at[s]).start(priority=0)
            pltpu.make_async_copy(_token_rows(ys_ref, b_ref[0, 0, t]), _token_rows(yb.at[s], t),
                                  sems.at[s]).start(priority=1)
            return c

        lax.fori_loop(0, tm, issue, 0, unroll=DMA_UNROLL)

    @pl.when(i == 0)
    def _():
        start_gather(da_ref, db_ref, slot)

    @pl.when(i + 1 < pl.num_programs(0))
    def _():
        start_gather(da_next_ref, db_next_ref, 1 - slot)

    def drain(t, c):
        pltpu.make_async_copy(_token_rows(ys_ref, 0), _token_rows(ya.at[slot], 0), sems.at[slot]).wait()
        pltpu.make_async_copy(_token_rows(ys_ref, 0), _token_rows(yb.at[slot], 0), sems.at[slot]).wait()
        return c

    lax.fori_loop(0, tm, drain, 0, unroll=DMA_UNROLL)
    rw = rw_ref[...]
    mix = [rw[:, 0:1] * a + rw[:, 1:2] * b
           for a, b in zip(_load_row_tiles(ya.at[slot], tm), _load_row_tiles(yb.at[slot], tm))]
    o_ref[...] = h_ref[...] + jnp.concatenate(mix, axis=-1)


def _combine(da, db, h2, rw, ys):
    n_tiles, _, tm = da.shape
    n_tok = h2.shape[0]
    smem = pl.BlockSpec((1, 1, tm), lambda i: (i, 0, 0), memory_space=pltpu.SMEM)
    smem_next = pl.BlockSpec((1, 1, tm), lambda i: (jnp.minimum(i + 1, n_tiles - 1), 0, 0), memory_space=pltpu.SMEM)
    row = lambda w: pl.BlockSpec((tm, w), lambda i: (i, 0))
    slots = pltpu.VMEM((2, tm * ROW_TILE, LANES), F32)
    return pl.pallas_call(
        _combine_kernel,
        grid=(n_tiles,),
        in_specs=[smem, smem, smem_next, smem_next, row(D_MODEL), row(LANES), pl.BlockSpec(memory_space=pl.ANY)],
        out_specs=row(D_MODEL),
        out_shape=jax.ShapeDtypeStruct((n_tok, D_MODEL), F32),
        scratch_shapes=[slots, slots, pltpu.SemaphoreType.DMA((2,))],
        compiler_params=_cparams(1),
    )(da, db, da, db, h2, rw, ys)


def _rel_bucket_np(dist):
    n = np.maximum(dist, 0)
    max_exact = NUM_BUCKETS // 2
    nf = np.maximum(n, 1).astype(np.float32)
    large = max_exact + (np.log(nf / max_exact) / math.log(MAX_DIST / max_exact)
                         * (NUM_BUCKETS - max_exact)).astype(np.int32)
    large = np.minimum(large, NUM_BUCKETS - 1)
    return np.where(n < max_exact, n, large).astype(np.int32)


def _toeplitz(vec, rows):
    width = vec.shape[-1] - 1
    flat = jnp.tile(vec, (1,) * (vec.ndim - 1) + (rows,))[..., :rows * width]
    return flat.reshape(vec.shape[:-1] + (rows, width))


def _bias_tables(rel_bias, seq):
    n_chunk = seq // CMP_STRIDE
    n_tiles = seq // TQ
    table = rel_bias.T.astype(F32)

    wide = NEAR + TQ
    k = np.arange(wide + 1)
    dw = np.where(k < NEAR, WINDOW - k, WINDOW + wide + 1 - k)
    used = (k < NEAR) | (k > wide + 1 - TQ)
    vals = table[:, _rel_bucket_np(dw)]

    n_var = WINDOW // TQ
    first_key = WINDOW - TQ * np.arange(n_var + 1)[:, None, None]
    in_seq = np.arange(NEAR)[None, None, :] >= first_key

    def near_tile(valid):
        t = _toeplitz(jnp.where(valid[None, :], vals, NEG_INF), TQ)[:, :, :NEAR]
        t = t.reshape(NSA_KV, NSA_HPG, TQ, NEAR).transpose(1, 0, 2, 3).reshape(1, QROWS2, NEAR)
        return jnp.where(in_seq, t, NEG_INF)

    bias_w = near_tile(used & (dw >= 0) & (dw < WINDOW))
    bias_s = near_tile(used & (dw >= 0))
    bias_far = table[:, NUM_BUCKETS - 1].reshape(NSA_KV, NSA_HPG, 1).transpose(1, 0, 2)
    bias_far = jnp.broadcast_to(bias_far, (NSA_HPG, NSA_KV, TQ)).reshape(QROWS2, 1)

    r = np.arange(CMP_STRIDE)[:, None]
    k = np.arange(2 * n_chunk + 1)[None, :]
    lag = 2 * n_chunk + 1 - k
    valid = (k > n_chunk + 1) & (CMP_STRIDE * lag + r >= CMP_L - 1)
    vals = table[:, _rel_bucket_np(CMP_STRIDE * lag + r - CMP_L // 2)]
    full = _toeplitz(jnp.where(valid[None], vals, NEG_INF), n_chunk)[..., :n_chunk]
    full = jnp.where(np.arange(n_chunk) < n_chunk - 1, full, NEG_INF)
    a4 = TQ // CMP_STRIDE
    full = full.reshape(NSA_KV, NSA_HPG, CMP_STRIDE, n_tiles, a4, n_chunk).transpose(3, 1, 0, 4, 2, 5)
    bias_c = full.reshape(n_tiles, QROWS2, n_chunk)
    return bias_c, bias_w, bias_s, bias_far


def _selection_tables(seq):
    n_chunk = seq // CMP_STRIDE
    n_blk = seq // SEL_L
    c = np.arange(n_chunk)
    n = np.arange(n_blk)
    start = c * CMP_STRIDE
    overlap_t = ((start[None, :] <= n[:, None] * SEL_L + SEL_L - 1) & (start[None, :] + CMP_L - 1 >= n[:, None] * SEL_L)
                 & (c < n_chunk - 1)[None, :])
    pos = np.arange(seq + WINDOW) - WINDOW
    lane_blk = np.arange(LANES) % n_blk
    hit = (pos[:, None] >= 0) & (pos[:, None] // SEL_L == lane_blk[None, :]) & (np.arange(LANES) < 2 * n_blk)[None, :]
    return jnp.asarray(overlap_t, BF16), jnp.asarray(np.where(hit, -UNSEL_PENALTY, 0.0), BF16)


def _block_ones(width, group):
    idx = np.arange(width) // group
    return jnp.asarray((idx[:, None] == idx[None, :]) / group, BF16)


def _block_diag(w):
    nb, n, m = w.shape
    eye = jnp.eye(nb, dtype=w.dtype)
    return jnp.einsum('hij,hg->higj', w, eye).reshape(nb * n, nb * m)


def _compress_weights(w1, w2, pos):
    half_l = CMP_L // 2
    parts = []
    for half in range(2):
        wh = w1[half * half_l * HEAD_DIM:(half + 1) * half_l * HEAD_DIM].reshape(half_l, HEAD_DIM, CMP_HIDDEN)
        z = jnp.zeros_like(wh)
        for g in range(NSA_KV):
            grp = [wh if gg == g else z for gg in range(NSA_KV)]
            parts.append(jnp.stack(grp, axis=1).reshape(half_l * KV_W, CMP_HIDDEN))
    w1cat = jnp.concatenate(parts, axis=1).astype(BF16)
    w2bd = _block_diag(jnp.stack([w2] * NSA_KV)).astype(BF16)
    prow = [jnp.tile(pos[half * half_l:(half + 1) * half_l][:, None, :], (1, NSA_KV, 1)).reshape(-1)
            for half in range(2)]
    pmat = jnp.zeros((8, half_l * KV_W), F32).at[0].set(prow[0]).at[1].set(prow[1]).astype(BF16)
    return w1cat, w2bd, pmat


def kernel(x, mem, rel_bias, norm_mix, w_in, rg_conv_w, rg_conv_b, rg_w_r, rg_b_r, rg_w_i, rg_b_i, rg_lambda, nsa_g_q, nsa_g_kc, nsa_g_ks, nsa_g_kw, cmp_pos_k, cmp_pos_v, cmp_k_w1, cmp_k_w2, cmp_v_w1, cmp_v_w2, out_g_rg, out_g_nsa, w_out, norm_x, norm_mem, xa_w_q, xa_w_kv, xa_w_o, xa_g_q, xa_g_k, norm_moe, router_g_w, router_g_b, router_e_w, router_e_b, exp_w1, exp_w3, exp_w2):
    bsz, seq, _ = x.shape
    n_tok = bsz * seq
    assert seq % FAR_TK == 0 and 2 * (seq // SEL_L) <= LANES and norm_mix.shape[0] == 1
    l = 0
    row = lambda v: v.reshape(1, -1).astype(F32)

    perm = np.array([(half * NSA_HPG + p) * HEAD_DIM + d
                     for p in range(NSA_HPG) for half in range(NSA_KV) for d in range(HEAD_DIM)])
    offs = np.cumsum([0, RG_WIDTH, RG_WIDTH, NSA_WIDTH] + [KV_W] * 6)
    w = w_in[l]
    wrg = w[:, :offs[2]].astype(BF16)
    wq = w[:, offs[2]:offs[3]][:, perm].astype(BF16)
    wkv = w[:, offs[3]:offs[9]].astype(BF16)
    wgl = jnp.pad(w[:, offs[9]:], ((0, 0), (0, LANES - 3 * NSA_HEADS))).astype(BF16)
    ones64 = _block_ones(NSA_WIDTH, HEAD_DIM)
    gq = row(jnp.tile(nsa_g_q[l], NSA_HEADS) * HEAD_DIM ** -0.5)
    u, gate, q, kc, vc, ks, vs, kw, vw, gates = _inproj(
        x.reshape(n_tok, D_MODEL), row(norm_mix[l]), wrg, wq, wkv, wgl, gq,
        row(jnp.tile(nsa_g_ks[l], NSA_KV)), row(jnp.tile(nsa_g_kw[l], NSA_KV)), ones64)

    wg = (0.5 * jnp.concatenate([_block_diag(rg_w_r[l]), _block_diag(rg_w_i[l])], axis=1)).astype(BF16)
    bg = 0.5 * jnp.concatenate([rg_b_r[l], rg_b_i[l]]).reshape(1, -1)
    y_rg = _rglru(u.reshape(bsz, seq, RG_WIDTH), gate.reshape(bsz, seq, RG_WIDTH),
                  rg_conv_w[l].reshape(CONV_W, RG_WIDTH), row(rg_conv_b[l]), wg, bg, row(rg_lambda[l]),
                  row(out_g_rg[l]))

    n_chunk = seq // CMP_STRIDE
    w1k, w2k, pk = _compress_weights(cmp_k_w1[l], cmp_k_w2[l], cmp_pos_k[l])
    w1v, w2v, pv = _compress_weights(cmp_v_w1[l], cmp_v_w2[l], cmp_pos_v[l])
    kcmp, vcmp = _compress(kc.reshape(bsz, n_chunk, CMP_STRIDE * KV_W), vc.reshape(bsz, n_chunk, CMP_STRIDE * KV_W),
                           w1k, w2k, pk, w1v, w2v, pv, row(jnp.tile(nsa_g_kc[l], NSA_KV)),
                           ones64[:KV_W, :KV_W])
    padw = lambda t: jnp.pad(t.reshape(bsz, seq, KV_W), ((0, 0), (WINDOW, 0), (0, 0)))
    bias_c, bias_w, bias_s, bias_far = _bias_tables(rel_bias, seq)
    overlap_t, penalty = _selection_tables(seq)
    ksx = jnp.concatenate([padw(ks), jnp.broadcast_to(penalty, (bsz,) + penalty.shape)], axis=-1)
    y_nsa = _nsa(q.reshape(bsz, seq, NSA_WIDTH), gates.reshape(bsz, seq, LANES), kcmp, vcmp,
                 ksx, padw(vs), padw(kw), padw(vw), overlap_t, bias_c, bias_w, bias_s, bias_far,
                 row(out_g_nsa[l][perm]))

    kx, vx = _memkv(mem, row(norm_mem[l]), xa_w_kv[l].astype(BF16), row(xa_g_k[l]))
    wo_mix = w_out[l]
    wr = jnp.pad(jnp.concatenate([router_g_w[l], router_e_w[l]], axis=1),
                 ((0, 0), (0, LANES - N_GROUPS - N_EXPERTS)))
    wr_hi = wr.astype(BF16)
    br = jnp.pad(jnp.concatenate([router_g_b[l], router_e_b[l]]), (0, LANES - N_GROUPS - N_EXPERTS)).reshape(1, -1)
    h2, xt, rw, ri, counts = _mid(
        x, y_rg, y_nsa, wo_mix[:RG_WIDTH].astype(BF16), wo_mix[RG_WIDTH:][perm].astype(BF16), row(norm_x[l]),
        xa_w_q[l].astype(BF16), row(xa_g_q[l] * X_HEAD_DIM ** -0.5), kx, vx, xa_w_o[l].astype(BF16),
        row(norm_moe[l]), wr_hi, (wr - wr_hi.astype(F32)).astype(BF16), br)

    n_slots = 2 * n_tok
    n_blocks = n_slots // MOE_TB + N_EXPERTS
    n_pad = n_blocks * MOE_TB
    cnt = counts[0, :N_EXPERTS].astype(jnp.int32)
    pcnt = (cnt + MOE_TB - 1) // MOE_TB * MOE_TB
    pends = jnp.cumsum(pcnt)
    pstart = jnp.pad((pends - pcnt).astype(F32), (0, LANES - N_EXPERTS)).reshape(1, LANES)
    blk_exp = jnp.minimum(jnp.sum(pends[None, :] <= jnp.arange(n_blocks, dtype=jnp.int32)[:, None] * MOE_TB, axis=1),
                          N_EXPERTS - 1).astype(jnp.int32)
    n_used = (pends[-1:] // MOE_TB).astype(jnp.int32)
    dest = _dest(ri.reshape(n_tok, LANES), pstart)
    tmd = min(TM_DMA, n_tok)
    da = dest[:, 0].reshape(n_tok // tmd, 1, tmd)
    db = dest[:, 1].reshape(n_tok // tmd, 1, tmd)
    xs = _dispatch(cnt, (pends - pcnt).astype(jnp.int32), da, db, xt, n_pad)
    ys = _ffn(blk_exp, n_used, xs, exp_w1[l].astype(BF16), exp_w3[l].astype(BF16), exp_w2[l].astype(BF16))
    out = _combine(da, db, h2.reshape(n_tok, D_MODEL), rw.reshape(n_tok, LANES), ys)
    return out.reshape(bsz, seq, D_MODEL)
```

```python
import math

import numpy as np
import jax
import jax.numpy as jnp
from jax import lax
from jax.experimental import pallas as pl
from jax.experimental.pallas import tpu as pltpu

F32 = jnp.float32
BF16 = jnp.bfloat16

D_MODEL = 1024
RG_WIDTH = 512
RG_BLOCKS = 8
RG_BLOCK = 64
CONV_W = 4
RG_C = 8.0
NSA_WIDTH = 512
NSA_HEADS = 8
HEAD_DIM = 64
NSA_KV = 2
NSA_HPG = 4
KV_W = 128
CMP_L = 32
CMP_STRIDE = 16
CMP_HIDDEN = 256
SEL_L = 64
N_SEL = 8
WINDOW = 512
NUM_BUCKETS = 32
MAX_DIST = 128
X_HEADS = 4
X_HEAD_DIM = 256
N_GROUPS = 4
EXP_PER_GROUP = 8
N_EXPERTS = 32
D_EXPERT = 512
EPS = 1e-6
LOG2E = 1.0 / math.log(2.0)
NEG_INF = -1e30
MASKED_BELOW = -1e29
SEL_FORCE = 1e9
LANES = 128

TQ = 64
NEAR = WINDOW + TQ
FAR_TK = 512
QROWS2 = NSA_HEADS * TQ
UNSEL_PENALTY = 2.0 ** 100
NSA_NB = 4

TM_PROJ = 1024
TM_MID = 512
TM_DEST = 512
TM_DMA = 512
DMA_UNROLL = 8
MOE_TB = 512
ROW_TILE = D_MODEL // LANES
RG_CHUNK = 256
SCAN_ROWS = 8
SCAN_UNROLL = 4
VMEM_LIMIT = 56 * 1024 * 1024


def _cparams(n_axes):
    return pltpu.CompilerParams(dimension_semantics=("arbitrary",) * n_axes,
                                vmem_limit_bytes=VMEM_LIMIT)


def _dot(a, b):
    return jnp.dot(a, b, preferred_element_type=F32)


def _dot_nt(a, b):
    return lax.dot_general(a, b, (((1,), (1,)), ((), ())), preferred_element_type=F32)


def _gelu_tanh(x):
    c = math.sqrt(2.0 / math.pi)
    half = 0.5 * x
    return half + half * jnp.tanh(x * (c + (c * 0.044715) * (x * x)))


def _sigmoid(x):
    return 0.5 * jnp.tanh(0.5 * x) + 0.5


def _rms(x, g):
    return x * lax.rsqrt(jnp.mean(x * x, axis=-1, keepdims=True) + EPS) * g


def _group_rms(x, ones_blk, g):
    ms = _dot((x * x).astype(BF16), ones_blk)
    return x * lax.rsqrt(ms + EPS) * g


def _inproj_kernel(x_ref, g_ref, wrg_ref, wq_ref, wkv_ref, wgl_ref, gq_ref, gks_ref, gkw_ref, ones_ref,
                   u_ref, gate_ref, q_ref, kc_ref, vc_ref, ks_ref, vs_ref, kw_ref, vw_ref, gates_ref):
    xb = _rms(x_ref[...], g_ref[...]).astype(BF16)
    rg = _dot(xb, wrg_ref[...])
    u_ref[...] = rg[:, :RG_WIDTH].astype(BF16)
    gate_ref[...] = rg[:, RG_WIDTH:].astype(BF16)
    q = _dot(xb, wq_ref[...])
    q_ref[...] = _group_rms(q, ones_ref[...], gq_ref[...]).astype(BF16)
    kv = _dot(xb, wkv_ref[...])
    ones_kv = ones_ref[:KV_W, :KV_W]
    kc_ref[...] = kv[:, 0 * KV_W:1 * KV_W].astype(BF16)
    vc_ref[...] = kv[:, 1 * KV_W:2 * KV_W].astype(BF16)
    ks_ref[...] = _group_rms(kv[:, 2 * KV_W:3 * KV_W], ones_kv, gks_ref[...]).astype(BF16)
    vs_ref[...] = kv[:, 3 * KV_W:4 * KV_W].astype(BF16)
    kw_ref[...] = _group_rms(kv[:, 4 * KV_W:5 * KV_W], ones_kv, gkw_ref[...]).astype(BF16)
    vw_ref[...] = kv[:, 5 * KV_W:6 * KV_W].astype(BF16)
    gates_ref[...] = _sigmoid(_dot(xb, wgl_ref[...]))


def _inproj(x2, g, wrg, wq, wkv, wgl, gq, gks, gkw, ones_blk):
    n_tok = x2.shape[0]
    tm = min(TM_PROJ, n_tok)
    full = lambda a: pl.BlockSpec(a.shape, lambda i: (0,) * a.ndim)
    row = lambda w: pl.BlockSpec((tm, w), lambda i: (i, 0))
    outs = [(RG_WIDTH, BF16), (RG_WIDTH, BF16), (NSA_WIDTH, BF16)] + [(KV_W, BF16)] * 6 + [(LANES, F32)]
    return pl.pallas_call(
        _inproj_kernel,
        grid=(n_tok // tm,),
        in_specs=[row(D_MODEL)] + [full(a) for a in (g, wrg, wq, wkv, wgl, gq, gks, gkw, ones_blk)],
        out_specs=[row(w) for w, _ in outs],
        out_shape=[jax.ShapeDtypeStruct((n_tok, w), dt) for w, dt in outs],
        compiler_params=_cparams(1),
    )(x2, g, wrg, wq, wkv, wgl, gq, gks, gkw, ones_blk)


def _rglru_kernel(u_ref, gate_ref, cw_ref, cb_ref, wg_ref, bg_ref, lam_ref, og_ref, y_ref, upad, a_s, h_s):
    seq = u_ref.shape[1]
    upad[0:8, :] = jnp.zeros((8, RG_WIDTH), F32)
    upad[8:8 + seq, :] = u_ref[0].astype(F32)
    neg_lam = -lam_ref[...]
    softplus = jnp.maximum(neg_lam, 0.0) + jnp.log(1.0 + jnp.exp(-jnp.abs(neg_lam)))
    log2_a_half = (-0.5 * RG_C * LOG2E) * softplus
    ch = min(RG_CHUNK, seq)
    for c in range(seq // ch):
        r0 = c * ch
        uc = cb_ref[...]
        for k in range(CONV_W):
            off = 8 + r0 - (CONV_W - 1) + k
            uc = uc + cw_ref[k:k + 1, :] * upad[off:off + ch, :]
        th = jnp.tanh(_dot(uc.astype(BF16), wg_ref[...]) + bg_ref[...])
        a = jnp.exp2(log2_a_half * th[:, :RG_WIDTH] + log2_a_half)
        a_s[r0:r0 + ch, :] = a
        s = 1.0 - a * a
        h_s[r0:r0 + ch, :] = s * lax.rsqrt(jnp.maximum(s, 1e-30)) * (0.5 * th[:, RG_WIDTH:] + 0.5) * uc

    row = lax.broadcasted_iota(jnp.int32, (SCAN_ROWS, RG_WIDTH), 0)

    def block(j, h_prev):
        rows = pl.ds(pl.multiple_of(j * SCAN_ROWS, SCAN_ROWS), SCAN_ROWS)
        a = a_s[rows, :]
        b = h_s[rows, :]
        k = 1
        while k < SCAN_ROWS:
            keep = row >= k
            b = jnp.where(keep, a * pltpu.roll(b, k, 0) + b, b)
            a = jnp.where(keep, a * pltpu.roll(a, k, 0), a)
            k *= 2
        h = a * h_prev + b
        h_s[rows, :] = h
        return h[SCAN_ROWS - 1:SCAN_ROWS, :]

    lax.fori_loop(0, seq // SCAN_ROWS, block, jnp.zeros((1, RG_WIDTH), F32), unroll=SCAN_UNROLL)

    for c in range(seq // ch):
        r0 = c * ch
        y = _gelu_tanh(gate_ref[0, r0:r0 + ch, :].astype(F32)) * h_s[r0:r0 + ch, :]
        y_ref[0, r0:r0 + ch, :] = _rms(y, og_ref[...]).astype(BF16)


def _rglru(u3, gate3, cw, cb, wg, bg, lam, og):
    bsz, seq, _ = u3.shape
    full = lambda a: pl.BlockSpec(a.shape, lambda b: (0,) * a.ndim)
    blk = pl.BlockSpec((1, seq, RG_WIDTH), lambda b: (b, 0, 0))
    return pl.pallas_call(
        _rglru_kernel,
        grid=(bsz,),
        in_specs=[blk, blk] + [full(a) for a in (cw, cb, wg, bg, lam, og)],
        out_specs=blk,
        out_shape=jax.ShapeDtypeStruct((bsz, seq, RG_WIDTH), BF16),
        scratch_shapes=[pltpu.VMEM((seq + 8, RG_WIDTH), F32), pltpu.VMEM((seq, RG_WIDTH), F32),
                        pltpu.VMEM((seq, RG_WIDTH), F32)],
        compiler_params=_cparams(1),
    )(u3, gate3, cw, cb, wg, bg, lam, og)


def _compress_kernel(kx_ref, vx_ref, w1k_ref, w2k_ref, pk_ref, w1v_ref, w2v_ref, pv_ref, gk_ref, ones_ref,
                     ko_ref, vo_ref):
    n_chunk = kx_ref.shape[1]
    half = NSA_KV * CMP_HIDDEN

    def mlp(x_ref, w1_ref, w2_ref, p_ref):
        ab = _dot(x_ref[0], w1_ref[...])
        pos = _dot(p_ref[...], w1_ref[...])
        hid = ab[:, :half] + pltpu.roll(ab[:, half:], n_chunk - 1, 0) + (pos[0:1, :half] + pos[1:2, half:])
        return _dot(_gelu_tanh(hid).astype(BF16), w2_ref[...])

    kc = mlp(kx_ref, w1k_ref, w2k_ref, pk_ref)
    ko_ref[0] = _group_rms(kc, ones_ref[...], gk_ref[...]).astype(BF16)
    vo_ref[0] = mlp(vx_ref, w1v_ref, w2v_ref, pv_ref).astype(BF16)


def _compress(kx, vx, w1k, w2k, pk, w1v, w2v, pv, gk, ones_kv):
    bsz, n_chunk, width = kx.shape
    full = lambda a: pl.BlockSpec(a.shape, lambda b: (0,) * a.ndim)
    xin = pl.BlockSpec((1, n_chunk, width), lambda b: (b, 0, 0))
    out = pl.BlockSpec((1, n_chunk, KV_W), lambda b: (b, 0, 0))
    return pl.pallas_call(
        _compress_kernel,
        grid=(bsz,),
        in_specs=[xin, xin] + [full(a) for a in (w1k, w2k, pk, w1v, w2v, pv, gk, ones_kv)],
        out_specs=[out, out],
        out_shape=[jax.ShapeDtypeStruct((bsz, n_chunk, KV_W), BF16)] * 2,
        compiler_params=_cparams(1),
    )(kx, vx, w1k, w2k, pk, w1v, w2v, pv, gk, ones_kv)


def _nsa_kernel(q_ref, gates_ref, kcmp_ref, vcmp_ref, ksx_ref, vsp_ref, kwp_ref, vwp_ref, ovt_ref,
                bc_ref, bw_ref, bs_ref, bf_ref, og_ref, y_ref):
    i = pl.program_id(1)
    t0 = pl.multiple_of(i * TQ, TQ)
    n_blk = ovt_ref.shape[0]
    lane = lax.broadcasted_iota(jnp.int32, (TQ, LANES), 1)
    lo_half = lane < HEAD_DIM
    n_batch = q_ref.shape[0]

    def near_part(bb):
        pieces = []
        for p in range(NSA_HPG):
            qs = q_ref[bb, :, p * LANES:(p + 1) * LANES]
            zero = jnp.zeros_like(qs)
            pieces += [jnp.where(lo_half, qs, zero), jnp.where(lo_half, zero, qs)]
        q8 = jnp.concatenate(pieces, axis=0)

        bc = bc_ref[0]
        lc = _dot_nt(q8, kcmp_ref[bb]) + bc
        ec = jnp.where(bc > MASKED_BELOW, jnp.exp2(lc - jnp.max(lc, axis=-1, keepdims=True)), 0.0)
        sc = jnp.sum(ec, axis=-1, keepdims=True)
        pc = ec / jnp.where(sc > 0.0, sc, 1.0)
        o_c = _dot(pc.astype(BF16), vcmp_ref[bb])

        blocks = [pc[r * TQ:(r + 1) * TQ] for r in range(NSA_HPG * NSA_KV)]
        pcs = jnp.concatenate([sum(blocks[g::NSA_KV]) for g in range(NSA_KV)], axis=0)
        pcs_hi = pcs.astype(BF16)
        pcs_lo = (pcs - pcs_hi.astype(F32)).astype(BF16)
        imp = _dot_nt(ovt_ref[...], pcs_hi) + _dot_nt(ovt_ref[...], pcs_lo)
        blk = lax.broadcasted_iota(jnp.int32, imp.shape, 0)
        forced = (blk == 0) | (blk == i) | (blk == i - 1)
        score = jnp.where(forced, SEL_FORCE, jnp.where(blk > i, -3e38, imp))
        rank = jnp.zeros(imp.shape, F32)
        for m in range(n_blk):
            row = score[m:m + 1, :]
            rank = rank + jnp.where(blk > m, jnp.where(row >= score, 1.0, 0.0), jnp.where(row > score, 1.0, 0.0))
        unsel = jnp.where(rank < N_SEL, 0.0, 1.0)
        unsel_far = jnp.where(blk >= i - WINDOW // SEL_L, 1.0, unsel)
        pad = jnp.zeros((LANES - 2 * n_blk, imp.shape[1]), F32)
        u_t = jnp.concatenate([unsel, unsel_far, pad], axis=0).T
        u_lane = lax.broadcasted_iota(jnp.int32, u_t.shape, 1)
        u_near = jnp.where(u_lane < n_blk, u_t, 0.0).astype(BF16)
        u_far = jnp.where(u_lane >= n_blk, u_t, 0.0).astype(BF16)
        qx_near = jnp.concatenate([q8, jnp.concatenate([u_near] * NSA_HPG, axis=0)], axis=1)
        qx_far = jnp.concatenate([q8, jnp.concatenate([u_far] * NSA_HPG, axis=0)], axis=1)

        lw = _dot_nt(q8, kwp_ref[bb, pl.ds(t0, NEAR), :]) + bw_ref[0]
        ew = jnp.exp2(lw - jnp.max(lw, axis=-1, keepdims=True))
        o_w = _dot(ew.astype(BF16), vwp_ref[bb, pl.ds(t0, NEAR), :]) / jnp.sum(ew, axis=-1, keepdims=True)

        ls = _dot_nt(qx_near, ksx_ref[bb, pl.ds(t0, NEAR), :]) + bs_ref[0]
        m1 = jnp.max(ls, axis=-1, keepdims=True)
        e1 = jnp.exp2(ls - m1)
        l1 = jnp.sum(e1, axis=-1, keepdims=True)
        acc1 = _dot(e1.astype(BF16), vsp_ref[bb, pl.ds(t0, NEAR), :])
        return o_c, o_w, qx_far, (m1, l1, acc1)

    near = [near_part(bb) for bb in range(n_batch)]
    bfar = bf_ref[...]

    def far_step(kf, carry):
        base = pl.multiple_of(WINDOW + kf * FAR_TK, FAR_TK)
        new = []
        for bb in range(n_batch):
            m, l, acc = carry[bb]
            lf = _dot_nt(near[bb][2], ksx_ref[bb, pl.ds(base, FAR_TK), :]) + bfar
            m_new = jnp.maximum(m, jnp.max(lf, axis=-1, keepdims=True))
            alpha = jnp.exp2(m - m_new)
            e = jnp.exp2(lf - m_new)
            l_new = alpha * l + jnp.sum(e, axis=-1, keepdims=True)
            new.append((m_new, l_new, alpha * acc + _dot(e.astype(BF16), vsp_ref[bb, pl.ds(base, FAR_TK), :])))
        return tuple(new)

    n_far = (jnp.maximum(t0 - WINDOW, 0) + FAR_TK - 1) // FAR_TK
    far = lax.fori_loop(0, n_far, far_step, tuple(part[3] for part in near))

    for bb in range(n_batch):
        o_c, o_w = near[bb][0], near[bb][1]
        _, l_s, acc_s = far[bb]
        o_s = acc_s / l_s
        gates = gates_ref[bb]

        def gate_col(j):
            cols = [gates[:, (g * NSA_HPG + p) * 3 + j:(g * NSA_HPG + p) * 3 + j + 1]
                    for p in range(NSA_HPG) for g in range(NSA_KV)]
            return jnp.concatenate(cols, axis=0)

        out = gate_col(0) * o_c + gate_col(1) * o_s + gate_col(2) * o_w
        slabs = [jnp.where(lo_half, out[(2 * p) * TQ:(2 * p + 1) * TQ], out[(2 * p + 1) * TQ:(2 * p + 2) * TQ])
                 for p in range(NSA_HPG)]
        y_ref[bb] = _rms(jnp.concatenate(slabs, axis=-1), og_ref[...]).astype(BF16)


def _nsa(q3, gates3, kcmp, vcmp, ksx, vsp, kwp, vwp, ovt, bias_c, bias_w, bias_s, bias_far, og):
    bsz, seq, _ = q3.shape
    n_chunk = kcmp.shape[1]
    n_var = bias_w.shape[0] - 1
    nb = NSA_NB if bsz % NSA_NB == 0 else 1
    full = lambda a: pl.BlockSpec(a.shape, lambda b, i: (0,) * a.ndim)
    per_b = lambda a: pl.BlockSpec((nb,) + a.shape[1:], lambda b, i: (b,) + (0,) * (a.ndim - 1))
    near = pl.BlockSpec((1, QROWS2, NEAR), lambda b, i: (jnp.minimum(i, n_var), 0, 0))
    return pl.pallas_call(
        _nsa_kernel,
        grid=(bsz // nb, seq // TQ),
        in_specs=[pl.BlockSpec((nb, TQ, NSA_WIDTH), lambda b, i: (b, i, 0)),
                  pl.BlockSpec((nb, TQ, LANES), lambda b, i: (b, i, 0)),
                  per_b(kcmp), per_b(vcmp), per_b(ksx), per_b(vsp), per_b(kwp), per_b(vwp),
                  full(ovt),
                  pl.BlockSpec((1, QROWS2, n_chunk), lambda b, i: (i, 0, 0)),
                  near, near, full(bias_far), full(og)],
        out_specs=pl.BlockSpec((nb, TQ, NSA_WIDTH), lambda b, i: (b, i, 0)),
        out_shape=jax.ShapeDtypeStruct((bsz, seq, NSA_WIDTH), BF16),
        compiler_params=_cparams(2),
    )(q3, gates3, kcmp, vcmp, ksx, vsp, kwp, vwp, ovt, bias_c, bias_w, bias_s, bias_far, og)


def _memkv_kernel(mem_ref, g_ref, wkv_ref, gk_ref, k_ref, v_ref):
    mn = _rms(mem_ref[0], g_ref[...]).astype(BF16)
    kv = _dot(mn, wkv_ref[...])
    for h in range(X_HEADS):
        sl = slice(h * X_HEAD_DIM, (h + 1) * X_HEAD_DIM)
        k_ref[0, :, sl] = _rms(kv[:, sl], gk_ref[...]).astype(BF16)
    v_ref[0] = kv[:, D_MODEL:].astype(BF16)


def _memkv(mem, g, wkv, gk):
    bsz, mlen, _ = mem.shape
    full = lambda a: pl.BlockSpec(a.shape, lambda b: (0,) * a.ndim)
    blk = pl.BlockSpec((1, mlen, D_MODEL), lambda b: (b, 0, 0))
    return pl.pallas_call(
        _memkv_kernel,
        grid=(bsz,),
        in_specs=[blk, full(g), full(wkv), full(gk)],
        out_specs=[blk, blk],
        out_shape=[jax.ShapeDtypeStruct((bsz, mlen, D_MODEL), BF16)] * 2,
        compiler_params=_cparams(1),
    )(mem, g, wkv, gk)


def _mid_kernel(x_ref, yrg_ref, ynsa_ref, woa_ref, wob_ref, gx_ref, wq_ref, gq_ref, k_ref, v_ref, wo_ref,
                gm_ref, wrh_ref, wrl_ref, br_ref, h_ref, xt_ref, rw_ref, ri_ref, cnt_ref):
    h1 = x_ref[0] + _dot(yrg_ref[0], woa_ref[...]) + _dot(ynsa_ref[0], wob_ref[...])

    q = _dot(_rms(h1, gx_ref[...]).astype(BF16), wq_ref[...])
    heads = []
    for h in range(X_HEADS):
        sl = slice(h * X_HEAD_DIM, (h + 1) * X_HEAD_DIM)
        qh = _rms(q[:, sl], gq_ref[...]).astype(BF16)
        lg = _dot_nt(qh, k_ref[0, :, sl])
        e = jnp.exp2(lg - jnp.max(lg, axis=-1, keepdims=True))
        heads.append(_dot(e.astype(BF16), v_ref[0, :, sl]) / jnp.sum(e, axis=-1, keepdims=True))
    h2 = h1 + _dot(jnp.concatenate(heads, axis=-1).astype(BF16), wo_ref[...])
    h_ref[0] = h2

    xt = _rms(h2, gm_ref[...])
    _store_row_tiles(xt_ref, xt)
    xt_hi = xt.astype(BF16)
    xt_lo = (xt - xt_hi.astype(F32)).astype(BF16)
    lg = _dot(xt_hi, wrh_ref[...]) + _dot(xt_lo, wrh_ref[...]) + _dot(xt_hi, wrl_ref[...]) + br_ref[...]
    lane = lax.broadcasted_iota(jnp.int32, lg.shape, 1)
    lane_f = lane.astype(F32)
    first_of = lambda hit: jnp.min(jnp.where(hit, lane_f, 1e9), axis=-1, keepdims=True)
    glog = jnp.where(lane < N_GROUPS, lg, -3e38)
    gmax = jnp.max(glog, axis=-1, keepdims=True)
    gsel = first_of(glog == gmax)
    p_g = 1.0 / jnp.sum(jnp.exp(glog - gmax), axis=-1, keepdims=True)
    lo = N_GROUPS + EXP_PER_GROUP * gsel
    el = jnp.where((lane_f >= lo) & (lane_f < lo + EXP_PER_GROUP), lg, -3e38)
    m_a = jnp.max(el, axis=-1, keepdims=True)
    i_a = first_of(el == m_a)
    el2 = jnp.where(lane_f == i_a, -3e38, el)
    m_b = jnp.max(el2, axis=-1, keepdims=True)
    i_b = first_of(el2 == m_b)
    r = jnp.exp(m_b - m_a)
    w_a = p_g / (1.0 + r)
    w_b = p_g * r / (1.0 + r)
    e_a = i_a - N_GROUPS
    e_b = i_b - N_GROUPS
    rw_ref[0] = jnp.where(lane == 0, w_a, jnp.where(lane == 1, w_b, 0.0))
    ri_ref[0] = jnp.where(lane == 0, e_a, jnp.where(lane == 1, e_b, 0.0)).astype(jnp.int32)

    @pl.when((pl.program_id(0) == 0) & (pl.program_id(1) == 0))
    def _():
        cnt_ref[...] = jnp.zeros_like(cnt_ref)

    hot = jnp.where((lane_f == e_a) | (lane_f == e_b), 1.0, 0.0)
    cnt_ref[...] += jnp.sum(hot, axis=0, keepdims=True)


def _mid(x, yrg, ynsa, woa, wob, gx, wq, gq, kx, vx, wo, gm, wrh, wrl, br):
    bsz, seq, _ = x.shape
    tm = min(TM_MID, seq)
    mlen = kx.shape[1]
    n_i = seq // tm
    full = lambda a: pl.BlockSpec(a.shape, lambda b, i: (0,) * a.ndim)
    tok = lambda w: pl.BlockSpec((1, tm, w), lambda b, i: (b, i, 0))
    memb = pl.BlockSpec((1, mlen, D_MODEL), lambda b, i: (b, 0, 0))
    xt_spec = pl.BlockSpec((tm * ROW_TILE, LANES), lambda b, i: (b * n_i + i, 0))
    return pl.pallas_call(
        _mid_kernel,
        grid=(bsz, seq // tm),
        in_specs=[tok(D_MODEL), tok(RG_WIDTH), tok(NSA_WIDTH), full(woa), full(wob), full(gx), full(wq), full(gq),
                  memb, memb, full(wo), full(gm), full(wrh), full(wrl), full(br)],
        out_specs=[tok(D_MODEL), xt_spec, tok(LANES), tok(LANES), pl.BlockSpec((1, LANES), lambda b, i: (0, 0))],
        out_shape=[jax.ShapeDtypeStruct((bsz, seq, D_MODEL), F32),
                   jax.ShapeDtypeStruct((bsz * seq * ROW_TILE, LANES), F32),
                   jax.ShapeDtypeStruct((bsz, seq, LANES), F32), jax.ShapeDtypeStruct((bsz, seq, LANES), jnp.int32),
                   jax.ShapeDtypeStruct((1, LANES), F32)],
        compiler_params=_cparams(2),
    )(x, yrg, ynsa, woa, wob, gx, wq, gq, kx, vx, wo, gm, wrh, wrl, br)


def _dest_kernel(ri_ref, pstart_ref, dest_ref, run_ref):
    @pl.when(pl.program_id(0) == 0)
    def _():
        run_ref[...] = jnp.zeros_like(run_ref)

    ri = ri_ref[...]
    tm = ri.shape[0]
    lane = lax.broadcasted_iota(jnp.int32, ri.shape, 1)
    e_a = ri[:, 0:1]
    e_b = ri[:, 1:2]
    hot_a = lane == e_a
    hot_b = lane == e_b
    hot = jnp.where(hot_a | hot_b, 1.0, 0.0)
    row = lax.broadcasted_iota(jnp.int32, (tm, tm), 0)
    col = lax.broadcasted_iota(jnp.int32, (tm, tm), 1)
    earlier = jnp.where(col < row, 1.0, 0.0).astype(BF16)
    base = _dot(earlier, hot.astype(BF16)) + run_ref[...] + pstart_ref[...]
    d_a = jnp.sum(jnp.where(hot_a, base, 0.0), axis=-1, keepdims=True)
    d_b = jnp.sum(jnp.where(hot_b, base, 0.0), axis=-1, keepdims=True)
    dest_ref[...] = jnp.where(lane == 0, d_a, jnp.where(lane == 1, d_b, 0.0)).astype(jnp.int32)
    run_ref[...] += jnp.sum(hot, axis=0, keepdims=True)


def _dest(ri2, pstart):
    n_tok = ri2.shape[0]
    tm = min(TM_DEST, n_tok)
    return pl.pallas_call(
        _dest_kernel,
        grid=(n_tok // tm,),
        in_specs=[pl.BlockSpec((tm, LANES), lambda i: (i, 0)), pl.BlockSpec((1, LANES), lambda i: (0, 0))],
        out_specs=pl.BlockSpec((tm, LANES), lambda i: (i, 0)),
        out_shape=jax.ShapeDtypeStruct((n_tok, LANES), jnp.int32),
        scratch_shapes=[pltpu.VMEM((1, LANES), F32)],
        compiler_params=_cparams(1),
    )(ri2, pstart)


def _store_row_tiles(ref, val):
    n = val.shape[0]
    for c in range(ROW_TILE):
        ref[pl.ds(c, n, stride=ROW_TILE), :] = val[:, c * LANES:(c + 1) * LANES]


def _load_row_tiles(ref, n):
    return [ref[pl.ds(c, n, stride=ROW_TILE), :] for c in range(ROW_TILE)]


def _token_rows(ref, t):
    return ref.at[pl.ds(pl.multiple_of(t * ROW_TILE, ROW_TILE), ROW_TILE), :]


def _dispatch_kernel(cnt_ref, pstart_ref, da_ref, db_ref, xt_ref, xs_ref, zrow, sem, zsem):
    tm = da_ref.shape[2]

    @pl.when(pl.program_id(0) == 0)
    def _():
        zrow[...] = jnp.zeros_like(zrow)

        def per_expert(e, c):
            used = cnt_ref[e]
            padded = (used + MOE_TB - 1) // MOE_TB * MOE_TB
            base = pstart_ref[e]

            def fill(r, c2):
                pltpu.make_async_copy(zrow, _token_rows(xs_ref, base + r), zsem).start()
                return c2

            def fill_done(r, c2):
                pltpu.make_async_copy(zrow, _token_rows(xs_ref, 0), zsem).wait()
                return c2

            lax.fori_loop(used, padded, fill, 0)
            lax.fori_loop(used, padded, fill_done, 0)
            return c

        lax.fori_loop(0, N_EXPERTS, per_expert, 0)

        last = N_EXPERTS - 1
        first_unused = (pstart_ref[last] + (cnt_ref[last] + MOE_TB - 1) // MOE_TB * MOE_TB) // MOE_TB

        def per_block(j, c):
            def fill(r, c2):
                pltpu.make_async_copy(zrow, _token_rows(xs_ref, j * MOE_TB + r), zsem).start()
                return c2

            def fill_done(r, c2):
                pltpu.make_async_copy(zrow, _token_rows(xs_ref, 0), zsem).wait()
                return c2

            lax.fori_loop(0, MOE_TB, fill, 0, unroll=DMA_UNROLL)
            lax.fori_loop(0, MOE_TB, fill_done, 0, unroll=DMA_UNROLL)
            return c

        lax.fori_loop(first_unused, xs_ref.shape[0] // (ROW_TILE * MOE_TB), per_block, 0)

    def issue(t, c):
        pltpu.make_async_copy(_token_rows(xt_ref, t), _token_rows(xs_ref, da_ref[0, 0, t]), sem).start(priority=0)
        pltpu.make_async_copy(_token_rows(xt_ref, t), _token_rows(xs_ref, db_ref[0, 0, t]), sem).start(priority=1)
        return c

    lax.fori_loop(0, tm, issue, 0, unroll=DMA_UNROLL)

    def drain(t, c):
        pltpu.make_async_copy(_token_rows(xt_ref, 0), _token_rows(xs_ref, 0), sem).wait()
        pltpu.make_async_copy(_token_rows(xt_ref, 0), _token_rows(xs_ref, 0), sem).wait()
        return c

    lax.fori_loop(0, tm, drain, 0, unroll=DMA_UNROLL)


def _dispatch(cnt, pstart, da, db, xt_rows, n_pad):
    n_tiles, _, tm = da.shape
    smem = pl.BlockSpec((1, 1, tm), lambda i, c, p: (i, 0, 0), memory_space=pltpu.SMEM)
    grid_spec = pltpu.PrefetchScalarGridSpec(
        num_scalar_prefetch=2,
        grid=(n_tiles,),
        in_specs=[smem, smem, pl.BlockSpec((tm * ROW_TILE, LANES), lambda i, c, p: (i, 0))],
        out_specs=pl.BlockSpec(memory_space=pl.ANY),
        scratch_shapes=[pltpu.VMEM((ROW_TILE, LANES), F32), pltpu.SemaphoreType.DMA(()),
                        pltpu.SemaphoreType.DMA(())],
    )
    return pl.pallas_call(
        _dispatch_kernel,
        grid_spec=grid_spec,
        out_shape=jax.ShapeDtypeStruct((n_pad * ROW_TILE, LANES), F32),
        compiler_params=pltpu.CompilerParams(dimension_semantics=("arbitrary",), has_side_effects=True,
                                             vmem_limit_bytes=VMEM_LIMIT),
    )(cnt, pstart, da, db, xt_rows)


def _ffn_kernel(bexp_ref, nused_ref, xs_ref, w1_ref, w3_ref, w2_ref, ys_ref):
    del bexp_ref
    j = pl.program_id(0)

    @pl.when(j < nused_ref[0])
    def _():
        xb = jnp.concatenate(_load_row_tiles(xs_ref, MOE_TB), axis=-1).astype(BF16)
        a = _dot(xb, w1_ref[0])
        h = a * _sigmoid(a) * _dot(xb, w3_ref[0])
        _store_row_tiles(ys_ref, _dot(h.astype(BF16), w2_ref[0]))

    @pl.when(j >= nused_ref[0])
    def _():
        ys_ref[...] = jnp.zeros_like(ys_ref)


def _ffn(blk_exp, n_used, xs, w1, w3, w2):
    n_blocks = xs.shape[0] // (MOE_TB * ROW_TILE)
    rows = pl.BlockSpec((MOE_TB * ROW_TILE, LANES), lambda j, be, nu: (j, 0))
    used_rows = pl.BlockSpec((MOE_TB * ROW_TILE, LANES), lambda j, be, nu: (jnp.minimum(j, nu[0] - 1), 0))
    grid_spec = pltpu.PrefetchScalarGridSpec(
        num_scalar_prefetch=2,
        grid=(n_blocks,),
        in_specs=[used_rows,
                  pl.BlockSpec((1, D_MODEL, D_EXPERT), lambda j, be, nu: (be[j], 0, 0)),
                  pl.BlockSpec((1, D_MODEL, D_EXPERT), lambda j, be, nu: (be[j], 0, 0)),
                  pl.BlockSpec((1, D_EXPERT, D_MODEL), lambda j, be, nu: (be[j], 0, 0))],
        out_specs=rows,
    )
    return pl.pallas_call(
        _ffn_kernel,
        grid_spec=grid_spec,
        out_shape=jax.ShapeDtypeStruct(xs.shape, F32),
        compiler_params=_cparams(1),
    )(blk_exp, n_used, xs, w1, w3, w2)


def _combine_kernel(da_ref, db_ref, da_next_ref, db_next_ref, h_ref, rw_ref, ys_ref, o_ref, ya, yb, sems):
    tm = da_ref.shape[2]
    i = pl.program_id(0)
    slot = i % 2

    def start_gather(a_ref, b_ref, s):
        def issue(t, c):
            pltpu.make_async_copy(_token_rows(ys_ref, a_ref[0, 0, t]), _token_rows(ya.at[s], t),
                                  sems.at[s]).start(priority=0)
            pltpu.make_async_copy(_token_rows(ys_ref, b_ref[0, 0, t]), _token_rows(yb.at[s], t),
                                  sems.at[s]).start(priority=1)
            return c

        lax.fori_loop(0, tm, issue, 0, unroll=DMA_UNROLL)

    @pl.when(i == 0)
    def _():
        start_gather(da_ref, db_ref, slot)

    @pl.when(i + 1 < pl.num_programs(0))
    def _():
        start_gather(da_next_ref, db_next_ref, 1 - slot)

    def drain(t, c):
        pltpu.make_async_copy(_token_rows(ys_ref, 0), _token_rows(ya.at[slot], 0), sems.at[slot]).wait()
        pltpu.make_async_copy(_token_rows(ys_ref, 0), _token_rows(yb.at[slot], 0), sems.at[slot]).wait()
        return c

    lax.fori_loop(0, tm, drain, 0, unroll=DMA_UNROLL)
    rw = rw_ref[...]
    mix = [rw[:, 0:1] * a + rw[:, 1:2] * b
           for a, b in zip(_load_row_tiles(ya.at[slot], tm), _load_row_tiles(yb.at[slot], tm))]
    o_ref[...] = h_ref[...] + jnp.concatenate(mix, axis=-1)


def _combine(da, db, h2, rw, ys):
    n_tiles, _, tm = da.shape
    n_tok = h2.shape[0]
    smem = pl.BlockSpec((1, 1, tm), lambda i: (i, 0, 0), memory_space=pltpu.SMEM)
    smem_next = pl.BlockSpec((1, 1, tm), lambda i: (jnp.minimum(i + 1, n_tiles - 1), 0, 0), memory_space=pltpu.SMEM)
    row = lambda w: pl.BlockSpec((tm, w), lambda i: (i, 0))
    slots = pltpu.VMEM((2, tm * ROW_TILE, LANES), F32)
    return pl.pallas_call(
        _combine_kernel,
        grid=(n_tiles,),
        in_specs=[smem, smem, smem_next, smem_next, row(D_MODEL), row(LANES), pl.BlockSpec(memory_space=pl.ANY)],
        out_specs=row(D_MODEL),
        out_shape=jax.ShapeDtypeStruct((n_tok, D_MODEL), F32),
        scratch_shapes=[slots, slots, pltpu.SemaphoreType.DMA((2,))],
        compiler_params=_cparams(1),
    )(da, db, da, db, h2, rw, ys)


def _rel_bucket_np(dist):
    n = np.maximum(dist, 0)
    max_exact = NUM_BUCKETS // 2
    nf = np.maximum(n, 1).astype(np.float32)
    large = max_exact + (np.log(nf / max_exact) / math.log(MAX_DIST / max_exact)
                         * (NUM_BUCKETS - max_exact)).astype(np.int32)
    large = np.minimum(large, NUM_BUCKETS - 1)
    return np.where(n < max_exact, n, large).astype(np.int32)


def _toeplitz(vec, rows):
    width = vec.shape[-1] - 1
    flat = jnp.tile(vec, (1,) * (vec.ndim - 1) + (rows,))[..., :rows * width]
    return flat.reshape(vec.shape[:-1] + (rows, width))


def _bias_tables(rel_bias, seq):
    n_chunk = seq // CMP_STRIDE
    n_tiles = seq // TQ
    table = rel_bias.T.astype(F32)

    wide = NEAR + TQ
    k = np.arange(wide + 1)
    dw = np.where(k < NEAR, WINDOW - k, WINDOW + wide + 1 - k)
    used = (k < NEAR) | (k > wide + 1 - TQ)
    vals = table[:, _rel_bucket_np(dw)]

    n_var = WINDOW // TQ
    first_key = WINDOW - TQ * np.arange(n_var + 1)[:, None, None]
    in_seq = np.arange(NEAR)[None, None, :] >= first_key

    def near_tile(valid):
        t = _toeplitz(jnp.where(valid[None, :], vals, NEG_INF), TQ)[:, :, :NEAR]
        t = t.reshape(NSA_KV, NSA_HPG, TQ, NEAR).transpose(1, 0, 2, 3).reshape(1, QROWS2, NEAR)
        return jnp.where(in_seq, t, NEG_INF)

    bias_w = near_tile(used & (dw >= 0) & (dw < WINDOW))
    bias_s = near_tile(used & (dw >= 0))
    bias_far = table[:, NUM_BUCKETS - 1].reshape(NSA_KV, NSA_HPG, 1).transpose(1, 0, 2)
    bias_far = jnp.broadcast_to(bias_far, (NSA_HPG, NSA_KV, TQ)).reshape(QROWS2, 1)

    r = np.arange(CMP_STRIDE)[:, None]
    k = np.arange(2 * n_chunk + 1)[None, :]
    lag = 2 * n_chunk + 1 - k
    valid = (k > n_chunk + 1) & (CMP_STRIDE * lag + r >= CMP_L - 1)
    vals = table[:, _rel_bucket_np(CMP_STRIDE * lag + r - CMP_L // 2)]
    full = _toeplitz(jnp.where(valid[None], vals, NEG_INF), n_chunk)[..., :n_chunk]
    full = jnp.where(np.arange(n_chunk) < n_chunk - 1, full, NEG_INF)
    a4 = TQ // CMP_STRIDE
    full = full.reshape(NSA_KV, NSA_HPG, CMP_STRIDE, n_tiles, a4, n_chunk).transpose(3, 1, 0, 4, 2, 5)
    bias_c = full.reshape(n_tiles, QROWS2, n_chunk)
    return tuple(LOG2E * t for t in (bias_c, bias_w, bias_s, bias_far))


def _selection_tables(seq):
    n_chunk = seq // CMP_STRIDE
    n_blk = seq // SEL_L
    c = np.arange(n_chunk)
    n = np.arange(n_blk)
    start = c * CMP_STRIDE
    overlap_t = ((start[None, :] <= n[:, None] * SEL_L + SEL_L - 1) & (start[None, :] + CMP_L - 1 >= n[:, None] * SEL_L)
                 & (c < n_chunk - 1)[None, :])
    pos = np.arange(seq + WINDOW) - WINDOW
    lane_blk = np.arange(LANES) % n_blk
    hit = (pos[:, None] >= 0) & (pos[:, None] // SEL_L == lane_blk[None, :]) & (np.arange(LANES) < 2 * n_blk)[None, :]
    return jnp.asarray(overlap_t, BF16), jnp.asarray(np.where(hit, -UNSEL_PENALTY, 0.0), BF16)


def _block_ones(width, group):
    idx = np.arange(width) // group
    return jnp.asarray((idx[:, None] == idx[None, :]) / group, BF16)


def _block_diag(w):
    nb, n, m = w.shape
    eye = jnp.eye(nb, dtype=w.dtype)
    return jnp.einsum('hij,hg->higj', w, eye).reshape(nb * n, nb * m)


def _compress_weights(w1, w2, pos):
    half_l = CMP_L // 2
    parts = []
    for half in range(2):
        wh = w1[half * half_l * HEAD_DIM:(half + 1) * half_l * HEAD_DIM].reshape(half_l, HEAD_DIM, CMP_HIDDEN)
        z = jnp.zeros_like(wh)
        for g in range(NSA_KV):
            grp = [wh if gg == g else z for gg in range(NSA_KV)]
            parts.append(jnp.stack(grp, axis=1).reshape(half_l * KV_W, CMP_HIDDEN))
    w1cat = jnp.concatenate(parts, axis=1).astype(BF16)
    w2bd = _block_diag(jnp.stack([w2] * NSA_KV)).astype(BF16)
    prow = [jnp.tile(pos[half * half_l:(half + 1) * half_l][:, None, :], (1, NSA_KV, 1)).reshape(-1)
            for half in range(2)]
    pmat = jnp.zeros((8, half_l * KV_W), F32).at[0].set(prow[0]).at[1].set(prow[1]).astype(BF16)
    return w1cat, w2bd, pmat


def kernel(x, mem, rel_bias, norm_mix, w_in, rg_conv_w, rg_conv_b, rg_w_r, rg_b_r, rg_w_i, rg_b_i, rg_lambda, nsa_g_q, nsa_g_kc, nsa_g_ks, nsa_g_kw, cmp_pos_k, cmp_pos_v, cmp_k_w1, cmp_k_w2, cmp_v_w1, cmp_v_w2, out_g_rg, out_g_nsa, w_out, norm_x, norm_mem, xa_w_q, xa_w_kv, xa_w_o, xa_g_q, xa_g_k, norm_moe, router_g_w, router_g_b, router_e_w, router_e_b, exp_w1, exp_w3, exp_w2):
    bsz, seq, _ = x.shape
    n_tok = bsz * seq
    assert seq % FAR_TK == 0 and 2 * (seq // SEL_L) <= LANES and norm_mix.shape[0] == 1
    l = 0
    row = lambda v: v.reshape(1, -1).astype(F32)

    perm = np.array([(half * NSA_HPG + p) * HEAD_DIM + d
                     for p in range(NSA_HPG) for half in range(NSA_KV) for d in range(HEAD_DIM)])
    offs = np.cumsum([0, RG_WIDTH, RG_WIDTH, NSA_WIDTH] + [KV_W] * 6)
    w = w_in[l]
    wrg = w[:, :offs[2]].astype(BF16)
    wq = w[:, offs[2]:offs[3]][:, perm].astype(BF16)
    wkv = w[:, offs[3]:offs[9]].astype(BF16)
    wgl = jnp.pad(w[:, offs[9]:], ((0, 0), (0, LANES - 3 * NSA_HEADS))).astype(BF16)
    ones64 = _block_ones(NSA_WIDTH, HEAD_DIM)
    gq = row(jnp.tile(nsa_g_q[l], NSA_HEADS) * (HEAD_DIM ** -0.5 * LOG2E))
    u, gate, q, kc, vc, ks, vs, kw, vw, gates = _inproj(
        x.reshape(n_tok, D_MODEL), row(norm_mix[l]), wrg, wq, wkv, wgl, gq,
        row(jnp.tile(nsa_g_ks[l], NSA_KV)), row(jnp.tile(nsa_g_kw[l], NSA_KV)), ones64)

    wg = (0.5 * jnp.concatenate([_block_diag(rg_w_r[l]), _block_diag(rg_w_i[l])], axis=1)).astype(BF16)
    bg = 0.5 * jnp.concatenate([rg_b_r[l], rg_b_i[l]]).reshape(1, -1)
    y_rg = _rglru(u.reshape(bsz, seq, RG_WIDTH), gate.reshape(bsz, seq, RG_WIDTH),
                  rg_conv_w[l].reshape(CONV_W, RG_WIDTH), row(rg_conv_b[l]), wg, bg, row(rg_lambda[l]),
                  row(out_g_rg[l]))

    n_chunk = seq // CMP_STRIDE
    w1k, w2k, pk = _compress_weights(cmp_k_w1[l], cmp_k_w2[l], cmp_pos_k[l])
    w1v, w2v, pv = _compress_weights(cmp_v_w1[l], cmp_v_w2[l], cmp_pos_v[l])
    kcmp, vcmp = _compress(kc.reshape(bsz, n_chunk, CMP_STRIDE * KV_W), vc.reshape(bsz, n_chunk, CMP_STRIDE * KV_W),
                           w1k, w2k, pk, w1v, w2v, pv, row(jnp.tile(nsa_g_kc[l], NSA_KV)),
                           ones64[:KV_W, :KV_W])
    padw = lambda t: jnp.pad(t.reshape(bsz, seq, KV_W), ((0, 0), (WINDOW, 0), (0, 0)))
    bias_c, bias_w, bias_s, bias_far = _bias_tables(rel_bias, seq)
    overlap_t, penalty = _selection_tables(seq)
    ksx = jnp.concatenate([padw(ks), jnp.broadcast_to(penalty, (bsz,) + penalty.shape)], axis=-1)
    y_nsa = _nsa(q.reshape(bsz, seq, NSA_WIDTH), gates.reshape(bsz, seq, LANES), kcmp, vcmp,
                 ksx, padw(vs), padw(kw), padw(vw), overlap_t, bias_c, bias_w, bias_s, bias_far,
                 row(out_g_nsa[l][perm]))

    kx, vx = _memkv(mem, row(norm_mem[l]), xa_w_kv[l].astype(BF16), row(xa_g_k[l]))
    wo_mix = w_out[l]
    wr = jnp.pad(jnp.concatenate([router_g_w[l], router_e_w[l]], axis=1),
                 ((0, 0), (0, LANES - N_GROUPS - N_EXPERTS)))
    wr_hi = wr.astype(BF16)
    br = jnp.pad(jnp.concatenate([router_g_b[l], router_e_b[l]]), (0, LANES - N_GROUPS - N_EXPERTS)).reshape(1, -1)
    h2, xt, rw, ri, counts = _mid(
        x, y_rg, y_nsa, wo_mix[:RG_WIDTH].astype(BF16), wo_mix[RG_WIDTH:][perm].astype(BF16), row(norm_x[l]),
        xa_w_q[l].astype(BF16), row(xa_g_q[l] * (X_HEAD_DIM ** -0.5 * LOG2E)), kx, vx, xa_w_o[l].astype(BF16),
        row(norm_moe[l]), wr_hi, (wr - wr_hi.astype(F32)).astype(BF16), br)

    n_slots = 2 * n_tok
    n_blocks = n_slots // MOE_TB + N_EXPERTS
    n_pad = n_blocks * MOE_TB
    cnt = counts[0, :N_EXPERTS].astype(jnp.int32)
    pcnt = (cnt + MOE_TB - 1) // MOE_TB * MOE_TB
    pends = jnp.cumsum(pcnt)
    pstart = jnp.pad((pends - pcnt).astype(F32), (0, LANES - N_EXPERTS)).reshape(1, LANES)
    blk_exp = jnp.minimum(jnp.sum(pends[None, :] <= jnp.arange(n_blocks, dtype=jnp.int32)[:, None] * MOE_TB, axis=1),
                          N_EXPERTS - 1).astype(jnp.int32)
    n_used = (pends[-1:] // MOE_TB).astype(jnp.int32)
    dest = _dest(ri.reshape(n_tok, LANES), pstart)
    tmd = min(TM_DMA, n_tok)
    da = dest[:, 0].reshape(n_tok // tmd, 1, tmd)
    db = dest[:, 1].reshape(n_tok // tmd, 1, tmd)
    xs = _dispatch(cnt, (pends - pcnt).astype(jnp.int32), da, db, xt, n_pad)
    ys = _ffn(blk_exp, n_used, xs, exp_w1[l].astype(BF16), exp_w3[l].astype(BF16), exp_w2[l].astype(BF16))
    out = _combine(da, db, h2.reshape(n_tok, D_MODEL), rw.reshape(n_tok, LANES), ys)
    return out.reshape(bsz, seq, D_MODEL)
```

```python
import math

import numpy as np
import jax
import jax.numpy as jnp
from jax import lax
from jax.experimental import pallas as pl
from jax.experimental.pallas import tpu as pltpu

F32 = jnp.float32
BF16 = jnp.bfloat16

D_MODEL = 1024
RG_WIDTH = 512
RG_BLOCKS = 8
RG_BLOCK = 64
CONV_W = 4
RG_C = 8.0
NSA_WIDTH = 512
NSA_HEADS = 8
HEAD_DIM = 64
NSA_KV = 2
NSA_HPG = 4
KV_W = 128
CMP_L = 32
CMP_STRIDE = 16
CMP_HIDDEN = 256
SEL_L = 64
N_SEL = 8
WINDOW = 512
NUM_BUCKETS = 32
MAX_DIST = 128
X_HEADS = 4
X_HEAD_DIM = 256
N_GROUPS = 4
EXP_PER_GROUP = 8
N_EXPERTS = 32
D_EXPERT = 512
EPS = 1e-6
LOG2E = 1.0 / math.log(2.0)
NEG_INF = -1e30
MASKED_BELOW = -1e29
SEL_FORCE = 1e9
LANES = 128

TQ = 64
NEAR = WINDOW + TQ
FAR_TK = 512
QROWS2 = NSA_HEADS * TQ
UNSEL_PENALTY = 2.0 ** 100
NSA_NB = 4

TM_PROJ = 1024
TM_MID = 512
TM_DEST = 512
TM_DMA = 512
DMA_UNROLL = 8
MOE_TB = 512
ROW_TILE = D_MODEL // LANES
RG_CHUNK = 256
SCAN_ROWS = 8
SCAN_UNROLL = 4
VMEM_LIMIT = 56 * 1024 * 1024


def _cparams(n_axes):
    return pltpu.CompilerParams(dimension_semantics=("arbitrary",) * n_axes,
                                vmem_limit_bytes=VMEM_LIMIT)


def _dot(a, b):
    return jnp.dot(a, b, preferred_element_type=F32)


def _dot_nt(a, b):
    return lax.dot_general(a, b, (((1,), (1,)), ((), ())), preferred_element_type=F32)


def _gelu_tanh(x):
    c = math.sqrt(2.0 / math.pi)
    half = 0.5 * x
    return half + half * jnp.tanh(x * (c + (c * 0.044715) * (x * x)))


def _sigmoid(x):
    return 0.5 * jnp.tanh(0.5 * x) + 0.5


def _rms(x, g):
    return x * lax.rsqrt(jnp.mean(x * x, axis=-1, keepdims=True) + EPS) * g


def _group_rms(x, ones_blk, g):
    ms = _dot((x * x).astype(BF16), ones_blk)
    return x * lax.rsqrt(ms + EPS) * g


def _inproj_kernel(x_ref, g_ref, wrg_ref, wq_ref, wkv_ref, wgl_ref, gq_ref, gks_ref, gkw_ref, ones_ref,
                   u_ref, gate_ref, q_ref, kc_ref, vc_ref, ks_ref, vs_ref, kw_ref, vw_ref, gates_ref):
    xb = _rms(x_ref[...], g_ref[...]).astype(BF16)
    rg = _dot(xb, wrg_ref[...])
    u_ref[...] = rg[:, :RG_WIDTH].astype(BF16)
    gate_ref[...] = rg[:, RG_WIDTH:].astype(BF16)
    q = _dot(xb, wq_ref[...])
    q_ref[...] = _group_rms(q, ones_ref[...], gq_ref[...]).astype(BF16)
    kv = _dot(xb, wkv_ref[...])
    ones_kv = ones_ref[:KV_W, :KV_W]
    kc_ref[...] = kv[:, 0 * KV_W:1 * KV_W].astype(BF16)
    vc_ref[...] = kv[:, 1 * KV_W:2 * KV_W].astype(BF16)
    ks_ref[...] = _group_rms(kv[:, 2 * KV_W:3 * KV_W], ones_kv, gks_ref[...]).astype(BF16)
    vs_ref[...] = kv[:, 3 * KV_W:4 * KV_W].astype(BF16)
    kw_ref[...] = _group_rms(kv[:, 4 * KV_W:5 * KV_W], ones_kv, gkw_ref[...]).astype(BF16)
    vw_ref[...] = kv[:, 5 * KV_W:6 * KV_W].astype(BF16)
    gates_ref[...] = _sigmoid(_dot(xb, wgl_ref[...]))


def _inproj(x2, g, wrg, wq, wkv, wgl, gq, gks, gkw, ones_blk):
    n_tok = x2.shape[0]
    tm = min(TM_PROJ, n_tok)
    full = lambda a: pl.BlockSpec(a.shape, lambda i: (0,) * a.ndim)
    row = lambda w: pl.BlockSpec((tm, w), lambda i: (i, 0))
    outs = [(RG_WIDTH, BF16), (RG_WIDTH, BF16), (NSA_WIDTH, BF16)] + [(KV_W, BF16)] * 6 + [(LANES, F32)]
    return pl.pallas_call(
        _inproj_kernel,
        grid=(n_tok // tm,),
        in_specs=[row(D_MODEL)] + [full(a) for a in (g, wrg, wq, wkv, wgl, gq, gks, gkw, ones_blk)],
        out_specs=[row(w) for w, _ in outs],
        out_shape=[jax.ShapeDtypeStruct((n_tok, w), dt) for w, dt in outs],
        compiler_params=_cparams(1),
    )(x2, g, wrg, wq, wkv, wgl, gq, gks, gkw, ones_blk)


def _rglru_kernel(u_ref, gate_ref, cw_ref, cb_ref, wg_ref, bg_ref, lam_ref, og_ref, y_ref, upad, a_s, h_s):
    seq = u_ref.shape[1]
    upad[0:8, :] = jnp.zeros((8, RG_WIDTH), F32)
    upad[8:8 + seq, :] = u_ref[0].astype(F32)
    neg_lam = -lam_ref[...]
    softplus = jnp.maximum(neg_lam, 0.0) + jnp.log(1.0 + jnp.exp(-jnp.abs(neg_lam)))
    log2_a_half = (-0.5 * RG_C * LOG2E) * softplus
    ch = min(RG_CHUNK, seq)
    for c in range(seq // ch):
        r0 = c * ch
        uc = cb_ref[...]
        for k in range(CONV_W):
            off = 8 + r0 - (CONV_W - 1) + k
            uc = uc + cw_ref[k:k + 1, :] * upad[off:off + ch, :]
        th = jnp.tanh(_dot(uc.astype(BF16), wg_ref[...]) + bg_ref[...])
        a = jnp.exp2(log2_a_half * th[:, :RG_WIDTH] + log2_a_half)
        a_s[r0:r0 + ch, :] = a
        s = 1.0 - a * a
        h_s[r0:r0 + ch, :] = s * lax.rsqrt(jnp.maximum(s, 1e-30)) * (0.5 * th[:, RG_WIDTH:] + 0.5) * uc

    row = lax.broadcasted_iota(jnp.int32, (SCAN_ROWS, RG_WIDTH), 0)

    def block(j, h_prev):
        rows = pl.ds(pl.multiple_of(j * SCAN_ROWS, SCAN_ROWS), SCAN_ROWS)
        a = a_s[rows, :]
        b = h_s[rows, :]
        k = 1
        while k < SCAN_ROWS:
            keep = row >= k
            b = jnp.where(keep, a * pltpu.roll(b, k, 0) + b, b)
            a = jnp.where(keep, a * pltpu.roll(a, k, 0), a)
            k *= 2
        h = a * h_prev + b
        h_s[rows, :] = h
        return h[SCAN_ROWS - 1:SCAN_ROWS, :]

    lax.fori_loop(0, seq // SCAN_ROWS, block, jnp.zeros((1, RG_WIDTH), F32), unroll=SCAN_UNROLL)

    for c in range(seq // ch):
        r0 = c * ch
        y = _gelu_tanh(gate_ref[0, r0:r0 + ch, :].astype(F32)) * h_s[r0:r0 + ch, :]
        y_ref[0, r0:r0 + ch, :] = _rms(y, og_ref[...]).astype(BF16)


def _rglru(u3, gate3, cw, cb, wg, bg, lam, og):
    bsz, seq, _ = u3.shape
    full = lambda a: pl.BlockSpec(a.shape, lambda b: (0,) * a.ndim)
    blk = pl.BlockSpec((1, seq, RG_WIDTH), lambda b: (b, 0, 0))
    return pl.pallas_call(
        _rglru_kernel,
        grid=(bsz,),
        in_specs=[blk, blk] + [full(a) for a in (cw, cb, wg, bg, lam, og)],
        out_specs=blk,
        out_shape=jax.ShapeDtypeStruct((bsz, seq, RG_WIDTH), BF16),
        scratch_shapes=[pltpu.VMEM((seq + 8, RG_WIDTH), F32), pltpu.VMEM((seq, RG_WIDTH), F32),
                        pltpu.VMEM((seq, RG_WIDTH), F32)],
        compiler_params=_cparams(1),
    )(u3, gate3, cw, cb, wg, bg, lam, og)


def _compress_kernel(kx_ref, vx_ref, w1k_ref, w2k_ref, pk_ref, w1v_ref, w2v_ref, pv_ref, gk_ref, ones_ref,
                     ko_ref, vo_ref):
    n_chunk = kx_ref.shape[1]
    half = NSA_KV * CMP_HIDDEN

    def mlp(x_ref, w1_ref, w2_ref, p_ref):
        ab = _dot(x_ref[0], w1_ref[...])
        pos = _dot(p_ref[...], w1_ref[...])
        hid = ab[:, :half] + pltpu.roll(ab[:, half:], n_chunk - 1, 0) + (pos[0:1, :half] + pos[1:2, half:])
        return _dot(_gelu_tanh(hid).astype(BF16), w2_ref[...])

    kc = mlp(kx_ref, w1k_ref, w2k_ref, pk_ref)
    ko_ref[0] = _group_rms(kc, ones_ref[...], gk_ref[...]).astype(BF16)
    vo_ref[0] = mlp(vx_ref, w1v_ref, w2v_ref, pv_ref).astype(BF16)


def _compress(kx, vx, w1k, w2k, pk, w1v, w2v, pv, gk, ones_kv):
    bsz, n_chunk, width = kx.shape
    full = lambda a: pl.BlockSpec(a.shape, lambda b: (0,) * a.ndim)
    xin = pl.BlockSpec((1, n_chunk, width), lambda b: (b, 0, 0))
    out = pl.BlockSpec((1, n_chunk, KV_W), lambda b: (b, 0, 0))
    return pl.pallas_call(
        _compress_kernel,
        grid=(bsz,),
        in_specs=[xin, xin] + [full(a) for a in (w1k, w2k, pk, w1v, w2v, pv, gk, ones_kv)],
        out_specs=[out, out],
        out_shape=[jax.ShapeDtypeStruct((bsz, n_chunk, KV_W), BF16)] * 2,
        compiler_params=_cparams(1),
    )(kx, vx, w1k, w2k, pk, w1v, w2v, pv, gk, ones_kv)


def _nsa_kernel(q_ref, gates_ref, kcmp_ref, vcmp_ref, ksx_ref, vsp_ref, kwp_ref, vwp_ref, ovt_ref,
                bc_ref, bw_ref, bs_ref, bf_ref, og_ref, y_ref):
    i = pl.program_id(1)
    t0 = pl.multiple_of(i * TQ, TQ)
    n_blk = ovt_ref.shape[0]
    lane = lax.broadcasted_iota(jnp.int32, (TQ, LANES), 1)
    lo_half = lane < HEAD_DIM
    n_batch = q_ref.shape[0]

    def with_ones(v):
        return jnp.concatenate([v, jnp.ones_like(v)], axis=1)

    def near_part(bb):
        pieces = []
        for p in range(NSA_HPG):
            qs = q_ref[bb, :, p * LANES:(p + 1) * LANES]
            zero = jnp.zeros_like(qs)
            pieces += [jnp.where(lo_half, qs, zero), jnp.where(lo_half, zero, qs)]
        q8 = jnp.concatenate(pieces, axis=0)

        bc = bc_ref[0]
        lc = _dot_nt(q8, kcmp_ref[bb]) + bc
        ec = jnp.where(bc > MASKED_BELOW, jnp.exp2(lc - jnp.max(lc, axis=-1, keepdims=True)), 0.0)
        sc = jnp.sum(ec, axis=-1, keepdims=True)
        pc = ec / jnp.where(sc > 0.0, sc, 1.0)
        o_c = _dot(pc.astype(BF16), vcmp_ref[bb])

        blocks = [pc[r * TQ:(r + 1) * TQ] for r in range(NSA_HPG * NSA_KV)]
        pcs = jnp.concatenate([sum(blocks[g::NSA_KV]) for g in range(NSA_KV)], axis=0)
        pcs_hi = pcs.astype(BF16)
        pcs_lo = (pcs - pcs_hi.astype(F32)).astype(BF16)
        imp = _dot_nt(ovt_ref[...], pcs_hi) + _dot_nt(ovt_ref[...], pcs_lo)
        blk = lax.broadcasted_iota(jnp.int32, imp.shape, 0)
        forced = (blk == 0) | (blk == i) | (blk == i - 1)
        score = jnp.where(forced, SEL_FORCE, jnp.where(blk > i, -3e38, imp))
        rank = jnp.zeros(imp.shape, F32)
        for m in range(n_blk):
            row = score[m:m + 1, :]
            rank = rank + jnp.where(blk > m, jnp.where(row >= score, 1.0, 0.0), jnp.where(row > score, 1.0, 0.0))
        unsel = jnp.where(rank < N_SEL, 0.0, 1.0)
        unsel_far = jnp.where(blk >= i - WINDOW // SEL_L, 1.0, unsel)
        pad = jnp.zeros((LANES - 2 * n_blk, imp.shape[1]), F32)
        u_t = jnp.concatenate([unsel, unsel_far, pad], axis=0).T
        u_lane = lax.broadcasted_iota(jnp.int32, u_t.shape, 1)
        u_near = jnp.where(u_lane < n_blk, u_t, 0.0).astype(BF16)
        u_far = jnp.where(u_lane >= n_blk, u_t, 0.0).astype(BF16)
        qx_near = jnp.concatenate([q8, jnp.concatenate([u_near] * NSA_HPG, axis=0)], axis=1)
        qx_far = jnp.concatenate([q8, jnp.concatenate([u_far] * NSA_HPG, axis=0)], axis=1)

        lw = _dot_nt(q8, kwp_ref[bb, pl.ds(t0, NEAR), :]) + bw_ref[0]
        ew = jnp.exp2(lw - jnp.max(lw, axis=-1, keepdims=True))
        ow2 = _dot(ew.astype(BF16), with_ones(vwp_ref[bb, pl.ds(t0, NEAR), :]))
        o_w = ow2[:, :LANES] / ow2[:, LANES:]

        ls = _dot_nt(qx_near, ksx_ref[bb, pl.ds(t0, NEAR), :]) + bs_ref[0]
        m1 = jnp.max(ls, axis=-1, keepdims=True)
        e1 = jnp.exp2(ls - m1)
        acc1 = _dot(e1.astype(BF16), with_ones(vsp_ref[bb, pl.ds(t0, NEAR), :]))
        return o_c, o_w, qx_far, (m1, acc1)

    near = [near_part(bb) for bb in range(n_batch)]
    bfar = bf_ref[...]

    def far_step(kf, carry):
        base = pl.multiple_of(WINDOW + kf * FAR_TK, FAR_TK)
        new = []
        for bb in range(n_batch):
            m, acc = carry[bb]
            lf = _dot_nt(near[bb][2], ksx_ref[bb, pl.ds(base, FAR_TK), :]) + bfar
            m_new = jnp.maximum(m, jnp.max(lf, axis=-1, keepdims=True))
            alpha = jnp.exp2(m - m_new)
            e = jnp.exp2(lf - m_new)
            new.append((m_new, alpha * acc + _dot(e.astype(BF16), with_ones(vsp_ref[bb, pl.ds(base, FAR_TK), :]))))
        return tuple(new)

    n_far = (jnp.maximum(t0 - WINDOW, 0) + FAR_TK - 1) // FAR_TK
    far = lax.fori_loop(0, n_far, far_step, tuple(part[3] for part in near))

    for bb in range(n_batch):
        o_c, o_w = near[bb][0], near[bb][1]
        _, acc_s = far[bb]
        o_s = acc_s[:, :LANES] / acc_s[:, LANES:]
        gates = gates_ref[bb]

        def gate_col(j):
            cols = [gates[:, (g * NSA_HPG + p) * 3 + j:(g * NSA_HPG + p) * 3 + j + 1]
                    for p in range(NSA_HPG) for g in range(NSA_KV)]
            return jnp.concatenate(cols, axis=0)

        out = gate_col(0) * o_c + gate_col(1) * o_s + gate_col(2) * o_w
        slabs = [jnp.where(lo_half, out[(2 * p) * TQ:(2 * p + 1) * TQ], out[(2 * p + 1) * TQ:(2 * p + 2) * TQ])
                 for p in range(NSA_HPG)]
        y_ref[bb] = _rms(jnp.concatenate(slabs, axis=-1), og_ref[...]).astype(BF16)


def _nsa(q3, gates3, kcmp, vcmp, ksx, vsp, kwp, vwp, ovt, bias_c, bias_w, bias_s, bias_far, og):
    bsz, seq, _ = q3.shape
    n_chunk = kcmp.shape[1]
    n_var = bias_w.shape[0] - 1
    nb = NSA_NB if bsz % NSA_NB == 0 else 1
    full = lambda a: pl.BlockSpec(a.shape, lambda b, i: (0,) * a.ndim)
    per_b = lambda a: pl.BlockSpec((nb,) + a.shape[1:], lambda b, i: (b,) + (0,) * (a.ndim - 1))
    near = pl.BlockSpec((1, QROWS2, NEAR), lambda b, i: (jnp.minimum(i, n_var), 0, 0))
    return pl.pallas_call(
        _nsa_kernel,
        grid=(bsz // nb, seq // TQ),
        in_specs=[pl.BlockSpec((nb, TQ, NSA_WIDTH), lambda b, i: (b, i, 0)),
                  pl.BlockSpec((nb, TQ, LANES), lambda b, i: (b, i, 0)),
                  per_b(kcmp), per_b(vcmp), per_b(ksx), per_b(vsp), per_b(kwp), per_b(vwp),
                  full(ovt),
                  pl.BlockSpec((1, QROWS2, n_chunk), lambda b, i: (i, 0, 0)),
                  near, near, full(bias_far), full(og)],
        out_specs=pl.BlockSpec((nb, TQ, NSA_WIDTH), lambda b, i: (b, i, 0)),
        out_shape=jax.ShapeDtypeStruct((bsz, seq, NSA_WIDTH), BF16),
        compiler_params=_cparams(2),
    )(q3, gates3, kcmp, vcmp, ksx, vsp, kwp, vwp, ovt, bias_c, bias_w, bias_s, bias_far, og)


def _memkv_kernel(mem_ref, g_ref, wkv_ref, gk_ref, k_ref, v_ref):
    mn = _rms(mem_ref[0], g_ref[...]).astype(BF16)
    kv = _dot(mn, wkv_ref[...])
    for h in range(X_HEADS):
        sl = slice(h * X_HEAD_DIM, (h + 1) * X_HEAD_DIM)
        k_ref[0, :, sl] = _rms(kv[:, sl], gk_ref[...]).astype(BF16)
    v_ref[0] = kv[:, D_MODEL:].astype(BF16)


def _memkv(mem, g, wkv, gk):
    bsz, mlen, _ = mem.shape
    full = lambda a: pl.BlockSpec(a.shape, lambda b: (0,) * a.ndim)
    blk = pl.BlockSpec((1, mlen, D_MODEL), lambda b: (b, 0, 0))
    return pl.pallas_call(
        _memkv_kernel,
        grid=(bsz,),
        in_specs=[blk, full(g), full(wkv), full(gk)],
        out_specs=[blk, blk],
        out_shape=[jax.ShapeDtypeStruct((bsz, mlen, D_MODEL), BF16)] * 2,
        compiler_params=_cparams(1),
    )(mem, g, wkv, gk)


def _mid_kernel(x_ref, yrg_ref, ynsa_ref, woa_ref, wob_ref, gx_ref, wq_ref, gq_ref, k_ref, v_ref, wo_ref,
                gm_ref, wrh_ref, wrl_ref, br_ref, h_ref, xt_ref, rw_ref, ri_ref, cnt_ref):
    h1 = x_ref[0] + _dot(yrg_ref[0], woa_ref[...]) + _dot(ynsa_ref[0], wob_ref[...])

    q = _dot(_rms(h1, gx_ref[...]).astype(BF16), wq_ref[...])
    heads = []
    for h in range(X_HEADS):
        sl = slice(h * X_HEAD_DIM, (h + 1) * X_HEAD_DIM)
        qh = _rms(q[:, sl], gq_ref[...]).astype(BF16)
        lg = _dot_nt(qh, k_ref[0, :, sl])
        e = jnp.exp2(lg - jnp.max(lg, axis=-1, keepdims=True))
        heads.append(_dot(e.astype(BF16), v_ref[0, :, sl]) / jnp.sum(e, axis=-1, keepdims=True))
    h2 = h1 + _dot(jnp.concatenate(heads, axis=-1).astype(BF16), wo_ref[...])
    h_ref[0] = h2

    xt = _rms(h2, gm_ref[...])
    _store_row_tiles(xt_ref, xt)
    xt_hi = xt.astype(BF16)
    xt_lo = (xt - xt_hi.astype(F32)).astype(BF16)
    lg = _dot(xt_hi, wrh_ref[...]) + _dot(xt_lo, wrh_ref[...]) + _dot(xt_hi, wrl_ref[...]) + br_ref[...]
    lane = lax.broadcasted_iota(jnp.int32, lg.shape, 1)
    lane_f = lane.astype(F32)
    first_of = lambda hit: jnp.min(jnp.where(hit, lane_f, 1e9), axis=-1, keepdims=True)
    glog = jnp.where(lane < N_GROUPS, lg, -3e38)
    gmax = jnp.max(glog, axis=-1, keepdims=True)
    gsel = first_of(glog == gmax)
    p_g = 1.0 / jnp.sum(jnp.exp(glog - gmax), axis=-1, keepdims=True)
    lo = N_GROUPS + EXP_PER_GROUP * gsel
    el = jnp.where((lane_f >= lo) & (lane_f < lo + EXP_PER_GROUP), lg, -3e38)
    m_a = jnp.max(el, axis=-1, keepdims=True)
    i_a = first_of(el == m_a)
    el2 = jnp.where(lane_f == i_a, -3e38, el)
    m_b = jnp.max(el2, axis=-1, keepdims=True)
    i_b = first_of(el2 == m_b)
    r = jnp.exp(m_b - m_a)
    w_a = p_g / (1.0 + r)
    w_b = p_g * r / (1.0 + r)
    e_a = i_a - N_GROUPS
    e_b = i_b - N_GROUPS
    rw_ref[0] = jnp.where(lane == 0, w_a, jnp.where(lane == 1, w_b, 0.0))
    ri_ref[0] = jnp.where(lane == 0, e_a, jnp.where(lane == 1, e_b, 0.0)).astype(jnp.int32)

    @pl.when((pl.program_id(0) == 0) & (pl.program_id(1) == 0))
    def _():
        cnt_ref[...] = jnp.zeros_like(cnt_ref)

    hot = jnp.where((lane_f == e_a) | (lane_f == e_b), 1.0, 0.0)
    cnt_ref[...] += jnp.sum(hot, axis=0, keepdims=True)


def _mid(x, yrg, ynsa, woa, wob, gx, wq, gq, kx, vx, wo, gm, wrh, wrl, br):
    bsz, seq, _ = x.shape
    tm = min(TM_MID, seq)
    mlen = kx.shape[1]
    n_i = seq // tm
    full = lambda a: pl.BlockSpec(a.shape, lambda b, i: (0,) * a.ndim)
    tok = lambda w: pl.BlockSpec((1, tm, w), lambda b, i: (b, i, 0))
    memb = pl.BlockSpec((1, mlen, D_MODEL), lambda b, i: (b, 0, 0))
    xt_spec = pl.BlockSpec((tm * ROW_TILE, LANES), lambda b, i: (b * n_i + i, 0))
    return pl.pallas_call(
        _mid_kernel,
        grid=(bsz, seq // tm),
        in_specs=[tok(D_MODEL), tok(RG_WIDTH), tok(NSA_WIDTH), full(woa), full(wob), full(gx), full(wq), full(gq),
                  memb, memb, full(wo), full(gm), full(wrh), full(wrl), full(br)],
        out_specs=[tok(D_MODEL), xt_spec, tok(LANES), tok(LANES), pl.BlockSpec((1, LANES), lambda b, i: (0, 0))],
        out_shape=[jax.ShapeDtypeStruct((bsz, seq, D_MODEL), F32),
                   jax.ShapeDtypeStruct((bsz * seq * ROW_TILE, LANES), F32),
                   jax.ShapeDtypeStruct((bsz, seq, LANES), F32), jax.ShapeDtypeStruct((bsz, seq, LANES), jnp.int32),
                   jax.ShapeDtypeStruct((1, LANES), F32)],
        compiler_params=_cparams(2),
    )(x, yrg, ynsa, woa, wob, gx, wq, gq, kx, vx, wo, gm, wrh, wrl, br)


def _dest_kernel(ri_ref, pstart_ref, dest_ref, run_ref):
    @pl.when(pl.program_id(0) == 0)
    def _():
        run_ref[...] = jnp.zeros_like(run_ref)

    ri = ri_ref[...]
    tm = ri.shape[0]
    lane = lax.broadcasted_iota(jnp.int32, ri.shape, 1)
    e_a = ri[:, 0:1]
    e_b = ri[:, 1:2]
    hot_a = lane == e_a
    hot_b = lane == e_b
    hot = jnp.where(hot_a | hot_b, 1.0, 0.0)
    row = lax.broadcasted_iota(jnp.int32, (tm, tm), 0)
    col = lax.broadcasted_iota(jnp.int32, (tm, tm), 1)
    earlier = jnp.where(col < row, 1.0, 0.0).astype(BF16)
    base = _dot(earlier, hot.astype(BF16)) + run_ref[...] + pstart_ref[...]
    d_a = jnp.sum(jnp.where(hot_a, base, 0.0), axis=-1, keepdims=True)
    d_b = jnp.sum(jnp.where(hot_b, base, 0.0), axis=-1, keepdims=True)
    dest_ref[...] = jnp.where(lane == 0, d_a, jnp.where(lane == 1, d_b, 0.0)).astype(jnp.int32)
    run_ref[...] += jnp.sum(hot, axis=0, keepdims=True)


def _dest(ri2, pstart):
    n_tok = ri2.shape[0]
    tm = min(TM_DEST, n_tok)
    return pl.pallas_call(
        _dest_kernel,
        grid=(n_tok // tm,),
        in_specs=[pl.BlockSpec((tm, LANES), lambda i: (i, 0)), pl.BlockSpec((1, LANES), lambda i: (0, 0))],
        out_specs=pl.BlockSpec((tm, LANES), lambda i: (i, 0)),
        out_shape=jax.ShapeDtypeStruct((n_tok, LANES), jnp.int32),
        scratch_shapes=[pltpu.VMEM((1, LANES), F32)],
        compiler_params=_cparams(1),
    )(ri2, pstart)


def _store_row_tiles(ref, val):
    n = val.shape[0]
    for c in range(ROW_TILE):
        ref[pl.ds(c, n, stride=ROW_TILE), :] = val[:, c * LANES:(c + 1) * LANES]


def _load_row_tiles(ref, n):
    return [ref[pl.ds(c, n, stride=ROW_TILE), :] for c in range(ROW_TILE)]


def _token_rows(ref, t):
    return ref.at[pl.ds(pl.multiple_of(t * ROW_TILE, ROW_TILE), ROW_TILE), :]


def _dispatch_kernel(cnt_ref, pstart_ref, da_ref, db_ref, xt_ref, xs_ref, zrow, sem, zsem):
    tm = da_ref.shape[2]

    @pl.when(pl.program_id(0) == 0)
    def _():
        zrow[...] = jnp.zeros_like(zrow)

        def per_expert(e, c):
            used = cnt_ref[e]
            padded = (used + MOE_TB - 1) // MOE_TB * MOE_TB
            base = pstart_ref[e]

            def fill(r, c2):
                pltpu.make_async_copy(zrow, _token_rows(xs_ref, base + r), zsem).start()
                return c2

            def fill_done(r, c2):
                pltpu.make_async_copy(zrow, _token_rows(xs_ref, 0), zsem).wait()
                return c2

            lax.fori_loop(used, padded, fill, 0)
            lax.fori_loop(used, padded, fill_done, 0)
            return c

        lax.fori_loop(0, N_EXPERTS, per_expert, 0)

        last = N_EXPERTS - 1
        first_unused = (pstart_ref[last] + (cnt_ref[last] + MOE_TB - 1) // MOE_TB * MOE_TB) // MOE_TB

        def per_block(j, c):
            def fill(r, c2):
                pltpu.make_async_copy(zrow, _token_rows(xs_ref, j * MOE_TB + r), zsem).start()
                return c2

            def fill_done(r, c2):
                pltpu.make_async_copy(zrow, _token_rows(xs_ref, 0), zsem).wait()
                return c2

            lax.fori_loop(0, MOE_TB, fill, 0, unroll=DMA_UNROLL)
            lax.fori_loop(0, MOE_TB, fill_done, 0, unroll=DMA_UNROLL)
            return c

        lax.fori_loop(first_unused, xs_ref.shape[0] // (ROW_TILE * MOE_TB), per_block, 0)

    def issue(t, c):
        pltpu.make_async_copy(_token_rows(xt_ref, t), _token_rows(xs_ref, da_ref[0, 0, t]), sem).start(priority=0)
        pltpu.make_async_copy(_token_rows(xt_ref, t), _token_rows(xs_ref, db_ref[0, 0, t]), sem).start(priority=1)
        return c

    lax.fori_loop(0, tm, issue, 0, unroll=DMA_UNROLL)

    def drain(t, c):
        pltpu.make_async_copy(_token_rows(xt_ref, 0), _token_rows(xs_ref, 0), sem).wait()
        pltpu.make_async_copy(_token_rows(xt_ref, 0), _token_rows(xs_ref, 0), sem).wait()
        return c

    lax.fori_loop(0, tm, drain, 0, unroll=DMA_UNROLL)


def _dispatch(cnt, pstart, da, db, xt_rows, n_pad):
    n_tiles, _, tm = da.shape
    smem = pl.BlockSpec((1, 1, tm), lambda i, c, p: (i, 0, 0), memory_space=pltpu.SMEM)
    grid_spec = pltpu.PrefetchScalarGridSpec(
        num_scalar_prefetch=2,
        grid=(n_tiles,),
        in_specs=[smem, smem, pl.BlockSpec((tm * ROW_TILE, LANES), lambda i, c, p: (i, 0))],
        out_specs=pl.BlockSpec(memory_space=pl.ANY),
        scratch_shapes=[pltpu.VMEM((ROW_TILE, LANES), F32), pltpu.SemaphoreType.DMA(()),
                        pltpu.SemaphoreType.DMA(())],
    )
    return pl.pallas_call(
        _dispatch_kernel,
        grid_spec=grid_spec,
        out_shape=jax.ShapeDtypeStruct((n_pad * ROW_TILE, LANES), F32),
        compiler_params=pltpu.CompilerParams(dimension_semantics=("arbitrary",), has_side_effects=True,
                                             vmem_limit_bytes=VMEM_LIMIT),
    )(cnt, pstart, da, db, xt_rows)


def _ffn_kernel(bexp_ref, nused_ref, xs_ref, w1_ref, w3_ref, w2_ref, ys_ref):
    del bexp_ref
    j = pl.program_id(0)

    @pl.when(j < nused_ref[0])
    def _():
        xb = jnp.concatenate(_load_row_tiles(xs_ref, MOE_TB), axis=-1).astype(BF16)
        a = _dot(xb, w1_ref[0])
        h = a * _sigmoid(a) * _dot(xb, w3_ref[0])
        _store_row_tiles(ys_ref, _dot(h.astype(BF16), w2_ref[0]))

    @pl.when(j >= nused_ref[0])
    def _():
        ys_ref[...] = jnp.zeros_like(ys_ref)


def _ffn(blk_exp, n_used, xs, w1, w3, w2):
    n_blocks = xs.shape[0] // (MOE_TB * ROW_TILE)
    rows = pl.BlockSpec((MOE_TB * ROW_TILE, LANES), lambda j, be, nu: (j, 0))
    used_rows = pl.BlockSpec((MOE_TB * ROW_TILE, LANES), lambda j, be, nu: (jnp.minimum(j, nu[0] - 1), 0))
    grid_spec = pltpu.PrefetchScalarGridSpec(
        num_scalar_prefetch=2,
        grid=(n_blocks,),
        in_specs=[used_rows,
                  pl.BlockSpec((1, D_MODEL, D_EXPERT), lambda j, be, nu: (be[j], 0, 0)),
                  pl.BlockSpec((1, D_MODEL, D_EXPERT), lambda j, be, nu: (be[j], 0, 0)),
                  pl.BlockSpec((1, D_EXPERT, D_MODEL), lambda j, be, nu: (be[j], 0, 0))],
        out_specs=rows,
    )
    return pl.pallas_call(
        _ffn_kernel,
        grid_spec=grid_spec,
        out_shape=jax.ShapeDtypeStruct(xs.shape, F32),
        compiler_params=_cparams(1),
    )(blk_exp, n_used, xs, w1, w3, w2)


def _combine_kernel(da_ref, db_ref, da_next_ref, db_next_ref, h_ref, rw_ref, ys_ref, o_ref, ya, yb, sems):
    tm = da_ref.shape[2]
    i = pl.program_id(0)
    slot = i % 2

    def start_gather(a_ref, b_ref, s):
        def issue(t, c):
            pltpu.make_async_copy(_token_rows(ys_ref, a_ref[0, 0, t]), _token_rows(ya.at[s], t),
                                  sems.at[s]).start(priority=0)
            pltpu.make_async_copy(_token_rows(ys_ref, b_ref[0, 0, t]), _token_rows(yb.at[s], t),
                                  sems.at[s]).start(priority=1)
            return c

        lax.fori_loop(0, tm, issue, 0, unroll=DMA_UNROLL)

    @pl.when(i == 0)
    def _():
        start_gather(da_ref, db_ref, slot)

    @pl.when(i + 1 < pl.num_programs(0))
    def _():
        start_gather(da_next_ref, db_next_ref, 1 - slot)

    def drain(t, c):
        pltpu.make_async_copy(_token_rows(ys_ref, 0), _token_rows(ya.at[slot], 0), sems.at[slot]).wait()
        pltpu.make_async_copy(_token_rows(ys_ref, 0), _token_rows(yb.at[slot], 0), sems.at[slot]).wait()
        return c

    lax.fori_loop(0, tm, drain, 0, unroll=DMA_UNROLL)
    rw = rw_ref[...]
    mix = [rw[:, 0:1] * a + rw[:, 1:2] * b
           for a, b in zip(_load_row_tiles(ya.at[slot], tm), _load_row_tiles(yb.at[slot], tm))]
    o_ref[...] = h_ref[...] + jnp.concatenate(mix, axis=-1)


def _combine(da, db, h2, rw, ys):
    n_tiles, _, tm = da.shape
    n_tok = h2.shape[0]
    smem = pl.BlockSpec((1, 1, tm), lambda i: (i, 0, 0), memory_space=pltpu.SMEM)
    smem_next = pl.BlockSpec((1, 1, tm), lambda i: (jnp.minimum(i + 1, n_tiles - 1), 0, 0), memory_space=pltpu.SMEM)
    row = lambda w: pl.BlockSpec((tm, w), lambda i: (i, 0))
    slots = pltpu.VMEM((2, tm * ROW_TILE, LANES), F32)
    return pl.pallas_call(
        _combine_kernel,
        grid=(n_tiles,),
        in_specs=[smem, smem, smem_next, smem_next, row(D_MODEL), row(LANES), pl.BlockSpec(memory_space=pl.ANY)],
        out_specs=row(D_MODEL),
        out_shape=jax.ShapeDtypeStruct((n_tok, D_MODEL), F32),
        scratch_shapes=[slots, slots, pltpu.SemaphoreType.DMA((2,))],
        compiler_params=_cparams(1),
    )(da, db, da, db, h2, rw, ys)


def _rel_bucket_np(dist):
    n = np.maximum(dist, 0)
    max_exact = NUM_BUCKETS // 2
    nf = np.maximum(n, 1).astype(np.float32)
    large = max_exact + (np.log(nf / max_exact) / math.log(MAX_DIST / max_exact)
                         * (NUM_BUCKETS - max_exact)).astype(np.int32)
    large = np.minimum(large, NUM_BUCKETS - 1)
    return np.where(n < max_exact, n, large).astype(np.int32)


def _toeplitz(vec, rows):
    width = vec.shape[-1] - 1
    flat = jnp.tile(vec, (1,) * (vec.ndim - 1) + (rows,))[..., :rows * width]
    return flat.reshape(vec.shape[:-1] + (rows, width))


def _bias_tables(rel_bias, seq):
    n_chunk = seq // CMP_STRIDE
    n_tiles = seq // TQ
    table = rel_bias.T.astype(F32)

    wide = NEAR + TQ
    k = np.arange(wide + 1)
    dw = np.where(k < NEAR, WINDOW - k, WINDOW + wide + 1 - k)
    used = (k < NEAR) | (k > wide + 1 - TQ)
    vals = table[:, _rel_bucket_np(dw)]

    n_var = WINDOW // TQ
    first_key = WINDOW - TQ * np.arange(n_var + 1)[:, None, None]
    in_seq = np.arange(NEAR)[None, None, :] >= first_key

    def near_tile(valid):
        t = _toeplitz(jnp.where(valid[None, :], vals, NEG_INF), TQ)[:, :, :NEAR]
        t = t.reshape(NSA_KV, NSA_HPG, TQ, NEAR).transpose(1, 0, 2, 3).reshape(1, QROWS2, NEAR)
        return jnp.where(in_seq, t, NEG_INF)

    bias_w = near_tile(used & (dw >= 0) & (dw < WINDOW))
    bias_s = near_tile(used & (dw >= 0))
    bias_far = table[:, NUM_BUCKETS - 1].reshape(NSA_KV, NSA_HPG, 1).transpose(1, 0, 2)
    bias_far = jnp.broadcast_to(bias_far, (NSA_HPG, NSA_KV, TQ)).reshape(QROWS2, 1)

    r = np.arange(CMP_STRIDE)[:, None]
    k = np.arange(2 * n_chunk + 1)[None, :]
    lag = 2 * n_chunk + 1 - k
    valid = (k > n_chunk + 1) & (CMP_STRIDE * lag + r >= CMP_L - 1)
    vals = table[:, _rel_bucket_np(CMP_STRIDE * lag + r - CMP_L // 2)]
    full = _toeplitz(jnp.where(valid[None], vals, NEG_INF), n_chunk)[..., :n_chunk]
    full = jnp.where(np.arange(n_chunk) < n_chunk - 1, full, NEG_INF)
    a4 = TQ // CMP_STRIDE
    full = full.reshape(NSA_KV, NSA_HPG, CMP_STRIDE, n_tiles, a4, n_chunk).transpose(3, 1, 0, 4, 2, 5)
    bias_c = full.reshape(n_tiles, QROWS2, n_chunk)
    return tuple(LOG2E * t for t in (bias_c, bias_w, bias_s, bias_far))


def _selection_tables(seq):
    n_chunk = seq // CMP_STRIDE
    n_blk = seq // SEL_L
    c = np.arange(n_chunk)
    n = np.arange(n_blk)
    start = c * CMP_STRIDE
    overlap_t = ((start[None, :] <= n[:, None] * SEL_L + SEL_L - 1) & (start[None, :] + CMP_L - 1 >= n[:, None] * SEL_L)
                 & (c < n_chunk - 1)[None, :])
    pos = np.arange(seq + WINDOW) - WINDOW
    lane_blk = np.arange(LANES) % n_blk
    hit = (pos[:, None] >= 0) & (pos[:, None] // SEL_L == lane_blk[None, :]) & (np.arange(LANES) < 2 * n_blk)[None, :]
    return jnp.asarray(overlap_t, BF16), jnp.asarray(np.where(hit, -UNSEL_PENALTY, 0.0), BF16)


def _block_ones(width, group):
    idx = np.arange(width) // group
    return jnp.asarray((idx[:, None] == idx[None, :]) / group, BF16)


def _block_diag(w):
    nb, n, m = w.shape
    eye = jnp.eye(nb, dtype=w.dtype)
    return jnp.einsum('hij,hg->higj', w, eye).reshape(nb * n, nb * m)


def _compress_weights(w1, w2, pos):
    half_l = CMP_L // 2
    parts = []
    for half in range(2):
        wh = w1[half * half_l * HEAD_DIM:(half + 1) * half_l * HEAD_DIM].reshape(half_l, HEAD_DIM, CMP_HIDDEN)
        z = jnp.zeros_like(wh)
        for g in range(NSA_KV):
            grp = [wh if gg == g else z for gg in range(NSA_KV)]
            parts.append(jnp.stack(grp, axis=1).reshape(half_l * KV_W, CMP_HIDDEN))
    w1cat = jnp.concatenate(parts, axis=1).astype(BF16)
    w2bd = _block_diag(jnp.stack([w2] * NSA_KV)).astype(BF16)
    prow = [jnp.tile(pos[half * half_l:(half + 1) * half_l][:, None, :], (1, NSA_KV, 1)).reshape(-1)
            for half in range(2)]
    pmat = jnp.zeros((8, half_l * KV_W), F32).at[0].set(prow[0]).at[1].set(prow[1]).astype(BF16)
    return w1cat, w2bd, pmat


def kernel(x, mem, rel_bias, norm_mix, w_in, rg_conv_w, rg_conv_b, rg_w_r, rg_b_r, rg_w_i, rg_b_i, rg_lambda, nsa_g_q, nsa_g_kc, nsa_g_ks, nsa_g_kw, cmp_pos_k, cmp_pos_v, cmp_k_w1, cmp_k_w2, cmp_v_w1, cmp_v_w2, out_g_rg, out_g_nsa, w_out, norm_x, norm_mem, xa_w_q, xa_w_kv, xa_w_o, xa_g_q, xa_g_k, norm_moe, router_g_w, router_g_b, router_e_w, router_e_b, exp_w1, exp_w3, exp_w2):
    bsz, seq, _ = x.shape
    n_tok = bsz * seq
    assert seq % FAR_TK == 0 and 2 * (seq // SEL_L) <= LANES and norm_mix.shape[0] == 1
    l = 0
    row = lambda v: v.reshape(1, -1).astype(F32)

    perm = np.array([(half * NSA_HPG + p) * HEAD_DIM + d
                     for p in range(NSA_HPG) for half in range(NSA_KV) for d in range(HEAD_DIM)])
    offs = np.cumsum([0, RG_WIDTH, RG_WIDTH, NSA_WIDTH] + [KV_W] * 6)
    w = w_in[l]
    wrg = w[:, :offs[2]].astype(BF16)
    wq = w[:, offs[2]:offs[3]][:, perm].astype(BF16)
    wkv = w[:, offs[3]:offs[9]].astype(BF16)
    wgl = jnp.pad(w[:, offs[9]:], ((0, 0), (0, LANES - 3 * NSA_HEADS))).astype(BF16)
    ones64 = _block_ones(NSA_WIDTH, HEAD_DIM)
    gq = row(jnp.tile(nsa_g_q[l], NSA_HEADS) * (HEAD_DIM ** -0.5 * LOG2E))
    u, gate, q, kc, vc, ks, vs, kw, vw, gates = _inproj(
        x.reshape(n_tok, D_MODEL), row(norm_mix[l]), wrg, wq, wkv, wgl, gq,
        row(jnp.tile(nsa_g_ks[l], NSA_KV)), row(jnp.tile(nsa_g_kw[l], NSA_KV)), ones64)

    wg = (0.5 * jnp.concatenate([_block_diag(rg_w_r[l]), _block_diag(rg_w_i[l])], axis=1)).astype(BF16)
    bg = 0.5 * jnp.concatenate([rg_b_r[l], rg_b_i[l]]).reshape(1, -1)
    y_rg = _rglru(u.reshape(bsz, seq, RG_WIDTH), gate.reshape(bsz, seq, RG_WIDTH),
                  rg_conv_w[l].reshape(CONV_W, RG_WIDTH), row(rg_conv_b[l]), wg, bg, row(rg_lambda[l]),
                  row(out_g_rg[l]))

    n_chunk = seq // CMP_STRIDE
    w1k, w2k, pk = _compress_weights(cmp_k_w1[l], cmp_k_w2[l], cmp_pos_k[l])
    w1v, w2v, pv = _compress_weights(cmp_v_w1[l], cmp_v_w2[l], cmp_pos_v[l])
    kcmp, vcmp = _compress(kc.reshape(bsz, n_chunk, CMP_STRIDE * KV_W), vc.reshape(bsz, n_chunk, CMP_STRIDE * KV_W),
                           w1k, w2k, pk, w1v, w2v, pv, row(jnp.tile(nsa_g_kc[l], NSA_KV)),
                           ones64[:KV_W, :KV_W])
    padw = lambda t: jnp.pad(t.reshape(bsz, seq, KV_W), ((0, 0), (WINDOW, 0), (0, 0)))
    bias_c, bias_w, bias_s, bias_far = _bias_tables(rel_bias, seq)
    overlap_t, penalty = _selection_tables(seq)
    ksx = jnp.concatenate([padw(ks), jnp.broadcast_to(penalty, (bsz,) + penalty.shape)], axis=-1)
    y_nsa = _nsa(q.reshape(bsz, seq, NSA_WIDTH), gates.reshape(bsz, seq, LANES), kcmp, vcmp,
                 ksx, padw(vs), padw(kw), padw(vw), overlap_t, bias_c, bias_w, bias_s, bias_far,
                 row(out_g_nsa[l][perm]))

    kx, vx = _memkv(mem, row(norm_mem[l]), xa_w_kv[l].astype(BF16), row(xa_g_k[l]))
    wo_mix = w_out[l]
    wr = jnp.pad(jnp.concatenate([router_g_w[l], router_e_w[l]], axis=1),
                 ((0, 0), (0, LANES - N_GROUPS - N_EXPERTS)))
    wr_hi = wr.astype(BF16)
    br = jnp.pad(jnp.concatenate([router_g_b[l], router_e_b[l]]), (0, LANES - N_GROUPS - N_EXPERTS)).reshape(1, -1)
    h2, xt, rw, ri, counts = _mid(
        x, y_rg, y_nsa, wo_mix[:RG_WIDTH].astype(BF16), wo_mix[RG_WIDTH:][perm].astype(BF16), row(norm_x[l]),
        xa_w_q[l].astype(BF16), row(xa_g_q[l] * (X_HEAD_DIM ** -0.5 * LOG2E)), kx, vx, xa_w_o[l].astype(BF16),
        row(norm_moe[l]), wr_hi, (wr - wr_hi.astype(F32)).astype(BF16), br)

    n_slots = 2 * n_tok
    n_blocks = n_slots // MOE_TB + N_EXPERTS
    n_pad = n_blocks * MOE_TB
    cnt = counts[0, :N_EXPERTS].astype(jnp.int32)
    pcnt = (cnt + MOE_TB - 1) // MOE_TB * MOE_TB
    pends = jnp.cumsum(pcnt)
    pstart = jnp.pad((pends - pcnt).astype(F32), (0, LANES - N_EXPERTS)).reshape(1, LANES)
    blk_exp = jnp.minimum(jnp.sum(pends[None, :] <= jnp.arange(n_blocks, dtype=jnp.int32)[:, None] * MOE_TB, axis=1),
                          N_EXPERTS - 1).astype(jnp.int32)
    n_used = (pends[-1:] // MOE_TB).astype(jnp.int32)
    dest = _dest(ri.reshape(n_tok, LANES), pstart)
    tmd = min(TM_DMA, n_tok)
    da = dest[:, 0].reshape(n_tok // tmd, 1, tmd)
    db = dest[:, 1].reshape(n_tok // tmd, 1, tmd)
    xs = _dispatch(cnt, (pends - pcnt).astype(jnp.int32), da, db, xt, n_pad)
    ys = _ffn(blk_exp, n_used, xs, exp_w1[l].astype(BF16), exp_w3[l].astype(BF16), exp_w2[l].astype(BF16))
    out = _combine(da, db, h2.reshape(n_tok, D_MODEL), rw.reshape(n_tok, LANES), ys)
    return out.reshape(bsz, seq, D_MODEL)
```

```python
import math

import numpy as np
import jax
import jax.numpy as jnp
from jax import lax
from jax.experimental import pallas as pl
from jax.experimental.pallas import tpu as pltpu

F32 = jnp.float32
BF16 = jnp.bfloat16

D_MODEL = 1024
RG_WIDTH = 512
RG_BLOCKS = 8
RG_BLOCK = 64
CONV_W = 4
RG_C = 8.0
NSA_WIDTH = 512
NSA_HEADS = 8
HEAD_DIM = 64
NSA_KV = 2
NSA_HPG = 4
KV_W = 128
CMP_L = 32
CMP_STRIDE = 16
CMP_HIDDEN = 256
SEL_L = 64
N_SEL = 8
WINDOW = 512
NUM_BUCKETS = 32
MAX_DIST = 128
X_HEADS = 4
X_HEAD_DIM = 256
N_GROUPS = 4
EXP_PER_GROUP = 8
N_EXPERTS = 32
D_EXPERT = 512
EPS = 1e-6
LOG2E = 1.0 / math.log(2.0)
NEG_INF = -1e30
MASKED_BELOW = -1e29
SEL_FORCE = 1e9
LANES = 128

TQ = 64
NEAR = WINDOW + TQ
FAR_TK = 512
QROWS2 = NSA_HEADS * TQ
UNSEL_PENALTY = 2.0 ** 100
NSA_NB = 4

TM_PROJ = 1024
TM_MID = 1024
TM_DEST = 512
TM_DMA = 512
DMA_UNROLL = 8
MOE_TB = 512
ROW_TILE = D_MODEL // LANES
RG_CHUNK = 256
SCAN_ROWS = 8
SCAN_UNROLL = 4
VMEM_LIMIT = 56 * 1024 * 1024


def _cparams(n_axes):
    return pltpu.CompilerParams(dimension_semantics=("arbitrary",) * n_axes,
                                vmem_limit_bytes=VMEM_LIMIT)


def _dot(a, b):
    return jnp.dot(a, b, preferred_element_type=F32)


def _dot_nt(a, b):
    return lax.dot_general(a, b, (((1,), (1,)), ((), ())), preferred_element_type=F32)


def _gelu_tanh(x):
    c = math.sqrt(2.0 / math.pi)
    half = 0.5 * x
    return half + half * jnp.tanh(x * (c + (c * 0.044715) * (x * x)))


def _sigmoid(x):
    return 0.5 * jnp.tanh(0.5 * x) + 0.5


def _rms(x, g):
    return x * lax.rsqrt(jnp.mean(x * x, axis=-1, keepdims=True) + EPS) * g


def _group_rms(x, ones_blk, g):
    ms = _dot((x * x).astype(BF16), ones_blk)
    return x * lax.rsqrt(ms + EPS) * g


def _inproj_kernel(x_ref, g_ref, wrg_ref, wq_ref, wkv_ref, wgl_ref, gq_ref, gks_ref, gkw_ref, ones_ref,
                   u_ref, gate_ref, q_ref, kc_ref, vc_ref, ks_ref, vs_ref, kw_ref, vw_ref, gates_ref):
    xb = _rms(x_ref[...], g_ref[...]).astype(BF16)
    rg = _dot(xb, wrg_ref[...])
    u_ref[...] = rg[:, :RG_WIDTH].astype(BF16)
    gate_ref[...] = rg[:, RG_WIDTH:].astype(BF16)
    q = _dot(xb, wq_ref[...])
    q_ref[...] = _group_rms(q, ones_ref[...], gq_ref[...]).astype(BF16)
    kv = _dot(xb, wkv_ref[...])
    ones_kv = ones_ref[:KV_W, :KV_W]
    kc_ref[...] = kv[:, 0 * KV_W:1 * KV_W].astype(BF16)
    vc_ref[...] = kv[:, 1 * KV_W:2 * KV_W].astype(BF16)
    ks_ref[...] = _group_rms(kv[:, 2 * KV_W:3 * KV_W], ones_kv, gks_ref[...]).astype(BF16)
    vs_ref[...] = kv[:, 3 * KV_W:4 * KV_W].astype(BF16)
    kw_ref[...] = _group_rms(kv[:, 4 * KV_W:5 * KV_W], ones_kv, gkw_ref[...]).astype(BF16)
    vw_ref[...] = kv[:, 5 * KV_W:6 * KV_W].astype(BF16)
    gates_ref[...] = _sigmoid(_dot(xb, wgl_ref[...]))


def _inproj(x2, g, wrg, wq, wkv, wgl, gq, gks, gkw, ones_blk):
    n_tok = x2.shape[0]
    tm = min(TM_PROJ, n_tok)
    full = lambda a: pl.BlockSpec(a.shape, lambda i: (0,) * a.ndim)
    row = lambda w: pl.BlockSpec((tm, w), lambda i: (i, 0))
    outs = [(RG_WIDTH, BF16), (RG_WIDTH, BF16), (NSA_WIDTH, BF16)] + [(KV_W, BF16)] * 6 + [(LANES, F32)]
    return pl.pallas_call(
        _inproj_kernel,
        grid=(n_tok // tm,),
        in_specs=[row(D_MODEL)] + [full(a) for a in (g, wrg, wq, wkv, wgl, gq, gks, gkw, ones_blk)],
        out_specs=[row(w) for w, _ in outs],
        out_shape=[jax.ShapeDtypeStruct((n_tok, w), dt) for w, dt in outs],
        compiler_params=_cparams(1),
    )(x2, g, wrg, wq, wkv, wgl, gq, gks, gkw, ones_blk)


def _rglru_kernel(u_ref, gate_ref, cw_ref, cb_ref, wg_ref, bg_ref, lam_ref, og_ref, y_ref, upad, a_s, h_s):
    seq = u_ref.shape[1]
    upad[0:8, :] = jnp.zeros((8, RG_WIDTH), F32)
    upad[8:8 + seq, :] = u_ref[0].astype(F32)
    neg_lam = -lam_ref[...]
    softplus = jnp.maximum(neg_lam, 0.0) + jnp.log(1.0 + jnp.exp(-jnp.abs(neg_lam)))
    log2_a_half = (-0.5 * RG_C * LOG2E) * softplus
    ch = min(RG_CHUNK, seq)
    for c in range(seq // ch):
        r0 = c * ch
        uc = cb_ref[...]
        for k in range(CONV_W):
            off = 8 + r0 - (CONV_W - 1) + k
            uc = uc + cw_ref[k:k + 1, :] * upad[off:off + ch, :]
        th = jnp.tanh(_dot(uc.astype(BF16), wg_ref[...]) + bg_ref[...])
        a = jnp.exp2(log2_a_half * th[:, :RG_WIDTH] + log2_a_half)
        a_s[r0:r0 + ch, :] = a
        s = 1.0 - a * a
        h_s[r0:r0 + ch, :] = s * lax.rsqrt(jnp.maximum(s, 1e-30)) * (0.5 * th[:, RG_WIDTH:] + 0.5) * uc

    row = lax.broadcasted_iota(jnp.int32, (SCAN_ROWS, RG_WIDTH), 0)

    def block(j, h_prev):
        rows = pl.ds(pl.multiple_of(j * SCAN_ROWS, SCAN_ROWS), SCAN_ROWS)
        a = a_s[rows, :]
        b = h_s[rows, :]
        k = 1
        while k < SCAN_ROWS:
            keep = row >= k
            b = jnp.where(keep, a * pltpu.roll(b, k, 0) + b, b)
            a = jnp.where(keep, a * pltpu.roll(a, k, 0), a)
            k *= 2
        h = a * h_prev + b
        h_s[rows, :] = h
        return h[SCAN_ROWS - 1:SCAN_ROWS, :]

    lax.fori_loop(0, seq // SCAN_ROWS, block, jnp.zeros((1, RG_WIDTH), F32), unroll=SCAN_UNROLL)

    for c in range(seq // ch):
        r0 = c * ch
        y = _gelu_tanh(gate_ref[0, r0:r0 + ch, :].astype(F32)) * h_s[r0:r0 + ch, :]
        y_ref[0, r0:r0 + ch, :] = _rms(y, og_ref[...]).astype(BF16)


def _rglru(u3, gate3, cw, cb, wg, bg, lam, og):
    bsz, seq, _ = u3.shape
    full = lambda a: pl.BlockSpec(a.shape, lambda b: (0,) * a.ndim)
    blk = pl.BlockSpec((1, seq, RG_WIDTH), lambda b: (b, 0, 0))
    return pl.pallas_call(
        _rglru_kernel,
        grid=(bsz,),
        in_specs=[blk, blk] + [full(a) for a in (cw, cb, wg, bg, lam, og)],
        out_specs=blk,
        out_shape=jax.ShapeDtypeStruct((bsz, seq, RG_WIDTH), BF16),
        scratch_shapes=[pltpu.VMEM((seq + 8, RG_WIDTH), F32), pltpu.VMEM((seq, RG_WIDTH), F32),
                        pltpu.VMEM((seq, RG_WIDTH), F32)],
        compiler_params=_cparams(1),
    )(u3, gate3, cw, cb, wg, bg, lam, og)


def _compress_kernel(kx_ref, vx_ref, w1k_ref, w2k_ref, pk_ref, w1v_ref, w2v_ref, pv_ref, gk_ref, ones_ref,
                     ko_ref, vo_ref):
    n_chunk = kx_ref.shape[1]
    half = NSA_KV * CMP_HIDDEN

    def mlp(x_ref, w1_ref, w2_ref, p_ref):
        ab = _dot(x_ref[0], w1_ref[...])
        pos = _dot(p_ref[...], w1_ref[...])
        hid = ab[:, :half] + pltpu.roll(ab[:, half:], n_chunk - 1, 0) + (pos[0:1, :half] + pos[1:2, half:])
        return _dot(_gelu_tanh(hid).astype(BF16), w2_ref[...])

    kc = mlp(kx_ref, w1k_ref, w2k_ref, pk_ref)
    ko_ref[0] = _group_rms(kc, ones_ref[...], gk_ref[...]).astype(BF16)
    vo_ref[0] = mlp(vx_ref, w1v_ref, w2v_ref, pv_ref).astype(BF16)


def _compress(kx, vx, w1k, w2k, pk, w1v, w2v, pv, gk, ones_kv):
    bsz, n_chunk, width = kx.shape
    full = lambda a: pl.BlockSpec(a.shape, lambda b: (0,) * a.ndim)
    xin = pl.BlockSpec((1, n_chunk, width), lambda b: (b, 0, 0))
    out = pl.BlockSpec((1, n_chunk, KV_W), lambda b: (b, 0, 0))
    return pl.pallas_call(
        _compress_kernel,
        grid=(bsz,),
        in_specs=[xin, xin] + [full(a) for a in (w1k, w2k, pk, w1v, w2v, pv, gk, ones_kv)],
        out_specs=[out, out],
        out_shape=[jax.ShapeDtypeStruct((bsz, n_chunk, KV_W), BF16)] * 2,
        compiler_params=_cparams(1),
    )(kx, vx, w1k, w2k, pk, w1v, w2v, pv, gk, ones_kv)


def _nsa_kernel(q_ref, gates_ref, kcmp_ref, vcmp_ref, ksx_ref, vsp_ref, kwp_ref, vwp_ref, ovt_ref,
                bc_ref, bw_ref, bs_ref, bf_ref, og_ref, y_ref):
    i = pl.program_id(1)
    t0 = pl.multiple_of(i * TQ, TQ)
    n_blk = ovt_ref.shape[0]
    lane = lax.broadcasted_iota(jnp.int32, (TQ, LANES), 1)
    lo_half = lane < HEAD_DIM
    n_batch = q_ref.shape[0]

    def with_ones(v):
        return jnp.concatenate([v, jnp.ones_like(v)], axis=1)

    def near_part(bb):
        pieces = []
        for p in range(NSA_HPG):
            qs = q_ref[bb, :, p * LANES:(p + 1) * LANES]
            zero = jnp.zeros_like(qs)
            pieces += [jnp.where(lo_half, qs, zero), jnp.where(lo_half, zero, qs)]
        q8 = jnp.concatenate(pieces, axis=0)

        bc = bc_ref[0]
        lc = _dot_nt(q8, kcmp_ref[bb]) + bc
        ec = jnp.where(bc > MASKED_BELOW, jnp.exp2(lc - jnp.max(lc, axis=-1, keepdims=True)), 0.0)
        sc = jnp.sum(ec, axis=-1, keepdims=True)
        pc = ec / jnp.where(sc > 0.0, sc, 1.0)
        o_c = _dot(pc.astype(BF16), vcmp_ref[bb])

        blocks = [pc[r * TQ:(r + 1) * TQ] for r in range(NSA_HPG * NSA_KV)]
        pcs = jnp.concatenate([sum(blocks[g::NSA_KV]) for g in range(NSA_KV)], axis=0)
        pcs_hi = pcs.astype(BF16)
        pcs_lo = (pcs - pcs_hi.astype(F32)).astype(BF16)
        imp = _dot_nt(ovt_ref[...], pcs_hi) + _dot_nt(ovt_ref[...], pcs_lo)
        blk = lax.broadcasted_iota(jnp.int32, imp.shape, 0)
        forced = (blk == 0) | (blk == i) | (blk == i - 1)
        score = jnp.where(forced, SEL_FORCE, jnp.where(blk > i, -3e38, imp))
        rank = jnp.zeros(imp.shape, F32)
        for m in range(n_blk):
            row = score[m:m + 1, :]
            rank = rank + jnp.where(blk > m, jnp.where(row >= score, 1.0, 0.0), jnp.where(row > score, 1.0, 0.0))
        unsel = jnp.where(rank < N_SEL, 0.0, 1.0)
        unsel_far = jnp.where(blk >= i - WINDOW // SEL_L, 1.0, unsel)
        pad = jnp.zeros((LANES - 2 * n_blk, imp.shape[1]), F32)
        u_t = jnp.concatenate([unsel, unsel_far, pad], axis=0).T
        u_lane = lax.broadcasted_iota(jnp.int32, u_t.shape, 1)
        u_near = jnp.where(u_lane < n_blk, u_t, 0.0).astype(BF16)
        u_far = jnp.where(u_lane >= n_blk, u_t, 0.0).astype(BF16)
        qx_near = jnp.concatenate([q8, jnp.concatenate([u_near] * NSA_HPG, axis=0)], axis=1)
        qx_far = jnp.concatenate([q8, jnp.concatenate([u_far] * NSA_HPG, axis=0)], axis=1)

        lw = _dot_nt(q8, kwp_ref[bb, pl.ds(t0, NEAR), :]) + bw_ref[0]
        ew = jnp.exp2(lw - jnp.max(lw, axis=-1, keepdims=True))
        ow2 = _dot(ew.astype(BF16), with_ones(vwp_ref[bb, pl.ds(t0, NEAR), :]))
        o_w = ow2[:, :LANES] / ow2[:, LANES:]

        ls = _dot_nt(qx_near, ksx_ref[bb, pl.ds(t0, NEAR), :]) + bs_ref[0]
        m1 = jnp.max(ls, axis=-1, keepdims=True)
        e1 = jnp.exp2(ls - m1)
        acc1 = _dot(e1.astype(BF16), with_ones(vsp_ref[bb, pl.ds(t0, NEAR), :]))
        return o_c, o_w, qx_far, (m1, acc1)

    near = [near_part(bb) for bb in range(n_batch)]
    bfar = bf_ref[...]

    def far_step(kf, carry):
        base = pl.multiple_of(WINDOW + kf * FAR_TK, FAR_TK)
        new = []
        for bb in range(n_batch):
            m, acc = carry[bb]
            lf = _dot_nt(near[bb][2], ksx_ref[bb, pl.ds(base, FAR_TK), :]) + bfar
            m_new = jnp.maximum(m, jnp.max(lf, axis=-1, keepdims=True))
            alpha = jnp.exp2(m - m_new)
            e = jnp.exp2(lf - m_new)
            new.append((m_new, alpha * acc + _dot(e.astype(BF16), with_ones(vsp_ref[bb, pl.ds(base, FAR_TK), :]))))
        return tuple(new)

    n_far = (jnp.maximum(t0 - WINDOW, 0) + FAR_TK - 1) // FAR_TK
    far = lax.fori_loop(0, n_far, far_step, tuple(part[3] for part in near))

    for bb in range(n_batch):
        o_c, o_w = near[bb][0], near[bb][1]
        _, acc_s = far[bb]
        o_s = acc_s[:, :LANES] / acc_s[:, LANES:]
        gates = gates_ref[bb]

        def gate_col(j):
            cols = [gates[:, (g * NSA_HPG + p) * 3 + j:(g * NSA_HPG + p) * 3 + j + 1]
                    for p in range(NSA_HPG) for g in range(NSA_KV)]
            return jnp.concatenate(cols, axis=0)

        out = gate_col(0) * o_c + gate_col(1) * o_s + gate_col(2) * o_w
        slabs = [jnp.where(lo_half, out[(2 * p) * TQ:(2 * p + 1) * TQ], out[(2 * p + 1) * TQ:(2 * p + 2) * TQ])
                 for p in range(NSA_HPG)]
        y_ref[bb] = _rms(jnp.concatenate(slabs, axis=-1), og_ref[...]).astype(BF16)


def _nsa(q3, gates3, kcmp, vcmp, ksx, vsp, kwp, vwp, ovt, bias_c, bias_w, bias_s, bias_far, og):
    bsz, seq, _ = q3.shape
    n_chunk = kcmp.shape[1]
    n_var = bias_w.shape[0] - 1
    nb = NSA_NB if bsz % NSA_NB == 0 else 1
    full = lambda a: pl.BlockSpec(a.shape, lambda b, i: (0,) * a.ndim)
    per_b = lambda a: pl.BlockSpec((nb,) + a.shape[1:], lambda b, i: (b,) + (0,) * (a.ndim - 1))
    near = pl.BlockSpec((1, QROWS2, NEAR), lambda b, i: (jnp.minimum(i, n_var), 0, 0))
    return pl.pallas_call(
        _nsa_kernel,
        grid=(bsz // nb, seq // TQ),
        in_specs=[pl.BlockSpec((nb, TQ, NSA_WIDTH), lambda b, i: (b, i, 0)),
                  pl.BlockSpec((nb, TQ, LANES), lambda b, i: (b, i, 0)),
                  per_b(kcmp), per_b(vcmp), per_b(ksx), per_b(vsp), per_b(kwp), per_b(vwp),
                  full(ovt),
                  pl.BlockSpec((1, QROWS2, n_chunk), lambda b, i: (i, 0, 0)),
                  near, near, full(bias_far), full(og)],
        out_specs=pl.BlockSpec((nb, TQ, NSA_WIDTH), lambda b, i: (b, i, 0)),
        out_shape=jax.ShapeDtypeStruct((bsz, seq, NSA_WIDTH), BF16),
        compiler_params=_cparams(2),
    )(q3, gates3, kcmp, vcmp, ksx, vsp, kwp, vwp, ovt, bias_c, bias_w, bias_s, bias_far, og)


def _memkv_kernel(mem_ref, g_ref, wkv_ref, gk_ref, k_ref, v_ref):
    mn = _rms(mem_ref[0], g_ref[...]).astype(BF16)
    kv = _dot(mn, wkv_ref[...])
    for h in range(X_HEADS):
        sl = slice(h * X_HEAD_DIM, (h + 1) * X_HEAD_DIM)
        k_ref[0, :, sl] = _rms(kv[:, sl], gk_ref[...]).astype(BF16)
    v_ref[0] = kv[:, D_MODEL:].astype(BF16)


def _memkv(mem, g, wkv, gk):
    bsz, mlen, _ = mem.shape
    full = lambda a: pl.BlockSpec(a.shape, lambda b: (0,) * a.ndim)
    blk = pl.BlockSpec((1, mlen, D_MODEL), lambda b: (b, 0, 0))
    return pl.pallas_call(
        _memkv_kernel,
        grid=(bsz,),
        in_specs=[blk, full(g), full(wkv), full(gk)],
        out_specs=[blk, blk],
        out_shape=[jax.ShapeDtypeStruct((bsz, mlen, D_MODEL), BF16)] * 2,
        compiler_params=_cparams(1),
    )(mem, g, wkv, gk)


def _mid_kernel(x_ref, yrg_ref, ynsa_ref, woa_ref, wob_ref, gx_ref, wq_ref, gq_ref, k_ref, v_ref, wo_ref,
                gm_ref, wrh_ref, wrl_ref, br_ref, h_ref, xt_ref, rw_ref, ri_ref, cnt_ref):
    h1 = x_ref[0] + _dot(yrg_ref[0], woa_ref[...]) + _dot(ynsa_ref[0], wob_ref[...])

    q = _dot(_rms(h1, gx_ref[...]).astype(BF16), wq_ref[...])
    heads = []
    for h in range(X_HEADS):
        sl = slice(h * X_HEAD_DIM, (h + 1) * X_HEAD_DIM)
        qh = _rms(q[:, sl], gq_ref[...]).astype(BF16)
        lg = _dot_nt(qh, k_ref[0, :, sl])
        e = jnp.exp2(lg - jnp.max(lg, axis=-1, keepdims=True))
        heads.append(_dot(e.astype(BF16), v_ref[0, :, sl]) / jnp.sum(e, axis=-1, keepdims=True))
    h2 = h1 + _dot(jnp.concatenate(heads, axis=-1).astype(BF16), wo_ref[...])
    h_ref[0] = h2

    xt = _rms(h2, gm_ref[...])
    _store_row_tiles(xt_ref, xt)
    xt_hi = xt.astype(BF16)
    xt_lo = (xt - xt_hi.astype(F32)).astype(BF16)
    lg = _dot(xt_hi, wrh_ref[...]) + _dot(xt_lo, wrh_ref[...]) + _dot(xt_hi, wrl_ref[...]) + br_ref[...]
    lane = lax.broadcasted_iota(jnp.int32, lg.shape, 1)
    lane_f = lane.astype(F32)
    first_of = lambda hit: jnp.min(jnp.where(hit, lane_f, 1e9), axis=-1, keepdims=True)
    glog = jnp.where(lane < N_GROUPS, lg, -3e38)
    gmax = jnp.max(glog, axis=-1, keepdims=True)
    gsel = first_of(glog == gmax)
    p_g = 1.0 / jnp.sum(jnp.exp(glog - gmax), axis=-1, keepdims=True)
    lo = N_GROUPS + EXP_PER_GROUP * gsel
    el = jnp.where((lane_f >= lo) & (lane_f < lo + EXP_PER_GROUP), lg, -3e38)
    m_a = jnp.max(el, axis=-1, keepdims=True)
    i_a = first_of(el == m_a)
    el2 = jnp.where(lane_f == i_a, -3e38, el)
    m_b = jnp.max(el2, axis=-1, keepdims=True)
    i_b = first_of(el2 == m_b)
    r = jnp.exp(m_b - m_a)
    w_a = p_g / (1.0 + r)
    w_b = p_g * r / (1.0 + r)
    e_a = i_a - N_GROUPS
    e_b = i_b - N_GROUPS
    rw_ref[0] = jnp.where(lane == 0, w_a, jnp.where(lane == 1, w_b, 0.0))
    ri_ref[0] = jnp.where(lane == 0, e_a, jnp.where(lane == 1, e_b, 0.0)).astype(jnp.int32)

    @pl.when((pl.program_id(0) == 0) & (pl.program_id(1) == 0))
    def _():
        cnt_ref[...] = jnp.zeros_like(cnt_ref)

    hot = jnp.where((lane_f == e_a) | (lane_f == e_b), 1.0, 0.0)
    cnt_ref[...] += jnp.sum(hot, axis=0, keepdims=True)


def _mid(x, yrg, ynsa, woa, wob, gx, wq, gq, kx, vx, wo, gm, wrh, wrl, br):
    bsz, seq, _ = x.shape
    tm = min(TM_MID, seq)
    mlen = kx.shape[1]
    n_i = seq // tm
    full = lambda a: pl.BlockSpec(a.shape, lambda b, i: (0,) * a.ndim)
    tok = lambda w: pl.BlockSpec((1, tm, w), lambda b, i: (b, i, 0))
    memb = pl.BlockSpec((1, mlen, D_MODEL), lambda b, i: (b, 0, 0))
    xt_spec = pl.BlockSpec((tm * ROW_TILE, LANES), lambda b, i: (b * n_i + i, 0))
    return pl.pallas_call(
        _mid_kernel,
        grid=(bsz, seq // tm),
        in_specs=[tok(D_MODEL), tok(RG_WIDTH), tok(NSA_WIDTH), full(woa), full(wob), full(gx), full(wq), full(gq),
                  memb, memb, full(wo), full(gm), full(wrh), full(wrl), full(br)],
        out_specs=[tok(D_MODEL), xt_spec, tok(LANES), tok(LANES), pl.BlockSpec((1, LANES), lambda b, i: (0, 0))],
        out_shape=[jax.ShapeDtypeStruct((bsz, seq, D_MODEL), F32),
                   jax.ShapeDtypeStruct((bsz * seq * ROW_TILE, LANES), F32),
                   jax.ShapeDtypeStruct((bsz, seq, LANES), F32), jax.ShapeDtypeStruct((bsz, seq, LANES), jnp.int32),
                   jax.ShapeDtypeStruct((1, LANES), F32)],
        compiler_params=_cparams(2),
    )(x, yrg, ynsa, woa, wob, gx, wq, gq, kx, vx, wo, gm, wrh, wrl, br)


def _dest_kernel(ri_ref, pstart_ref, dest_ref, run_ref):
    @pl.when(pl.program_id(0) == 0)
    def _():
        run_ref[...] = jnp.zeros_like(run_ref)

    ri = ri_ref[...]
    tm = ri.shape[0]
    lane = lax.broadcasted_iota(jnp.int32, ri.shape, 1)
    e_a = ri[:, 0:1]
    e_b = ri[:, 1:2]
    hot_a = lane == e_a
    hot_b = lane == e_b
    hot = jnp.where(hot_a | hot_b, 1.0, 0.0)
    row = lax.broadcasted_iota(jnp.int32, (tm, tm), 0)
    col = lax.broadcasted_iota(jnp.int32, (tm, tm), 1)
    earlier = jnp.where(col < row, 1.0, 0.0).astype(BF16)
    base = _dot(earlier, hot.astype(BF16)) + run_ref[...] + pstart_ref[...]
    d_a = jnp.sum(jnp.where(hot_a, base, 0.0), axis=-1, keepdims=True)
    d_b = jnp.sum(jnp.where(hot_b, base, 0.0), axis=-1, keepdims=True)
    dest_ref[...] = jnp.where(lane == 0, d_a, jnp.where(lane == 1, d_b, 0.0)).astype(jnp.int32)
    run_ref[...] += jnp.sum(hot, axis=0, keepdims=True)


def _dest(ri2, pstart):
    n_tok = ri2.shape[0]
    tm = min(TM_DEST, n_tok)
    return pl.pallas_call(
        _dest_kernel,
        grid=(n_tok // tm,),
        in_specs=[pl.BlockSpec((tm, LANES), lambda i: (i, 0)), pl.BlockSpec((1, LANES), lambda i: (0, 0))],
        out_specs=pl.BlockSpec((tm, LANES), lambda i: (i, 0)),
        out_shape=jax.ShapeDtypeStruct((n_tok, LANES), jnp.int32),
        scratch_shapes=[pltpu.VMEM((1, LANES), F32)],
        compiler_params=_cparams(1),
    )(ri2, pstart)


def _store_row_tiles(ref, val):
    n = val.shape[0]
    for c in range(ROW_TILE):
        ref[pl.ds(c, n, stride=ROW_TILE), :] = val[:, c * LANES:(c + 1) * LANES]


def _load_row_tiles(ref, n):
    return [ref[pl.ds(c, n, stride=ROW_TILE), :] for c in range(ROW_TILE)]


def _token_rows(ref, t):
    return ref.at[pl.ds(pl.multiple_of(t * ROW_TILE, ROW_TILE), ROW_TILE), :]


def _dispatch_kernel(cnt_ref, pstart_ref, da_ref, db_ref, xt_ref, xs_ref, zrow, sem, zsem):
    tm = da_ref.shape[2]

    @pl.when(pl.program_id(0) == 0)
    def _():
        zrow[...] = jnp.zeros_like(zrow)

        def per_expert(e, c):
            used = cnt_ref[e]
            padded = (used + MOE_TB - 1) // MOE_TB * MOE_TB
            base = pstart_ref[e]

            def fill(r, c2):
                pltpu.make_async_copy(zrow, _token_rows(xs_ref, base + r), zsem).start()
                return c2

            def fill_done(r, c2):
                pltpu.make_async_copy(zrow, _token_rows(xs_ref, 0), zsem).wait()
                return c2

            lax.fori_loop(used, padded, fill, 0)
            lax.fori_loop(used, padded, fill_done, 0)
            return c

        lax.fori_loop(0, N_EXPERTS, per_expert, 0)

        last = N_EXPERTS - 1
        first_unused = (pstart_ref[last] + (cnt_ref[last] + MOE_TB - 1) // MOE_TB * MOE_TB) // MOE_TB

        def per_block(j, c):
            def fill(r, c2):
                pltpu.make_async_copy(zrow, _token_rows(xs_ref, j * MOE_TB + r), zsem).start()
                return c2

            def fill_done(r, c2):
                pltpu.make_async_copy(zrow, _token_rows(xs_ref, 0), zsem).wait()
                return c2

            lax.fori_loop(0, MOE_TB, fill, 0, unroll=DMA_UNROLL)
            lax.fori_loop(0, MOE_TB, fill_done, 0, unroll=DMA_UNROLL)
            return c

        lax.fori_loop(first_unused, xs_ref.shape[0] // (ROW_TILE * MOE_TB), per_block, 0)

    def issue(t, c):
        pltpu.make_async_copy(_token_rows(xt_ref, t), _token_rows(xs_ref, da_ref[0, 0, t]), sem).start(priority=0)
        pltpu.make_async_copy(_token_rows(xt_ref, t), _token_rows(xs_ref, db_ref[0, 0, t]), sem).start(priority=1)
        return c

    lax.fori_loop(0, tm, issue, 0, unroll=DMA_UNROLL)

    def drain(t, c):
        pltpu.make_async_copy(_token_rows(xt_ref, 0), _token_rows(xs_ref, 0), sem).wait()
        pltpu.make_async_copy(_token_rows(xt_ref, 0), _token_rows(xs_ref, 0), sem).wait()
        return c

    lax.fori_loop(0, tm, drain, 0, unroll=DMA_UNROLL)


def _dispatch(cnt, pstart, da, db, xt_rows, n_pad):
    n_tiles, _, tm = da.shape
    smem = pl.BlockSpec((1, 1, tm), lambda i, c, p: (i, 0, 0), memory_space=pltpu.SMEM)
    grid_spec = pltpu.PrefetchScalarGridSpec(
        num_scalar_prefetch=2,
        grid=(n_tiles,),
        in_specs=[smem, smem, pl.BlockSpec((tm * ROW_TILE, LANES), lambda i, c, p: (i, 0))],
        out_specs=pl.BlockSpec(memory_space=pl.ANY),
        scratch_shapes=[pltpu.VMEM((ROW_TILE, LANES), F32), pltpu.SemaphoreType.DMA(()),
                        pltpu.SemaphoreType.DMA(())],
    )
    return pl.pallas_call(
        _dispatch_kernel,
        grid_spec=grid_spec,
        out_shape=jax.ShapeDtypeStruct((n_pad * ROW_TILE, LANES), F32),
        compiler_params=pltpu.CompilerParams(dimension_semantics=("arbitrary",), has_side_effects=True,
                                             vmem_limit_bytes=VMEM_LIMIT),
    )(cnt, pstart, da, db, xt_rows)


def _ffn_kernel(bexp_ref, nused_ref, xs_ref, w1_ref, w3_ref, w2_ref, ys_ref):
    del bexp_ref
    j = pl.program_id(0)

    @pl.when(j < nused_ref[0])
    def _():
        xb = jnp.concatenate(_load_row_tiles(xs_ref, MOE_TB), axis=-1).astype(BF16)
        a = _dot(xb, w1_ref[0].astype(BF16))
        h = a * _sigmoid(a) * _dot(xb, w3_ref[0].astype(BF16))
        _store_row_tiles(ys_ref, _dot(h.astype(BF16), w2_ref[0].astype(BF16)))

    @pl.when(j >= nused_ref[0])
    def _():
        ys_ref[...] = jnp.zeros_like(ys_ref)


def _ffn(blk_exp, n_used, xs, w1, w3, w2):
    n_blocks = xs.shape[0] // (MOE_TB * ROW_TILE)
    rows = pl.BlockSpec((MOE_TB * ROW_TILE, LANES), lambda j, be, nu: (j, 0))
    used_rows = pl.BlockSpec((MOE_TB * ROW_TILE, LANES), lambda j, be, nu: (jnp.minimum(j, nu[0] - 1), 0))
    grid_spec = pltpu.PrefetchScalarGridSpec(
        num_scalar_prefetch=2,
        grid=(n_blocks,),
        in_specs=[used_rows,
                  pl.BlockSpec((1, D_MODEL, D_EXPERT), lambda j, be, nu: (be[j], 0, 0)),
                  pl.BlockSpec((1, D_MODEL, D_EXPERT), lambda j, be, nu: (be[j], 0, 0)),
                  pl.BlockSpec((1, D_EXPERT, D_MODEL), lambda j, be, nu: (be[j], 0, 0))],
        out_specs=rows,
    )
    return pl.pallas_call(
        _ffn_kernel,
        grid_spec=grid_spec,
        out_shape=jax.ShapeDtypeStruct(xs.shape, F32),
        compiler_params=_cparams(1),
    )(blk_exp, n_used, xs, w1, w3, w2)


def _combine_kernel(da_ref, db_ref, da_next_ref, db_next_ref, h_ref, rw_ref, ys_ref, o_ref, ya, yb, sems):
    tm = da_ref.shape[2]
    i = pl.program_id(0)
    slot = i % 2

    def start_gather(a_ref, b_ref, s):
        def issue(t, c):
            pltpu.make_async_copy(_token_rows(ys_ref, a_ref[0, 0, t]), _token_rows(ya.at[s], t),
                                  sems.at[s]).start(priority=0)
            pltpu.make_async_copy(_token_rows(ys_ref, b_ref[0, 0, t]), _token_rows(yb.at[s], t),
                                  sems.at[s]).start(priority=1)
            return c

        lax.fori_loop(0, tm, issue, 0, unroll=DMA_UNROLL)

    @pl.when(i == 0)
    def _():
        start_gather(da_ref, db_ref, slot)

    @pl.when(i + 1 < pl.num_programs(0))
    def _():
        start_gather(da_next_ref, db_next_ref, 1 - slot)

    def drain(t, c):
        pltpu.make_async_copy(_token_rows(ys_ref, 0), _token_rows(ya.at[slot], 0), sems.at[slot]).wait()
        pltpu.make_async_copy(_token_rows(ys_ref, 0), _token_rows(yb.at[slot], 0), sems.at[slot]).wait()
        return c

    lax.fori_loop(0, tm, drain, 0, unroll=DMA_UNROLL)
    rw = rw_ref[...]
    mix = [rw[:, 0:1] * a + rw[:, 1:2] * b
           for a, b in zip(_load_row_tiles(ya.at[slot], tm), _load_row_tiles(yb.at[slot], tm))]
    o_ref[...] = h_ref[...] + jnp.concatenate(mix, axis=-1)


def _combine(da, db, h2, rw, ys):
    n_tiles, _, tm = da.shape
    n_tok = h2.shape[0]
    smem = pl.BlockSpec((1, 1, tm), lambda i: (i, 0, 0), memory_space=pltpu.SMEM)
    smem_next = pl.BlockSpec((1, 1, tm), lambda i: (jnp.minimum(i + 1, n_tiles - 1), 0, 0), memory_space=pltpu.SMEM)
    row = lambda w: pl.BlockSpec((tm, w), lambda i: (i, 0))
    slots = pltpu.VMEM((2, tm * ROW_TILE, LANES), F32)
    return pl.pallas_call(
        _combine_kernel,
        grid=(n_tiles,),
        in_specs=[smem, smem, smem_next, smem_next, row(D_MODEL), row(LANES), pl.BlockSpec(memory_space=pl.ANY)],
        out_specs=row(D_MODEL),
        out_shape=jax.ShapeDtypeStruct((n_tok, D_MODEL), F32),
        scratch_shapes=[slots, slots, pltpu.SemaphoreType.DMA((2,))],
        compiler_params=_cparams(1),
    )(da, db, da, db, h2, rw, ys)


def _rel_bucket_np(dist):
    n = np.maximum(dist, 0)
    max_exact = NUM_BUCKETS // 2
    nf = np.maximum(n, 1).astype(np.float32)
    large = max_exact + (np.log(nf / max_exact) / math.log(MAX_DIST / max_exact)
                         * (NUM_BUCKETS - max_exact)).astype(np.int32)
    large = np.minimum(large, NUM_BUCKETS - 1)
    return np.where(n < max_exact, n, large).astype(np.int32)


def _toeplitz(vec, rows):
    width = vec.shape[-1] - 1
    flat = jnp.tile(vec, (1,) * (vec.ndim - 1) + (rows,))[..., :rows * width]
    return flat.reshape(vec.shape[:-1] + (rows, width))


def _bias_tables(rel_bias, seq):
    n_chunk = seq // CMP_STRIDE
    n_tiles = seq // TQ
    table = rel_bias.T.astype(F32)

    wide = NEAR + TQ
    k = np.arange(wide + 1)
    dw = np.where(k < NEAR, WINDOW - k, WINDOW + wide + 1 - k)
    used = (k < NEAR) | (k > wide + 1 - TQ)
    vals = table[:, _rel_bucket_np(dw)]

    n_var = WINDOW // TQ
    first_key = WINDOW - TQ * np.arange(n_var + 1)[:, None, None]
    in_seq = np.arange(NEAR)[None, None, :] >= first_key

    def near_tile(valid):
        t = _toeplitz(jnp.where(valid[None, :], vals, NEG_INF), TQ)[:, :, :NEAR]
        t = t.reshape(NSA_KV, NSA_HPG, TQ, NEAR).transpose(1, 0, 2, 3).reshape(1, QROWS2, NEAR)
        return jnp.where(in_seq, t, NEG_INF)

    bias_w = near_tile(used & (dw >= 0) & (dw < WINDOW))
    bias_s = near_tile(used & (dw >= 0))
    bias_far = table[:, NUM_BUCKETS - 1].reshape(NSA_KV, NSA_HPG, 1).transpose(1, 0, 2)
    bias_far = jnp.broadcast_to(bias_far, (NSA_HPG, NSA_KV, TQ)).reshape(QROWS2, 1)

    r = np.arange(CMP_STRIDE)[:, None]
    k = np.arange(2 * n_chunk + 1)[None, :]
    lag = 2 * n_chunk + 1 - k
    valid = (k > n_chunk + 1) & (CMP_STRIDE * lag + r >= CMP_L - 1)
    vals = table[:, _rel_bucket_np(CMP_STRIDE * lag + r - CMP_L // 2)]
    full = _toeplitz(jnp.where(valid[None], vals, NEG_INF), n_chunk)[..., :n_chunk]
    full = jnp.where(np.arange(n_chunk) < n_chunk - 1, full, NEG_INF)
    a4 = TQ // CMP_STRIDE
    full = full.reshape(NSA_KV, NSA_HPG, CMP_STRIDE, n_tiles, a4, n_chunk).transpose(3, 1, 0, 4, 2, 5)
    bias_c = full.reshape(n_tiles, QROWS2, n_chunk)
    return tuple(LOG2E * t for t in (bias_c, bias_w, bias_s, bias_far))


def _selection_tables(seq):
    n_chunk = seq // CMP_STRIDE
    n_blk = seq // SEL_L
    c = np.arange(n_chunk)
    n = np.arange(n_blk)
    start = c * CMP_STRIDE
    overlap_t = ((start[None, :] <= n[:, None] * SEL_L + SEL_L - 1) & (start[None, :] + CMP_L - 1 >= n[:, None] * SEL_L)
                 & (c < n_chunk - 1)[None, :])
    pos = np.arange(seq + WINDOW) - WINDOW
    lane_blk = np.arange(LANES) % n_blk
    hit = (pos[:, None] >= 0) & (pos[:, None] // SEL_L == lane_blk[None, :]) & (np.arange(LANES) < 2 * n_blk)[None, :]
    return jnp.asarray(overlap_t, BF16), jnp.asarray(np.where(hit, -UNSEL_PENALTY, 0.0), BF16)


def _block_ones(width, group):
    idx = np.arange(width) // group
    return jnp.asarray((idx[:, None] == idx[None, :]) / group, BF16)


def _block_diag(w):
    nb, n, m = w.shape
    eye = jnp.eye(nb, dtype=w.dtype)
    return jnp.einsum('hij,hg->higj', w, eye).reshape(nb * n, nb * m)


def _compress_weights(w1, w2, pos):
    half_l = CMP_L // 2
    parts = []
    for half in range(2):
        wh = w1[half * half_l * HEAD_DIM:(half + 1) * half_l * HEAD_DIM].reshape(half_l, HEAD_DIM, CMP_HIDDEN)
        z = jnp.zeros_like(wh)
        for g in range(NSA_KV):
            grp = [wh if gg == g else z for gg in range(NSA_KV)]
            parts.append(jnp.stack(grp, axis=1).reshape(half_l * KV_W, CMP_HIDDEN))
    w1cat = jnp.concatenate(parts, axis=1).astype(BF16)
    w2bd = _block_diag(jnp.stack([w2] * NSA_KV)).astype(BF16)
    prow = [jnp.tile(pos[half * half_l:(half + 1) * half_l][:, None, :], (1, NSA_KV, 1)).reshape(-1)
            for half in range(2)]
    pmat = jnp.zeros((8, half_l * KV_W), F32).at[0].set(prow[0]).at[1].set(prow[1]).astype(BF16)
    return w1cat, w2bd, pmat


def kernel(x, mem, rel_bias, norm_mix, w_in, rg_conv_w, rg_conv_b, rg_w_r, rg_b_r, rg_w_i, rg_b_i, rg_lambda, nsa_g_q, nsa_g_kc, nsa_g_ks, nsa_g_kw, cmp_pos_k, cmp_pos_v, cmp_k_w1, cmp_k_w2, cmp_v_w1, cmp_v_w2, out_g_rg, out_g_nsa, w_out, norm_x, norm_mem, xa_w_q, xa_w_kv, xa_w_o, xa_g_q, xa_g_k, norm_moe, router_g_w, router_g_b, router_e_w, router_e_b, exp_w1, exp_w3, exp_w2):
    bsz, seq, _ = x.shape
    n_tok = bsz * seq
    assert seq % FAR_TK == 0 and 2 * (seq // SEL_L) <= LANES and norm_mix.shape[0] == 1
    l = 0
    row = lambda v: v.reshape(1, -1).astype(F32)

    perm = np.array([(half * NSA_HPG + p) * HEAD_DIM + d
                     for p in range(NSA_HPG) for half in range(NSA_KV) for d in range(HEAD_DIM)])
    offs = np.cumsum([0, RG_WIDTH, RG_WIDTH, NSA_WIDTH] + [KV_W] * 6)
    w = w_in[l]
    wrg = w[:, :offs[2]].astype(BF16)
    wq = w[:, offs[2]:offs[3]][:, perm].astype(BF16)
    wkv = w[:, offs[3]:offs[9]].astype(BF16)
    wgl = jnp.pad(w[:, offs[9]:], ((0, 0), (0, LANES - 3 * NSA_HEADS))).astype(BF16)
    ones64 = _block_ones(NSA_WIDTH, HEAD_DIM)
    gq = row(jnp.tile(nsa_g_q[l], NSA_HEADS) * (HEAD_DIM ** -0.5 * LOG2E))
    u, gate, q, kc, vc, ks, vs, kw, vw, gates = _inproj(
        x.reshape(n_tok, D_MODEL), row(norm_mix[l]), wrg, wq, wkv, wgl, gq,
        row(jnp.tile(nsa_g_ks[l], NSA_KV)), row(jnp.tile(nsa_g_kw[l], NSA_KV)), ones64)

    wg = (0.5 * jnp.concatenate([_block_diag(rg_w_r[l]), _block_diag(rg_w_i[l])], axis=1)).astype(BF16)
    bg = 0.5 * jnp.concatenate([rg_b_r[l], rg_b_i[l]]).reshape(1, -1)
    y_rg = _rglru(u.reshape(bsz, seq, RG_WIDTH), gate.reshape(bsz, seq, RG_WIDTH),
                  rg_conv_w[l].reshape(CONV_W, RG_WIDTH), row(rg_conv_b[l]), wg, bg, row(rg_lambda[l]),
                  row(out_g_rg[l]))

    n_chunk = seq // CMP_STRIDE
    w1k, w2k, pk = _compress_weights(cmp_k_w1[l], cmp_k_w2[l], cmp_pos_k[l])
    w1v, w2v, pv = _compress_weights(cmp_v_w1[l], cmp_v_w2[l], cmp_pos_v[l])
    kcmp, vcmp = _compress(kc.reshape(bsz, n_chunk, CMP_STRIDE * KV_W), vc.reshape(bsz, n_chunk, CMP_STRIDE * KV_W),
                           w1k, w2k, pk, w1v, w2v, pv, row(jnp.tile(nsa_g_kc[l], NSA_KV)),
                           ones64[:KV_W, :KV_W])
    padw = lambda t: jnp.pad(t.reshape(bsz, seq, KV_W), ((0, 0), (WINDOW, 0), (0, 0)))
    bias_c, bias_w, bias_s, bias_far = _bias_tables(rel_bias, seq)
    overlap_t, penalty = _selection_tables(seq)
    ksx = jnp.concatenate([padw(ks), jnp.broadcast_to(penalty, (bsz,) + penalty.shape)], axis=-1)
    y_nsa = _nsa(q.reshape(bsz, seq, NSA_WIDTH), gates.reshape(bsz, seq, LANES), kcmp, vcmp,
                 ksx, padw(vs), padw(kw), padw(vw), overlap_t, bias_c, bias_w, bias_s, bias_far,
                 row(out_g_nsa[l][perm]))

    kx, vx = _memkv(mem, row(norm_mem[l]), xa_w_kv[l].astype(BF16), row(xa_g_k[l]))
    wo_mix = w_out[l]
    wr = jnp.pad(jnp.concatenate([router_g_w[l], router_e_w[l]], axis=1),
                 ((0, 0), (0, LANES - N_GROUPS - N_EXPERTS)))
    wr_hi = wr.astype(BF16)
    br = jnp.pad(jnp.concatenate([router_g_b[l], router_e_b[l]]), (0, LANES - N_GROUPS - N_EXPERTS)).reshape(1, -1)
    h2, xt, rw, ri, counts = _mid(
        x, y_rg, y_nsa, wo_mix[:RG_WIDTH].astype(BF16), wo_mix[RG_WIDTH:][perm].astype(BF16), row(norm_x[l]),
        xa_w_q[l].astype(BF16), row(xa_g_q[l] * (X_HEAD_DIM ** -0.5 * LOG2E)), kx, vx, xa_w_o[l].astype(BF16),
        row(norm_moe[l]), wr_hi, (wr - wr_hi.astype(F32)).astype(BF16), br)

    n_slots = 2 * n_tok
    n_blocks = n_slots // MOE_TB + N_EXPERTS
    n_pad = n_blocks * MOE_TB
    cnt = counts[0, :N_EXPERTS].astype(jnp.int32)
    pcnt = (cnt + MOE_TB - 1) // MOE_TB * MOE_TB
    pends = jnp.cumsum(pcnt)
    pstart = jnp.pad((pends - pcnt).astype(F32), (0, LANES - N_EXPERTS)).reshape(1, LANES)
    blk_exp = jnp.minimum(jnp.sum(pends[None, :] <= jnp.arange(n_blocks, dtype=jnp.int32)[:, None] * MOE_TB, axis=1),
                          N_EXPERTS - 1).astype(jnp.int32)
    n_used = (pends[-1:] // MOE_TB).astype(jnp.int32)
    dest = _dest(ri.reshape(n_tok, LANES), pstart)
    tmd = min(TM_DMA, n_tok)
    da = dest[:, 0].reshape(n_tok // tmd, 1, tmd)
    db = dest[:, 1].reshape(n_tok // tmd, 1, tmd)
    xs = _dispatch(cnt, (pends - pcnt).astype(jnp.int32), da, db, xt, n_pad)
    ys = _ffn(blk_exp, n_used, xs, exp_w1[l], exp_w3[l], exp_w2[l])
    out = _combine(da, db, h2.reshape(n_tok, D_MODEL), rw.reshape(n_tok, LANES), ys)
    return out.reshape(bsz, seq, D_MODEL)
```

```python
import functools
import math

import numpy as np
import jax
import jax.numpy as jnp
from jax import lax
from jax.experimental import pallas as pl
from jax.experimental.pallas import tpu as pltpu

F32 = jnp.float32
BF16 = jnp.bfloat16

D_MODEL = 1024
RG_WIDTH = 512
RG_BLOCKS = 8
RG_BLOCK = 64
CONV_W = 4
RG_C = 8.0
NSA_WIDTH = 512
NSA_HEADS = 8
HEAD_DIM = 64
NSA_KV = 2
NSA_HPG = 4
KV_W = 128
CMP_L = 32
CMP_STRIDE = 16
CMP_HIDDEN = 256
SEL_L = 64
N_SEL = 8
WINDOW = 512
NUM_BUCKETS = 32
MAX_DIST = 128
X_HEADS = 4
X_HEAD_DIM = 256
N_GROUPS = 4
EXP_PER_GROUP = 8
N_EXPERTS = 32
D_EXPERT = 512
EPS = 1e-6
LOG2E = 1.0 / math.log(2.0)
NEG_INF = -1e30
MASKED_BELOW = -1e29
SEL_FORCE = 1e9
LANES = 128

TQ = 64
NEAR = WINDOW + TQ
FAR_TK = 512
QROWS2 = NSA_HEADS * TQ
UNSEL_PENALTY = 2.0 ** 100
NSA_NB = 4

TM_PROJ = 1024
TM_MID = 1024
TM_DEST = 512
TM_DMA = 512
DMA_UNROLL = 8
MOE_TB = 512
ROW_TILE = D_MODEL // LANES
RG_CHUNK = 256
SCAN_ROWS = 8
SCAN_UNROLL = 4
VMEM_LIMIT = 56 * 1024 * 1024


def _cparams(n_axes):
    return pltpu.CompilerParams(dimension_semantics=("arbitrary",) * n_axes,
                                vmem_limit_bytes=VMEM_LIMIT)


def _dot(a, b):
    return jnp.dot(a, b, preferred_element_type=F32)


def _dot_nt(a, b):
    return lax.dot_general(a, b, (((1,), (1,)), ((), ())), preferred_element_type=F32)


def _gelu_tanh(x):
    c = math.sqrt(2.0 / math.pi)
    half = 0.5 * x
    return half + half * jnp.tanh(x * (c + (c * 0.044715) * (x * x)))


def _sigmoid(x):
    return 0.5 * jnp.tanh(0.5 * x) + 0.5


def _rms(x, g):
    return x * lax.rsqrt(jnp.mean(x * x, axis=-1, keepdims=True) + EPS) * g


def _group_rms(x, ones_blk, g):
    ms = _dot((x * x).astype(BF16), ones_blk)
    return x * lax.rsqrt(ms + EPS) * g


def _inproj_kernel(x_ref, g_ref, wrg_ref, wq_ref, wkv_ref, wgl_ref, gq_ref, gks_ref, gkw_ref, ones_ref,
                   u_ref, gate_ref, q_ref, kc_ref, vc_ref, ks_ref, vs_ref, kw_ref, vw_ref, gates_ref):
    xb = _rms(x_ref[...], g_ref[...]).astype(BF16)
    rg = _dot(xb, wrg_ref[...])
    u_ref[...] = rg[:, :RG_WIDTH].astype(BF16)
    gate_ref[...] = rg[:, RG_WIDTH:].astype(BF16)
    q = _dot(xb, wq_ref[...])
    q_ref[...] = _group_rms(q, ones_ref[...], gq_ref[...]).astype(BF16)
    kv = _dot(xb, wkv_ref[...])
    ones_kv = ones_ref[:KV_W, :KV_W]
    kc_ref[...] = kv[:, 0 * KV_W:1 * KV_W].astype(BF16)
    vc_ref[...] = kv[:, 1 * KV_W:2 * KV_W].astype(BF16)
    ks_ref[...] = _group_rms(kv[:, 2 * KV_W:3 * KV_W], ones_kv, gks_ref[...]).astype(BF16)
    vs_ref[...] = kv[:, 3 * KV_W:4 * KV_W].astype(BF16)
    kw_ref[...] = _group_rms(kv[:, 4 * KV_W:5 * KV_W], ones_kv, gkw_ref[...]).astype(BF16)
    vw_ref[...] = kv[:, 5 * KV_W:6 * KV_W].astype(BF16)
    gates_ref[...] = _sigmoid(_dot(xb, wgl_ref[...]))


def _inproj(x2, g, wrg, wq, wkv, wgl, gq, gks, gkw, ones_blk):
    n_tok = x2.shape[0]
    tm = min(TM_PROJ, n_tok)
    full = lambda a: pl.BlockSpec(a.shape, lambda i: (0,) * a.ndim)
    row = lambda w: pl.BlockSpec((tm, w), lambda i: (i, 0))
    outs = [(RG_WIDTH, BF16), (RG_WIDTH, BF16), (NSA_WIDTH, BF16)] + [(KV_W, BF16)] * 6 + [(LANES, F32)]
    return pl.pallas_call(
        _inproj_kernel,
        grid=(n_tok // tm,),
        in_specs=[row(D_MODEL)] + [full(a) for a in (g, wrg, wq, wkv, wgl, gq, gks, gkw, ones_blk)],
        out_specs=[row(w) for w, _ in outs],
        out_shape=[jax.ShapeDtypeStruct((n_tok, w), dt) for w, dt in outs],
        compiler_params=_cparams(1),
    )(x2, g, wrg, wq, wkv, wgl, gq, gks, gkw, ones_blk)


def _rglru_kernel(u_ref, gate_ref, cw_ref, cb_ref, wg_ref, bg_ref, lam_ref, og_ref, y_ref, upad, a_s, h_s):
    seq = u_ref.shape[1]
    upad[0:8, :] = jnp.zeros((8, RG_WIDTH), F32)
    upad[8:8 + seq, :] = u_ref[0].astype(F32)
    neg_lam = -lam_ref[...]
    softplus = jnp.maximum(neg_lam, 0.0) + jnp.log(1.0 + jnp.exp(-jnp.abs(neg_lam)))
    log2_a_half = (-0.5 * RG_C * LOG2E) * softplus
    ch = min(RG_CHUNK, seq)
    for c in range(seq // ch):
        r0 = c * ch
        uc = cb_ref[...]
        for k in range(CONV_W):
            off = 8 + r0 - (CONV_W - 1) + k
            uc = uc + cw_ref[k:k + 1, :] * upad[off:off + ch, :]
        th = jnp.tanh(_dot(uc.astype(BF16), wg_ref[...]) + bg_ref[...])
        a = jnp.exp2(log2_a_half * th[:, :RG_WIDTH] + log2_a_half)
        a_s[r0:r0 + ch, :] = a
        s = 1.0 - a * a
        h_s[r0:r0 + ch, :] = s * lax.rsqrt(jnp.maximum(s, 1e-30)) * (0.5 * th[:, RG_WIDTH:] + 0.5) * uc

    row = lax.broadcasted_iota(jnp.int32, (SCAN_ROWS, RG_WIDTH), 0)

    def block(j, h_prev):
        rows = pl.ds(pl.multiple_of(j * SCAN_ROWS, SCAN_ROWS), SCAN_ROWS)
        a = a_s[rows, :]
        b = h_s[rows, :]
        k = 1
        while k < SCAN_ROWS:
            keep = row >= k
            b = jnp.where(keep, a * pltpu.roll(b, k, 0) + b, b)
            a = jnp.where(keep, a * pltpu.roll(a, k, 0), a)
            k *= 2
        h = a * h_prev + b
        h_s[rows, :] = h
        return h[SCAN_ROWS - 1:SCAN_ROWS, :]

    lax.fori_loop(0, seq // SCAN_ROWS, block, jnp.zeros((1, RG_WIDTH), F32), unroll=SCAN_UNROLL)

    for c in range(seq // ch):
        r0 = c * ch
        y = _gelu_tanh(gate_ref[0, r0:r0 + ch, :].astype(F32)) * h_s[r0:r0 + ch, :]
        y_ref[0, r0:r0 + ch, :] = _rms(y, og_ref[...]).astype(BF16)


def _rglru(u3, gate3, cw, cb, wg, bg, lam, og):
    bsz, seq, _ = u3.shape
    full = lambda a: pl.BlockSpec(a.shape, lambda b: (0,) * a.ndim)
    blk = pl.BlockSpec((1, seq, RG_WIDTH), lambda b: (b, 0, 0))
    return pl.pallas_call(
        _rglru_kernel,
        grid=(bsz,),
        in_specs=[blk, blk] + [full(a) for a in (cw, cb, wg, bg, lam, og)],
        out_specs=blk,
        out_shape=jax.ShapeDtypeStruct((bsz, seq, RG_WIDTH), BF16),
        scratch_shapes=[pltpu.VMEM((seq + 8, RG_WIDTH), F32), pltpu.VMEM((seq, RG_WIDTH), F32),
                        pltpu.VMEM((seq, RG_WIDTH), F32)],
        compiler_params=_cparams(1),
    )(u3, gate3, cw, cb, wg, bg, lam, og)


def _compress_kernel(kx_ref, vx_ref, w1k_ref, w2k_ref, pk_ref, w1v_ref, w2v_ref, pv_ref, gk_ref, ones_ref,
                     ko_ref, vo_ref):
    n_chunk = kx_ref.shape[1]
    half = NSA_KV * CMP_HIDDEN

    def mlp(x_ref, w1_ref, w2_ref, p_ref):
        ab = _dot(x_ref[0], w1_ref[...])
        pos = _dot(p_ref[...], w1_ref[...])
        hid = ab[:, :half] + pltpu.roll(ab[:, half:], n_chunk - 1, 0) + (pos[0:1, :half] + pos[1:2, half:])
        return _dot(_gelu_tanh(hid).astype(BF16), w2_ref[...])

    kc = mlp(kx_ref, w1k_ref, w2k_ref, pk_ref)
    ko_ref[0] = _group_rms(kc, ones_ref[...], gk_ref[...]).astype(BF16)
    vo_ref[0] = mlp(vx_ref, w1v_ref, w2v_ref, pv_ref).astype(BF16)


def _compress(kx, vx, w1k, w2k, pk, w1v, w2v, pv, gk, ones_kv):
    bsz, n_chunk, width = kx.shape
    full = lambda a: pl.BlockSpec(a.shape, lambda b: (0,) * a.ndim)
    xin = pl.BlockSpec((1, n_chunk, width), lambda b: (b, 0, 0))
    out = pl.BlockSpec((1, n_chunk, KV_W), lambda b: (b, 0, 0))
    return pl.pallas_call(
        _compress_kernel,
        grid=(bsz,),
        in_specs=[xin, xin] + [full(a) for a in (w1k, w2k, pk, w1v, w2v, pv, gk, ones_kv)],
        out_specs=[out, out],
        out_shape=[jax.ShapeDtypeStruct((bsz, n_chunk, KV_W), BF16)] * 2,
        compiler_params=_cparams(1),
    )(kx, vx, w1k, w2k, pk, w1v, w2v, pv, gk, ones_kv)


def _nsa_kernel(q_ref, gates_ref, kcmp_ref, vcmp_ref, ksx_ref, vsp_ref, kwp_ref, vwp_ref, ovt_ref,
                bc_ref, bw_ref, bs_ref, bf_ref, og_ref, y_ref):
    i = pl.program_id(1)
    t0 = pl.multiple_of(i * TQ, TQ)
    n_blk = ovt_ref.shape[0]
    lane = lax.broadcasted_iota(jnp.int32, (TQ, LANES), 1)
    lo_half = lane < HEAD_DIM
    n_batch = q_ref.shape[0]

    def with_ones(v):
        return jnp.concatenate([v, jnp.ones_like(v)], axis=1)

    def near_part(bb):
        pieces = []
        for p in range(NSA_HPG):
            qs = q_ref[bb, :, p * LANES:(p + 1) * LANES]
            zero = jnp.zeros_like(qs)
            pieces += [jnp.where(lo_half, qs, zero), jnp.where(lo_half, zero, qs)]
        q8 = jnp.concatenate(pieces, axis=0)

        bc = bc_ref[0]
        lc = _dot_nt(q8, kcmp_ref[bb]) + bc
        ec = jnp.where(bc > MASKED_BELOW, jnp.exp2(lc - jnp.max(lc, axis=-1, keepdims=True)), 0.0)
        sc = jnp.sum(ec, axis=-1, keepdims=True)
        pc = ec / jnp.where(sc > 0.0, sc, 1.0)
        o_c = _dot(pc.astype(BF16), vcmp_ref[bb])

        blocks = [pc[r * TQ:(r + 1) * TQ] for r in range(NSA_HPG * NSA_KV)]
        pcs = jnp.concatenate([sum(blocks[g::NSA_KV]) for g in range(NSA_KV)], axis=0)
        pcs_hi = pcs.astype(BF16)
        pcs_lo = (pcs - pcs_hi.astype(F32)).astype(BF16)
        imp = _dot_nt(ovt_ref[...], pcs_hi) + _dot_nt(ovt_ref[...], pcs_lo)
        blk = lax.broadcasted_iota(jnp.int32, imp.shape, 0)
        forced = (blk == 0) | (blk == i) | (blk == i - 1)
        score = jnp.where(forced, SEL_FORCE, jnp.where(blk > i, -3e38, imp))
        rank = jnp.zeros(imp.shape, F32)
        for m in range(n_blk):
            row = score[m:m + 1, :]
            rank = rank + jnp.where(blk > m, jnp.where(row >= score, 1.0, 0.0), jnp.where(row > score, 1.0, 0.0))
        unsel = jnp.where(rank < N_SEL, 0.0, 1.0)
        unsel_far = jnp.where(blk >= i - WINDOW // SEL_L, 1.0, unsel)
        pad = jnp.zeros((LANES - 2 * n_blk, imp.shape[1]), F32)
        u_t = jnp.concatenate([unsel, unsel_far, pad], axis=0).T
        u_lane = lax.broadcasted_iota(jnp.int32, u_t.shape, 1)
        u_near = jnp.where(u_lane < n_blk, u_t, 0.0).astype(BF16)
        u_far = jnp.where(u_lane >= n_blk, u_t, 0.0).astype(BF16)
        qx_near = jnp.concatenate([q8, jnp.concatenate([u_near] * NSA_HPG, axis=0)], axis=1)
        qx_far = jnp.concatenate([q8, jnp.concatenate([u_far] * NSA_HPG, axis=0)], axis=1)

        lw = _dot_nt(q8, kwp_ref[bb, pl.ds(t0, NEAR), :]) + bw_ref[0]
        ew = jnp.exp2(lw - jnp.max(lw, axis=-1, keepdims=True))
        ow2 = _dot(ew.astype(BF16), with_ones(vwp_ref[bb, pl.ds(t0, NEAR), :]))
        o_w = ow2[:, :LANES] / ow2[:, LANES:]

        ls = _dot_nt(qx_near, ksx_ref[bb, pl.ds(t0, NEAR), :]) + bs_ref[0]
        m1 = jnp.max(ls, axis=-1, keepdims=True)
        e1 = jnp.exp2(ls - m1)
        acc1 = _dot(e1.astype(BF16), with_ones(vsp_ref[bb, pl.ds(t0, NEAR), :]))
        return o_c, o_w, qx_far, (m1, acc1)

    near = [near_part(bb) for bb in range(n_batch)]
    bfar = bf_ref[...]

    def far_step(kf, carry):
        base = pl.multiple_of(WINDOW + kf * FAR_TK, FAR_TK)
        new = []
        for bb in range(n_batch):
            m, acc = carry[bb]
            lf = _dot_nt(near[bb][2], ksx_ref[bb, pl.ds(base, FAR_TK), :]) + bfar
            m_new = jnp.maximum(m, jnp.max(lf, axis=-1, keepdims=True))
            alpha = jnp.exp2(m - m_new)
            e = jnp.exp2(lf - m_new)
            new.append((m_new, alpha * acc + _dot(e.astype(BF16), with_ones(vsp_ref[bb, pl.ds(base, FAR_TK), :]))))
        return tuple(new)

    n_far = (jnp.maximum(t0 - WINDOW, 0) + FAR_TK - 1) // FAR_TK
    far = lax.fori_loop(0, n_far, far_step, tuple(part[3] for part in near))

    for bb in range(n_batch):
        o_c, o_w = near[bb][0], near[bb][1]
        _, acc_s = far[bb]
        o_s = acc_s[:, :LANES] / acc_s[:, LANES:]
        gates = gates_ref[bb]

        def gate_col(j):
            cols = [gates[:, (g * NSA_HPG + p) * 3 + j:(g * NSA_HPG + p) * 3 + j + 1]
                    for p in range(NSA_HPG) for g in range(NSA_KV)]
            return jnp.concatenate(cols, axis=0)

        out = gate_col(0) * o_c + gate_col(1) * o_s + gate_col(2) * o_w
        slabs = [jnp.where(lo_half, out[(2 * p) * TQ:(2 * p + 1) * TQ], out[(2 * p + 1) * TQ:(2 * p + 2) * TQ])
                 for p in range(NSA_HPG)]
        y_ref[bb] = _rms(jnp.concatenate(slabs, axis=-1), og_ref[...]).astype(BF16)


def _nsa(q3, gates3, kcmp, vcmp, ksx, vsp, kwp, vwp, ovt, bias_c, bias_w, bias_s, bias_far, og):
    bsz, seq, _ = q3.shape
    n_chunk = kcmp.shape[1]
    n_var = bias_w.shape[0] - 1
    nb = NSA_NB if bsz % NSA_NB == 0 else 1
    full = lambda a: pl.BlockSpec(a.shape, lambda b, i: (0,) * a.ndim)
    per_b = lambda a: pl.BlockSpec((nb,) + a.shape[1:], lambda b, i: (b,) + (0,) * (a.ndim - 1))
    near = pl.BlockSpec((1, QROWS2, NEAR), lambda b, i: (jnp.minimum(i, n_var), 0, 0))
    return pl.pallas_call(
        _nsa_kernel,
        grid=(bsz // nb, seq // TQ),
        in_specs=[pl.BlockSpec((nb, TQ, NSA_WIDTH), lambda b, i: (b, i, 0)),
                  pl.BlockSpec((nb, TQ, LANES), lambda b, i: (b, i, 0)),
                  per_b(kcmp), per_b(vcmp), per_b(ksx), per_b(vsp), per_b(kwp), per_b(vwp),
                  full(ovt),
                  pl.BlockSpec((1, QROWS2, n_chunk), lambda b, i: (i, 0, 0)),
                  near, near, full(bias_far), full(og)],
        out_specs=pl.BlockSpec((nb, TQ, NSA_WIDTH), lambda b, i: (b, i, 0)),
        out_shape=jax.ShapeDtypeStruct((bsz, seq, NSA_WIDTH), BF16),
        compiler_params=_cparams(2),
    )(q3, gates3, kcmp, vcmp, ksx, vsp, kwp, vwp, ovt, bias_c, bias_w, bias_s, bias_far, og)


def _memkv_kernel(mem_ref, g_ref, wkv_ref, gk_ref, k_ref, v_ref):
    mn = _rms(mem_ref[0], g_ref[...]).astype(BF16)
    kv = _dot(mn, wkv_ref[...])
    for h in range(X_HEADS):
        sl = slice(h * X_HEAD_DIM, (h + 1) * X_HEAD_DIM)
        k_ref[0, :, sl] = _rms(kv[:, sl], gk_ref[...]).astype(BF16)
    v_ref[0] = kv[:, D_MODEL:].astype(BF16)


def _memkv(mem, g, wkv, gk):
    bsz, mlen, _ = mem.shape
    full = lambda a: pl.BlockSpec(a.shape, lambda b: (0,) * a.ndim)
    blk = pl.BlockSpec((1, mlen, D_MODEL), lambda b: (b, 0, 0))
    return pl.pallas_call(
        _memkv_kernel,
        grid=(bsz,),
        in_specs=[blk, full(g), full(wkv), full(gk)],
        out_specs=[blk, blk],
        out_shape=[jax.ShapeDtypeStruct((bsz, mlen, D_MODEL), BF16)] * 2,
        compiler_params=_cparams(1),
    )(mem, g, wkv, gk)


def _mid_kernel(x_ref, yrg_ref, ynsa_ref, woa_ref, wob_ref, gx_ref, wq_ref, gq_ref, k_ref, v_ref, wo_ref,
                gm_ref, wrh_ref, wrl_ref, br_ref, h_ref, xt_ref, rw_ref, ri_ref, cnt_ref):
    h1 = x_ref[0] + _dot(yrg_ref[0], woa_ref[...]) + _dot(ynsa_ref[0], wob_ref[...])

    q = _dot(_rms(h1, gx_ref[...]).astype(BF16), wq_ref[...])
    heads = []
    for h in range(X_HEADS):
        sl = slice(h * X_HEAD_DIM, (h + 1) * X_HEAD_DIM)
        qh = _rms(q[:, sl], gq_ref[...]).astype(BF16)
        lg = _dot_nt(qh, k_ref[0, :, sl])
        e = jnp.exp2(lg - jnp.max(lg, axis=-1, keepdims=True))
        heads.append(_dot(e.astype(BF16), v_ref[0, :, sl]) / jnp.sum(e, axis=-1, keepdims=True))
    h2 = h1 + _dot(jnp.concatenate(heads, axis=-1).astype(BF16), wo_ref[...])
    h_ref[0] = h2

    xt = _rms(h2, gm_ref[...])
    _store_row_tiles(xt_ref, xt)
    xt_hi = xt.astype(BF16)
    xt_lo = (xt - xt_hi.astype(F32)).astype(BF16)
    lg = _dot(xt_hi, wrh_ref[...]) + _dot(xt_lo, wrh_ref[...]) + _dot(xt_hi, wrl_ref[...]) + br_ref[...]
    lane = lax.broadcasted_iota(jnp.int32, lg.shape, 1)
    lane_f = lane.astype(F32)
    first_of = lambda hit: jnp.min(jnp.where(hit, lane_f, 1e9), axis=-1, keepdims=True)
    glog = jnp.where(lane < N_GROUPS, lg, -3e38)
    gmax = jnp.max(glog, axis=-1, keepdims=True)
    gsel = first_of(glog == gmax)
    p_g = 1.0 / jnp.sum(jnp.exp(glog - gmax), axis=-1, keepdims=True)
    lo = N_GROUPS + EXP_PER_GROUP * gsel
    el = jnp.where((lane_f >= lo) & (lane_f < lo + EXP_PER_GROUP), lg, -3e38)
    m_a = jnp.max(el, axis=-1, keepdims=True)
    i_a = first_of(el == m_a)
    el2 = jnp.where(lane_f == i_a, -3e38, el)
    m_b = jnp.max(el2, axis=-1, keepdims=True)
    i_b = first_of(el2 == m_b)
    r = jnp.exp(m_b - m_a)
    w_a = p_g / (1.0 + r)
    w_b = p_g * r / (1.0 + r)
    e_a = i_a - N_GROUPS
    e_b = i_b - N_GROUPS
    rw_ref[0] = jnp.where(lane == 0, w_a, jnp.where(lane == 1, w_b, 0.0))
    ri_ref[0] = jnp.where(lane == 0, e_a, jnp.where(lane == 1, e_b, 0.0)).astype(jnp.int32)

    @pl.when((pl.program_id(0) == 0) & (pl.program_id(1) == 0))
    def _():
        cnt_ref[...] = jnp.zeros_like(cnt_ref)

    hot = jnp.where((lane_f == e_a) | (lane_f == e_b), 1.0, 0.0)
    cnt_ref[...] += jnp.sum(hot, axis=0, keepdims=True)


def _mid(x, yrg, ynsa, woa, wob, gx, wq, gq, kx, vx, wo, gm, wrh, wrl, br):
    bsz, seq, _ = x.shape
    tm = min(TM_MID, seq)
    mlen = kx.shape[1]
    n_i = seq // tm
    full = lambda a: pl.BlockSpec(a.shape, lambda b, i: (0,) * a.ndim)
    tok = lambda w: pl.BlockSpec((1, tm, w), lambda b, i: (b, i, 0))
    memb = pl.BlockSpec((1, mlen, D_MODEL), lambda b, i: (b, 0, 0))
    xt_spec = pl.BlockSpec((tm * ROW_TILE, LANES), lambda b, i: (b * n_i + i, 0))
    return pl.pallas_call(
        _mid_kernel,
        grid=(bsz, seq // tm),
        in_specs=[tok(D_MODEL), tok(RG_WIDTH), tok(NSA_WIDTH), full(woa), full(wob), full(gx), full(wq), full(gq),
                  memb, memb, full(wo), full(gm), full(wrh), full(wrl), full(br)],
        out_specs=[tok(D_MODEL), xt_spec, tok(LANES), tok(LANES), pl.BlockSpec((1, LANES), lambda b, i: (0, 0))],
        out_shape=[jax.ShapeDtypeStruct((bsz, seq, D_MODEL), F32),
                   jax.ShapeDtypeStruct((bsz * seq * ROW_TILE, LANES), F32),
                   jax.ShapeDtypeStruct((bsz, seq, LANES), F32), jax.ShapeDtypeStruct((bsz, seq, LANES), jnp.int32),
                   jax.ShapeDtypeStruct((1, LANES), F32)],
        compiler_params=_cparams(2),
    )(x, yrg, ynsa, woa, wob, gx, wq, gq, kx, vx, wo, gm, wrh, wrl, br)


def _dest_kernel(ri_ref, pstart_ref, dest_ref, run_ref):
    @pl.when(pl.program_id(0) == 0)
    def _():
        run_ref[...] = jnp.zeros_like(run_ref)

    ri = ri_ref[...]
    tm = ri.shape[0]
    lane = lax.broadcasted_iota(jnp.int32, ri.shape, 1)
    e_a = ri[:, 0:1]
    e_b = ri[:, 1:2]
    hot_a = lane == e_a
    hot_b = lane == e_b
    hot = jnp.where(hot_a | hot_b, 1.0, 0.0)
    row = lax.broadcasted_iota(jnp.int32, (tm, tm), 0)
    col = lax.broadcasted_iota(jnp.int32, (tm, tm), 1)
    earlier = jnp.where(col < row, 1.0, 0.0).astype(BF16)
    base = _dot(earlier, hot.astype(BF16)) + run_ref[...] + pstart_ref[...]
    d_a = jnp.sum(jnp.where(hot_a, base, 0.0), axis=-1, keepdims=True)
    d_b = jnp.sum(jnp.where(hot_b, base, 0.0), axis=-1, keepdims=True)
    dest_ref[...] = jnp.where(lane == 0, d_a, jnp.where(lane == 1, d_b, 0.0)).astype(jnp.int32)
    run_ref[...] += jnp.sum(hot, axis=0, keepdims=True)


def _dest(ri2, pstart):
    n_tok = ri2.shape[0]
    tm = min(TM_DEST, n_tok)
    return pl.pallas_call(
        _dest_kernel,
        grid=(n_tok // tm,),
        in_specs=[pl.BlockSpec((tm, LANES), lambda i: (i, 0)), pl.BlockSpec((1, LANES), lambda i: (0, 0))],
        out_specs=pl.BlockSpec((tm, LANES), lambda i: (i, 0)),
        out_shape=jax.ShapeDtypeStruct((n_tok, LANES), jnp.int32),
        scratch_shapes=[pltpu.VMEM((1, LANES), F32)],
        compiler_params=_cparams(1),
    )(ri2, pstart)


def _store_row_tiles(ref, val):
    n = val.shape[0]
    for c in range(ROW_TILE):
        ref[pl.ds(c, n, stride=ROW_TILE), :] = val[:, c * LANES:(c + 1) * LANES]


def _load_row_tiles(ref, n):
    return [ref[pl.ds(c, n, stride=ROW_TILE), :] for c in range(ROW_TILE)]


def _token_rows(ref, t):
    return ref.at[pl.ds(pl.multiple_of(t * ROW_TILE, ROW_TILE), ROW_TILE), :]


def _dispatch_kernel(cnt_ref, pstart_ref, da_ref, db_ref, xt_ref, xs_ref, zrow, sem, zsem, *, zero_jobs_per_step):
    tm = da_ref.shape[2]
    i = pl.program_id(0)
    n_blocks = xs_ref.shape[0] // (ROW_TILE * MOE_TB)

    @pl.when(i == 0)
    def _():
        zrow[...] = jnp.zeros_like(zrow)

    def issue(t, c):
        pltpu.make_async_copy(_token_rows(xt_ref, t), _token_rows(xs_ref, da_ref[0, 0, t]), sem).start(priority=0)
        pltpu.make_async_copy(_token_rows(xt_ref, t), _token_rows(xs_ref, db_ref[0, 0, t]), sem).start(priority=1)
        return c

    lax.fori_loop(0, tm, issue, 0, unroll=DMA_UNROLL)

    last = N_EXPERTS - 1
    first_unused = (pstart_ref[last] + (cnt_ref[last] + MOE_TB - 1) // MOE_TB * MOE_TB) // MOE_TB

    def zero_job(k):
        e = jnp.minimum(k, last)
        used = cnt_ref[e]
        tail = (used + MOE_TB - 1) // MOE_TB * MOE_TB - used
        blk = first_unused + (k - N_EXPERTS)
        start = jnp.where(k < N_EXPERTS, pstart_ref[e] + used, blk * MOE_TB)
        rows = jnp.where(k < N_EXPERTS, tail, jnp.where((k < 2 * N_EXPERTS) & (blk < n_blocks), MOE_TB, 0))
        return start, rows

    jobs = [zero_job(i * zero_jobs_per_step + s) for s in range(zero_jobs_per_step)]
    for start, rows in jobs:
        def fill(r, c, start=start):
            pltpu.make_async_copy(zrow, _token_rows(xs_ref, start + r), zsem).start()
            return c

        lax.fori_loop(0, rows, fill, 0)

    def drain(t, c):
        pltpu.make_async_copy(_token_rows(xt_ref, 0), _token_rows(xs_ref, 0), sem).wait()
        pltpu.make_async_copy(_token_rows(xt_ref, 0), _token_rows(xs_ref, 0), sem).wait()
        return c

    lax.fori_loop(0, tm, drain, 0, unroll=DMA_UNROLL)

    for _, rows in jobs:
        def fill_done(r, c):
            pltpu.make_async_copy(zrow, _token_rows(xs_ref, 0), zsem).wait()
            return c

        lax.fori_loop(0, rows, fill_done, 0)


def _dispatch(cnt, pstart, da, db, xt_rows, n_pad):
    n_tiles, _, tm = da.shape
    body = functools.partial(_dispatch_kernel, zero_jobs_per_step=pl.cdiv(2 * N_EXPERTS, n_tiles))
    smem = pl.BlockSpec((1, 1, tm), lambda i, c, p: (i, 0, 0), memory_space=pltpu.SMEM)
    grid_spec = pltpu.PrefetchScalarGridSpec(
        num_scalar_prefetch=2,
        grid=(n_tiles,),
        in_specs=[smem, smem, pl.BlockSpec((tm * ROW_TILE, LANES), lambda i, c, p: (i, 0))],
        out_specs=pl.BlockSpec(memory_space=pl.ANY),
        scratch_shapes=[pltpu.VMEM((ROW_TILE, LANES), F32), pltpu.SemaphoreType.DMA(()),
                        pltpu.SemaphoreType.DMA(())],
    )
    return pl.pallas_call(
        body,
        grid_spec=grid_spec,
        out_shape=jax.ShapeDtypeStruct((n_pad * ROW_TILE, LANES), F32),
        compiler_params=pltpu.CompilerParams(dimension_semantics=("arbitrary",), has_side_effects=True,
                                             vmem_limit_bytes=VMEM_LIMIT),
    )(cnt, pstart, da, db, xt_rows)


def _ffn_kernel(bexp_ref, nused_ref, xs_ref, w1_ref, w3_ref, w2_ref, ys_ref):
    del bexp_ref
    j = pl.program_id(0)

    @pl.when(j < nused_ref[0])
    def _():
        xb = jnp.concatenate(_load_row_tiles(xs_ref, MOE_TB), axis=-1).astype(BF16)
        a = _dot(xb, w1_ref[0].astype(BF16))
        h = a * _sigmoid(a) * _dot(xb, w3_ref[0].astype(BF16))
        _store_row_tiles(ys_ref, _dot(h.astype(BF16), w2_ref[0].astype(BF16)))

    @pl.when(j >= nused_ref[0])
    def _():
        ys_ref[...] = jnp.zeros_like(ys_ref)


def _ffn(blk_exp, n_used, xs, w1, w3, w2):
    n_blocks = xs.shape[0] // (MOE_TB * ROW_TILE)
    rows = pl.BlockSpec((MOE_TB * ROW_TILE, LANES), lambda j, be, nu: (j, 0))
    used_rows = pl.BlockSpec((MOE_TB * ROW_TILE, LANES), lambda j, be, nu: (jnp.minimum(j, nu[0] - 1), 0))
    grid_spec = pltpu.PrefetchScalarGridSpec(
        num_scalar_prefetch=2,
        grid=(n_blocks,),
        in_specs=[used_rows,
                  pl.BlockSpec((1, D_MODEL, D_EXPERT), lambda j, be, nu: (be[j], 0, 0)),
                  pl.BlockSpec((1, D_MODEL, D_EXPERT), lambda j, be, nu: (be[j], 0, 0)),
                  pl.BlockSpec((1, D_EXPERT, D_MODEL), lambda j, be, nu: (be[j], 0, 0))],
        out_specs=rows,
    )
    return pl.pallas_call(
        _ffn_kernel,
        grid_spec=grid_spec,
        out_shape=jax.ShapeDtypeStruct(xs.shape, F32),
        compiler_params=_cparams(1),
    )(blk_exp, n_used, xs, w1, w3, w2)


def _combine_kernel(da_ref, db_ref, da_next_ref, db_next_ref, h_ref, rw_ref, ys_ref, o_ref, ya, yb, sems):
    tm = da_ref.shape[2]
    i = pl.program_id(0)
    slot = i % 2

    def start_gather(a_ref, b_ref, s):
        def issue(t, c):
            pltpu.make_async_copy(_token_rows(ys_ref, a_ref[0, 0, t]), _token_rows(ya.at[s], t),
                                  sems.at[s]).start(priority=0)
            pltpu.make_async_copy(_token_rows(ys_ref, b_ref[0, 0, t]), _token_rows(yb.at[s], t),
                                  sems.at[s]).start(priority=1)
            return c

        lax.fori_loop(0, tm, issue, 0, unroll=DMA_UNROLL)

    @pl.when(i == 0)
    def _():
        start_gather(da_ref, db_ref, slot)

    @pl.when(i + 1 < pl.num_programs(0))
    def _():
        start_gather(da_next_ref, db_next_ref, 1 - slot)

    def drain(t, c):
        pltpu.make_async_copy(_token_rows(ys_ref, 0), _token_rows(ya.at[slot], 0), sems.at[slot]).wait()
        pltpu.make_async_copy(_token_rows(ys_ref, 0), _token_rows(yb.at[slot], 0), sems.at[slot]).wait()
        return c

    lax.fori_loop(0, tm, drain, 0, unroll=DMA_UNROLL)
    rw = rw_ref[...]
    mix = [rw[:, 0:1] * a + rw[:, 1:2] * b
           for a, b in zip(_load_row_tiles(ya.at[slot], tm), _load_row_tiles(yb.at[slot], tm))]
    o_ref[...] = h_ref[...] + jnp.concatenate(mix, axis=-1)


def _combine(da, db, h2, rw, ys):
    n_tiles, _, tm = da.shape
    n_tok = h2.shape[0]
    smem = pl.BlockSpec((1, 1, tm), lambda i: (i, 0, 0), memory_space=pltpu.SMEM)
    smem_next = pl.BlockSpec((1, 1, tm), lambda i: (jnp.minimum(i + 1, n_tiles - 1), 0, 0), memory_space=pltpu.SMEM)
    row = lambda w: pl.BlockSpec((tm, w), lambda i: (i, 0))
    slots = pltpu.VMEM((2, tm * ROW_TILE, LANES), F32)
    return pl.pallas_call(
        _combine_kernel,
        grid=(n_tiles,),
        in_specs=[smem, smem, smem_next, smem_next, row(D_MODEL), row(LANES), pl.BlockSpec(memory_space=pl.ANY)],
        out_specs=row(D_MODEL),
        out_shape=jax.ShapeDtypeStruct((n_tok, D_MODEL), F32),
        scratch_shapes=[slots, slots, pltpu.SemaphoreType.DMA((2,))],
        compiler_params=_cparams(1),
    )(da, db, da, db, h2, rw, ys)


def _rel_bucket_np(dist):
    n = np.maximum(dist, 0)
    max_exact = NUM_BUCKETS // 2
    nf = np.maximum(n, 1).astype(np.float32)
    large = max_exact + (np.log(nf / max_exact) / math.log(MAX_DIST / max_exact)
                         * (NUM_BUCKETS - max_exact)).astype(np.int32)
    large = np.minimum(large, NUM_BUCKETS - 1)
    return np.where(n < max_exact, n, large).astype(np.int32)


def _toeplitz(vec, rows):
    width = vec.shape[-1] - 1
    flat = jnp.tile(vec, (1,) * (vec.ndim - 1) + (rows,))[..., :rows * width]
    return flat.reshape(vec.shape[:-1] + (rows, width))


def _bias_tables(rel_bias, seq):
    n_chunk = seq // CMP_STRIDE
    n_tiles = seq // TQ
    table = rel_bias.T.astype(F32)

    wide = NEAR + TQ
    k = np.arange(wide + 1)
    dw = np.where(k < NEAR, WINDOW - k, WINDOW + wide + 1 - k)
    used = (k < NEAR) | (k > wide + 1 - TQ)
    vals = table[:, _rel_bucket_np(dw)]

    n_var = WINDOW // TQ
    first_key = WINDOW - TQ * np.arange(n_var + 1)[:, None, None]
    in_seq = np.arange(NEAR)[None, None, :] >= first_key

    def near_tile(valid):
        t = _toeplitz(jnp.where(valid[None, :], vals, NEG_INF), TQ)[:, :, :NEAR]
        t = t.reshape(NSA_KV, NSA_HPG, TQ, NEAR).transpose(1, 0, 2, 3).reshape(1, QROWS2, NEAR)
        return jnp.where(in_seq, t, NEG_INF)

    bias_w = near_tile(used & (dw >= 0) & (dw < WINDOW))
    bias_s = near_tile(used & (dw >= 0))
    bias_far = table[:, NUM_BUCKETS - 1].reshape(NSA_KV, NSA_HPG, 1).transpose(1, 0, 2)
    bias_far = jnp.broadcast_to(bias_far, (NSA_HPG, NSA_KV, TQ)).reshape(QROWS2, 1)

    r = np.arange(CMP_STRIDE)[:, None]
    k = np.arange(2 * n_chunk + 1)[None, :]
    lag = 2 * n_chunk + 1 - k
    valid = (k > n_chunk + 1) & (CMP_STRIDE * lag + r >= CMP_L - 1)
    vals = table[:, _rel_bucket_np(CMP_STRIDE * lag + r - CMP_L // 2)]
    full = _toeplitz(jnp.where(valid[None], vals, NEG_INF), n_chunk)[..., :n_chunk]
    full = jnp.where(np.arange(n_chunk) < n_chunk - 1, full, NEG_INF)
    a4 = TQ // CMP_STRIDE
    full = full.reshape(NSA_KV, NSA_HPG, CMP_STRIDE, n_tiles, a4, n_chunk).transpose(3, 1, 0, 4, 2, 5)
    bias_c = full.reshape(n_tiles, QROWS2, n_chunk)
    return tuple(LOG2E * t for t in (bias_c, bias_w, bias_s, bias_far))


def _selection_tables(seq):
    n_chunk = seq // CMP_STRIDE
    n_blk = seq // SEL_L
    c = np.arange(n_chunk)
    n = np.arange(n_blk)
    start = c * CMP_STRIDE
    overlap_t = ((start[None, :] <= n[:, None] * SEL_L + SEL_L - 1) & (start[None, :] + CMP_L - 1 >= n[:, None] * SEL_L)
                 & (c < n_chunk - 1)[None, :])
    pos = np.arange(seq + WINDOW) - WINDOW
    lane_blk = np.arange(LANES) % n_blk
    hit = (pos[:, None] >= 0) & (pos[:, None] // SEL_L == lane_blk[None, :]) & (np.arange(LANES) < 2 * n_blk)[None, :]
    return jnp.asarray(overlap_t, BF16), jnp.asarray(np.where(hit, -UNSEL_PENALTY, 0.0), BF16)


def _block_ones(width, group):
    idx = np.arange(width) // group
    return jnp.asarray((idx[:, None] == idx[None, :]) / group, BF16)


def _block_diag(w):
    nb, n, m = w.shape
    eye = jnp.eye(nb, dtype=w.dtype)
    return jnp.einsum('hij,hg->higj', w, eye).reshape(nb * n, nb * m)


def _compress_weights(w1, w2, pos):
    half_l = CMP_L // 2
    parts = []
    for half in range(2):
        wh = w1[half * half_l * HEAD_DIM:(half + 1) * half_l * HEAD_DIM].reshape(half_l, HEAD_DIM, CMP_HIDDEN)
        z = jnp.zeros_like(wh)
        for g in range(NSA_KV):
            grp = [wh if gg == g else z for gg in range(NSA_KV)]
            parts.append(jnp.stack(grp, axis=1).reshape(half_l * KV_W, CMP_HIDDEN))
    w1cat = jnp.concatenate(parts, axis=1).astype(BF16)
    w2bd = _block_diag(jnp.stack([w2] * NSA_KV)).astype(BF16)
    prow = [jnp.tile(pos[half * half_l:(half + 1) * half_l][:, None, :], (1, NSA_KV, 1)).reshape(-1)
            for half in range(2)]
    pmat = jnp.zeros((8, half_l * KV_W), F32).at[0].set(prow[0]).at[1].set(prow[1]).astype(BF16)
    return w1cat, w2bd, pmat


def kernel(x, mem, rel_bias, norm_mix, w_in, rg_conv_w, rg_conv_b, rg_w_r, rg_b_r, rg_w_i, rg_b_i, rg_lambda, nsa_g_q, nsa_g_kc, nsa_g_ks, nsa_g_kw, cmp_pos_k, cmp_pos_v, cmp_k_w1, cmp_k_w2, cmp_v_w1, cmp_v_w2, out_g_rg, out_g_nsa, w_out, norm_x, norm_mem, xa_w_q, xa_w_kv, xa_w_o, xa_g_q, xa_g_k, norm_moe, router_g_w, router_g_b, router_e_w, router_e_b, exp_w1, exp_w3, exp_w2):
    bsz, seq, _ = x.shape
    n_tok = bsz * seq
    assert seq % FAR_TK == 0 and 2 * (seq // SEL_L) <= LANES and norm_mix.shape[0] == 1
    l = 0
    row = lambda v: v.reshape(1, -1).astype(F32)

    perm = np.array([(half * NSA_HPG + p) * HEAD_DIM + d
                     for p in range(NSA_HPG) for half in range(NSA_KV) for d in range(HEAD_DIM)])
    offs = np.cumsum([0, RG_WIDTH, RG_WIDTH, NSA_WIDTH] + [KV_W] * 6)
    w = w_in[l]
    wrg = w[:, :offs[2]].astype(BF16)
    wq = w[:, offs[2]:offs[3]][:, perm].astype(BF16)
    wkv = w[:, offs[3]:offs[9]].astype(BF16)
    wgl = jnp.pad(w[:, offs[9]:], ((0, 0), (0, LANES - 3 * NSA_HEADS))).astype(BF16)
    ones64 = _block_ones(NSA_WIDTH, HEAD_DIM)
    gq = row(jnp.tile(nsa_g_q[l], NSA_HEADS) * (HEAD_DIM ** -0.5 * LOG2E))
    u, gate, q, kc, vc, ks, vs, kw, vw, gates = _inproj(
        x.reshape(n_tok, D_MODEL), row(norm_mix[l]), wrg, wq, wkv, wgl, gq,
        row(jnp.tile(nsa_g_ks[l], NSA_KV)), row(jnp.tile(nsa_g_kw[l], NSA_KV)), ones64)

    wg = (0.5 * jnp.concatenate([_block_diag(rg_w_r[l]), _block_diag(rg_w_i[l])], axis=1)).astype(BF16)
    bg = 0.5 * jnp.concatenate([rg_b_r[l], rg_b_i[l]]).reshape(1, -1)
    y_rg = _rglru(u.reshape(bsz, seq, RG_WIDTH), gate.reshape(bsz, seq, RG_WIDTH),
                  rg_conv_w[l].reshape(CONV_W, RG_WIDTH), row(rg_conv_b[l]), wg, bg, row(rg_lambda[l]),
                  row(out_g_rg[l]))

    n_chunk = seq // CMP_STRIDE
    w1k, w2k, pk = _compress_weights(cmp_k_w1[l], cmp_k_w2[l], cmp_pos_k[l])
    w1v, w2v, pv = _compress_weights(cmp_v_w1[l], cmp_v_w2[l], cmp_pos_v[l])
    kcmp, vcmp = _compress(kc.reshape(bsz, n_chunk, CMP_STRIDE * KV_W), vc.reshape(bsz, n_chunk, CMP_STRIDE * KV_W),
                           w1k, w2k, pk, w1v, w2v, pv, row(jnp.tile(nsa_g_kc[l], NSA_KV)),
                           ones64[:KV_W, :KV_W])
    padw = lambda t: jnp.pad(t.reshape(bsz, seq, KV_W), ((0, 0), (WINDOW, 0), (0, 0)))
    bias_c, bias_w, bias_s, bias_far = _bias_tables(rel_bias, seq)
    overlap_t, penalty = _selection_tables(seq)
    ksx = jnp.concatenate([padw(ks), jnp.broadcast_to(penalty, (bsz,) + penalty.shape)], axis=-1)
    y_nsa = _nsa(q.reshape(bsz, seq, NSA_WIDTH), gates.reshape(bsz, seq, LANES), kcmp, vcmp,
                 ksx, padw(vs), padw(kw), padw(vw), overlap_t, bias_c, bias_w, bias_s, bias_far,
                 row(out_g_nsa[l][perm]))

    kx, vx = _memkv(mem, row(norm_mem[l]), xa_w_kv[l].astype(BF16), row(xa_g_k[l]))
    wo_mix = w_out[l]
    wr = jnp.pad(jnp.concatenate([router_g_w[l], router_e_w[l]], axis=1),
                 ((0, 0), (0, LANES - N_GROUPS - N_EXPERTS)))
    wr_hi = wr.astype(BF16)
    br = jnp.pad(jnp.concatenate([router_g_b[l], router_e_b[l]]), (0, LANES - N_GROUPS - N_EXPERTS)).reshape(1, -1)
    h2, xt, rw, ri, counts = _mid(
        x, y_rg, y_nsa, wo_mix[:RG_WIDTH].astype(BF16), wo_mix[RG_WIDTH:][perm].astype(BF16), row(norm_x[l]),
        xa_w_q[l].astype(BF16), row(xa_g_q[l] * (X_HEAD_DIM ** -0.5 * LOG2E)), kx, vx, xa_w_o[l].astype(BF16),
        row(norm_moe[l]), wr_hi, (wr - wr_hi.astype(F32)).astype(BF16), br)

    n_slots = 2 * n_tok
    n_blocks = n_slots // MOE_TB + N_EXPERTS
    n_pad = n_blocks * MOE_TB
    cnt = counts[0, :N_EXPERTS].astype(jnp.int32)
    pcnt = (cnt + MOE_TB - 1) // MOE_TB * MOE_TB
    pends = jnp.cumsum(pcnt)
    pstart = jnp.pad((pends - pcnt).astype(F32), (0, LANES - N_EXPERTS)).reshape(1, LANES)
    blk_exp = jnp.minimum(jnp.sum(pends[None, :] <= jnp.arange(n_blocks, dtype=jnp.int32)[:, None] * MOE_TB, axis=1),
                          N_EXPERTS - 1).astype(jnp.int32)
    n_used = (pends[-1:] // MOE_TB).astype(jnp.int32)
    dest = _dest(ri.reshape(n_tok, LANES), pstart)
    tmd = min(TM_DMA, n_tok)
    da = dest[:, 0].reshape(n_tok // tmd, 1, tmd)
    db = dest[:, 1].reshape(n_tok // tmd, 1, tmd)
    xs = _dispatch(cnt, (pends - pcnt).astype(jnp.int32), da, db, xt, n_pad)
    ys = _ffn(blk_exp, n_used, xs, exp_w1[l], exp_w3[l], exp_w2[l])
    out = _combine(da, db, h2.reshape(n_tok, D_MODEL), rw.reshape(n_tok, LANES), ys)
    return out.reshape(bsz, seq, D_MODEL)
```

```python
import math

import numpy as np
import jax
import jax.numpy as jnp
from jax import lax
from jax.experimental import pallas as pl
from jax.experimental.pallas import tpu as pltpu

F32 = jnp.float32
BF16 = jnp.bfloat16

D_MODEL = 1024
RG_WIDTH = 512
RG_BLOCKS = 8
RG_BLOCK = 64
CONV_W = 4
RG_C = 8.0
NSA_WIDTH = 512
NSA_HEADS = 8
HEAD_DIM = 64
NSA_KV = 2
NSA_HPG = 4
KV_W = 128
CMP_L = 32
CMP_STRIDE = 16
CMP_HIDDEN = 256
SEL_L = 64
N_SEL = 8
WINDOW = 512
NUM_BUCKETS = 32
MAX_DIST = 128
X_HEADS = 4
X_HEAD_DIM = 256
N_GROUPS = 4
EXP_PER_GROUP = 8
N_EXPERTS = 32
D_EXPERT = 512
EPS = 1e-6
LOG2E = 1.0 / math.log(2.0)
NEG_INF = -1e30
MASKED_BELOW = -1e29
SEL_FORCE = 1e9
LANES = 128

TQ = 64
NEAR = WINDOW + TQ
FAR_TK = 512
QROWS2 = NSA_HEADS * TQ
UNSEL_PENALTY = 2.0 ** 100
NSA_NB = 4

TM_PROJ = 1024
TM_MID = 1024
TM_DEST = 512
TM_DMA = 512
DMA_UNROLL = 8
MOE_TB = 512
ROW_TILE = D_MODEL // LANES
RG_CHUNK = 256
SCAN_ROWS = 8
SCAN_UNROLL = 4
VMEM_LIMIT = 56 * 1024 * 1024


def _cparams(n_axes):
    return pltpu.CompilerParams(dimension_semantics=("arbitrary",) * n_axes,
                                vmem_limit_bytes=VMEM_LIMIT)


def _dot(a, b):
    return jnp.dot(a, b, preferred_element_type=F32)


def _dot_nt(a, b):
    return lax.dot_general(a, b, (((1,), (1,)), ((), ())), preferred_element_type=F32)


def _gelu_tanh(x):
    c = math.sqrt(2.0 / math.pi)
    half = 0.5 * x
    return half + half * jnp.tanh(x * (c + (c * 0.044715) * (x * x)))


def _sigmoid(x):
    return 0.5 * jnp.tanh(0.5 * x) + 0.5


def _rms(x, g):
    return x * lax.rsqrt(jnp.mean(x * x, axis=-1, keepdims=True) + EPS) * g


def _group_rms(x, ones_blk, g):
    ms = _dot((x * x).astype(BF16), ones_blk)
    return x * lax.rsqrt(ms + EPS) * g


def _inproj_kernel(x_ref, g_ref, wrg_ref, wq_ref, wkv_ref, wgl_ref, gq_ref, gks_ref, gkw_ref, ones_ref,
                   u_ref, gate_ref, q_ref, kc_ref, vc_ref, ks_ref, vs_ref, kw_ref, vw_ref, gates_ref):
    xb = _rms(x_ref[...], g_ref[...]).astype(BF16)
    rg = _dot(xb, wrg_ref[...])
    u_ref[...] = rg[:, :RG_WIDTH].astype(BF16)
    gate_ref[...] = rg[:, RG_WIDTH:].astype(BF16)
    q = _dot(xb, wq_ref[...])
    q_ref[...] = _group_rms(q, ones_ref[...], gq_ref[...]).astype(BF16)
    kv = _dot(xb, wkv_ref[...])
    ones_kv = ones_ref[:KV_W, :KV_W]
    kc_ref[...] = kv[:, 0 * KV_W:1 * KV_W].astype(BF16)
    vc_ref[...] = kv[:, 1 * KV_W:2 * KV_W].astype(BF16)
    ks_ref[...] = _group_rms(kv[:, 2 * KV_W:3 * KV_W], ones_kv, gks_ref[...]).astype(BF16)
    vs_ref[...] = kv[:, 3 * KV_W:4 * KV_W].astype(BF16)
    kw_ref[...] = _group_rms(kv[:, 4 * KV_W:5 * KV_W], ones_kv, gkw_ref[...]).astype(BF16)
    vw_ref[...] = kv[:, 5 * KV_W:6 * KV_W].astype(BF16)
    gates_ref[...] = _sigmoid(_dot(xb, wgl_ref[...]))


def _inproj(x2, g, wrg, wq, wkv, wgl, gq, gks, gkw, ones_blk):
    n_tok = x2.shape[0]
    tm = min(TM_PROJ, n_tok)
    full = lambda a: pl.BlockSpec(a.shape, lambda i: (0,) * a.ndim)
    row = lambda w: pl.BlockSpec((tm, w), lambda i: (i, 0))
    outs = [(RG_WIDTH, BF16), (RG_WIDTH, BF16), (NSA_WIDTH, BF16)] + [(KV_W, BF16)] * 6 + [(LANES, F32)]
    return pl.pallas_call(
        _inproj_kernel,
        grid=(n_tok // tm,),
        in_specs=[row(D_MODEL)] + [full(a) for a in (g, wrg, wq, wkv, wgl, gq, gks, gkw, ones_blk)],
        out_specs=[row(w) for w, _ in outs],
        out_shape=[jax.ShapeDtypeStruct((n_tok, w), dt) for w, dt in outs],
        compiler_params=_cparams(1),
    )(x2, g, wrg, wq, wkv, wgl, gq, gks, gkw, ones_blk)


def _rglru_kernel(u_ref, gate_ref, cw_ref, cb_ref, wg_ref, bg_ref, lam_ref, og_ref, y_ref, upad, a_s, h_s):
    seq = u_ref.shape[1]
    upad[0:8, :] = jnp.zeros((8, RG_WIDTH), F32)
    upad[8:8 + seq, :] = u_ref[0].astype(F32)
    neg_lam = -lam_ref[...]
    softplus = jnp.maximum(neg_lam, 0.0) + jnp.log(1.0 + jnp.exp(-jnp.abs(neg_lam)))
    log2_a_half = (-0.5 * RG_C * LOG2E) * softplus
    ch = min(RG_CHUNK, seq)
    for c in range(seq // ch):
        r0 = c * ch
        uc = cb_ref[...]
        for k in range(CONV_W):
            off = 8 + r0 - (CONV_W - 1) + k
            uc = uc + cw_ref[k:k + 1, :] * upad[off:off + ch, :]
        th = jnp.tanh(_dot(uc.astype(BF16), wg_ref[...]) + bg_ref[...])
        a = jnp.exp2(log2_a_half * th[:, :RG_WIDTH] + log2_a_half)
        a_s[r0:r0 + ch, :] = a
        s = 1.0 - a * a
        h_s[r0:r0 + ch, :] = s * lax.rsqrt(jnp.maximum(s, 1e-30)) * (0.5 * th[:, RG_WIDTH:] + 0.5) * uc

    row = lax.broadcasted_iota(jnp.int32, (SCAN_ROWS, RG_WIDTH), 0)

    def block(j, h_prev):
        rows = pl.ds(pl.multiple_of(j * SCAN_ROWS, SCAN_ROWS), SCAN_ROWS)
        a = a_s[rows, :]
        b = h_s[rows, :]
        k = 1
        while k < SCAN_ROWS:
            keep = row >= k
            b = jnp.where(keep, a * pltpu.roll(b, k, 0) + b, b)
            a = jnp.where(keep, a * pltpu.roll(a, k, 0), a)
            k *= 2
        h = a * h_prev + b
        h_s[rows, :] = h
        return h[SCAN_ROWS - 1:SCAN_ROWS, :]

    lax.fori_loop(0, seq // SCAN_ROWS, block, jnp.zeros((1, RG_WIDTH), F32), unroll=SCAN_UNROLL)

    for c in range(seq // ch):
        r0 = c * ch
        y = _gelu_tanh(gate_ref[0, r0:r0 + ch, :].astype(F32)) * h_s[r0:r0 + ch, :]
        y_ref[0, r0:r0 + ch, :] = _rms(y, og_ref[...]).astype(BF16)


def _rglru(u3, gate3, cw, cb, wg, bg, lam, og):
    bsz, seq, _ = u3.shape
    full = lambda a: pl.BlockSpec(a.shape, lambda b: (0,) * a.ndim)
    blk = pl.BlockSpec((1, seq, RG_WIDTH), lambda b: (b, 0, 0))
    return pl.pallas_call(
        _rglru_kernel,
        grid=(bsz,),
        in_specs=[blk, blk] + [full(a) for a in (cw, cb, wg, bg, lam, og)],
        out_specs=blk,
        out_shape=jax.ShapeDtypeStruct((bsz, seq, RG_WIDTH), BF16),
        scratch_shapes=[pltpu.VMEM((seq + 8, RG_WIDTH), F32), pltpu.VMEM((seq, RG_WIDTH), F32),
                        pltpu.VMEM((seq, RG_WIDTH), F32)],
        compiler_params=_cparams(1),
    )(u3, gate3, cw, cb, wg, bg, lam, og)


def _compress_kernel(kx_ref, vx_ref, w1k_ref, w2k_ref, pk_ref, w1v_ref, w2v_ref, pv_ref, gk_ref, ones_ref,
                     ko_ref, vo_ref):
    n_chunk = kx_ref.shape[1]
    half = NSA_KV * CMP_HIDDEN

    def mlp(x_ref, w1_ref, w2_ref, p_ref):
        ab = _dot(x_ref[0], w1_ref[...])
        pos = _dot(p_ref[...], w1_ref[...])
        hid = ab[:, :half] + pltpu.roll(ab[:, half:], n_chunk - 1, 0) + (pos[0:1, :half] + pos[1:2, half:])
        return _dot(_gelu_tanh(hid).astype(BF16), w2_ref[...])

    kc = mlp(kx_ref, w1k_ref, w2k_ref, pk_ref)
    ko_ref[0] = _group_rms(kc, ones_ref[...], gk_ref[...]).astype(BF16)
    vo_ref[0] = mlp(vx_ref, w1v_ref, w2v_ref, pv_ref).astype(BF16)


def _compress(kx, vx, w1k, w2k, pk, w1v, w2v, pv, gk, ones_kv):
    bsz, n_chunk, width = kx.shape
    full = lambda a: pl.BlockSpec(a.shape, lambda b: (0,) * a.ndim)
    xin = pl.BlockSpec((1, n_chunk, width), lambda b: (b, 0, 0))
    out = pl.BlockSpec((1, n_chunk, KV_W), lambda b: (b, 0, 0))
    return pl.pallas_call(
        _compress_kernel,
        grid=(bsz,),
        in_specs=[xin, xin] + [full(a) for a in (w1k, w2k, pk, w1v, w2v, pv, gk, ones_kv)],
        out_specs=[out, out],
        out_shape=[jax.ShapeDtypeStruct((bsz, n_chunk, KV_W), BF16)] * 2,
        compiler_params=_cparams(1),
    )(kx, vx, w1k, w2k, pk, w1v, w2v, pv, gk, ones_kv)


def _nsa_kernel(q_ref, gates_ref, kcmp_ref, vcmp_ref, ksx_ref, vsp_ref, kwp_ref, vwp_ref, ovt_ref,
                bc_ref, bw_ref, bs_ref, bf_ref, og_ref, y_ref):
    i = pl.program_id(1)
    t0 = pl.multiple_of(i * TQ, TQ)
    n_blk = ovt_ref.shape[0]
    lane = lax.broadcasted_iota(jnp.int32, (TQ, LANES), 1)
    lo_half = lane < HEAD_DIM
    n_batch = q_ref.shape[0]

    def with_ones(v):
        return jnp.concatenate([v, jnp.ones_like(v)], axis=1)

    def near_part(bb):
        pieces = []
        for p in range(NSA_HPG):
            qs = q_ref[bb, :, p * LANES:(p + 1) * LANES]
            zero = jnp.zeros_like(qs)
            pieces += [jnp.where(lo_half, qs, zero), jnp.where(lo_half, zero, qs)]
        q8 = jnp.concatenate(pieces, axis=0)

        bc = bc_ref[0]
        lc = _dot_nt(q8, kcmp_ref[bb]) + bc
        ec = jnp.where(bc > MASKED_BELOW, jnp.exp2(lc - jnp.max(lc, axis=-1, keepdims=True)), 0.0)
        sc = jnp.sum(ec, axis=-1, keepdims=True)
        pc = ec / jnp.where(sc > 0.0, sc, 1.0)
        o_c = _dot(pc.astype(BF16), vcmp_ref[bb])

        blocks = [pc[r * TQ:(r + 1) * TQ] for r in range(NSA_HPG * NSA_KV)]
        pcs = jnp.concatenate([sum(blocks[g::NSA_KV]) for g in range(NSA_KV)], axis=0)
        pcs_hi = pcs.astype(BF16)
        pcs_lo = (pcs - pcs_hi.astype(F32)).astype(BF16)
        imp = _dot_nt(ovt_ref[...], pcs_hi) + _dot_nt(ovt_ref[...], pcs_lo)
        blk = lax.broadcasted_iota(jnp.int32, imp.shape, 0)
        forced = (blk == 0) | (blk == i) | (blk == i - 1)
        score = jnp.where(forced, SEL_FORCE, jnp.where(blk > i, -3e38, imp))
        rank = jnp.zeros(imp.shape, F32)
        for m in range(n_blk):
            row = score[m:m + 1, :]
            rank = rank + jnp.where(blk > m, jnp.where(row >= score, 1.0, 0.0), jnp.where(row > score, 1.0, 0.0))
        unsel = jnp.where(rank < N_SEL, 0.0, 1.0)
        unsel_far = jnp.where(blk >= i - WINDOW // SEL_L, 1.0, unsel)
        pad = jnp.zeros((LANES - 2 * n_blk, imp.shape[1]), F32)
        u_t = jnp.concatenate([unsel, unsel_far, pad], axis=0).T
        u_lane = lax.broadcasted_iota(jnp.int32, u_t.shape, 1)
        u_near = jnp.where(u_lane < n_blk, u_t, 0.0).astype(BF16)
        u_far = jnp.where(u_lane >= n_blk, u_t, 0.0).astype(BF16)
        qx_near = jnp.concatenate([q8, jnp.concatenate([u_near] * NSA_HPG, axis=0)], axis=1)
        qx_far = jnp.concatenate([q8, jnp.concatenate([u_far] * NSA_HPG, axis=0)], axis=1)

        lw = _dot_nt(q8, kwp_ref[bb, pl.ds(t0, NEAR), :]) + bw_ref[0]
        ew = jnp.exp2(lw - jnp.max(lw, axis=-1, keepdims=True))
        ow2 = _dot(ew.astype(BF16), with_ones(vwp_ref[bb, pl.ds(t0, NEAR), :]))
        o_w = ow2[:, :LANES] / ow2[:, LANES:]

        ls = _dot_nt(qx_near, ksx_ref[bb, pl.ds(t0, NEAR), :]) + bs_ref[0]
        m1 = jnp.max(ls, axis=-1, keepdims=True)
        e1 = jnp.exp2(ls - m1)
        acc1 = _dot(e1.astype(BF16), with_ones(vsp_ref[bb, pl.ds(t0, NEAR), :]))
        return o_c, o_w, qx_far, (m1, acc1)

    near = [near_part(bb) for bb in range(n_batch)]
    bfar = bf_ref[...]

    def far_step(kf, carry):
        base = pl.multiple_of(WINDOW + kf * FAR_TK, FAR_TK)
        new = []
        for bb in range(n_batch):
            m, acc = carry[bb]
            lf = _dot_nt(near[bb][2], ksx_ref[bb, pl.ds(base, FAR_TK), :]) + bfar
            m_new = jnp.maximum(m, jnp.max(lf, axis=-1, keepdims=True))
            alpha = jnp.exp2(m - m_new)
            e = jnp.exp2(lf - m_new)
            new.append((m_new, alpha * acc + _dot(e.astype(BF16), with_ones(vsp_ref[bb, pl.ds(base, FAR_TK), :]))))
        return tuple(new)

    n_far = (jnp.maximum(t0 - WINDOW, 0) + FAR_TK - 1) // FAR_TK
    far = lax.fori_loop(0, n_far, far_step, tuple(part[3] for part in near))

    for bb in range(n_batch):
        o_c, o_w = near[bb][0], near[bb][1]
        _, acc_s = far[bb]
        o_s = acc_s[:, :LANES] / acc_s[:, LANES:]
        gates = gates_ref[bb]

        def gate_col(j):
            cols = [gates[:, (g * NSA_HPG + p) * 3 + j:(g * NSA_HPG + p) * 3 + j + 1]
                    for p in range(NSA_HPG) for g in range(NSA_KV)]
            return jnp.concatenate(cols, axis=0)

        out = gate_col(0) * o_c + gate_col(1) * o_s + gate_col(2) * o_w
        slabs = [jnp.where(lo_half, out[(2 * p) * TQ:(2 * p + 1) * TQ], out[(2 * p + 1) * TQ:(2 * p + 2) * TQ])
                 for p in range(NSA_HPG)]
        y_ref[bb] = _rms(jnp.concatenate(slabs, axis=-1), og_ref[...]).astype(BF16)


def _nsa(q3, gates3, kcmp, vcmp, ksx, vsp, kwp, vwp, ovt, bias_c, bias_w, bias_s, bias_far, og):
    bsz, seq, _ = q3.shape
    n_chunk = kcmp.shape[1]
    n_var = bias_w.shape[0] - 1
    nb = NSA_NB if bsz % NSA_NB == 0 else 1
    full = lambda a: pl.BlockSpec(a.shape, lambda b, i: (0,) * a.ndim)
    per_b = lambda a: pl.BlockSpec((nb,) + a.shape[1:], lambda b, i: (b,) + (0,) * (a.ndim - 1))
    near = pl.BlockSpec((1, QROWS2, NEAR), lambda b, i: (jnp.minimum(i, n_var), 0, 0))
    return pl.pallas_call(
        _nsa_kernel,
        grid=(bsz // nb, seq // TQ),
        in_specs=[pl.BlockSpec((nb, TQ, NSA_WIDTH), lambda b, i: (b, i, 0)),
                  pl.BlockSpec((nb, TQ, LANES), lambda b, i: (b, i, 0)),
                  per_b(kcmp), per_b(vcmp), per_b(ksx), per_b(vsp), per_b(kwp), per_b(vwp),
                  full(ovt),
                  pl.BlockSpec((1, QROWS2, n_chunk), lambda b, i: (i, 0, 0)),
                  near, near, full(bias_far), full(og)],
        out_specs=pl.BlockSpec((nb, TQ, NSA_WIDTH), lambda b, i: (b, i, 0)),
        out_shape=jax.ShapeDtypeStruct((bsz, seq, NSA_WIDTH), BF16),
        compiler_params=_cparams(2),
    )(q3, gates3, kcmp, vcmp, ksx, vsp, kwp, vwp, ovt, bias_c, bias_w, bias_s, bias_far, og)


def _memkv_kernel(mem_ref, g_ref, wkv_ref, gk_ref, k_ref, v_ref):
    mn = _rms(mem_ref[0], g_ref[...]).astype(BF16)
    kv = _dot(mn, wkv_ref[...])
    for h in range(X_HEADS):
        sl = slice(h * X_HEAD_DIM, (h + 1) * X_HEAD_DIM)
        k_ref[0, :, sl] = _rms(kv[:, sl], gk_ref[...]).astype(BF16)
    v_ref[0] = kv[:, D_MODEL:].astype(BF16)


def _memkv(mem, g, wkv, gk):
    bsz, mlen, _ = mem.shape
    full = lambda a: pl.BlockSpec(a.shape, lambda b: (0,) * a.ndim)
    blk = pl.BlockSpec((1, mlen, D_MODEL), lambda b: (b, 0, 0))
    return pl.pallas_call(
        _memkv_kernel,
        grid=(bsz,),
        in_specs=[blk, full(g), full(wkv), full(gk)],
        out_specs=[blk, blk],
        out_shape=[jax.ShapeDtypeStruct((bsz, mlen, D_MODEL), BF16)] * 2,
        compiler_params=_cparams(1),
    )(mem, g, wkv, gk)


def _mid_kernel(x_ref, yrg_ref, ynsa_ref, woa_ref, wob_ref, gx_ref, wq_ref, gq_ref, k_ref, v_ref, wo_ref,
                gm_ref, wrh_ref, br_ref, h_ref, xt_ref, rw_ref, ri_ref, cnt_ref):
    h1 = x_ref[0] + _dot(yrg_ref[0], woa_ref[...]) + _dot(ynsa_ref[0], wob_ref[...])

    q = _dot(_rms(h1, gx_ref[...]).astype(BF16), wq_ref[...])
    heads = []
    for h in range(X_HEADS):
        sl = slice(h * X_HEAD_DIM, (h + 1) * X_HEAD_DIM)
        qh = _rms(q[:, sl], gq_ref[...]).astype(BF16)
        lg = _dot_nt(qh, k_ref[0, :, sl])
        e = jnp.exp2(lg - jnp.max(lg, axis=-1, keepdims=True))
        heads.append(_dot(e.astype(BF16), v_ref[0, :, sl]) / jnp.sum(e, axis=-1, keepdims=True))
    h2 = h1 + _dot(jnp.concatenate(heads, axis=-1).astype(BF16), wo_ref[...])
    h_ref[0] = h2

    xt = _rms(h2, gm_ref[...])
    _store_row_tiles(xt_ref, xt)
    xt_hi = xt.astype(BF16)
    xt_lo = (xt - xt_hi.astype(F32)).astype(BF16)
    hi2 = _dot(xt_hi, wrh_ref[...])
    lg = hi2[:, :LANES] + hi2[:, LANES:] + _dot(xt_lo, wrh_ref[:, :LANES]) + br_ref[...]
    lane = lax.broadcasted_iota(jnp.int32, lg.shape, 1)
    lane_f = lane.astype(F32)
    first_of = lambda hit: jnp.min(jnp.where(hit, lane_f, 1e9), axis=-1, keepdims=True)
    glog = jnp.where(lane < N_GROUPS, lg, -3e38)
    gmax = jnp.max(glog, axis=-1, keepdims=True)
    gsel = first_of(glog == gmax)
    p_g = 1.0 / jnp.sum(jnp.exp(glog - gmax), axis=-1, keepdims=True)
    lo = N_GROUPS + EXP_PER_GROUP * gsel
    el = jnp.where((lane_f >= lo) & (lane_f < lo + EXP_PER_GROUP), lg, -3e38)
    m_a = jnp.max(el, axis=-1, keepdims=True)
    i_a = first_of(el == m_a)
    el2 = jnp.where(lane_f == i_a, -3e38, el)
    m_b = jnp.max(el2, axis=-1, keepdims=True)
    i_b = first_of(el2 == m_b)
    r = jnp.exp(m_b - m_a)
    w_a = p_g / (1.0 + r)
    w_b = p_g * r / (1.0 + r)
    e_a = i_a - N_GROUPS
    e_b = i_b - N_GROUPS
    rw_ref[0] = jnp.where(lane == 0, w_a, jnp.where(lane == 1, w_b, 0.0))
    ri_ref[0] = jnp.where(lane == 0, e_a, jnp.where(lane == 1, e_b, 0.0)).astype(jnp.int32)

    @pl.when((pl.program_id(0) == 0) & (pl.program_id(1) == 0))
    def _():
        cnt_ref[...] = jnp.zeros_like(cnt_ref)

    hot = jnp.where((lane_f == e_a) | (lane_f == e_b), 1.0, 0.0)
    cnt_ref[...] += jnp.sum(hot, axis=0, keepdims=True)


def _mid(x, yrg, ynsa, woa, wob, gx, wq, gq, kx, vx, wo, gm, wrh, br):
    bsz, seq, _ = x.shape
    tm = min(TM_MID, seq)
    mlen = kx.shape[1]
    n_i = seq // tm
    full = lambda a: pl.BlockSpec(a.shape, lambda b, i: (0,) * a.ndim)
    tok = lambda w: pl.BlockSpec((1, tm, w), lambda b, i: (b, i, 0))
    memb = pl.BlockSpec((1, mlen, D_MODEL), lambda b, i: (b, 0, 0))
    xt_spec = pl.BlockSpec((tm * ROW_TILE, LANES), lambda b, i: (b * n_i + i, 0))
    return pl.pallas_call(
        _mid_kernel,
        grid=(bsz, seq // tm),
        in_specs=[tok(D_MODEL), tok(RG_WIDTH), tok(NSA_WIDTH), full(woa), full(wob), full(gx), full(wq), full(gq),
                  memb, memb, full(wo), full(gm), full(wrh), full(br)],
        out_specs=[tok(D_MODEL), xt_spec, tok(LANES), tok(LANES), pl.BlockSpec((1, LANES), lambda b, i: (0, 0))],
        out_shape=[jax.ShapeDtypeStruct((bsz, seq, D_MODEL), F32),
                   jax.ShapeDtypeStruct((bsz * seq * ROW_TILE, LANES), F32),
                   jax.ShapeDtypeStruct((bsz, seq, LANES), F32), jax.ShapeDtypeStruct((bsz, seq, LANES), jnp.int32),
                   jax.ShapeDtypeStruct((1, LANES), F32)],
        compiler_params=_cparams(2),
    )(x, yrg, ynsa, woa, wob, gx, wq, gq, kx, vx, wo, gm, wrh, br)


def _dest_kernel(ri_ref, pstart_ref, earlier_ref, dest_ref, run_ref):
    @pl.when(pl.program_id(0) == 0)
    def _():
        run_ref[...] = jnp.zeros_like(run_ref)

    ri = ri_ref[...]
    lane = lax.broadcasted_iota(jnp.int32, ri.shape, 1)
    e_a = ri[:, 0:1]
    e_b = ri[:, 1:2]
    hot_a = lane == e_a
    hot_b = lane == e_b
    hot = jnp.where(hot_a | hot_b, 1.0, 0.0)
    base = _dot(earlier_ref[...], hot.astype(BF16)) + run_ref[...] + pstart_ref[...]
    d_a = jnp.sum(jnp.where(hot_a, base, 0.0), axis=-1, keepdims=True)
    d_b = jnp.sum(jnp.where(hot_b, base, 0.0), axis=-1, keepdims=True)
    dest_ref[...] = jnp.where(lane == 0, d_a, jnp.where(lane == 1, d_b, 0.0)).astype(jnp.int32)
    run_ref[...] += jnp.sum(hot, axis=0, keepdims=True)


def _dest(ri2, pstart):
    n_tok = ri2.shape[0]
    tm = min(TM_DEST, n_tok)
    earlier = jnp.asarray(np.tril(np.ones((tm, tm)), -1), BF16)
    return pl.pallas_call(
        _dest_kernel,
        grid=(n_tok // tm,),
        in_specs=[pl.BlockSpec((tm, LANES), lambda i: (i, 0)), pl.BlockSpec((1, LANES), lambda i: (0, 0)),
                  pl.BlockSpec((tm, tm), lambda i: (0, 0))],
        out_specs=pl.BlockSpec((tm, LANES), lambda i: (i, 0)),
        out_shape=jax.ShapeDtypeStruct((n_tok, LANES), jnp.int32),
        scratch_shapes=[pltpu.VMEM((1, LANES), F32)],
        compiler_params=_cparams(1),
    )(ri2, pstart, earlier)


def _store_row_tiles(ref, val):
    n = val.shape[0]
    for c in range(ROW_TILE):
        ref[pl.ds(c, n, stride=ROW_TILE), :] = val[:, c * LANES:(c + 1) * LANES]


def _load_row_tiles(ref, n):
    return [ref[pl.ds(c, n, stride=ROW_TILE), :] for c in range(ROW_TILE)]


def _token_rows(ref, t):
    return ref.at[pl.ds(pl.multiple_of(t * ROW_TILE, ROW_TILE), ROW_TILE), :]


def _dispatch_kernel(cnt_ref, pstart_ref, da_ref, db_ref, xt_ref, xs_ref, zrow, sem, zsem):
    tm = da_ref.shape[2]

    @pl.when(pl.program_id(0) == 0)
    def _():
        zrow[...] = jnp.zeros_like(zrow)

        def per_expert(e, c):
            used = cnt_ref[e]
            padded = (used + MOE_TB - 1) // MOE_TB * MOE_TB
            base = pstart_ref[e]

            def fill(r, c2):
                pltpu.make_async_copy(zrow, _token_rows(xs_ref, base + r), zsem).start()
                return c2

            def fill_done(r, c2):
                pltpu.make_async_copy(zrow, _token_rows(xs_ref, 0), zsem).wait()
                return c2

            lax.fori_loop(used, padded, fill, 0)
            lax.fori_loop(used, padded, fill_done, 0)
            return c

        lax.fori_loop(0, N_EXPERTS, per_expert, 0)

        last = N_EXPERTS - 1
        first_unused = (pstart_ref[last] + (cnt_ref[last] + MOE_TB - 1) // MOE_TB * MOE_TB) // MOE_TB

        def per_block(j, c):
            def fill(r, c2):
                pltpu.make_async_copy(zrow, _token_rows(xs_ref, j * MOE_TB + r), zsem).start()
                return c2

            def fill_done(r, c2):
                pltpu.make_async_copy(zrow, _token_rows(xs_ref, 0), zsem).wait()
                return c2

            lax.fori_loop(0, MOE_TB, fill, 0, unroll=DMA_UNROLL)
            lax.fori_loop(0, MOE_TB, fill_done, 0, unroll=DMA_UNROLL)
            return c

        lax.fori_loop(first_unused, xs_ref.shape[0] // (ROW_TILE * MOE_TB), per_block, 0)

    def issue(t, c):
        pltpu.make_async_copy(_token_rows(xt_ref, t), _token_rows(xs_ref, da_ref[0, 0, t]), sem).start(priority=0)
        pltpu.make_async_copy(_token_rows(xt_ref, t), _token_rows(xs_ref, db_ref[0, 0, t]), sem).start(priority=1)
        return c

    lax.fori_loop(0, tm, issue, 0, unroll=DMA_UNROLL)

    def drain(t, c):
        pltpu.make_async_copy(_token_rows(xt_ref, 0), _token_rows(xs_ref, 0), sem).wait()
        pltpu.make_async_copy(_token_rows(xt_ref, 0), _token_rows(xs_ref, 0), sem).wait()
        return c

    lax.fori_loop(0, tm, drain, 0, unroll=DMA_UNROLL)


def _dispatch(cnt, pstart, da, db, xt_rows, n_pad):
    n_tiles, _, tm = da.shape
    smem = pl.BlockSpec((1, 1, tm), lambda i, c, p: (i, 0, 0), memory_space=pltpu.SMEM)
    grid_spec = pltpu.PrefetchScalarGridSpec(
        num_scalar_prefetch=2,
        grid=(n_tiles,),
        in_specs=[smem, smem, pl.BlockSpec((tm * ROW_TILE, LANES), lambda i, c, p: (i, 0))],
        out_specs=pl.BlockSpec(memory_space=pl.ANY),
        scratch_shapes=[pltpu.VMEM((ROW_TILE, LANES), F32), pltpu.SemaphoreType.DMA(()),
                        pltpu.SemaphoreType.DMA(())],
    )
    return pl.pallas_call(
        _dispatch_kernel,
        grid_spec=grid_spec,
        out_shape=jax.ShapeDtypeStruct((n_pad * ROW_TILE, LANES), F32),
        compiler_params=pltpu.CompilerParams(dimension_semantics=("arbitrary",), has_side_effects=True,
                                             vmem_limit_bytes=VMEM_LIMIT),
    )(cnt, pstart, da, db, xt_rows)


def _ffn_kernel(bexp_ref, nused_ref, xs_ref, w1_ref, w3_ref, w2_ref, ys_ref, w1b, w3b, w2b):
    j = pl.program_id(0)
    used = j < nused_ref[0]

    @pl.when(used & ((j == 0) | (bexp_ref[j] != bexp_ref[jnp.maximum(j - 1, 0)])))
    def _():
        w1b[...] = w1_ref[0].astype(BF16)
        w3b[...] = w3_ref[0].astype(BF16)
        w2b[...] = w2_ref[0].astype(BF16)

    @pl.when(used)
    def _():
        xb = jnp.concatenate(_load_row_tiles(xs_ref, MOE_TB), axis=-1).astype(BF16)
        a = _dot(xb, w1b[...])
        h = a * _sigmoid(a) * _dot(xb, w3b[...])
        _store_row_tiles(ys_ref, _dot(h.astype(BF16), w2b[...]))

    @pl.when(j >= nused_ref[0])
    def _():
        ys_ref[...] = jnp.zeros_like(ys_ref)


def _ffn(blk_exp, n_used, xs, w1, w3, w2):
    n_blocks = xs.shape[0] // (MOE_TB * ROW_TILE)
    rows = pl.BlockSpec((MOE_TB * ROW_TILE, LANES), lambda j, be, nu: (j, 0))
    used_rows = pl.BlockSpec((MOE_TB * ROW_TILE, LANES), lambda j, be, nu: (jnp.minimum(j, nu[0] - 1), 0))
    grid_spec = pltpu.PrefetchScalarGridSpec(
        num_scalar_prefetch=2,
        grid=(n_blocks,),
        in_specs=[used_rows,
                  pl.BlockSpec((1, D_MODEL, D_EXPERT), lambda j, be, nu: (be[j], 0, 0)),
                  pl.BlockSpec((1, D_MODEL, D_EXPERT), lambda j, be, nu: (be[j], 0, 0)),
                  pl.BlockSpec((1, D_EXPERT, D_MODEL), lambda j, be, nu: (be[j], 0, 0))],
        out_specs=rows,
        scratch_shapes=[pltpu.VMEM((D_MODEL, D_EXPERT), BF16), pltpu.VMEM((D_MODEL, D_EXPERT), BF16),
                        pltpu.VMEM((D_EXPERT, D_MODEL), BF16)],
    )
    return pl.pallas_call(
        _ffn_kernel,
        grid_spec=grid_spec,
        out_shape=jax.ShapeDtypeStruct(xs.shape, F32),
        compiler_params=_cparams(1),
    )(blk_exp, n_used, xs, w1, w3, w2)


def _combine_kernel(da_ref, db_ref, da_next_ref, db_next_ref, h_ref, rw_ref, ys_ref, o_ref, ya, yb, sems):
    tm = da_ref.shape[2]
    i = pl.program_id(0)
    slot = i % 2

    def start_gather(a_ref, b_ref, s):
        def issue(t, c):
            pltpu.make_async_copy(_token_rows(ys_ref, a_ref[0, 0, t]), _token_rows(ya.at[s], t),
                                  sems.at[s]).start(priority=0)
            pltpu.make_async_copy(_token_rows(ys_ref, b_ref[0, 0, t]), _token_rows(yb.at[s], t),
                                  sems.at[s]).start(priority=1)
            return c

        lax.fori_loop(0, tm, issue, 0, unroll=DMA_UNROLL)

    @pl.when(i == 0)
    def _():
        start_gather(da_ref, db_ref, slot)

    @pl.when(i + 1 < pl.num_programs(0))
    def _():
        start_gather(da_next_ref, db_next_ref, 1 - slot)

    def drain(t, c):
        pltpu.make_async_copy(_token_rows(ys_ref, 0), _token_rows(ya.at[slot], 0), sems.at[slot]).wait()
        pltpu.make_async_copy(_token_rows(ys_ref, 0), _token_rows(yb.at[slot], 0), sems.at[slot]).wait()
        return c

    lax.fori_loop(0, tm, drain, 0, unroll=DMA_UNROLL)
    rw = rw_ref[...]
    mix = [rw[:, 0:1] * a + rw[:, 1:2] * b
           for a, b in zip(_load_row_tiles(ya.at[slot], tm), _load_row_tiles(yb.at[slot], tm))]
    o_ref[...] = h_ref[...] + jnp.concatenate(mix, axis=-1)


def _combine(da, db, h2, rw, ys):
    n_tiles, _, tm = da.shape
    n_tok = h2.shape[0]
    smem = pl.BlockSpec((1, 1, tm), lambda i: (i, 0, 0), memory_space=pltpu.SMEM)
    smem_next = pl.BlockSpec((1, 1, tm), lambda i: (jnp.minimum(i + 1, n_tiles - 1), 0, 0), memory_space=pltpu.SMEM)
    row = lambda w: pl.BlockSpec((tm, w), lambda i: (i, 0))
    slots = pltpu.VMEM((2, tm * ROW_TILE, LANES), F32)
    return pl.pallas_call(
        _combine_kernel,
        grid=(n_tiles,),
        in_specs=[smem, smem, smem_next, smem_next, row(D_MODEL), row(LANES), pl.BlockSpec(memory_space=pl.ANY)],
        out_specs=row(D_MODEL),
        out_shape=jax.ShapeDtypeStruct((n_tok, D_MODEL), F32),
        scratch_shapes=[slots, slots, pltpu.SemaphoreType.DMA((2,))],
        compiler_params=_cparams(1),
    )(da, db, da, db, h2, rw, ys)


def _rel_bucket_np(dist):
    n = np.maximum(dist, 0)
    max_exact = NUM_BUCKETS // 2
    nf = np.maximum(n, 1).astype(np.float32)
    large = max_exact + (np.log(nf / max_exact) / math.log(MAX_DIST / max_exact)
                         * (NUM_BUCKETS - max_exact)).astype(np.int32)
    large = np.minimum(large, NUM_BUCKETS - 1)
    return np.where(n < max_exact, n, large).astype(np.int32)


def _toeplitz(vec, rows):
    width = vec.shape[-1] - 1
    flat = jnp.tile(vec, (1,) * (vec.ndim - 1) + (rows,))[..., :rows * width]
    return flat.reshape(vec.shape[:-1] + (rows, width))


def _bias_tables(rel_bias, seq):
    n_chunk = seq // CMP_STRIDE
    n_tiles = seq // TQ
    table = rel_bias.T.astype(F32)

    wide = NEAR + TQ
    k = np.arange(wide + 1)
    dw = np.where(k < NEAR, WINDOW - k, WINDOW + wide + 1 - k)
    used = (k < NEAR) | (k > wide + 1 - TQ)
    vals = table[:, _rel_bucket_np(dw)]

    n_var = WINDOW // TQ
    first_key = WINDOW - TQ * np.arange(n_var + 1)[:, None, None]
    in_seq = np.arange(NEAR)[None, None, :] >= first_key

    def near_tile(valid):
        t = _toeplitz(jnp.where(valid[None, :], vals, NEG_INF), TQ)[:, :, :NEAR]
        t = t.reshape(NSA_KV, NSA_HPG, TQ, NEAR).transpose(1, 0, 2, 3).reshape(1, QROWS2, NEAR)
        return jnp.where(in_seq, t, NEG_INF)

    bias_w = near_tile(used & (dw >= 0) & (dw < WINDOW))
    bias_s = near_tile(used & (dw >= 0))
    bias_far = table[:, NUM_BUCKETS - 1].reshape(NSA_KV, NSA_HPG, 1).transpose(1, 0, 2)
    bias_far = jnp.broadcast_to(bias_far, (NSA_HPG, NSA_KV, TQ)).reshape(QROWS2, 1)

    r = np.arange(CMP_STRIDE)[:, None]
    k = np.arange(2 * n_chunk + 1)[None, :]
    lag = 2 * n_chunk + 1 - k
    valid = (k > n_chunk + 1) & (CMP_STRIDE * lag + r >= CMP_L - 1)
    vals = table[:, _rel_bucket_np(CMP_STRIDE * lag + r - CMP_L // 2)]
    full = _toeplitz(jnp.where(valid[None], vals, NEG_INF), n_chunk)[..., :n_chunk]
    full = jnp.where(np.arange(n_chunk) < n_chunk - 1, full, NEG_INF)
    a4 = TQ // CMP_STRIDE
    full = full.reshape(NSA_KV, NSA_HPG, CMP_STRIDE, n_tiles, a4, n_chunk).transpose(3, 1, 0, 4, 2, 5)
    bias_c = full.reshape(n_tiles, QROWS2, n_chunk)
    return tuple(LOG2E * t for t in (bias_c, bias_w, bias_s, bias_far))


def _selection_tables(seq):
    n_chunk = seq // CMP_STRIDE
    n_blk = seq // SEL_L
    c = np.arange(n_chunk)
    n = np.arange(n_blk)
    start = c * CMP_STRIDE
    overlap_t = ((start[None, :] <= n[:, None] * SEL_L + SEL_L - 1) & (start[None, :] + CMP_L - 1 >= n[:, None] * SEL_L)
                 & (c < n_chunk - 1)[None, :])
    pos = np.arange(seq + WINDOW) - WINDOW
    lane_blk = np.arange(LANES) % n_blk
    hit = (pos[:, None] >= 0) & (pos[:, None] // SEL_L == lane_blk[None, :]) & (np.arange(LANES) < 2 * n_blk)[None, :]
    return jnp.asarray(overlap_t, BF16), jnp.asarray(np.where(hit, -UNSEL_PENALTY, 0.0), BF16)


def _block_ones(width, group):
    idx = np.arange(width) // group
    return jnp.asarray((idx[:, None] == idx[None, :]) / group, BF16)


def _block_diag(w):
    nb, n, m = w.shape
    eye = jnp.eye(nb, dtype=w.dtype)
    return jnp.einsum('hij,hg->higj', w, eye).reshape(nb * n, nb * m)


def _compress_weights(w1, w2, pos):
    half_l = CMP_L // 2
    parts = []
    for half in range(2):
        wh = w1[half * half_l * HEAD_DIM:(half + 1) * half_l * HEAD_DIM].reshape(half_l, HEAD_DIM, CMP_HIDDEN)
        z = jnp.zeros_like(wh)
        for g in range(NSA_KV):
            grp = [wh if gg == g else z for gg in range(NSA_KV)]
            parts.append(jnp.stack(grp, axis=1).reshape(half_l * KV_W, CMP_HIDDEN))
    w1cat = jnp.concatenate(parts, axis=1).astype(BF16)
    w2bd = _block_diag(jnp.stack([w2] * NSA_KV)).astype(BF16)
    prow = [jnp.tile(pos[half * half_l:(half + 1) * half_l][:, None, :], (1, NSA_KV, 1)).reshape(-1)
            for half in range(2)]
    pmat = jnp.zeros((8, half_l * KV_W), F32).at[0].set(prow[0]).at[1].set(prow[1]).astype(BF16)
    return w1cat, w2bd, pmat


def kernel(x, mem, rel_bias, norm_mix, w_in, rg_conv_w, rg_conv_b, rg_w_r, rg_b_r, rg_w_i, rg_b_i, rg_lambda, nsa_g_q, nsa_g_kc, nsa_g_ks, nsa_g_kw, cmp_pos_k, cmp_pos_v, cmp_k_w1, cmp_k_w2, cmp_v_w1, cmp_v_w2, out_g_rg, out_g_nsa, w_out, norm_x, norm_mem, xa_w_q, xa_w_kv, xa_w_o, xa_g_q, xa_g_k, norm_moe, router_g_w, router_g_b, router_e_w, router_e_b, exp_w1, exp_w3, exp_w2):
    bsz, seq, _ = x.shape
    n_tok = bsz * seq
    assert seq % FAR_TK == 0 and 2 * (seq // SEL_L) <= LANES and norm_mix.shape[0] == 1
    l = 0
    row = lambda v: v.reshape(1, -1).astype(F32)

    perm = np.array([(half * NSA_HPG + p) * HEAD_DIM + d
                     for p in range(NSA_HPG) for half in range(NSA_KV) for d in range(HEAD_DIM)])
    offs = np.cumsum([0, RG_WIDTH, RG_WIDTH, NSA_WIDTH] + [KV_W] * 6)
    w = w_in[l]
    wrg = w[:, :offs[2]].astype(BF16)
    wq = w[:, offs[2]:offs[3]][:, perm].astype(BF16)
    wkv = w[:, offs[3]:offs[9]].astype(BF16)
    wgl = jnp.pad(w[:, offs[9]:], ((0, 0), (0, LANES - 3 * NSA_HEADS))).astype(BF16)
    ones64 = _block_ones(NSA_WIDTH, HEAD_DIM)
    gq = row(jnp.tile(nsa_g_q[l], NSA_HEADS) * (HEAD_DIM ** -0.5 * LOG2E))
    u, gate, q, kc, vc, ks, vs, kw, vw, gates = _inproj(
        x.reshape(n_tok, D_MODEL), row(norm_mix[l]), wrg, wq, wkv, wgl, gq,
        row(jnp.tile(nsa_g_ks[l], NSA_KV)), row(jnp.tile(nsa_g_kw[l], NSA_KV)), ones64)

    wg = (0.5 * jnp.concatenate([_block_diag(rg_w_r[l]), _block_diag(rg_w_i[l])], axis=1)).astype(BF16)
    bg = 0.5 * jnp.concatenate([rg_b_r[l], rg_b_i[l]]).reshape(1, -1)
    y_rg = _rglru(u.reshape(bsz, seq, RG_WIDTH), gate.reshape(bsz, seq, RG_WIDTH),
                  rg_conv_w[l].reshape(CONV_W, RG_WIDTH), row(rg_conv_b[l]), wg, bg, row(rg_lambda[l]),
                  row(out_g_rg[l]))

    n_chunk = seq // CMP_STRIDE
    w1k, w2k, pk = _compress_weights(cmp_k_w1[l], cmp_k_w2[l], cmp_pos_k[l])
    w1v, w2v, pv = _compress_weights(cmp_v_w1[l], cmp_v_w2[l], cmp_pos_v[l])
    kcmp, vcmp = _compress(kc.reshape(bsz, n_chunk, CMP_STRIDE * KV_W), vc.reshape(bsz, n_chunk, CMP_STRIDE * KV_W),
                           w1k, w2k, pk, w1v, w2v, pv, row(jnp.tile(nsa_g_kc[l], NSA_KV)),
                           ones64[:KV_W, :KV_W])
    padw = lambda t: jnp.pad(t.reshape(bsz, seq, KV_W), ((0, 0), (WINDOW, 0), (0, 0)))
    bias_c, bias_w, bias_s, bias_far = _bias_tables(rel_bias, seq)
    overlap_t, penalty = _selection_tables(seq)
    ksx = jnp.concatenate([padw(ks), jnp.broadcast_to(penalty, (bsz,) + penalty.shape)], axis=-1)
    y_nsa = _nsa(q.reshape(bsz, seq, NSA_WIDTH), gates.reshape(bsz, seq, LANES), kcmp, vcmp,
                 ksx, padw(vs), padw(kw), padw(vw), overlap_t, bias_c, bias_w, bias_s, bias_far,
                 row(out_g_nsa[l][perm]))

    kx, vx = _memkv(mem, row(norm_mem[l]), xa_w_kv[l].astype(BF16), row(xa_g_k[l]))
    wo_mix = w_out[l]
    wr = jnp.pad(jnp.concatenate([router_g_w[l], router_e_w[l]], axis=1),
                 ((0, 0), (0, LANES - N_GROUPS - N_EXPERTS)))
    wr_hi = wr.astype(BF16)
    br = jnp.pad(jnp.concatenate([router_g_b[l], router_e_b[l]]), (0, LANES - N_GROUPS - N_EXPERTS)).reshape(1, -1)
    h2, xt, rw, ri, counts = _mid(
        x, y_rg, y_nsa, wo_mix[:RG_WIDTH].astype(BF16), wo_mix[RG_WIDTH:][perm].astype(BF16), row(norm_x[l]),
        xa_w_q[l].astype(BF16), row(xa_g_q[l] * (X_HEAD_DIM ** -0.5 * LOG2E)), kx, vx, xa_w_o[l].astype(BF16),
        row(norm_moe[l]), jnp.concatenate([wr_hi, (wr - wr_hi.astype(F32)).astype(BF16)], axis=1), br)

    n_slots = 2 * n_tok
    n_blocks = n_slots // MOE_TB + N_EXPERTS
    n_pad = n_blocks * MOE_TB
    cnt = counts[0, :N_EXPERTS].astype(jnp.int32)
    pcnt = (cnt + MOE_TB - 1) // MOE_TB * MOE_TB
    pends = jnp.cumsum(pcnt)
    pstart = jnp.pad((pends - pcnt).astype(F32), (0, LANES - N_EXPERTS)).reshape(1, LANES)
    blk_exp = jnp.minimum(jnp.sum(pends[None, :] <= jnp.arange(n_blocks, dtype=jnp.int32)[:, None] * MOE_TB, axis=1),
                          N_EXPERTS - 1).astype(jnp.int32)
    n_used = (pends[-1:] // MOE_TB).astype(jnp.int32)
    dest = _dest(ri.reshape(n_tok, LANES), pstart)
    tmd = min(TM_DMA, n_tok)
    da = dest[:, 0].reshape(n_tok // tmd, 1, tmd)
    db = dest[:, 1].reshape(n_tok // tmd, 1, tmd)
    xs = _dispatch(cnt, (pends - pcnt).astype(jnp.int32), da, db, xt, n_pad)
    ys = _ffn(blk_exp, n_used, xs, exp_w1[l], exp_w3[l], exp_w2[l])
    out = _combine(da, db, h2.reshape(n_tok, D_MODEL), rw.reshape(n_tok, LANES), ys)
    return out.reshape(bsz, seq, D_MODEL)
```

```python
import math

import numpy as np
import jax
import jax.numpy as jnp
from jax import lax
from jax.experimental import pallas as pl
from jax.experimental.pallas import tpu as pltpu

F32 = jnp.float32
BF16 = jnp.bfloat16

D_MODEL = 1024
RG_WIDTH = 512
RG_BLOCKS = 8
RG_BLOCK = 64
CONV_W = 4
RG_C = 8.0
NSA_WIDTH = 512
NSA_HEADS = 8
HEAD_DIM = 64
NSA_KV = 2
NSA_HPG = 4
KV_W = 128
CMP_L = 32
CMP_STRIDE = 16
CMP_HIDDEN = 256
SEL_L = 64
N_SEL = 8
WINDOW = 512
NUM_BUCKETS = 32
MAX_DIST = 128
X_HEADS = 4
X_HEAD_DIM = 256
N_GROUPS = 4
EXP_PER_GROUP = 8
N_EXPERTS = 32
D_EXPERT = 512
EPS = 1e-6
LOG2E = 1.0 / math.log(2.0)
NEG_INF = -1e30
MASKED_BELOW = -1e29
SEL_FORCE = 1e9
LANES = 128

TQ = 64
NEAR = WINDOW + TQ
FAR_TK = 512
QROWS2 = NSA_HEADS * TQ
UNSEL_PENALTY = 2.0 ** 100
NSA_NB = 4

TM_PROJ = 1024
TM_MID = 1024
TM_DEST = 512
TM_DMA = 512
DMA_UNROLL = 8
MOE_TB = 512
ROW_TILE = D_MODEL // LANES
RG_CHUNK = 256
SCAN_ROWS = 8
SCAN_UNROLL = 4
VMEM_LIMIT = 56 * 1024 * 1024


def _cparams(n_axes):
    return pltpu.CompilerParams(dimension_semantics=("arbitrary",) * n_axes,
                                vmem_limit_bytes=VMEM_LIMIT)


def _dot(a, b):
    return jnp.dot(a, b, preferred_element_type=F32)


def _dot_nt(a, b):
    return lax.dot_general(a, b, (((1,), (1,)), ((), ())), preferred_element_type=F32)


def _gelu_tanh(x):
    c = math.sqrt(2.0 / math.pi)
    half = 0.5 * x
    return half + half * jnp.tanh(x * (c + (c * 0.044715) * (x * x)))


def _sigmoid(x):
    return 0.5 * jnp.tanh(0.5 * x) + 0.5


def _rms(x, g):
    return x * lax.rsqrt(jnp.mean(x * x, axis=-1, keepdims=True) + EPS) * g


def _group_rms(x, ones_blk, g):
    ms = _dot((x * x).astype(BF16), ones_blk)
    return x * lax.rsqrt(ms + EPS) * g


def _inproj_kernel(x_ref, g_ref, wrg_ref, wq_ref, wkv_ref, wgl_ref, gq_ref, gks_ref, gkw_ref, ones_ref,
                   u_ref, gate_ref, q_ref, kc_ref, vc_ref, ks_ref, vs_ref, kw_ref, vw_ref, gates_ref):
    xb = _rms(x_ref[...], g_ref[...]).astype(BF16)
    rg = _dot(xb, wrg_ref[...])
    u_ref[...] = rg[:, :RG_WIDTH].astype(BF16)
    gate_ref[...] = rg[:, RG_WIDTH:].astype(BF16)
    q = _dot(xb, wq_ref[...])
    q_ref[...] = _group_rms(q, ones_ref[...], gq_ref[...]).astype(BF16)
    kv = _dot(xb, wkv_ref[...])
    ones_kv = ones_ref[:KV_W, :KV_W]
    kc_ref[...] = kv[:, 0 * KV_W:1 * KV_W].astype(BF16)
    vc_ref[...] = kv[:, 1 * KV_W:2 * KV_W].astype(BF16)
    ks_ref[...] = _group_rms(kv[:, 2 * KV_W:3 * KV_W], ones_kv, gks_ref[...]).astype(BF16)
    vs_ref[...] = kv[:, 3 * KV_W:4 * KV_W].astype(BF16)
    kw_ref[...] = _group_rms(kv[:, 4 * KV_W:5 * KV_W], ones_kv, gkw_ref[...]).astype(BF16)
    vw_ref[...] = kv[:, 5 * KV_W:6 * KV_W].astype(BF16)
    gates_ref[...] = _sigmoid(_dot(xb, wgl_ref[...]))


def _inproj(x2, g, wrg, wq, wkv, wgl, gq, gks, gkw, ones_blk):
    n_tok = x2.shape[0]
    tm = min(TM_PROJ, n_tok)
    full = lambda a: pl.BlockSpec(a.shape, lambda i: (0,) * a.ndim)
    row = lambda w: pl.BlockSpec((tm, w), lambda i: (i, 0))
    outs = [(RG_WIDTH, BF16), (RG_WIDTH, BF16), (NSA_WIDTH, BF16)] + [(KV_W, BF16)] * 6 + [(LANES, F32)]
    return pl.pallas_call(
        _inproj_kernel,
        grid=(n_tok // tm,),
        in_specs=[row(D_MODEL)] + [full(a) for a in (g, wrg, wq, wkv, wgl, gq, gks, gkw, ones_blk)],
        out_specs=[row(w) for w, _ in outs],
        out_shape=[jax.ShapeDtypeStruct((n_tok, w), dt) for w, dt in outs],
        compiler_params=_cparams(1),
    )(x2, g, wrg, wq, wkv, wgl, gq, gks, gkw, ones_blk)


def _rglru_kernel(u_ref, gate_ref, cw_ref, cb_ref, wg_ref, bg_ref, lam_ref, og_ref, y_ref, upad, a_s, h_s):
    seq = u_ref.shape[1]
    upad[0:8, :] = jnp.zeros((8, RG_WIDTH), F32)
    upad[8:8 + seq, :] = u_ref[0].astype(F32)
    neg_lam = -lam_ref[...]
    softplus = jnp.maximum(neg_lam, 0.0) + jnp.log(1.0 + jnp.exp(-jnp.abs(neg_lam)))
    log2_a_half = (-0.5 * RG_C * LOG2E) * softplus
    ch = min(RG_CHUNK, seq)
    for c in range(seq // ch):
        r0 = c * ch
        uc = cb_ref[...]
        for k in range(CONV_W):
            off = 8 + r0 - (CONV_W - 1) + k
            uc = uc + cw_ref[k:k + 1, :] * upad[off:off + ch, :]
        th = jnp.tanh(_dot(uc.astype(BF16), wg_ref[...]) + bg_ref[...])
        a = jnp.exp2(log2_a_half * th[:, :RG_WIDTH] + log2_a_half)
        a_s[r0:r0 + ch, :] = a
        s = 1.0 - a * a
        h_s[r0:r0 + ch, :] = s * lax.rsqrt(jnp.maximum(s, 1e-30)) * (0.5 * th[:, RG_WIDTH:] + 0.5) * uc

    row = lax.broadcasted_iota(jnp.int32, (SCAN_ROWS, RG_WIDTH), 0)

    def block(j, h_prev):
        rows = pl.ds(pl.multiple_of(j * SCAN_ROWS, SCAN_ROWS), SCAN_ROWS)
        a = a_s[rows, :]
        b = h_s[rows, :]
        k = 1
        while k < SCAN_ROWS:
            keep = row >= k
            b = jnp.where(keep, a * pltpu.roll(b, k, 0) + b, b)
            a = jnp.where(keep, a * pltpu.roll(a, k, 0), a)
            k *= 2
        h = a * h_prev + b
        h_s[rows, :] = h
        return h[SCAN_ROWS - 1:SCAN_ROWS, :]

    lax.fori_loop(0, seq // SCAN_ROWS, block, jnp.zeros((1, RG_WIDTH), F32), unroll=SCAN_UNROLL)

    for c in range(seq // ch):
        r0 = c * ch
        y = _gelu_tanh(gate_ref[0, r0:r0 + ch, :].astype(F32)) * h_s[r0:r0 + ch, :]
        y_ref[0, r0:r0 + ch, :] = _rms(y, og_ref[...]).astype(BF16)


def _rglru(u3, gate3, cw, cb, wg, bg, lam, og):
    bsz, seq, _ = u3.shape
    full = lambda a: pl.BlockSpec(a.shape, lambda b: (0,) * a.ndim)
    blk = pl.BlockSpec((1, seq, RG_WIDTH), lambda b: (b, 0, 0))
    return pl.pallas_call(
        _rglru_kernel,
        grid=(bsz,),
        in_specs=[blk, blk] + [full(a) for a in (cw, cb, wg, bg, lam, og)],
        out_specs=blk,
        out_shape=jax.ShapeDtypeStruct((bsz, seq, RG_WIDTH), BF16),
        scratch_shapes=[pltpu.VMEM((seq + 8, RG_WIDTH), F32), pltpu.VMEM((seq, RG_WIDTH), F32),
                        pltpu.VMEM((seq, RG_WIDTH), F32)],
        compiler_params=_cparams(1),
    )(u3, gate3, cw, cb, wg, bg, lam, og)


def _compress_kernel(kx_ref, vx_ref, w1k_ref, w2k_ref, pk_ref, w1v_ref, w2v_ref, pv_ref, gk_ref, ones_ref,
                     ko_ref, vo_ref):
    n_chunk = kx_ref.shape[1]
    half = NSA_KV * CMP_HIDDEN

    def mlp(x_ref, w1_ref, w2_ref, p_ref):
        ab = _dot(x_ref[0], w1_ref[...])
        pos = _dot(p_ref[...], w1_ref[...])
        hid = ab[:, :half] + pltpu.roll(ab[:, half:], n_chunk - 1, 0) + (pos[0:1, :half] + pos[1:2, half:])
        return _dot(_gelu_tanh(hid).astype(BF16), w2_ref[...])

    kc = mlp(kx_ref, w1k_ref, w2k_ref, pk_ref)
    ko_ref[0] = _group_rms(kc, ones_ref[...], gk_ref[...]).astype(BF16)
    vo_ref[0] = mlp(vx_ref, w1v_ref, w2v_ref, pv_ref).astype(BF16)


def _compress(kx, vx, w1k, w2k, pk, w1v, w2v, pv, gk, ones_kv):
    bsz, n_chunk, width = kx.shape
    full = lambda a: pl.BlockSpec(a.shape, lambda b: (0,) * a.ndim)
    xin = pl.BlockSpec((1, n_chunk, width), lambda b: (b, 0, 0))
    out = pl.BlockSpec((1, n_chunk, KV_W), lambda b: (b, 0, 0))
    return pl.pallas_call(
        _compress_kernel,
        grid=(bsz,),
        in_specs=[xin, xin] + [full(a) for a in (w1k, w2k, pk, w1v, w2v, pv, gk, ones_kv)],
        out_specs=[out, out],
        out_shape=[jax.ShapeDtypeStruct((bsz, n_chunk, KV_W), BF16)] * 2,
        compiler_params=_cparams(1),
    )(kx, vx, w1k, w2k, pk, w1v, w2v, pv, gk, ones_kv)


def _nsa_kernel(q_ref, gates_ref, kcmp_ref, vcmp_ref, ksx_ref, vsp_ref, kwp_ref, vwp_ref, ovt_ref,
                bc_ref, bw_ref, bs_ref, bf_ref, og_ref, y_ref):
    i = pl.program_id(1)
    t0 = pl.multiple_of(i * TQ, TQ)
    n_blk = ovt_ref.shape[0]
    lane = lax.broadcasted_iota(jnp.int32, (TQ, LANES), 1)
    lo_half = lane < HEAD_DIM
    n_batch = q_ref.shape[0]

    def with_ones(v):
        return jnp.concatenate([v, jnp.ones_like(v)], axis=1)

    def near_part(bb):
        pieces = []
        for p in range(NSA_HPG):
            qs = q_ref[bb, :, p * LANES:(p + 1) * LANES]
            zero = jnp.zeros_like(qs)
            pieces += [jnp.where(lo_half, qs, zero), jnp.where(lo_half, zero, qs)]
        q8 = jnp.concatenate(pieces, axis=0)

        bc = bc_ref[0]
        lc = _dot_nt(q8, kcmp_ref[bb]) + bc
        ec = jnp.where(bc > MASKED_BELOW, jnp.exp2(lc - jnp.max(lc, axis=-1, keepdims=True)), 0.0)
        sc = jnp.sum(ec, axis=-1, keepdims=True)
        pc = ec / jnp.where(sc > 0.0, sc, 1.0)
        o_c = _dot(pc.astype(BF16), vcmp_ref[bb])

        blocks = [pc[r * TQ:(r + 1) * TQ] for r in range(NSA_HPG * NSA_KV)]
        pcs = jnp.concatenate([sum(blocks[g::NSA_KV]) for g in range(NSA_KV)], axis=0)
        pcs_hi = pcs.astype(BF16)
        pcs_lo = (pcs - pcs_hi.astype(F32)).astype(BF16)
        imp = _dot_nt(ovt_ref[...], pcs_hi) + _dot_nt(ovt_ref[...], pcs_lo)
        blk = lax.broadcasted_iota(jnp.int32, imp.shape, 0)
        forced = (blk == 0) | (blk == i) | (blk == i - 1)
        score = jnp.where(forced, SEL_FORCE, jnp.where(blk > i, -3e38, imp))
        rank = jnp.zeros(imp.shape, F32)
        for m in range(n_blk):
            row = score[m:m + 1, :]
            rank = rank + jnp.where(blk > m, jnp.where(row >= score, 1.0, 0.0), jnp.where(row > score, 1.0, 0.0))
        unsel = jnp.where(rank < N_SEL, 0.0, 1.0)
        unsel_far = jnp.where(blk >= i - WINDOW // SEL_L, 1.0, unsel)
        pad = jnp.zeros((LANES - 2 * n_blk, imp.shape[1]), F32)
        u_t = jnp.concatenate([unsel, unsel_far, pad], axis=0).T
        u_lane = lax.broadcasted_iota(jnp.int32, u_t.shape, 1)
        u_near = jnp.where(u_lane < n_blk, u_t, 0.0).astype(BF16)
        u_far = jnp.where(u_lane >= n_blk, u_t, 0.0).astype(BF16)
        qx_near = jnp.concatenate([q8, jnp.concatenate([u_near] * NSA_HPG, axis=0)], axis=1)
        qx_far = jnp.concatenate([q8, jnp.concatenate([u_far] * NSA_HPG, axis=0)], axis=1)

        lw = _dot_nt(q8, kwp_ref[bb, pl.ds(t0, NEAR), :]) + bw_ref[0]
        ew = jnp.exp2(lw - jnp.max(lw, axis=-1, keepdims=True))
        ow2 = _dot(ew.astype(BF16), with_ones(vwp_ref[bb, pl.ds(t0, NEAR), :]))
        o_w = ow2[:, :LANES] / ow2[:, LANES:]

        ls = _dot_nt(qx_near, ksx_ref[bb, pl.ds(t0, NEAR), :]) + bs_ref[0]
        m1 = jnp.max(ls, axis=-1, keepdims=True)
        e1 = jnp.exp2(ls - m1)
        acc1 = _dot(e1.astype(BF16), with_ones(vsp_ref[bb, pl.ds(t0, NEAR), :]))
        return o_c, o_w, qx_far, (m1, acc1)

    near = [near_part(bb) for bb in range(n_batch)]
    bfar = bf_ref[...]

    def far_step(kf, carry):
        base = pl.multiple_of(WINDOW + kf * FAR_TK, FAR_TK)
        new = []
        for bb in range(n_batch):
            m, acc = carry[bb]
            lf = _dot_nt(near[bb][2], ksx_ref[bb, pl.ds(base, FAR_TK), :]) + bfar
            m_new = jnp.maximum(m, jnp.max(lf, axis=-1, keepdims=True))
            alpha = jnp.exp2(m - m_new)
            e = jnp.exp2(lf - m_new)
            new.append((m_new, alpha * acc + _dot(e.astype(BF16), with_ones(vsp_ref[bb, pl.ds(base, FAR_TK), :]))))
        return tuple(new)

    n_far = (jnp.maximum(t0 - WINDOW, 0) + FAR_TK - 1) // FAR_TK
    far = lax.fori_loop(0, n_far, far_step, tuple(part[3] for part in near))

    for bb in range(n_batch):
        o_c, o_w = near[bb][0], near[bb][1]
        _, acc_s = far[bb]
        o_s = acc_s[:, :LANES] / acc_s[:, LANES:]
        gates = gates_ref[bb]

        def gate_col(j):
            cols = [gates[:, (g * NSA_HPG + p) * 3 + j:(g * NSA_HPG + p) * 3 + j + 1]
                    for p in range(NSA_HPG) for g in range(NSA_KV)]
            return jnp.concatenate(cols, axis=0)

        out = gate_col(0) * o_c + gate_col(1) * o_s + gate_col(2) * o_w
        slabs = [jnp.where(lo_half, out[(2 * p) * TQ:(2 * p + 1) * TQ], out[(2 * p + 1) * TQ:(2 * p + 2) * TQ])
                 for p in range(NSA_HPG)]
        y_ref[bb] = _rms(jnp.concatenate(slabs, axis=-1), og_ref[...]).astype(BF16)


def _nsa(q3, gates3, kcmp, vcmp, ksx, vsp, kwp, vwp, ovt, bias_c, bias_w, bias_s, bias_far, og):
    bsz, seq, _ = q3.shape
    n_chunk = kcmp.shape[1]
    n_var = bias_w.shape[0] - 1
    nb = NSA_NB if bsz % NSA_NB == 0 else 1
    full = lambda a: pl.BlockSpec(a.shape, lambda b, i: (0,) * a.ndim)
    per_b = lambda a: pl.BlockSpec((nb,) + a.shape[1:], lambda b, i: (b,) + (0,) * (a.ndim - 1))
    near = pl.BlockSpec((1, QROWS2, NEAR), lambda b, i: (jnp.minimum(i, n_var), 0, 0))
    return pl.pallas_call(
        _nsa_kernel,
        grid=(bsz // nb, seq // TQ),
        in_specs=[pl.BlockSpec((nb, TQ, NSA_WIDTH), lambda b, i: (b, i, 0)),
                  pl.BlockSpec((nb, TQ, LANES), lambda b, i: (b, i, 0)),
                  per_b(kcmp), per_b(vcmp), per_b(ksx), per_b(vsp), per_b(kwp), per_b(vwp),
                  full(ovt),
                  pl.BlockSpec((1, QROWS2, n_chunk), lambda b, i: (i, 0, 0)),
                  near, near, full(bias_far), full(og)],
        out_specs=pl.BlockSpec((nb, TQ, NSA_WIDTH), lambda b, i: (b, i, 0)),
        out_shape=jax.ShapeDtypeStruct((bsz, seq, NSA_WIDTH), BF16),
        compiler_params=_cparams(2),
    )(q3, gates3, kcmp, vcmp, ksx, vsp, kwp, vwp, ovt, bias_c, bias_w, bias_s, bias_far, og)


def _memkv_kernel(mem_ref, g_ref, wkv_ref, gk_ref, k_ref, v_ref):
    mn = _rms(mem_ref[0], g_ref[...]).astype(BF16)
    kv = _dot(mn, wkv_ref[...])
    for h in range(X_HEADS):
        sl = slice(h * X_HEAD_DIM, (h + 1) * X_HEAD_DIM)
        k_ref[0, :, sl] = _rms(kv[:, sl], gk_ref[...]).astype(BF16)
    v_ref[0] = kv[:, D_MODEL:].astype(BF16)


def _memkv(mem, g, wkv, gk):
    bsz, mlen, _ = mem.shape
    full = lambda a: pl.BlockSpec(a.shape, lambda b: (0,) * a.ndim)
    blk = pl.BlockSpec((1, mlen, D_MODEL), lambda b: (b, 0, 0))
    return pl.pallas_call(
        _memkv_kernel,
        grid=(bsz,),
        in_specs=[blk, full(g), full(wkv), full(gk)],
        out_specs=[blk, blk],
        out_shape=[jax.ShapeDtypeStruct((bsz, mlen, D_MODEL), BF16)] * 2,
        compiler_params=_cparams(1),
    )(mem, g, wkv, gk)


def _mid_kernel(x_ref, yrg_ref, ynsa_ref, woa_ref, wob_ref, gx_ref, wq_ref, gq_ref, k_ref, v_ref, wo_ref,
                gm_ref, wrh_ref, br_ref, h_ref, xt_ref, rw_ref, ri_ref, cnt_ref):
    h1 = x_ref[0] + _dot(yrg_ref[0], woa_ref[...]) + _dot(ynsa_ref[0], wob_ref[...])

    q = _dot(_rms(h1, gx_ref[...]).astype(BF16), wq_ref[...])
    heads = []
    for h in range(X_HEADS):
        sl = slice(h * X_HEAD_DIM, (h + 1) * X_HEAD_DIM)
        qh = _rms(q[:, sl], gq_ref[...]).astype(BF16)
        lg = _dot_nt(qh, k_ref[0, :, sl])
        e = jnp.exp2(lg - jnp.max(lg, axis=-1, keepdims=True))
        heads.append(_dot(e.astype(BF16), v_ref[0, :, sl]) / jnp.sum(e, axis=-1, keepdims=True))
    h2 = h1 + _dot(jnp.concatenate(heads, axis=-1).astype(BF16), wo_ref[...])
    h_ref[0] = h2

    xt = _rms(h2, gm_ref[...])
    _store_row_tiles(xt_ref, xt)
    xt_hi = xt.astype(BF16)
    xt_lo = (xt - xt_hi.astype(F32)).astype(BF16)
    hi2 = _dot(xt_hi, wrh_ref[...])
    lg = hi2[:, :LANES] + hi2[:, LANES:] + _dot(xt_lo, wrh_ref[:, :LANES]) + br_ref[...]
    lane = lax.broadcasted_iota(jnp.int32, lg.shape, 1)
    lane_f = lane.astype(F32)
    first_of = lambda hit: jnp.min(jnp.where(hit, lane_f, 1e9), axis=-1, keepdims=True)
    glog = jnp.where(lane < N_GROUPS, lg, -3e38)
    gmax = jnp.max(glog, axis=-1, keepdims=True)
    gsel = first_of(glog == gmax)
    p_g = 1.0 / jnp.sum(jnp.exp(glog - gmax), axis=-1, keepdims=True)
    lo = N_GROUPS + EXP_PER_GROUP * gsel
    el = jnp.where((lane_f >= lo) & (lane_f < lo + EXP_PER_GROUP), lg, -3e38)
    m_a = jnp.max(el, axis=-1, keepdims=True)
    i_a = first_of(el == m_a)
    el2 = jnp.where(lane_f == i_a, -3e38, el)
    m_b = jnp.max(el2, axis=-1, keepdims=True)
    i_b = first_of(el2 == m_b)
    r = jnp.exp(m_b - m_a)
    w_a = p_g / (1.0 + r)
    w_b = p_g * r / (1.0 + r)
    e_a = i_a - N_GROUPS
    e_b = i_b - N_GROUPS
    rw_ref[0] = jnp.where(lane == 0, w_a, jnp.where(lane == 1, w_b, 0.0))
    ri_ref[0] = jnp.where(lane == 0, e_a, jnp.where(lane == 1, e_b, 0.0)).astype(jnp.int32)

    @pl.when((pl.program_id(0) == 0) & (pl.program_id(1) == 0))
    def _():
        cnt_ref[...] = jnp.zeros_like(cnt_ref)

    hot = jnp.where((lane_f == e_a) | (lane_f == e_b), 1.0, 0.0)
    cnt_ref[...] += jnp.sum(hot, axis=0, keepdims=True)


def _mid(x, yrg, ynsa, woa, wob, gx, wq, gq, kx, vx, wo, gm, wrh, br):
    bsz, seq, _ = x.shape
    tm = min(TM_MID, seq)
    mlen = kx.shape[1]
    n_i = seq // tm
    full = lambda a: pl.BlockSpec(a.shape, lambda b, i: (0,) * a.ndim)
    tok = lambda w: pl.BlockSpec((1, tm, w), lambda b, i: (b, i, 0))
    memb = pl.BlockSpec((1, mlen, D_MODEL), lambda b, i: (b, 0, 0))
    xt_spec = pl.BlockSpec((tm * ROW_TILE, LANES), lambda b, i: (b * n_i + i, 0))
    return pl.pallas_call(
        _mid_kernel,
        grid=(bsz, seq // tm),
        in_specs=[tok(D_MODEL), tok(RG_WIDTH), tok(NSA_WIDTH), full(woa), full(wob), full(gx), full(wq), full(gq),
                  memb, memb, full(wo), full(gm), full(wrh), full(br)],
        out_specs=[tok(D_MODEL), xt_spec, tok(LANES), tok(LANES), pl.BlockSpec((1, LANES), lambda b, i: (0, 0))],
        out_shape=[jax.ShapeDtypeStruct((bsz, seq, D_MODEL), F32),
                   jax.ShapeDtypeStruct((bsz * seq * ROW_TILE, LANES), F32),
                   jax.ShapeDtypeStruct((bsz, seq, LANES), F32), jax.ShapeDtypeStruct((bsz, seq, LANES), jnp.int32),
                   jax.ShapeDtypeStruct((1, LANES), F32)],
        compiler_params=_cparams(2),
    )(x, yrg, ynsa, woa, wob, gx, wq, gq, kx, vx, wo, gm, wrh, br)


def _dest_kernel(ri_ref, pstart_ref, dest_ref, run_ref):
    @pl.when(pl.program_id(0) == 0)
    def _():
        run_ref[...] = jnp.zeros_like(run_ref)

    ri = ri_ref[...]
    tm = ri.shape[0]
    lane = lax.broadcasted_iota(jnp.int32, ri.shape, 1)
    e_a = ri[:, 0:1]
    e_b = ri[:, 1:2]
    hot_a = lane == e_a
    hot_b = lane == e_b
    hot = jnp.where(hot_a | hot_b, 1.0, 0.0)
    row = lax.broadcasted_iota(jnp.int32, (tm, tm), 0)
    col = lax.broadcasted_iota(jnp.int32, (tm, tm), 1)
    earlier = jnp.where(col < row, 1.0, 0.0).astype(BF16)
    base = _dot(earlier, hot.astype(BF16)) + run_ref[...] + pstart_ref[...]
    d_a = jnp.sum(jnp.where(hot_a, base, 0.0), axis=-1, keepdims=True)
    d_b = jnp.sum(jnp.where(hot_b, base, 0.0), axis=-1, keepdims=True)
    dest_ref[...] = jnp.where(lane == 0, d_a, jnp.where(lane == 1, d_b, 0.0)).astype(jnp.int32)
    run_ref[...] += jnp.sum(hot, axis=0, keepdims=True)


def _dest(ri2, pstart):
    n_tok = ri2.shape[0]
    tm = min(TM_DEST, n_tok)
    return pl.pallas_call(
        _dest_kernel,
        grid=(n_tok // tm,),
        in_specs=[pl.BlockSpec((tm, LANES), lambda i: (i, 0)), pl.BlockSpec((1, LANES), lambda i: (0, 0))],
        out_specs=pl.BlockSpec((tm, LANES), lambda i: (i, 0)),
        out_shape=jax.ShapeDtypeStruct((n_tok, LANES), jnp.int32),
        scratch_shapes=[pltpu.VMEM((1, LANES), F32)],
        compiler_params=_cparams(1),
    )(ri2, pstart)


def _store_row_tiles(ref, val):
    n = val.shape[0]
    for c in range(ROW_TILE):
        ref[pl.ds(c, n, stride=ROW_TILE), :] = val[:, c * LANES:(c + 1) * LANES]


def _load_row_tiles(ref, n):
    return [ref[pl.ds(c, n, stride=ROW_TILE), :] for c in range(ROW_TILE)]


def _token_rows(ref, t):
    return ref.at[pl.ds(pl.multiple_of(t * ROW_TILE, ROW_TILE), ROW_TILE), :]


def _dispatch_kernel(cnt_ref, pstart_ref, da_ref, db_ref, xt_ref, xs_ref, zrow, sem, zsem):
    tm = da_ref.shape[2]

    @pl.when(pl.program_id(0) == 0)
    def _():
        zrow[...] = jnp.zeros_like(zrow)

        def per_expert(e, c):
            used = cnt_ref[e]
            padded = (used + MOE_TB - 1) // MOE_TB * MOE_TB
            base = pstart_ref[e]

            def fill(r, c2):
                pltpu.make_async_copy(zrow, _token_rows(xs_ref, base + r), zsem).start()
                return c2

            def fill_done(r, c2):
                pltpu.make_async_copy(zrow, _token_rows(xs_ref, 0), zsem).wait()
                return c2

            lax.fori_loop(used, padded, fill, 0)
            lax.fori_loop(used, padded, fill_done, 0)
            return c

        lax.fori_loop(0, N_EXPERTS, per_expert, 0)

        last = N_EXPERTS - 1
        first_unused = (pstart_ref[last] + (cnt_ref[last] + MOE_TB - 1) // MOE_TB * MOE_TB) // MOE_TB

        def per_block(j, c):
            def fill(r, c2):
                pltpu.make_async_copy(zrow, _token_rows(xs_ref, j * MOE_TB + r), zsem).start()
                return c2

            def fill_done(r, c2):
                pltpu.make_async_copy(zrow, _token_rows(xs_ref, 0), zsem).wait()
                return c2

            lax.fori_loop(0, MOE_TB, fill, 0, unroll=DMA_UNROLL)
            lax.fori_loop(0, MOE_TB, fill_done, 0, unroll=DMA_UNROLL)
            return c

        lax.fori_loop(first_unused, xs_ref.shape[0] // (ROW_TILE * MOE_TB), per_block, 0)

    def issue(t, c):
        pltpu.make_async_copy(_token_rows(xt_ref, t), _token_rows(xs_ref, da_ref[0, 0, t]), sem).start(priority=0)
        pltpu.make_async_copy(_token_rows(xt_ref, t), _token_rows(xs_ref, db_ref[0, 0, t]), sem).start(priority=1)
        return c

    lax.fori_loop(0, tm, issue, 0, unroll=DMA_UNROLL)

    def drain(t, c):
        pltpu.make_async_copy(_token_rows(xt_ref, 0), _token_rows(xs_ref, 0), sem).wait()
        pltpu.make_async_copy(_token_rows(xt_ref, 0), _token_rows(xs_ref, 0), sem).wait()
        return c

    lax.fori_loop(0, tm, drain, 0, unroll=DMA_UNROLL)


def _dispatch(cnt, pstart, da, db, xt_rows, n_pad):
    n_tiles, _, tm = da.shape
    smem = pl.BlockSpec((1, 1, tm), lambda i, c, p: (i, 0, 0), memory_space=pltpu.SMEM)
    grid_spec = pltpu.PrefetchScalarGridSpec(
        num_scalar_prefetch=2,
        grid=(n_tiles,),
        in_specs=[smem, smem, pl.BlockSpec((tm * ROW_TILE, LANES), lambda i, c, p: (i, 0))],
        out_specs=pl.BlockSpec(memory_space=pl.ANY),
        scratch_shapes=[pltpu.VMEM((ROW_TILE, LANES), F32), pltpu.SemaphoreType.DMA(()),
                        pltpu.SemaphoreType.DMA(())],
    )
    return pl.pallas_call(
        _dispatch_kernel,
        grid_spec=grid_spec,
        out_shape=jax.ShapeDtypeStruct((n_pad * ROW_TILE, LANES), F32),
        compiler_params=pltpu.CompilerParams(dimension_semantics=("arbitrary",), has_side_effects=True,
                                             vmem_limit_bytes=VMEM_LIMIT),
    )(cnt, pstart, da, db, xt_rows)


def _ffn_kernel(bexp_ref, nused_ref, xs_ref, w1_ref, w3_ref, w2_ref, ys_ref):
    del bexp_ref
    j = pl.program_id(0)

    @pl.when(j < nused_ref[0])
    def _():
        xb = jnp.concatenate(_load_row_tiles(xs_ref, MOE_TB), axis=-1).astype(BF16)
        a = _dot(xb, w1_ref[0].astype(BF16))
        h = a * _sigmoid(a) * _dot(xb, w3_ref[0].astype(BF16))
        _store_row_tiles(ys_ref, _dot(h.astype(BF16), w2_ref[0].astype(BF16)))

    @pl.when(j >= nused_ref[0])
    def _():
        ys_ref[...] = jnp.zeros_like(ys_ref)


def _ffn(blk_exp, n_used, xs, w1, w3, w2):
    n_blocks = xs.shape[0] // (MOE_TB * ROW_TILE)
    rows = pl.BlockSpec((MOE_TB * ROW_TILE, LANES), lambda j, be, nu: (j, 0))
    used_rows = pl.BlockSpec((MOE_TB * ROW_TILE, LANES), lambda j, be, nu: (jnp.minimum(j, nu[0] - 1), 0))
    grid_spec = pltpu.PrefetchScalarGridSpec(
        num_scalar_prefetch=2,
        grid=(n_blocks,),
        in_specs=[used_rows,
                  pl.BlockSpec((1, D_MODEL, D_EXPERT), lambda j, be, nu: (be[j], 0, 0)),
                  pl.BlockSpec((1, D_MODEL, D_EXPERT), lambda j, be, nu: (be[j], 0, 0)),
                  pl.BlockSpec((1, D_EXPERT, D_MODEL), lambda j, be, nu: (be[j], 0, 0))],
        out_specs=rows,
    )
    return pl.pallas_call(
        _ffn_kernel,
        grid_spec=grid_spec,
        out_shape=jax.ShapeDtypeStruct(xs.shape, F32),
        compiler_params=_cparams(1),
    )(blk_exp, n_used, xs, w1, w3, w2)


def _combine_kernel(da_ref, db_ref, da_next_ref, db_next_ref, h_ref, rw_ref, ys_ref, o_ref, ya, yb, sems):
    tm = da_ref.shape[2]
    i = pl.program_id(0)
    slot = i % 2

    def start_gather(a_ref, b_ref, s):
        def issue(t, c):
            pltpu.make_async_copy(_token_rows(ys_ref, a_ref[0, 0, t]), _token_rows(ya.at[s], t),
                                  sems.at[s]).start(priority=0)
            pltpu.make_async_copy(_token_rows(ys_ref, b_ref[0, 0, t]), _token_rows(yb.at[s], t),
                                  sems.at[s]).start(priority=1)
            return c

        lax.fori_loop(0, tm, issue, 0, unroll=DMA_UNROLL)

    @pl.when(i == 0)
    def _():
        start_gather(da_ref, db_ref, slot)

    @pl.when(i + 1 < pl.num_programs(0))
    def _():
        start_gather(da_next_ref, db_next_ref, 1 - slot)

    def drain(t, c):
        pltpu.make_async_copy(_token_rows(ys_ref, 0), _token_rows(ya.at[slot], 0), sems.at[slot]).wait()
        pltpu.make_async_copy(_token_rows(ys_ref, 0), _token_rows(yb.at[slot], 0), sems.at[slot]).wait()
        return c

    lax.fori_loop(0, tm, drain, 0, unroll=DMA_UNROLL)
    rw = rw_ref[...]
    mix = [rw[:, 0:1] * a + rw[:, 1:2] * b
           for a, b in zip(_load_row_tiles(ya.at[slot], tm), _load_row_tiles(yb.at[slot], tm))]
    o_ref[...] = h_ref[...] + jnp.concatenate(mix, axis=-1)


def _combine(da, db, h2, rw, ys):
    n_tiles, _, tm = da.shape
    n_tok = h2.shape[0]
    smem = pl.BlockSpec((1, 1, tm), lambda i: (i, 0, 0), memory_space=pltpu.SMEM)
    smem_next = pl.BlockSpec((1, 1, tm), lambda i: (jnp.minimum(i + 1, n_tiles - 1), 0, 0), memory_space=pltpu.SMEM)
    row = lambda w: pl.BlockSpec((tm, w), lambda i: (i, 0))
    slots = pltpu.VMEM((2, tm * ROW_TILE, LANES), F32)
    return pl.pallas_call(
        _combine_kernel,
        grid=(n_tiles,),
        in_specs=[smem, smem, smem_next, smem_next, row(D_MODEL), row(LANES), pl.BlockSpec(memory_space=pl.ANY)],
        out_specs=row(D_MODEL),
        out_shape=jax.ShapeDtypeStruct((n_tok, D_MODEL), F32),
        scratch_shapes=[slots, slots, pltpu.SemaphoreType.DMA((2,))],
        compiler_params=_cparams(1),
    )(da, db, da, db, h2, rw, ys)


def _rel_bucket_np(dist):
    n = np.maximum(dist, 0)
    max_exact = NUM_BUCKETS // 2
    nf = np.maximum(n, 1).astype(np.float32)
    large = max_exact + (np.log(nf / max_exact) / math.log(MAX_DIST / max_exact)
                         * (NUM_BUCKETS - max_exact)).astype(np.int32)
    large = np.minimum(large, NUM_BUCKETS - 1)
    return np.where(n < max_exact, n, large).astype(np.int32)


def _toeplitz(vec, rows):
    width = vec.shape[-1] - 1
    flat = jnp.tile(vec, (1,) * (vec.ndim - 1) + (rows,))[..., :rows * width]
    return flat.reshape(vec.shape[:-1] + (rows, width))


def _bias_tables(rel_bias, seq):
    n_chunk = seq // CMP_STRIDE
    n_tiles = seq // TQ
    table = rel_bias.T.astype(F32)

    wide = NEAR + TQ
    k = np.arange(wide + 1)
    dw = np.where(k < NEAR, WINDOW - k, WINDOW + wide + 1 - k)
    used = (k < NEAR) | (k > wide + 1 - TQ)
    vals = table[:, _rel_bucket_np(dw)]

    n_var = WINDOW // TQ
    first_key = WINDOW - TQ * np.arange(n_var + 1)[:, None, None]
    in_seq = np.arange(NEAR)[None, None, :] >= first_key

    def near_tile(valid):
        t = _toeplitz(jnp.where(valid[None, :], vals, NEG_INF), TQ)[:, :, :NEAR]
        t = t.reshape(NSA_KV, NSA_HPG, TQ, NEAR).transpose(1, 0, 2, 3).reshape(1, QROWS2, NEAR)
        return jnp.where(in_seq, t, NEG_INF)

    bias_w = near_tile(used & (dw >= 0) & (dw < WINDOW))
    bias_s = near_tile(used & (dw >= 0))
    bias_far = table[:, NUM_BUCKETS - 1].reshape(NSA_KV, NSA_HPG, 1).transpose(1, 0, 2)
    bias_far = jnp.broadcast_to(bias_far, (NSA_HPG, NSA_KV, TQ)).reshape(QROWS2, 1)

    r = np.arange(CMP_STRIDE)[:, None]
    k = np.arange(2 * n_chunk + 1)[None, :]
    lag = 2 * n_chunk + 1 - k
    valid = (k > n_chunk + 1) & (CMP_STRIDE * lag + r >= CMP_L - 1)
    vals = table[:, _rel_bucket_np(CMP_STRIDE * lag + r - CMP_L // 2)]
    full = _toeplitz(jnp.where(valid[None], vals, NEG_INF), n_chunk)[..., :n_chunk]
    full = jnp.where(np.arange(n_chunk) < n_chunk - 1, full, NEG_INF)
    a4 = TQ // CMP_STRIDE
    full = full.reshape(NSA_KV, NSA_HPG, CMP_STRIDE, n_tiles, a4, n_chunk).transpose(3, 1, 0, 4, 2, 5)
    bias_c = full.reshape(n_tiles, QROWS2, n_chunk)
    return tuple(LOG2E * t for t in (bias_c, bias_w, bias_s, bias_far))


def _selection_tables(seq):
    n_chunk = seq // CMP_STRIDE
    n_blk = seq // SEL_L
    c = np.arange(n_chunk)
    n = np.arange(n_blk)
    start = c * CMP_STRIDE
    overlap_t = ((start[None, :] <= n[:, None] * SEL_L + SEL_L - 1) & (start[None, :] + CMP_L - 1 >= n[:, None] * SEL_L)
                 & (c < n_chunk - 1)[None, :])
    pos = np.arange(seq + WINDOW) - WINDOW
    lane_blk = np.arange(LANES) % n_blk
    hit = (pos[:, None] >= 0) & (pos[:, None] // SEL_L == lane_blk[None, :]) & (np.arange(LANES) < 2 * n_blk)[None, :]
    return jnp.asarray(overlap_t, BF16), jnp.asarray(np.where(hit, -UNSEL_PENALTY, 0.0), BF16)


def _block_ones(width, group):
    idx = np.arange(width) // group
    return jnp.asarray((idx[:, None] == idx[None, :]) / group, BF16)


def _block_diag(w):
    nb, n, m = w.shape
    eye = jnp.eye(nb, dtype=w.dtype)
    return jnp.einsum('hij,hg->higj', w, eye).reshape(nb * n, nb * m)


def _compress_weights(w1, w2, pos):
    half_l = CMP_L // 2
    parts = []
    for half in range(2):
        wh = w1[half * half_l * HEAD_DIM:(half + 1) * half_l * HEAD_DIM].reshape(half_l, HEAD_DIM, CMP_HIDDEN)
        z = jnp.zeros_like(wh)
        for g in range(NSA_KV):
            grp = [wh if gg == g else z for gg in range(NSA_KV)]
            parts.append(jnp.stack(grp, axis=1).reshape(half_l * KV_W, CMP_HIDDEN))
    w1cat = jnp.concatenate(parts, axis=1).astype(BF16)
    w2bd = _block_diag(jnp.stack([w2] * NSA_KV)).astype(BF16)
    prow = [jnp.tile(pos[half * half_l:(half + 1) * half_l][:, None, :], (1, NSA_KV, 1)).reshape(-1)
            for half in range(2)]
    pmat = jnp.zeros((8, half_l * KV_W), F32).at[0].set(prow[0]).at[1].set(prow[1]).astype(BF16)
    return w1cat, w2bd, pmat


def kernel(x, mem, rel_bias, norm_mix, w_in, rg_conv_w, rg_conv_b, rg_w_r, rg_b_r, rg_w_i, rg_b_i, rg_lambda, nsa_g_q, nsa_g_kc, nsa_g_ks, nsa_g_kw, cmp_pos_k, cmp_pos_v, cmp_k_w1, cmp_k_w2, cmp_v_w1, cmp_v_w2, out_g_rg, out_g_nsa, w_out, norm_x, norm_mem, xa_w_q, xa_w_kv, xa_w_o, xa_g_q, xa_g_k, norm_moe, router_g_w, router_g_b, router_e_w, router_e_b, exp_w1, exp_w3, exp_w2):
    bsz, seq, _ = x.shape
    n_tok = bsz * seq
    assert seq % FAR_TK == 0 and 2 * (seq // SEL_L) <= LANES and norm_mix.shape[0] == 1
    l = 0
    row = lambda v: v.reshape(1, -1).astype(F32)

    perm = np.array([(half * NSA_HPG + p) * HEAD_DIM + d
                     for p in range(NSA_HPG) for half in range(NSA_KV) for d in range(HEAD_DIM)])
    offs = np.cumsum([0, RG_WIDTH, RG_WIDTH, NSA_WIDTH] + [KV_W] * 6)
    w = w_in[l]
    wrg = w[:, :offs[2]].astype(BF16)
    wq = w[:, offs[2]:offs[3]][:, perm].astype(BF16)
    wkv = w[:, offs[3]:offs[9]].astype(BF16)
    wgl = jnp.pad(w[:, offs[9]:], ((0, 0), (0, LANES - 3 * NSA_HEADS))).astype(BF16)
    ones64 = _block_ones(NSA_WIDTH, HEAD_DIM)
    gq = row(jnp.tile(nsa_g_q[l], NSA_HEADS) * (HEAD_DIM ** -0.5 * LOG2E))
    u, gate, q, kc, vc, ks, vs, kw, vw, gates = _inproj(
        x.reshape(n_tok, D_MODEL), row(norm_mix[l]), wrg, wq, wkv, wgl, gq,
        row(jnp.tile(nsa_g_ks[l], NSA_KV)), row(jnp.tile(nsa_g_kw[l], NSA_KV)), ones64)

    wg = (0.5 * jnp.concatenate([_block_diag(rg_w_r[l]), _block_diag(rg_w_i[l])], axis=1)).astype(BF16)
    bg = 0.5 * jnp.concatenate([rg_b_r[l], rg_b_i[l]]).reshape(1, -1)
    y_rg = _rglru(u.reshape(bsz, seq, RG_WIDTH), gate.reshape(bsz, seq, RG_WIDTH),
                  rg_conv_w[l].reshape(CONV_W, RG_WIDTH), row(rg_conv_b[l]), wg, bg, row(rg_lambda[l]),
                  row(out_g_rg[l]))

    n_chunk = seq // CMP_STRIDE
    w1k, w2k, pk = _compress_weights(cmp_k_w1[l], cmp_k_w2[l], cmp_pos_k[l])
    w1v, w2v, pv = _compress_weights(cmp_v_w1[l], cmp_v_w2[l], cmp_pos_v[l])
    kcmp, vcmp = _compress(kc.reshape(bsz, n_chunk, CMP_STRIDE * KV_W), vc.reshape(bsz, n_chunk, CMP_STRIDE * KV_W),
                           w1k, w2k, pk, w1v, w2v, pv, row(jnp.tile(nsa_g_kc[l], NSA_KV)),
                           ones64[:KV_W, :KV_W])
    padw = lambda t: jnp.pad(t.reshape(bsz, seq, KV_W), ((0, 0), (WINDOW, 0), (0, 0)))
    bias_c, bias_w, bias_s, bias_far = _bias_tables(rel_bias, seq)
    overlap_t, penalty = _selection_tables(seq)
    ksx = jnp.concatenate([padw(ks), jnp.broadcast_to(penalty, (bsz,) + penalty.shape)], axis=-1)
    y_nsa = _nsa(q.reshape(bsz, seq, NSA_WIDTH), gates.reshape(bsz, seq, LANES), kcmp, vcmp,
                 ksx, padw(vs), padw(kw), padw(vw), overlap_t, bias_c, bias_w, bias_s, bias_far,
                 row(out_g_nsa[l][perm]))

    kx, vx = _memkv(mem, row(norm_mem[l]), xa_w_kv[l].astype(BF16), row(xa_g_k[l]))
    wo_mix = w_out[l]
    wr = jnp.pad(jnp.concatenate([router_g_w[l], router_e_w[l]], axis=1),
                 ((0, 0), (0, LANES - N_GROUPS - N_EXPERTS)))
    wr_hi = wr.astype(BF16)
    br = jnp.pad(jnp.concatenate([router_g_b[l], router_e_b[l]]), (0, LANES - N_GROUPS - N_EXPERTS)).reshape(1, -1)
    h2, xt, rw, ri, counts = _mid(
        x, y_rg, y_nsa, wo_mix[:RG_WIDTH].astype(BF16), wo_mix[RG_WIDTH:][perm].astype(BF16), row(norm_x[l]),
        xa_w_q[l].astype(BF16), row(xa_g_q[l] * (X_HEAD_DIM ** -0.5 * LOG2E)), kx, vx, xa_w_o[l].astype(BF16),
        row(norm_moe[l]), jnp.concatenate([wr_hi, (wr - wr_hi.astype(F32)).astype(BF16)], axis=1), br)

    n_slots = 2 * n_tok
    n_blocks = n_slots // MOE_TB + N_EXPERTS
    n_pad = n_blocks * MOE_TB
    cnt = counts[0, :N_EXPERTS].astype(jnp.int32)
    pcnt = (cnt + MOE_TB - 1) // MOE_TB * MOE_TB
    pends = jnp.cumsum(pcnt)
    pstart = jnp.pad((pends - pcnt).astype(F32), (0, LANES - N_EXPERTS)).reshape(1, LANES)
    blk_exp = jnp.minimum(jnp.sum(pends[None, :] <= jnp.arange(n_blocks, dtype=jnp.int32)[:, None] * MOE_TB, axis=1),
                          N_EXPERTS - 1).astype(jnp.int32)
    n_used = (pends[-1:] // MOE_TB).astype(jnp.int32)
    dest = _dest(ri.reshape(n_tok, LANES), pstart)
    tmd = min(TM_DMA, n_tok)
    da = dest[:, 0].reshape(n_tok // tmd, 1, tmd)
    db = dest[:, 1].reshape(n_tok // tmd, 1, tmd)
    xs = _dispatch(cnt, (pends - pcnt).astype(jnp.int32), da, db, xt, n_pad)
    ys = _ffn(blk_exp, n_used, xs, exp_w1[l], exp_w3[l], exp_w2[l])
    out = _combine(da, db, h2.reshape(n_tok, D_MODEL), rw.reshape(n_tok, LANES), ys)
    return out.reshape(bsz, seq, D_MODEL)
```

```python
import math

import numpy as np
import jax
import jax.numpy as jnp
from jax import lax
from jax.experimental import pallas as pl
from jax.experimental.pallas import tpu as pltpu

F32 = jnp.float32
BF16 = jnp.bfloat16

D_MODEL = 1024
RG_WIDTH = 512
RG_BLOCKS = 8
RG_BLOCK = 64
CONV_W = 4
RG_C = 8.0
NSA_WIDTH = 512
NSA_HEADS = 8
HEAD_DIM = 64
NSA_KV = 2
NSA_HPG = 4
KV_W = 128
CMP_L = 32
CMP_STRIDE = 16
CMP_HIDDEN = 256
SEL_L = 64
N_SEL = 8
WINDOW = 512
NUM_BUCKETS = 32
MAX_DIST = 128
X_HEADS = 4
X_HEAD_DIM = 256
N_GROUPS = 4
EXP_PER_GROUP = 8
N_EXPERTS = 32
D_EXPERT = 512
EPS = 1e-6
LOG2E = 1.0 / math.log(2.0)
NEG_INF = -1e30
MASKED_BELOW = -1e29
SEL_FORCE = 1e9
LANES = 128

TQ = 64
NEAR = WINDOW + TQ
FAR_TK = 512
QROWS2 = NSA_HEADS * TQ
UNSEL_PENALTY = 2.0 ** 100
NSA_NB = 4

TM_PROJ = 1024
TM_MID = 1024
TM_DEST = 512
DEST_ROWS = 8
TM_DMA = 512
DMA_UNROLL = 8
MOE_TB = 512
ROW_TILE = D_MODEL // LANES
RG_CHUNK = 256
SCAN_ROWS = 8
SCAN_UNROLL = 8
VMEM_LIMIT = 56 * 1024 * 1024


def _cparams(n_axes):
    return pltpu.CompilerParams(dimension_semantics=("arbitrary",) * n_axes,
                                vmem_limit_bytes=VMEM_LIMIT)


def _dot(a, b):
    return jnp.dot(a, b, preferred_element_type=F32)


def _dot_nt(a, b):
    return lax.dot_general(a, b, (((1,), (1,)), ((), ())), preferred_element_type=F32)


def _gelu_tanh(x):
    c = math.sqrt(2.0 / math.pi)
    half = 0.5 * x
    return half + half * jnp.tanh(x * (c + (c * 0.044715) * (x * x)))


def _sigmoid(x):
    return 0.5 * jnp.tanh(0.5 * x) + 0.5


def _rms(x, g):
    return x * lax.rsqrt(jnp.mean(x * x, axis=-1, keepdims=True) + EPS) * g


def _group_rms(x, ones_blk, g):
    ms = _dot((x * x).astype(BF16), ones_blk)
    return x * lax.rsqrt(ms + EPS) * g


def _inproj_kernel(x_ref, g_ref, wrg_ref, wq_ref, wkv_ref, wgl_ref, gq_ref, gks_ref, gkw_ref, ones_ref,
                   u_ref, gate_ref, q_ref, kc_ref, vc_ref, ks_ref, vs_ref, kw_ref, vw_ref, gates_ref):
    xb = _rms(x_ref[...], g_ref[...]).astype(BF16)
    rg = _dot(xb, wrg_ref[...])
    u_ref[...] = rg[:, :RG_WIDTH].astype(BF16)
    gate_ref[...] = rg[:, RG_WIDTH:].astype(BF16)
    q = _dot(xb, wq_ref[...])
    q_ref[...] = _group_rms(q, ones_ref[...], gq_ref[...]).astype(BF16)
    kv = _dot(xb, wkv_ref[...])
    ones_kv = ones_ref[:KV_W, :KV_W]
    kc_ref[...] = kv[:, 0 * KV_W:1 * KV_W].astype(BF16)
    vc_ref[...] = kv[:, 1 * KV_W:2 * KV_W].astype(BF16)
    ks_ref[...] = _group_rms(kv[:, 2 * KV_W:3 * KV_W], ones_kv, gks_ref[...]).astype(BF16)
    vs_ref[...] = kv[:, 3 * KV_W:4 * KV_W].astype(BF16)
    kw_ref[...] = _group_rms(kv[:, 4 * KV_W:5 * KV_W], ones_kv, gkw_ref[...]).astype(BF16)
    vw_ref[...] = kv[:, 5 * KV_W:6 * KV_W].astype(BF16)
    gates_ref[...] = _sigmoid(_dot(xb, wgl_ref[...]))


def _inproj(x2, g, wrg, wq, wkv, wgl, gq, gks, gkw, ones_blk):
    n_tok = x2.shape[0]
    tm = min(TM_PROJ, n_tok)
    full = lambda a: pl.BlockSpec(a.shape, lambda i: (0,) * a.ndim)
    row = lambda w: pl.BlockSpec((tm, w), lambda i: (i, 0))
    outs = [(RG_WIDTH, BF16), (RG_WIDTH, BF16), (NSA_WIDTH, BF16)] + [(KV_W, BF16)] * 6 + [(LANES, F32)]
    return pl.pallas_call(
        _inproj_kernel,
        grid=(n_tok // tm,),
        in_specs=[row(D_MODEL)] + [full(a) for a in (g, wrg, wq, wkv, wgl, gq, gks, gkw, ones_blk)],
        out_specs=[row(w) for w, _ in outs],
        out_shape=[jax.ShapeDtypeStruct((n_tok, w), dt) for w, dt in outs],
        compiler_params=_cparams(1),
    )(x2, g, wrg, wq, wkv, wgl, gq, gks, gkw, ones_blk)


def _rglru_kernel(u_ref, gate_ref, cw_ref, cb_ref, wg_ref, bg_ref, lam_ref, og_ref, y_ref, upad, a_s, h_s):
    seq = u_ref.shape[1]
    upad[0:8, :] = jnp.zeros((8, RG_WIDTH), F32)
    upad[8:8 + seq, :] = u_ref[0].astype(F32)
    neg_lam = -lam_ref[...]
    softplus = jnp.maximum(neg_lam, 0.0) + jnp.log(1.0 + jnp.exp(-jnp.abs(neg_lam)))
    log2_a_half = (-0.5 * RG_C * LOG2E) * softplus
    ch = min(RG_CHUNK, seq)
    for c in range(seq // ch):
        r0 = c * ch
        uc = cb_ref[...]
        for k in range(CONV_W):
            off = 8 + r0 - (CONV_W - 1) + k
            uc = uc + cw_ref[k:k + 1, :] * upad[off:off + ch, :]
        th = jnp.tanh(_dot(uc.astype(BF16), wg_ref[...]) + bg_ref[...])
        a = jnp.exp2(log2_a_half * th[:, :RG_WIDTH] + log2_a_half)
        a_s[r0:r0 + ch, :] = a
        s = 1.0 - a * a
        h_s[r0:r0 + ch, :] = s * lax.rsqrt(jnp.maximum(s, 1e-30)) * (0.5 * th[:, RG_WIDTH:] + 0.5) * uc

    row = lax.broadcasted_iota(jnp.int32, (SCAN_ROWS, RG_WIDTH), 0)

    def block(j, h_prev):
        rows = pl.ds(pl.multiple_of(j * SCAN_ROWS, SCAN_ROWS), SCAN_ROWS)
        a = a_s[rows, :]
        b = h_s[rows, :]
        k = 1
        while k < SCAN_ROWS:
            keep = row >= k
            b = jnp.where(keep, a * pltpu.roll(b, k, 0) + b, b)
            a = jnp.where(keep, a * pltpu.roll(a, k, 0), a)
            k *= 2
        h = a * h_prev + b
        h_s[rows, :] = h
        return h[SCAN_ROWS - 1:SCAN_ROWS, :]

    lax.fori_loop(0, seq // SCAN_ROWS, block, jnp.zeros((1, RG_WIDTH), F32), unroll=SCAN_UNROLL)

    for c in range(seq // ch):
        r0 = c * ch
        y = _gelu_tanh(gate_ref[0, r0:r0 + ch, :].astype(F32)) * h_s[r0:r0 + ch, :]
        y_ref[0, r0:r0 + ch, :] = _rms(y, og_ref[...]).astype(BF16)


def _rglru(u3, gate3, cw, cb, wg, bg, lam, og):
    bsz, seq, _ = u3.shape
    full = lambda a: pl.BlockSpec(a.shape, lambda b: (0,) * a.ndim)
    blk = pl.BlockSpec((1, seq, RG_WIDTH), lambda b: (b, 0, 0))
    return pl.pallas_call(
        _rglru_kernel,
        grid=(bsz,),
        in_specs=[blk, blk] + [full(a) for a in (cw, cb, wg, bg, lam, og)],
        out_specs=blk,
        out_shape=jax.ShapeDtypeStruct((bsz, seq, RG_WIDTH), BF16),
        scratch_shapes=[pltpu.VMEM((seq + 8, RG_WIDTH), F32), pltpu.VMEM((seq, RG_WIDTH), F32),
                        pltpu.VMEM((seq, RG_WIDTH), F32)],
        compiler_params=_cparams(1),
    )(u3, gate3, cw, cb, wg, bg, lam, og)


def _compress_kernel(kx_ref, vx_ref, w1k_ref, w2k_ref, pk_ref, w1v_ref, w2v_ref, pv_ref, gk_ref, ones_ref,
                     ko_ref, vo_ref):
    n_chunk = kx_ref.shape[1]
    half = NSA_KV * CMP_HIDDEN

    def mlp(x_ref, w1_ref, w2_ref, p_ref):
        ab = _dot(x_ref[0], w1_ref[...])
        pos = _dot(p_ref[...], w1_ref[...])
        hid = ab[:, :half] + pltpu.roll(ab[:, half:], n_chunk - 1, 0) + (pos[0:1, :half] + pos[1:2, half:])
        return _dot(_gelu_tanh(hid).astype(BF16), w2_ref[...])

    kc = mlp(kx_ref, w1k_ref, w2k_ref, pk_ref)
    ko_ref[0] = _group_rms(kc, ones_ref[...], gk_ref[...]).astype(BF16)
    vo_ref[0] = mlp(vx_ref, w1v_ref, w2v_ref, pv_ref).astype(BF16)


def _compress(kx, vx, w1k, w2k, pk, w1v, w2v, pv, gk, ones_kv):
    bsz, n_chunk, width = kx.shape
    full = lambda a: pl.BlockSpec(a.shape, lambda b: (0,) * a.ndim)
    xin = pl.BlockSpec((1, n_chunk, width), lambda b: (b, 0, 0))
    out = pl.BlockSpec((1, n_chunk, KV_W), lambda b: (b, 0, 0))
    return pl.pallas_call(
        _compress_kernel,
        grid=(bsz,),
        in_specs=[xin, xin] + [full(a) for a in (w1k, w2k, pk, w1v, w2v, pv, gk, ones_kv)],
        out_specs=[out, out],
        out_shape=[jax.ShapeDtypeStruct((bsz, n_chunk, KV_W), BF16)] * 2,
        compiler_params=_cparams(1),
    )(kx, vx, w1k, w2k, pk, w1v, w2v, pv, gk, ones_kv)


def _nsa_kernel(q_ref, gates_ref, kcmp_ref, vcmp_ref, ksx_ref, vsp_ref, kwp_ref, vwp_ref, ovt_ref,
                bc_ref, bw_ref, bs_ref, bf_ref, og_ref, y_ref):
    i = pl.program_id(1)
    t0 = pl.multiple_of(i * TQ, TQ)
    n_blk = ovt_ref.shape[0]
    lane = lax.broadcasted_iota(jnp.int32, (TQ, LANES), 1)
    lo_half = lane < HEAD_DIM
    n_batch = q_ref.shape[0]

    def with_ones(v):
        return jnp.concatenate([v, jnp.ones_like(v)], axis=1)

    def near_part(bb):
        pieces = []
        for p in range(NSA_HPG):
            qs = q_ref[bb, :, p * LANES:(p + 1) * LANES]
            zero = jnp.zeros_like(qs)
            pieces += [jnp.where(lo_half, qs, zero), jnp.where(lo_half, zero, qs)]
        q8 = jnp.concatenate(pieces, axis=0)

        bc = bc_ref[0]
        lc = _dot_nt(q8, kcmp_ref[bb]) + bc
        ec = jnp.where(bc > MASKED_BELOW, jnp.exp2(lc - jnp.max(lc, axis=-1, keepdims=True)), 0.0)
        sc = jnp.sum(ec, axis=-1, keepdims=True)
        pc = ec / jnp.where(sc > 0.0, sc, 1.0)
        o_c = _dot(pc.astype(BF16), vcmp_ref[bb])

        blocks = [pc[r * TQ:(r + 1) * TQ] for r in range(NSA_HPG * NSA_KV)]
        pcs = jnp.concatenate([sum(blocks[g::NSA_KV]) for g in range(NSA_KV)], axis=0)
        pcs_hi = pcs.astype(BF16)
        pcs_lo = (pcs - pcs_hi.astype(F32)).astype(BF16)
        imp = _dot_nt(ovt_ref[...], pcs_hi) + _dot_nt(ovt_ref[...], pcs_lo)
        blk = lax.broadcasted_iota(jnp.int32, imp.shape, 0)
        forced = (blk == 0) | (blk == i) | (blk == i - 1)
        score = jnp.where(forced, SEL_FORCE, jnp.where(blk > i, -3e38, imp))
        rank = jnp.zeros(imp.shape, F32)
        for m in range(n_blk):
            row = score[m:m + 1, :]
            rank = rank + jnp.where(blk > m, jnp.where(row >= score, 1.0, 0.0), jnp.where(row > score, 1.0, 0.0))
        unsel = jnp.where(rank < N_SEL, 0.0, 1.0)
        unsel_far = jnp.where(blk >= i - WINDOW // SEL_L, 1.0, unsel)
        pad = jnp.zeros((LANES - 2 * n_blk, imp.shape[1]), F32)
        u_t = jnp.concatenate([unsel, unsel_far, pad], axis=0).T
        u_lane = lax.broadcasted_iota(jnp.int32, u_t.shape, 1)
        u_near = jnp.where(u_lane < n_blk, u_t, 0.0).astype(BF16)
        u_far = jnp.where(u_lane >= n_blk, u_t, 0.0).astype(BF16)
        qx_near = jnp.concatenate([q8, jnp.concatenate([u_near] * NSA_HPG, axis=0)], axis=1)
        qx_far = jnp.concatenate([q8, jnp.concatenate([u_far] * NSA_HPG, axis=0)], axis=1)

        lw = _dot_nt(q8, kwp_ref[bb, pl.ds(t0, NEAR), :]) + bw_ref[0]
        ew = jnp.exp2(lw - jnp.max(lw, axis=-1, keepdims=True))
        ow2 = _dot(ew.astype(BF16), with_ones(vwp_ref[bb, pl.ds(t0, NEAR), :]))
        o_w = ow2[:, :LANES] / ow2[:, LANES:]

        ls = _dot_nt(qx_near, ksx_ref[bb, pl.ds(t0, NEAR), :]) + bs_ref[0]
        m1 = jnp.max(ls, axis=-1, keepdims=True)
        e1 = jnp.exp2(ls - m1)
        acc1 = _dot(e1.astype(BF16), with_ones(vsp_ref[bb, pl.ds(t0, NEAR), :]))
        return o_c, o_w, qx_far, (m1, acc1)

    near = [near_part(bb) for bb in range(n_batch)]
    bfar = bf_ref[...]

    def far_step(kf, carry):
        base = pl.multiple_of(WINDOW + kf * FAR_TK, FAR_TK)
        new = []
        for bb in range(n_batch):
            m, acc = carry[bb]
            lf = _dot_nt(near[bb][2], ksx_ref[bb, pl.ds(base, FAR_TK), :]) + bfar
            m_new = jnp.maximum(m, jnp.max(lf, axis=-1, keepdims=True))
            alpha = jnp.exp2(m - m_new)
            e = jnp.exp2(lf - m_new)
            new.append((m_new, alpha * acc + _dot(e.astype(BF16), with_ones(vsp_ref[bb, pl.ds(base, FAR_TK), :]))))
        return tuple(new)

    n_far = (jnp.maximum(t0 - WINDOW, 0) + FAR_TK - 1) // FAR_TK
    far = lax.fori_loop(0, n_far, far_step, tuple(part[3] for part in near))

    for bb in range(n_batch):
        o_c, o_w = near[bb][0], near[bb][1]
        _, acc_s = far[bb]
        o_s = acc_s[:, :LANES] / acc_s[:, LANES:]
        gates = gates_ref[bb]

        def gate_col(j):
            cols = [gates[:, (g * NSA_HPG + p) * 3 + j:(g * NSA_HPG + p) * 3 + j + 1]
                    for p in range(NSA_HPG) for g in range(NSA_KV)]
            return jnp.concatenate(cols, axis=0)

        out = gate_col(0) * o_c + gate_col(1) * o_s + gate_col(2) * o_w
        slabs = [jnp.where(lo_half, out[(2 * p) * TQ:(2 * p + 1) * TQ], out[(2 * p + 1) * TQ:(2 * p + 2) * TQ])
                 for p in range(NSA_HPG)]
        y_ref[bb] = _rms(jnp.concatenate(slabs, axis=-1), og_ref[...]).astype(BF16)


def _nsa(q3, gates3, kcmp, vcmp, ksx, vsp, kwp, vwp, ovt, bias_c, bias_w, bias_s, bias_far, og):
    bsz, seq, _ = q3.shape
    n_chunk = kcmp.shape[1]
    n_var = bias_w.shape[0] - 1
    nb = NSA_NB if bsz % NSA_NB == 0 else 1
    full = lambda a: pl.BlockSpec(a.shape, lambda b, i: (0,) * a.ndim)
    per_b = lambda a: pl.BlockSpec((nb,) + a.shape[1:], lambda b, i: (b,) + (0,) * (a.ndim - 1))
    near = pl.BlockSpec((1, QROWS2, NEAR), lambda b, i: (jnp.minimum(i, n_var), 0, 0))
    return pl.pallas_call(
        _nsa_kernel,
        grid=(bsz // nb, seq // TQ),
        in_specs=[pl.BlockSpec((nb, TQ, NSA_WIDTH), lambda b, i: (b, i, 0)),
                  pl.BlockSpec((nb, TQ, LANES), lambda b, i: (b, i, 0)),
                  per_b(kcmp), per_b(vcmp), per_b(ksx), per_b(vsp), per_b(kwp), per_b(vwp),
                  full(ovt),
                  pl.BlockSpec((1, QROWS2, n_chunk), lambda b, i: (i, 0, 0)),
                  near, near, full(bias_far), full(og)],
        out_specs=pl.BlockSpec((nb, TQ, NSA_WIDTH), lambda b, i: (b, i, 0)),
        out_shape=jax.ShapeDtypeStruct((bsz, seq, NSA_WIDTH), BF16),
        compiler_params=_cparams(2),
    )(q3, gates3, kcmp, vcmp, ksx, vsp, kwp, vwp, ovt, bias_c, bias_w, bias_s, bias_far, og)


def _memkv_kernel(mem_ref, g_ref, wkv_ref, gk_ref, k_ref, v_ref):
    mn = _rms(mem_ref[0], g_ref[...]).astype(BF16)
    kv = _dot(mn, wkv_ref[...])
    for h in range(X_HEADS):
        sl = slice(h * X_HEAD_DIM, (h + 1) * X_HEAD_DIM)
        k_ref[0, :, sl] = _rms(kv[:, sl], gk_ref[...]).astype(BF16)
    v_ref[0] = kv[:, D_MODEL:].astype(BF16)


def _memkv(mem, g, wkv, gk):
    bsz, mlen, _ = mem.shape
    full = lambda a: pl.BlockSpec(a.shape, lambda b: (0,) * a.ndim)
    blk = pl.BlockSpec((1, mlen, D_MODEL), lambda b: (b, 0, 0))
    return pl.pallas_call(
        _memkv_kernel,
        grid=(bsz,),
        in_specs=[blk, full(g), full(wkv), full(gk)],
        out_specs=[blk, blk],
        out_shape=[jax.ShapeDtypeStruct((bsz, mlen, D_MODEL), BF16)] * 2,
        compiler_params=_cparams(1),
    )(mem, g, wkv, gk)


def _mid_kernel(x_ref, yrg_ref, ynsa_ref, woa_ref, wob_ref, gx_ref, wq_ref, gq_ref, k_ref, v_ref, wo_ref,
                gm_ref, wrh_ref, br_ref, h_ref, xt_ref, rw_ref, ri_ref, cnt_ref):
    h1 = x_ref[0] + _dot(yrg_ref[0], woa_ref[...]) + _dot(ynsa_ref[0], wob_ref[...])

    q = _dot(_rms(h1, gx_ref[...]).astype(BF16), wq_ref[...])
    heads = []
    for h in range(X_HEADS):
        sl = slice(h * X_HEAD_DIM, (h + 1) * X_HEAD_DIM)
        qh = _rms(q[:, sl], gq_ref[...]).astype(BF16)
        lg = _dot_nt(qh, k_ref[0, :, sl])
        e = jnp.exp2(lg - jnp.max(lg, axis=-1, keepdims=True))
        heads.append(_dot(e.astype(BF16), v_ref[0, :, sl]) / jnp.sum(e, axis=-1, keepdims=True))
    h2 = h1 + _dot(jnp.concatenate(heads, axis=-1).astype(BF16), wo_ref[...])
    h_ref[0] = h2

    xt = _rms(h2, gm_ref[...])
    _store_row_tiles(xt_ref, xt)
    xt_hi = xt.astype(BF16)
    xt_lo = (xt - xt_hi.astype(F32)).astype(BF16)
    hi2 = _dot(xt_hi, wrh_ref[...])
    lg = hi2[:, :LANES] + hi2[:, LANES:] + _dot(xt_lo, wrh_ref[:, :LANES]) + br_ref[...]
    lane = lax.broadcasted_iota(jnp.int32, lg.shape, 1)
    lane_f = lane.astype(F32)
    first_of = lambda hit: jnp.min(jnp.where(hit, lane_f, 1e9), axis=-1, keepdims=True)
    glog = jnp.where(lane < N_GROUPS, lg, -3e38)
    gmax = jnp.max(glog, axis=-1, keepdims=True)
    gsel = first_of(glog == gmax)
    p_g = 1.0 / jnp.sum(jnp.exp(glog - gmax), axis=-1, keepdims=True)
    lo = N_GROUPS + EXP_PER_GROUP * gsel
    el = jnp.where((lane_f >= lo) & (lane_f < lo + EXP_PER_GROUP), lg, -3e38)
    m_a = jnp.max(el, axis=-1, keepdims=True)
    i_a = first_of(el == m_a)
    el2 = jnp.where(lane_f == i_a, -3e38, el)
    m_b = jnp.max(el2, axis=-1, keepdims=True)
    i_b = first_of(el2 == m_b)
    r = jnp.exp(m_b - m_a)
    w_a = p_g / (1.0 + r)
    w_b = p_g * r / (1.0 + r)
    e_a = i_a - N_GROUPS
    e_b = i_b - N_GROUPS
    rw_ref[0] = jnp.where(lane == 0, w_a, jnp.where(lane == 1, w_b, 0.0))
    ri_ref[0] = jnp.where(lane == 0, e_a, jnp.where(lane == 1, e_b, 0.0)).astype(jnp.int32)

    @pl.when((pl.program_id(0) == 0) & (pl.program_id(1) == 0))
    def _():
        cnt_ref[...] = jnp.zeros_like(cnt_ref)

    hot = jnp.where((lane_f == e_a) | (lane_f == e_b), 1.0, 0.0)
    cnt_ref[...] += jnp.sum(hot, axis=0, keepdims=True)


def _mid(x, yrg, ynsa, woa, wob, gx, wq, gq, kx, vx, wo, gm, wrh, br):
    bsz, seq, _ = x.shape
    tm = min(TM_MID, seq)
    mlen = kx.shape[1]
    n_i = seq // tm
    full = lambda a: pl.BlockSpec(a.shape, lambda b, i: (0,) * a.ndim)
    tok = lambda w: pl.BlockSpec((1, tm, w), lambda b, i: (b, i, 0))
    memb = pl.BlockSpec((1, mlen, D_MODEL), lambda b, i: (b, 0, 0))
    xt_spec = pl.BlockSpec((tm * ROW_TILE, LANES), lambda b, i: (b * n_i + i, 0))
    return pl.pallas_call(
        _mid_kernel,
        grid=(bsz, seq // tm),
        in_specs=[tok(D_MODEL), tok(RG_WIDTH), tok(NSA_WIDTH), full(woa), full(wob), full(gx), full(wq), full(gq),
                  memb, memb, full(wo), full(gm), full(wrh), full(br)],
        out_specs=[tok(D_MODEL), xt_spec, tok(LANES), tok(LANES), pl.BlockSpec((1, LANES), lambda b, i: (0, 0))],
        out_shape=[jax.ShapeDtypeStruct((bsz, seq, D_MODEL), F32),
                   jax.ShapeDtypeStruct((bsz * seq * ROW_TILE, LANES), F32),
                   jax.ShapeDtypeStruct((bsz, seq, LANES), F32), jax.ShapeDtypeStruct((bsz, seq, LANES), jnp.int32),
                   jax.ShapeDtypeStruct((1, LANES), F32)],
        compiler_params=_cparams(2),
    )(x, yrg, ynsa, woa, wob, gx, wq, gq, kx, vx, wo, gm, wrh, br)


def _dest_kernel(ri_ref, pstart_ref, dest_ref, run_ref):
    @pl.when(pl.program_id(0) == 0)
    def _():
        run_ref[...] = jnp.zeros_like(run_ref)

    ri = ri_ref[...]
    tm = ri.shape[0]
    lane = lax.broadcasted_iota(jnp.int32, ri.shape, 1)
    e_a = ri[:, 0:1]
    e_b = ri[:, 1:2]
    hot_a = lane == e_a
    hot_b = lane == e_b
    hot = jnp.where(hot_a | hot_b, 1.0, 0.0)
    row = lax.broadcasted_iota(jnp.int32, (tm, tm), 0)
    col = lax.broadcasted_iota(jnp.int32, (tm, tm), 1)
    earlier = jnp.where(col < row, 1.0, 0.0).astype(BF16)
    base = _dot(earlier, hot.astype(BF16)) + run_ref[...] + pstart_ref[...]
    d_a = jnp.sum(jnp.where(hot_a, base, 0.0), axis=-1, keepdims=True)
    d_b = jnp.sum(jnp.where(hot_b, base, 0.0), axis=-1, keepdims=True)
    both = jnp.where(lane == 0, d_a, jnp.where(lane == 1, d_b, 0.0))
    dest_ref[...] = both.T[:DEST_ROWS, :].astype(jnp.int32)
    run_ref[...] += jnp.sum(hot, axis=0, keepdims=True)


def _dest(ri2, pstart):
    n_tok = ri2.shape[0]
    tm = min(TM_DEST, n_tok)
    return pl.pallas_call(
        _dest_kernel,
        grid=(n_tok // tm,),
        in_specs=[pl.BlockSpec((tm, LANES), lambda i: (i, 0)), pl.BlockSpec((1, LANES), lambda i: (0, 0))],
        out_specs=pl.BlockSpec((DEST_ROWS, tm), lambda i: (0, i)),
        out_shape=jax.ShapeDtypeStruct((DEST_ROWS, n_tok), jnp.int32),
        scratch_shapes=[pltpu.VMEM((1, LANES), F32)],
        compiler_params=_cparams(1),
    )(ri2, pstart)


def _store_row_tiles(ref, val):
    n = val.shape[0]
    for c in range(ROW_TILE):
        ref[pl.ds(c, n, stride=ROW_TILE), :] = val[:, c * LANES:(c + 1) * LANES]


def _load_row_tiles(ref, n):
    return [ref[pl.ds(c, n, stride=ROW_TILE), :] for c in range(ROW_TILE)]


def _token_rows(ref, t):
    return ref.at[pl.ds(pl.multiple_of(t * ROW_TILE, ROW_TILE), ROW_TILE), :]


def _dispatch_kernel(cnt_ref, pstart_ref, da_ref, db_ref, xt_ref, xs_ref, zrow, sem, zsem):
    tm = da_ref.shape[2]

    @pl.when(pl.program_id(0) == 0)
    def _():
        zrow[...] = jnp.zeros_like(zrow)

        def per_expert(e, c):
            used = cnt_ref[e]
            padded = (used + MOE_TB - 1) // MOE_TB * MOE_TB
            base = pstart_ref[e]

            def fill(r, c2):
                pltpu.make_async_copy(zrow, _token_rows(xs_ref, base + r), zsem).start()
                return c2

            def fill_done(r, c2):
                pltpu.make_async_copy(zrow, _token_rows(xs_ref, 0), zsem).wait()
                return c2

            lax.fori_loop(used, padded, fill, 0)
            lax.fori_loop(used, padded, fill_done, 0)
            return c

        lax.fori_loop(0, N_EXPERTS, per_expert, 0)

        last = N_EXPERTS - 1
        first_unused = (pstart_ref[last] + (cnt_ref[last] + MOE_TB - 1) // MOE_TB * MOE_TB) // MOE_TB

        def per_block(j, c):
            def fill(r, c2):
                pltpu.make_async_copy(zrow, _token_rows(xs_ref, j * MOE_TB + r), zsem).start()
                return c2

            def fill_done(r, c2):
                pltpu.make_async_copy(zrow, _token_rows(xs_ref, 0), zsem).wait()
                return c2

            lax.fori_loop(0, MOE_TB, fill, 0, unroll=DMA_UNROLL)
            lax.fori_loop(0, MOE_TB, fill_done, 0, unroll=DMA_UNROLL)
            return c

        lax.fori_loop(first_unused, xs_ref.shape[0] // (ROW_TILE * MOE_TB), per_block, 0)

    def issue(t, c):
        pltpu.make_async_copy(_token_rows(xt_ref, t), _token_rows(xs_ref, da_ref[0, 0, t]), sem).start(priority=0)
        pltpu.make_async_copy(_token_rows(xt_ref, t), _token_rows(xs_ref, db_ref[0, 0, t]), sem).start(priority=1)
        return c

    lax.fori_loop(0, tm, issue, 0, unroll=DMA_UNROLL)

    def drain(t, c):
        pltpu.make_async_copy(_token_rows(xt_ref, 0), _token_rows(xs_ref, 0), sem).wait()
        pltpu.make_async_copy(_token_rows(xt_ref, 0), _token_rows(xs_ref, 0), sem).wait()
        return c

    lax.fori_loop(0, tm, drain, 0, unroll=DMA_UNROLL)


def _dispatch(cnt, pstart, da, db, xt_rows, n_pad):
    n_tiles, _, tm = da.shape
    smem = pl.BlockSpec((1, 1, tm), lambda i, c, p: (i, 0, 0), memory_space=pltpu.SMEM)
    grid_spec = pltpu.PrefetchScalarGridSpec(
        num_scalar_prefetch=2,
        grid=(n_tiles,),
        in_specs=[smem, smem, pl.BlockSpec((tm * ROW_TILE, LANES), lambda i, c, p: (i, 0))],
        out_specs=pl.BlockSpec(memory_space=pl.ANY),
        scratch_shapes=[pltpu.VMEM((ROW_TILE, LANES), F32), pltpu.SemaphoreType.DMA(()),
                        pltpu.SemaphoreType.DMA(())],
    )
    return pl.pallas_call(
        _dispatch_kernel,
        grid_spec=grid_spec,
        out_shape=jax.ShapeDtypeStruct((n_pad * ROW_TILE, LANES), F32),
        compiler_params=pltpu.CompilerParams(dimension_semantics=("arbitrary",), has_side_effects=True,
                                             vmem_limit_bytes=VMEM_LIMIT),
    )(cnt, pstart, da, db, xt_rows)


def _ffn_kernel(bexp_ref, nused_ref, xs_ref, w1_ref, w3_ref, w2_ref, ys_ref):
    del bexp_ref
    j = pl.program_id(0)

    @pl.when(j < nused_ref[0])
    def _():
        xb = jnp.concatenate(_load_row_tiles(xs_ref, MOE_TB), axis=-1).astype(BF16)
        a = _dot(xb, w1_ref[0].astype(BF16))
        h = a * _sigmoid(a) * _dot(xb, w3_ref[0].astype(BF16))
        _store_row_tiles(ys_ref, _dot(h.astype(BF16), w2_ref[0].astype(BF16)))

    @pl.when(j >= nused_ref[0])
    def _():
        ys_ref[...] = jnp.zeros_like(ys_ref)


def _ffn(blk_exp, n_used, xs, w1, w3, w2):
    n_blocks = xs.shape[0] // (MOE_TB * ROW_TILE)
    rows = pl.BlockSpec((MOE_TB * ROW_TILE, LANES), lambda j, be, nu: (j, 0))
    used_rows = pl.BlockSpec((MOE_TB * ROW_TILE, LANES), lambda j, be, nu: (jnp.minimum(j, nu[0] - 1), 0))
    grid_spec = pltpu.PrefetchScalarGridSpec(
        num_scalar_prefetch=2,
        grid=(n_blocks,),
        in_specs=[used_rows,
                  pl.BlockSpec((1, D_MODEL, D_EXPERT), lambda j, be, nu: (be[j], 0, 0)),
                  pl.BlockSpec((1, D_MODEL, D_EXPERT), lambda j, be, nu: (be[j], 0, 0)),
                  pl.BlockSpec((1, D_EXPERT, D_MODEL), lambda j, be, nu: (be[j], 0, 0))],
        out_specs=rows,
    )
    return pl.pallas_call(
        _ffn_kernel,
        grid_spec=grid_spec,
        out_shape=jax.ShapeDtypeStruct(xs.shape, F32),
        compiler_params=_cparams(1),
    )(blk_exp, n_used, xs, w1, w3, w2)


def _combine_kernel(da_ref, db_ref, da_next_ref, db_next_ref, h_ref, rw_ref, ys_ref, o_ref, ya, yb, sems):
    tm = da_ref.shape[2]
    i = pl.program_id(0)
    slot = i % 2

    def start_gather(a_ref, b_ref, s):
        def issue(t, c):
            pltpu.make_async_copy(_token_rows(ys_ref, a_ref[0, 0, t]), _token_rows(ya.at[s], t),
                                  sems.at[s]).start(priority=0)
            pltpu.make_async_copy(_token_rows(ys_ref, b_ref[0, 0, t]), _token_rows(yb.at[s], t),
                                  sems.at[s]).start(priority=1)
            return c

        lax.fori_loop(0, tm, issue, 0, unroll=DMA_UNROLL)

    @pl.when(i == 0)
    def _():
        start_gather(da_ref, db_ref, slot)

    @pl.when(i + 1 < pl.num_programs(0))
    def _():
        start_gather(da_next_ref, db_next_ref, 1 - slot)

    def drain(t, c):
        pltpu.make_async_copy(_token_rows(ys_ref, 0), _token_rows(ya.at[slot], 0), sems.at[slot]).wait()
        pltpu.make_async_copy(_token_rows(ys_ref, 0), _token_rows(yb.at[slot], 0), sems.at[slot]).wait()
        return c

    lax.fori_loop(0, tm, drain, 0, unroll=DMA_UNROLL)
    rw = rw_ref[...]
    mix = [rw[:, 0:1] * a + rw[:, 1:2] * b
           for a, b in zip(_load_row_tiles(ya.at[slot], tm), _load_row_tiles(yb.at[slot], tm))]
    o_ref[...] = h_ref[...] + jnp.concatenate(mix, axis=-1)


def _combine(da, db, h2, rw, ys):
    n_tiles, _, tm = da.shape
    n_tok = h2.shape[0]
    smem = pl.BlockSpec((1, 1, tm), lambda i: (i, 0, 0), memory_space=pltpu.SMEM)
    smem_next = pl.BlockSpec((1, 1, tm), lambda i: (jnp.minimum(i + 1, n_tiles - 1), 0, 0), memory_space=pltpu.SMEM)
    row = lambda w: pl.BlockSpec((tm, w), lambda i: (i, 0))
    slots = pltpu.VMEM((2, tm * ROW_TILE, LANES), F32)
    return pl.pallas_call(
        _combine_kernel,
        grid=(n_tiles,),
        in_specs=[smem, smem, smem_next, smem_next, row(D_MODEL), row(LANES), pl.BlockSpec(memory_space=pl.ANY)],
        out_specs=row(D_MODEL),
        out_shape=jax.ShapeDtypeStruct((n_tok, D_MODEL), F32),
        scratch_shapes=[slots, slots, pltpu.SemaphoreType.DMA((2,))],
        compiler_params=_cparams(1),
    )(da, db, da, db, h2, rw, ys)


def _rel_bucket_np(dist):
    n = np.maximum(dist, 0)
    max_exact = NUM_BUCKETS // 2
    nf = np.maximum(n, 1).astype(np.float32)
    large = max_exact + (np.log(nf / max_exact) / math.log(MAX_DIST / max_exact)
                         * (NUM_BUCKETS - max_exact)).astype(np.int32)
    large = np.minimum(large, NUM_BUCKETS - 1)
    return np.where(n < max_exact, n, large).astype(np.int32)


def _toeplitz(vec, rows):
    width = vec.shape[-1] - 1
    flat = jnp.tile(vec, (1,) * (vec.ndim - 1) + (rows,))[..., :rows * width]
    return flat.reshape(vec.shape[:-1] + (rows, width))


def _bias_tables(rel_bias, seq):
    n_chunk = seq // CMP_STRIDE
    n_tiles = seq // TQ
    table = rel_bias.T.astype(F32)

    wide = NEAR + TQ
    k = np.arange(wide + 1)
    dw = np.where(k < NEAR, WINDOW - k, WINDOW + wide + 1 - k)
    used = (k < NEAR) | (k > wide + 1 - TQ)
    vals = table[:, _rel_bucket_np(dw)]

    n_var = WINDOW // TQ
    first_key = WINDOW - TQ * np.arange(n_var + 1)[:, None, None]
    in_seq = np.arange(NEAR)[None, None, :] >= first_key

    def near_tile(valid):
        t = _toeplitz(jnp.where(valid[None, :], vals, NEG_INF), TQ)[:, :, :NEAR]
        t = t.reshape(NSA_KV, NSA_HPG, TQ, NEAR).transpose(1, 0, 2, 3).reshape(1, QROWS2, NEAR)
        return jnp.where(in_seq, t, NEG_INF)

    bias_w = near_tile(used & (dw >= 0) & (dw < WINDOW))
    bias_s = near_tile(used & (dw >= 0))
    bias_far = table[:, NUM_BUCKETS - 1].reshape(NSA_KV, NSA_HPG, 1).transpose(1, 0, 2)
    bias_far = jnp.broadcast_to(bias_far, (NSA_HPG, NSA_KV, TQ)).reshape(QROWS2, 1)

    r = np.arange(CMP_STRIDE)[:, None]
    k = np.arange(2 * n_chunk + 1)[None, :]
    lag = 2 * n_chunk + 1 - k
    valid = (k > n_chunk + 1) & (CMP_STRIDE * lag + r >= CMP_L - 1)
    vals = table[:, _rel_bucket_np(CMP_STRIDE * lag + r - CMP_L // 2)]
    full = _toeplitz(jnp.where(valid[None], vals, NEG_INF), n_chunk)[..., :n_chunk]
    full = jnp.where(np.arange(n_chunk) < n_chunk - 1, full, NEG_INF)
    a4 = TQ // CMP_STRIDE
    full = full.reshape(NSA_KV, NSA_HPG, CMP_STRIDE, n_tiles, a4, n_chunk).transpose(3, 1, 0, 4, 2, 5)
    bias_c = full.reshape(n_tiles, QROWS2, n_chunk)
    return tuple(LOG2E * t for t in (bias_c, bias_w, bias_s, bias_far))


def _selection_tables(seq):
    n_chunk = seq // CMP_STRIDE
    n_blk = seq // SEL_L
    c = np.arange(n_chunk)
    n = np.arange(n_blk)
    start = c * CMP_STRIDE
    overlap_t = ((start[None, :] <= n[:, None] * SEL_L + SEL_L - 1) & (start[None, :] + CMP_L - 1 >= n[:, None] * SEL_L)
                 & (c < n_chunk - 1)[None, :])
    pos = np.arange(seq + WINDOW) - WINDOW
    lane_blk = np.arange(LANES) % n_blk
    hit = (pos[:, None] >= 0) & (pos[:, None] // SEL_L == lane_blk[None, :]) & (np.arange(LANES) < 2 * n_blk)[None, :]
    return jnp.asarray(overlap_t, BF16), jnp.asarray(np.where(hit, -UNSEL_PENALTY, 0.0), BF16)


def _block_ones(width, group):
    idx = np.arange(width) // group
    return jnp.asarray((idx[:, None] == idx[None, :]) / group, BF16)


def _block_diag(w):
    nb, n, m = w.shape
    eye = jnp.eye(nb, dtype=w.dtype)
    return jnp.einsum('hij,hg->higj', w, eye).reshape(nb * n, nb * m)


def _compress_weights(w1, w2, pos):
    half_l = CMP_L // 2
    parts = []
    for half in range(2):
        wh = w1[half * half_l * HEAD_DIM:(half + 1) * half_l * HEAD_DIM].reshape(half_l, HEAD_DIM, CMP_HIDDEN)
        z = jnp.zeros_like(wh)
        for g in range(NSA_KV):
            grp = [wh if gg == g else z for gg in range(NSA_KV)]
            parts.append(jnp.stack(grp, axis=1).reshape(half_l * KV_W, CMP_HIDDEN))
    w1cat = jnp.concatenate(parts, axis=1).astype(BF16)
    w2bd = _block_diag(jnp.stack([w2] * NSA_KV)).astype(BF16)
    prow = [jnp.tile(pos[half * half_l:(half + 1) * half_l][:, None, :], (1, NSA_KV, 1)).reshape(-1)
            for half in range(2)]
    pmat = jnp.zeros((8, half_l * KV_W), F32).at[0].set(prow[0]).at[1].set(prow[1]).astype(BF16)
    return w1cat, w2bd, pmat


def kernel(x, mem, rel_bias, norm_mix, w_in, rg_conv_w, rg_conv_b, rg_w_r, rg_b_r, rg_w_i, rg_b_i, rg_lambda, nsa_g_q, nsa_g_kc, nsa_g_ks, nsa_g_kw, cmp_pos_k, cmp_pos_v, cmp_k_w1, cmp_k_w2, cmp_v_w1, cmp_v_w2, out_g_rg, out_g_nsa, w_out, norm_x, norm_mem, xa_w_q, xa_w_kv, xa_w_o, xa_g_q, xa_g_k, norm_moe, router_g_w, router_g_b, router_e_w, router_e_b, exp_w1, exp_w3, exp_w2):
    bsz, seq, _ = x.shape
    n_tok = bsz * seq
    assert seq % FAR_TK == 0 and 2 * (seq // SEL_L) <= LANES and norm_mix.shape[0] == 1
    l = 0
    row = lambda v: v.reshape(1, -1).astype(F32)

    perm = np.array([(half * NSA_HPG + p) * HEAD_DIM + d
                     for p in range(NSA_HPG) for half in range(NSA_KV) for d in range(HEAD_DIM)])
    offs = np.cumsum([0, RG_WIDTH, RG_WIDTH, NSA_WIDTH] + [KV_W] * 6)
    w = w_in[l]
    wrg = w[:, :offs[2]].astype(BF16)
    wq = w[:, offs[2]:offs[3]][:, perm].astype(BF16)
    wkv = w[:, offs[3]:offs[9]].astype(BF16)
    wgl = jnp.pad(w[:, offs[9]:], ((0, 0), (0, LANES - 3 * NSA_HEADS))).astype(BF16)
    ones64 = _block_ones(NSA_WIDTH, HEAD_DIM)
    gq = row(jnp.tile(nsa_g_q[l], NSA_HEADS) * (HEAD_DIM ** -0.5 * LOG2E))
    u, gate, q, kc, vc, ks, vs, kw, vw, gates = _inproj(
        x.reshape(n_tok, D_MODEL), row(norm_mix[l]), wrg, wq, wkv, wgl, gq,
        row(jnp.tile(nsa_g_ks[l], NSA_KV)), row(jnp.tile(nsa_g_kw[l], NSA_KV)), ones64)

    wg = (0.5 * jnp.concatenate([_block_diag(rg_w_r[l]), _block_diag(rg_w_i[l])], axis=1)).astype(BF16)
    bg = 0.5 * jnp.concatenate([rg_b_r[l], rg_b_i[l]]).reshape(1, -1)
    y_rg = _rglru(u.reshape(bsz, seq, RG_WIDTH), gate.reshape(bsz, seq, RG_WIDTH),
                  rg_conv_w[l].reshape(CONV_W, RG_WIDTH), row(rg_conv_b[l]), wg, bg, row(rg_lambda[l]),
                  row(out_g_rg[l]))

    n_chunk = seq // CMP_STRIDE
    w1k, w2k, pk = _compress_weights(cmp_k_w1[l], cmp_k_w2[l], cmp_pos_k[l])
    w1v, w2v, pv = _compress_weights(cmp_v_w1[l], cmp_v_w2[l], cmp_pos_v[l])
    kcmp, vcmp = _compress(kc.reshape(bsz, n_chunk, CMP_STRIDE * KV_W), vc.reshape(bsz, n_chunk, CMP_STRIDE * KV_W),
                           w1k, w2k, pk, w1v, w2v, pv, row(jnp.tile(nsa_g_kc[l], NSA_KV)),
                           ones64[:KV_W, :KV_W])
    padw = lambda t: jnp.pad(t.reshape(bsz, seq, KV_W), ((0, 0), (WINDOW, 0), (0, 0)))
    bias_c, bias_w, bias_s, bias_far = _bias_tables(rel_bias, seq)
    overlap_t, penalty = _selection_tables(seq)
    ksx = jnp.concatenate([padw(ks), jnp.broadcast_to(penalty, (bsz,) + penalty.shape)], axis=-1)
    y_nsa = _nsa(q.reshape(bsz, seq, NSA_WIDTH), gates.reshape(bsz, seq, LANES), kcmp, vcmp,
                 ksx, padw(vs), padw(kw), padw(vw), overlap_t, bias_c, bias_w, bias_s, bias_far,
                 row(out_g_nsa[l][perm]))

    kx, vx = _memkv(mem, row(norm_mem[l]), xa_w_kv[l].astype(BF16), row(xa_g_k[l]))
    wo_mix = w_out[l]
    wr = jnp.pad(jnp.concatenate([router_g_w[l], router_e_w[l]], axis=1),
                 ((0, 0), (0, LANES - N_GROUPS - N_EXPERTS)))
    wr_hi = wr.astype(BF16)
    br = jnp.pad(jnp.concatenate([router_g_b[l], router_e_b[l]]), (0, LANES - N_GROUPS - N_EXPERTS)).reshape(1, -1)
    h2, xt, rw, ri, counts = _mid(
        x, y_rg, y_nsa, wo_mix[:RG_WIDTH].astype(BF16), wo_mix[RG_WIDTH:][perm].astype(BF16), row(norm_x[l]),
        xa_w_q[l].astype(BF16), row(xa_g_q[l] * (X_HEAD_DIM ** -0.5 * LOG2E)), kx, vx, xa_w_o[l].astype(BF16),
        row(norm_moe[l]), jnp.concatenate([wr_hi, (wr - wr_hi.astype(F32)).astype(BF16)], axis=1), br)

    n_slots = 2 * n_tok
    n_blocks = n_slots // MOE_TB + N_EXPERTS
    n_pad = n_blocks * MOE_TB
    cnt = counts[0, :N_EXPERTS].astype(jnp.int32)
    pcnt = (cnt + MOE_TB - 1) // MOE_TB * MOE_TB
    pends = jnp.cumsum(pcnt)
    pstart = jnp.pad((pends - pcnt).astype(F32), (0, LANES - N_EXPERTS)).reshape(1, LANES)
    blk_exp = jnp.minimum(jnp.sum(pends[None, :] <= jnp.arange(n_blocks, dtype=jnp.int32)[:, None] * MOE_TB, axis=1),
                          N_EXPERTS - 1).astype(jnp.int32)
    n_used = (pends[-1:] // MOE_TB).astype(jnp.int32)
    dest = _dest(ri.reshape(n_tok, LANES), pstart)
    tmd = min(TM_DMA, n_tok)
    da = dest[0].reshape(n_tok // tmd, 1, tmd)
    db = dest[1].reshape(n_tok // tmd, 1, tmd)
    xs = _dispatch(cnt, (pends - pcnt).astype(jnp.int32), da, db, xt, n_pad)
    ys = _ffn(blk_exp, n_used, xs, exp_w1[l], exp_w3[l], exp_w2[l])
    out = _combine(da, db, h2.reshape(n_tok, D_MODEL), rw.reshape(n_tok, LANES), ys)
    return out.reshape(bsz, seq, D_MODEL)
```

```python
import math

import numpy as np
import jax
import jax.numpy as jnp
from jax import lax
from jax.experimental import pallas as pl
from jax.experimental.pallas import tpu as pltpu

F32 = jnp.float32
BF16 = jnp.bfloat16

D_MODEL = 1024
RG_WIDTH = 512
RG_BLOCKS = 8
RG_BLOCK = 64
CONV_W = 4
RG_C = 8.0
NSA_WIDTH = 512
NSA_HEADS = 8
HEAD_DIM = 64
NSA_KV = 2
NSA_HPG = 4
KV_W = 128
CMP_L = 32
CMP_STRIDE = 16
CMP_HIDDEN = 256
SEL_L = 64
N_SEL = 8
WINDOW = 512
NUM_BUCKETS = 32
MAX_DIST = 128
X_HEADS = 4
X_HEAD_DIM = 256
N_GROUPS = 4
EXP_PER_GROUP = 8
N_EXPERTS = 32
D_EXPERT = 512
EPS = 1e-6
LOG2E = 1.0 / math.log(2.0)
NEG_INF = -1e30
MASKED_BELOW = -1e29
SEL_FORCE = 1e9
LANES = 128

TQ = 64
NEAR = WINDOW + TQ
FAR_TK = 512
QROWS2 = NSA_HEADS * TQ
UNSEL_PENALTY = 2.0 ** 100
NSA_NB = 4

TM_PROJ = 1024
TM_MID = 1024
TM_DEST = 512
DEST_ROWS = 8
TM_DMA = 512
DMA_UNROLL = 8
MOE_TB = 512
ROW_TILE = D_MODEL // LANES
RG_CHUNK = 256
SCAN_ROWS = 8
SCAN_UNROLL = 8
VMEM_LIMIT = 56 * 1024 * 1024


def _cparams(n_axes):
    return pltpu.CompilerParams(dimension_semantics=("arbitrary",) * n_axes,
                                vmem_limit_bytes=VMEM_LIMIT)


def _dot(a, b):
    return jnp.dot(a, b, preferred_element_type=F32)


def _dot_nt(a, b):
    return lax.dot_general(a, b, (((1,), (1,)), ((), ())), preferred_element_type=F32)


def _gelu_tanh(x):
    c = math.sqrt(2.0 / math.pi)
    half = 0.5 * x
    return half + half * jnp.tanh(x * (c + (c * 0.044715) * (x * x)))


def _sigmoid(x):
    return 0.5 * jnp.tanh(0.5 * x) + 0.5


def _rms(x, g):
    return x * lax.rsqrt(jnp.mean(x * x, axis=-1, keepdims=True) + EPS) * g


def _group_rms(x, ones_blk, g):
    ms = _dot((x * x).astype(BF16), ones_blk)
    return x * lax.rsqrt(ms + EPS) * g


def _inproj_kernel(x_ref, g_ref, wrg_ref, wq_ref, wkv_ref, wgl_ref, gq_ref, gks_ref, gkw_ref, ones_ref,
                   u_ref, gate_ref, q_ref, kc_ref, vc_ref, ks_ref, vs_ref, kw_ref, vw_ref, gates_ref):
    xb = _rms(x_ref[...], g_ref[...]).astype(BF16)
    rg = _dot(xb, wrg_ref[...])
    u_ref[...] = rg[:, :RG_WIDTH].astype(BF16)
    gate_ref[...] = rg[:, RG_WIDTH:].astype(BF16)
    q = _dot(xb, wq_ref[...])
    q_ref[...] = _group_rms(q, ones_ref[...], gq_ref[...]).astype(BF16)
    kv = _dot(xb, wkv_ref[...])
    ones_kv = ones_ref[:KV_W, :KV_W]
    kc_ref[...] = kv[:, 0 * KV_W:1 * KV_W].astype(BF16)
    vc_ref[...] = kv[:, 1 * KV_W:2 * KV_W].astype(BF16)
    ks_ref[...] = _group_rms(kv[:, 2 * KV_W:3 * KV_W], ones_kv, gks_ref[...]).astype(BF16)
    vs_ref[...] = kv[:, 3 * KV_W:4 * KV_W].astype(BF16)
    kw_ref[...] = _group_rms(kv[:, 4 * KV_W:5 * KV_W], ones_kv, gkw_ref[...]).astype(BF16)
    vw_ref[...] = kv[:, 5 * KV_W:6 * KV_W].astype(BF16)
    gates_ref[...] = _sigmoid(_dot(xb, wgl_ref[...]))


def _inproj(x2, g, wrg, wq, wkv, wgl, gq, gks, gkw, ones_blk):
    n_tok = x2.shape[0]
    tm = min(TM_PROJ, n_tok)
    full = lambda a: pl.BlockSpec(a.shape, lambda i: (0,) * a.ndim)
    row = lambda w: pl.BlockSpec((tm, w), lambda i: (i, 0))
    outs = [(RG_WIDTH, BF16), (RG_WIDTH, BF16), (NSA_WIDTH, BF16)] + [(KV_W, BF16)] * 6 + [(LANES, F32)]
    return pl.pallas_call(
        _inproj_kernel,
        grid=(n_tok // tm,),
        in_specs=[row(D_MODEL)] + [full(a) for a in (g, wrg, wq, wkv, wgl, gq, gks, gkw, ones_blk)],
        out_specs=[row(w) for w, _ in outs],
        out_shape=[jax.ShapeDtypeStruct((n_tok, w), dt) for w, dt in outs],
        compiler_params=_cparams(1),
    )(x2, g, wrg, wq, wkv, wgl, gq, gks, gkw, ones_blk)


def _rglru_kernel(u_ref, gate_ref, cw_ref, cb_ref, wg_ref, bg_ref, lam_ref, og_ref, y_ref, upad, a_s, h_s):
    seq = u_ref.shape[1]
    upad[0:8, :] = jnp.zeros((8, RG_WIDTH), F32)
    upad[8:8 + seq, :] = u_ref[0].astype(F32)
    neg_lam = -lam_ref[...]
    softplus = jnp.maximum(neg_lam, 0.0) + jnp.log(1.0 + jnp.exp(-jnp.abs(neg_lam)))
    log2_a_half = (-0.5 * RG_C * LOG2E) * softplus
    ch = min(RG_CHUNK, seq)
    for c in range(seq // ch):
        r0 = c * ch
        uc = cb_ref[...]
        for k in range(CONV_W):
            off = 8 + r0 - (CONV_W - 1) + k
            uc = uc + cw_ref[k:k + 1, :] * upad[off:off + ch, :]
        th = jnp.tanh(_dot(uc.astype(BF16), wg_ref[...]) + bg_ref[...])
        a = jnp.exp2(log2_a_half * th[:, :RG_WIDTH] + log2_a_half)
        a_s[r0:r0 + ch, :] = a
        s = 1.0 - a * a
        h_s[r0:r0 + ch, :] = s * lax.rsqrt(jnp.maximum(s, 1e-30)) * (0.5 * th[:, RG_WIDTH:] + 0.5) * uc

    row = lax.broadcasted_iota(jnp.int32, (SCAN_ROWS, RG_WIDTH), 0)

    def block(j, h_prev):
        rows = pl.ds(pl.multiple_of(j * SCAN_ROWS, SCAN_ROWS), SCAN_ROWS)
        a = a_s[rows, :]
        b = h_s[rows, :]
        k = 1
        while k < SCAN_ROWS:
            keep = row >= k
            b = jnp.where(keep, a * pltpu.roll(b, k, 0) + b, b)
            a = jnp.where(keep, a * pltpu.roll(a, k, 0), a)
            k *= 2
        h = a * h_prev + b
        h_s[rows, :] = h
        return h[SCAN_ROWS - 1:SCAN_ROWS, :]

    lax.fori_loop(0, seq // SCAN_ROWS, block, jnp.zeros((1, RG_WIDTH), F32), unroll=SCAN_UNROLL)

    for c in range(seq // ch):
        r0 = c * ch
        y = _gelu_tanh(gate_ref[0, r0:r0 + ch, :].astype(F32)) * h_s[r0:r0 + ch, :]
        y_ref[0, r0:r0 + ch, :] = _rms(y, og_ref[...]).astype(BF16)


def _rglru(u3, gate3, cw, cb, wg, bg, lam, og):
    bsz, seq, _ = u3.shape
    full = lambda a: pl.BlockSpec(a.shape, lambda b: (0,) * a.ndim)
    blk = pl.BlockSpec((1, seq, RG_WIDTH), lambda b: (b, 0, 0))
    return pl.pallas_call(
        _rglru_kernel,
        grid=(bsz,),
        in_specs=[blk, blk] + [full(a) for a in (cw, cb, wg, bg, lam, og)],
        out_specs=blk,
        out_shape=jax.ShapeDtypeStruct((bsz, seq, RG_WIDTH), BF16),
        scratch_shapes=[pltpu.VMEM((seq + 8, RG_WIDTH), F32), pltpu.VMEM((seq, RG_WIDTH), F32),
                        pltpu.VMEM((seq, RG_WIDTH), F32)],
        compiler_params=_cparams(1),
    )(u3, gate3, cw, cb, wg, bg, lam, og)


def _compress_kernel(kx_ref, vx_ref, w1k_ref, w2k_ref, pk_ref, w1v_ref, w2v_ref, pv_ref, gk_ref, ones_ref,
                     ko_ref, vo_ref):
    n_chunk = kx_ref.shape[1]
    half = NSA_KV * CMP_HIDDEN

    def mlp(x_ref, w1_ref, w2_ref, p_ref):
        ab = _dot(x_ref[0], w1_ref[...])
        pos = _dot(p_ref[...], w1_ref[...])
        hid = ab[:, :half] + pltpu.roll(ab[:, half:], n_chunk - 1, 0) + (pos[0:1, :half] + pos[1:2, half:])
        return _dot(_gelu_tanh(hid).astype(BF16), w2_ref[...])

    kc = mlp(kx_ref, w1k_ref, w2k_ref, pk_ref)
    ko_ref[0] = _group_rms(kc, ones_ref[...], gk_ref[...]).astype(BF16)
    vo_ref[0] = mlp(vx_ref, w1v_ref, w2v_ref, pv_ref).astype(BF16)


def _compress(kx, vx, w1k, w2k, pk, w1v, w2v, pv, gk, ones_kv):
    bsz, n_chunk, width = kx.shape
    full = lambda a: pl.BlockSpec(a.shape, lambda b: (0,) * a.ndim)
    xin = pl.BlockSpec((1, n_chunk, width), lambda b: (b, 0, 0))
    out = pl.BlockSpec((1, n_chunk, KV_W), lambda b: (b, 0, 0))
    return pl.pallas_call(
        _compress_kernel,
        grid=(bsz,),
        in_specs=[xin, xin] + [full(a) for a in (w1k, w2k, pk, w1v, w2v, pv, gk, ones_kv)],
        out_specs=[out, out],
        out_shape=[jax.ShapeDtypeStruct((bsz, n_chunk, KV_W), BF16)] * 2,
        compiler_params=_cparams(1),
    )(kx, vx, w1k, w2k, pk, w1v, w2v, pv, gk, ones_kv)


def _nsa_kernel(q_ref, gates_ref, kcmp_ref, vcmp_ref, ksx_ref, vsp_ref, kwp_ref, vwp_ref, ovt_ref,
                bc_ref, bw_ref, bs_ref, bf_ref, og_ref, y_ref):
    i = pl.program_id(1)
    t0 = pl.multiple_of(i * TQ, TQ)
    n_blk = ovt_ref.shape[0]
    lane = lax.broadcasted_iota(jnp.int32, (TQ, LANES), 1)
    lo_half = lane < HEAD_DIM
    n_batch = q_ref.shape[0]

    def with_ones(v):
        return jnp.concatenate([v, jnp.ones_like(v)], axis=1)

    def near_part(bb):
        pieces = []
        for p in range(NSA_HPG):
            qs = q_ref[bb, :, p * LANES:(p + 1) * LANES]
            zero = jnp.zeros_like(qs)
            pieces += [jnp.where(lo_half, qs, zero), jnp.where(lo_half, zero, qs)]
        q8 = jnp.concatenate(pieces, axis=0)

        bc = bc_ref[0]
        lc = _dot_nt(q8, kcmp_ref[bb]) + bc
        ec = jnp.where(bc > MASKED_BELOW, jnp.exp2(lc - jnp.max(lc, axis=-1, keepdims=True)), 0.0)
        sc = jnp.sum(ec, axis=-1, keepdims=True)
        pc = ec / jnp.where(sc > 0.0, sc, 1.0)
        o_c = _dot(pc.astype(BF16), vcmp_ref[bb])

        blocks = [pc[r * TQ:(r + 1) * TQ] for r in range(NSA_HPG * NSA_KV)]
        pcs = jnp.concatenate([sum(blocks[g::NSA_KV]) for g in range(NSA_KV)], axis=0)
        pcs_hi = pcs.astype(BF16)
        pcs_lo = (pcs - pcs_hi.astype(F32)).astype(BF16)
        imp = _dot_nt(ovt_ref[...], pcs_hi) + _dot_nt(ovt_ref[...], pcs_lo)
        blk = lax.broadcasted_iota(jnp.int32, imp.shape, 0)
        forced = (blk == 0) | (blk == i) | (blk == i - 1)
        score = jnp.where(forced, SEL_FORCE, jnp.where(blk > i, -3e38, imp))
        rank = jnp.zeros(imp.shape, F32)
        for m in range(n_blk):
            row = score[m:m + 1, :]
            rank = rank + jnp.where(blk > m, jnp.where(row >= score, 1.0, 0.0), jnp.where(row > score, 1.0, 0.0))
        unsel = jnp.where(rank < N_SEL, 0.0, 1.0)
        unsel_far = jnp.where(blk >= i - WINDOW // SEL_L, 1.0, unsel)
        pad = jnp.zeros((LANES - 2 * n_blk, imp.shape[1]), F32)
        u_t = jnp.concatenate([unsel, unsel_far, pad], axis=0).T
        u_lane = lax.broadcasted_iota(jnp.int32, u_t.shape, 1)
        u_near = jnp.where(u_lane < n_blk, u_t, 0.0).astype(BF16)
        u_far = jnp.where(u_lane >= n_blk, u_t, 0.0).astype(BF16)
        qx_near = jnp.concatenate([q8, jnp.concatenate([u_near] * NSA_HPG, axis=0)], axis=1)
        qx_far = jnp.concatenate([q8, jnp.concatenate([u_far] * NSA_HPG, axis=0)], axis=1)

        lw = _dot_nt(q8, kwp_ref[bb, pl.ds(t0, NEAR), :]) + bw_ref[0]
        ew = jnp.exp2(lw - jnp.max(lw, axis=-1, keepdims=True))
        ow2 = _dot(ew.astype(BF16), with_ones(vwp_ref[bb, pl.ds(t0, NEAR), :]))
        o_w = ow2[:, :LANES] / ow2[:, LANES:]

        ls = _dot_nt(qx_near, ksx_ref[bb, pl.ds(t0, NEAR), :]) + bs_ref[0]
        m1 = jnp.max(ls, axis=-1, keepdims=True)
        return o_c, o_w, qx_far, (m1, jnp.zeros((QROWS2, 2 * LANES), F32)), ls

    near = [near_part(bb) for bb in range(n_batch)]
    bfar = bf_ref[...]

    def far_step(kf, carry):
        base = pl.multiple_of(WINDOW + kf * FAR_TK, FAR_TK)
        new = []
        for bb in range(n_batch):
            m, acc = carry[bb]
            lf = _dot_nt(near[bb][2], ksx_ref[bb, pl.ds(base, FAR_TK), :]) + bfar
            m_new = jnp.maximum(m, jnp.max(lf, axis=-1, keepdims=True))
            alpha = jnp.exp2(m - m_new)
            e = jnp.exp2(lf - m_new)
            new.append((m_new, alpha * acc + _dot(e.astype(BF16), with_ones(vsp_ref[bb, pl.ds(base, FAR_TK), :]))))
        return tuple(new)

    n_far = (jnp.maximum(t0 - WINDOW, 0) + FAR_TK - 1) // FAR_TK
    far = lax.fori_loop(0, n_far, far_step, tuple(part[3] for part in near))

    for bb in range(n_batch):
        o_c, o_w = near[bb][0], near[bb][1]
        m_all, acc_far = far[bb]
        e1 = jnp.exp2(near[bb][4] - m_all)
        acc_s = acc_far + _dot(e1.astype(BF16), with_ones(vsp_ref[bb, pl.ds(t0, NEAR), :]))
        o_s = acc_s[:, :LANES] / acc_s[:, LANES:]
        gates = gates_ref[bb]

        def gate_col(j):
            cols = [gates[:, (g * NSA_HPG + p) * 3 + j:(g * NSA_HPG + p) * 3 + j + 1]
                    for p in range(NSA_HPG) for g in range(NSA_KV)]
            return jnp.concatenate(cols, axis=0)

        out = gate_col(0) * o_c + gate_col(1) * o_s + gate_col(2) * o_w
        slabs = [jnp.where(lo_half, out[(2 * p) * TQ:(2 * p + 1) * TQ], out[(2 * p + 1) * TQ:(2 * p + 2) * TQ])
                 for p in range(NSA_HPG)]
        y_ref[bb] = _rms(jnp.concatenate(slabs, axis=-1), og_ref[...]).astype(BF16)


def _nsa(q3, gates3, kcmp, vcmp, ksx, vsp, kwp, vwp, ovt, bias_c, bias_w, bias_s, bias_far, og):
    bsz, seq, _ = q3.shape
    n_chunk = kcmp.shape[1]
    n_var = bias_w.shape[0] - 1
    nb = NSA_NB if bsz % NSA_NB == 0 else 1
    full = lambda a: pl.BlockSpec(a.shape, lambda b, i: (0,) * a.ndim)
    per_b = lambda a: pl.BlockSpec((nb,) + a.shape[1:], lambda b, i: (b,) + (0,) * (a.ndim - 1))
    near = pl.BlockSpec((1, QROWS2, NEAR), lambda b, i: (jnp.minimum(i, n_var), 0, 0))
    return pl.pallas_call(
        _nsa_kernel,
        grid=(bsz // nb, seq // TQ),
        in_specs=[pl.BlockSpec((nb, TQ, NSA_WIDTH), lambda b, i: (b, i, 0)),
                  pl.BlockSpec((nb, TQ, LANES), lambda b, i: (b, i, 0)),
                  per_b(kcmp), per_b(vcmp), per_b(ksx), per_b(vsp), per_b(kwp), per_b(vwp),
                  full(ovt),
                  pl.BlockSpec((1, QROWS2, n_chunk), lambda b, i: (i, 0, 0)),
                  near, near, full(bias_far), full(og)],
        out_specs=pl.BlockSpec((nb, TQ, NSA_WIDTH), lambda b, i: (b, i, 0)),
        out_shape=jax.ShapeDtypeStruct((bsz, seq, NSA_WIDTH), BF16),
        compiler_params=_cparams(2),
    )(q3, gates3, kcmp, vcmp, ksx, vsp, kwp, vwp, ovt, bias_c, bias_w, bias_s, bias_far, og)


def _memkv_kernel(mem_ref, g_ref, wkv_ref, gk_ref, k_ref, v_ref):
    mn = _rms(mem_ref[0], g_ref[...]).astype(BF16)
    kv = _dot(mn, wkv_ref[...])
    for h in range(X_HEADS):
        sl = slice(h * X_HEAD_DIM, (h + 1) * X_HEAD_DIM)
        k_ref[0, :, sl] = _rms(kv[:, sl], gk_ref[...]).astype(BF16)
    v_ref[0] = kv[:, D_MODEL:].astype(BF16)


def _memkv(mem, g, wkv, gk):
    bsz, mlen, _ = mem.shape
    full = lambda a: pl.BlockSpec(a.shape, lambda b: (0,) * a.ndim)
    blk = pl.BlockSpec((1, mlen, D_MODEL), lambda b: (b, 0, 0))
    return pl.pallas_call(
        _memkv_kernel,
        grid=(bsz,),
        in_specs=[blk, full(g), full(wkv), full(gk)],
        out_specs=[blk, blk],
        out_shape=[jax.ShapeDtypeStruct((bsz, mlen, D_MODEL), BF16)] * 2,
        compiler_params=_cparams(1),
    )(mem, g, wkv, gk)


def _mid_kernel(x_ref, yrg_ref, ynsa_ref, woa_ref, wob_ref, gx_ref, wq_ref, gq_ref, k_ref, v_ref, wo_ref,
                gm_ref, wrh_ref, br_ref, h_ref, xt_ref, rw_ref, ri_ref, cnt_ref):
    h1 = x_ref[0] + _dot(yrg_ref[0], woa_ref[...]) + _dot(ynsa_ref[0], wob_ref[...])

    q = _dot(_rms(h1, gx_ref[...]).astype(BF16), wq_ref[...])
    heads = []
    for h in range(X_HEADS):
        sl = slice(h * X_HEAD_DIM, (h + 1) * X_HEAD_DIM)
        qh = _rms(q[:, sl], gq_ref[...]).astype(BF16)
        lg = _dot_nt(qh, k_ref[0, :, sl])
        e = jnp.exp2(lg - jnp.max(lg, axis=-1, keepdims=True))
        heads.append(_dot(e.astype(BF16), v_ref[0, :, sl]) / jnp.sum(e, axis=-1, keepdims=True))
    h2 = h1 + _dot(jnp.concatenate(heads, axis=-1).astype(BF16), wo_ref[...])
    h_ref[0] = h2

    xt = _rms(h2, gm_ref[...])
    _store_row_tiles(xt_ref, xt)
    xt_hi = xt.astype(BF16)
    xt_lo = (xt - xt_hi.astype(F32)).astype(BF16)
    hi2 = _dot(xt_hi, wrh_ref[...])
    lg = hi2[:, :LANES] + hi2[:, LANES:] + _dot(xt_lo, wrh_ref[:, :LANES]) + br_ref[...]
    lane = lax.broadcasted_iota(jnp.int32, lg.shape, 1)
    lane_f = lane.astype(F32)
    first_of = lambda hit: jnp.min(jnp.where(hit, lane_f, 1e9), axis=-1, keepdims=True)
    glog = jnp.where(lane < N_GROUPS, lg, -3e38)
    gmax = jnp.max(glog, axis=-1, keepdims=True)
    gsel = first_of(glog == gmax)
    p_g = 1.0 / jnp.sum(jnp.exp(glog - gmax), axis=-1, keepdims=True)
    lo = N_GROUPS + EXP_PER_GROUP * gsel
    el = jnp.where((lane_f >= lo) & (lane_f < lo + EXP_PER_GROUP), lg, -3e38)
    m_a = jnp.max(el, axis=-1, keepdims=True)
    i_a = first_of(el == m_a)
    el2 = jnp.where(lane_f == i_a, -3e38, el)
    m_b = jnp.max(el2, axis=-1, keepdims=True)
    i_b = first_of(el2 == m_b)
    r = jnp.exp(m_b - m_a)
    w_a = p_g / (1.0 + r)
    w_b = p_g * r / (1.0 + r)
    e_a = i_a - N_GROUPS
    e_b = i_b - N_GROUPS
    rw_ref[0] = jnp.where(lane == 0, w_a, jnp.where(lane == 1, w_b, 0.0))
    ri_ref[0] = jnp.where(lane == 0, e_a, jnp.where(lane == 1, e_b, 0.0)).astype(jnp.int32)

    @pl.when((pl.program_id(0) == 0) & (pl.program_id(1) == 0))
    def _():
        cnt_ref[...] = jnp.zeros_like(cnt_ref)

    hot = jnp.where((lane_f == e_a) | (lane_f == e_b), 1.0, 0.0)
    cnt_ref[...] += jnp.sum(hot, axis=0, keepdims=True)


def _mid(x, yrg, ynsa, woa, wob, gx, wq, gq, kx, vx, wo, gm, wrh, br):
    bsz, seq, _ = x.shape
    tm = min(TM_MID, seq)
    mlen = kx.shape[1]
    n_i = seq // tm
    full = lambda a: pl.BlockSpec(a.shape, lambda b, i: (0,) * a.ndim)
    tok = lambda w: pl.BlockSpec((1, tm, w), lambda b, i: (b, i, 0))
    memb = pl.BlockSpec((1, mlen, D_MODEL), lambda b, i: (b, 0, 0))
    xt_spec = pl.BlockSpec((tm * ROW_TILE, LANES), lambda b, i: (b * n_i + i, 0))
    return pl.pallas_call(
        _mid_kernel,
        grid=(bsz, seq // tm),
        in_specs=[tok(D_MODEL), tok(RG_WIDTH), tok(NSA_WIDTH), full(woa), full(wob), full(gx), full(wq), full(gq),
                  memb, memb, full(wo), full(gm), full(wrh), full(br)],
        out_specs=[tok(D_MODEL), xt_spec, tok(LANES), tok(LANES), pl.BlockSpec((1, LANES), lambda b, i: (0, 0))],
        out_shape=[jax.ShapeDtypeStruct((bsz, seq, D_MODEL), F32),
                   jax.ShapeDtypeStruct((bsz * seq * ROW_TILE, LANES), F32),
                   jax.ShapeDtypeStruct((bsz, seq, LANES), F32), jax.ShapeDtypeStruct((bsz, seq, LANES), jnp.int32),
                   jax.ShapeDtypeStruct((1, LANES), F32)],
        compiler_params=_cparams(2),
    )(x, yrg, ynsa, woa, wob, gx, wq, gq, kx, vx, wo, gm, wrh, br)


def _dest_kernel(ri_ref, pstart_ref, dest_ref, run_ref):
    @pl.when(pl.program_id(0) == 0)
    def _():
        run_ref[...] = jnp.zeros_like(run_ref)

    ri = ri_ref[...]
    tm = ri.shape[0]
    lane = lax.broadcasted_iota(jnp.int32, ri.shape, 1)
    e_a = ri[:, 0:1]
    e_b = ri[:, 1:2]
    hot_a = lane == e_a
    hot_b = lane == e_b
    hot = jnp.where(hot_a | hot_b, 1.0, 0.0)
    row = lax.broadcasted_iota(jnp.int32, (tm, tm), 0)
    col = lax.broadcasted_iota(jnp.int32, (tm, tm), 1)
    earlier = jnp.where(col < row, 1.0, 0.0).astype(BF16)
    base = _dot(earlier, hot.astype(BF16)) + run_ref[...] + pstart_ref[...]
    d_a = jnp.sum(jnp.where(hot_a, base, 0.0), axis=-1, keepdims=True)
    d_b = jnp.sum(jnp.where(hot_b, base, 0.0), axis=-1, keepdims=True)
    both = jnp.where(lane == 0, d_a, jnp.where(lane == 1, d_b, 0.0))
    dest_ref[...] = both.T[:DEST_ROWS, :].astype(jnp.int32)
    run_ref[...] += jnp.sum(hot, axis=0, keepdims=True)


def _dest(ri2, pstart):
    n_tok = ri2.shape[0]
    tm = min(TM_DEST, n_tok)
    return pl.pallas_call(
        _dest_kernel,
        grid=(n_tok // tm,),
        in_specs=[pl.BlockSpec((tm, LANES), lambda i: (i, 0)), pl.BlockSpec((1, LANES), lambda i: (0, 0))],
        out_specs=pl.BlockSpec((DEST_ROWS, tm), lambda i: (0, i)),
        out_shape=jax.ShapeDtypeStruct((DEST_ROWS, n_tok), jnp.int32),
        scratch_shapes=[pltpu.VMEM((1, LANES), F32)],
        compiler_params=_cparams(1),
    )(ri2, pstart)


def _store_row_tiles(ref, val):
    n = val.shape[0]
    for c in range(ROW_TILE):
        ref[pl.ds(c, n, stride=ROW_TILE), :] = val[:, c * LANES:(c + 1) * LANES]


def _load_row_tiles(ref, n):
    return [ref[pl.ds(c, n, stride=ROW_TILE), :] for c in range(ROW_TILE)]


def _token_rows(ref, t):
    return ref.at[pl.ds(pl.multiple_of(t * ROW_TILE, ROW_TILE), ROW_TILE), :]


def _dispatch_kernel(cnt_ref, pstart_ref, da_ref, db_ref, xt_ref, xs_ref, zrow, sem, zsem):
    tm = da_ref.shape[2]

    @pl.when(pl.program_id(0) == 0)
    def _():
        zrow[...] = jnp.zeros_like(zrow)

        def per_expert(e, c):
            used = cnt_ref[e]
            padded = (used + MOE_TB - 1) // MOE_TB * MOE_TB
            base = pstart_ref[e]

            def fill(r, c2):
                pltpu.make_async_copy(zrow, _token_rows(xs_ref, base + r), zsem).start()
                return c2

            def fill_done(r, c2):
                pltpu.make_async_copy(zrow, _token_rows(xs_ref, 0), zsem).wait()
                return c2

            lax.fori_loop(used, padded, fill, 0)
            lax.fori_loop(used, padded, fill_done, 0)
            return c

        lax.fori_loop(0, N_EXPERTS, per_expert, 0)

        last = N_EXPERTS - 1
        first_unused = (pstart_ref[last] + (cnt_ref[last] + MOE_TB - 1) // MOE_TB * MOE_TB) // MOE_TB

        def per_block(j, c):
            def fill(r, c2):
                pltpu.make_async_copy(zrow, _token_rows(xs_ref, j * MOE_TB + r), zsem).start()
                return c2

            def fill_done(r, c2):
                pltpu.make_async_copy(zrow, _token_rows(xs_ref, 0), zsem).wait()
                return c2

            lax.fori_loop(0, MOE_TB, fill, 0, unroll=DMA_UNROLL)
            lax.fori_loop(0, MOE_TB, fill_done, 0, unroll=DMA_UNROLL)
            return c

        lax.fori_loop(first_unused, xs_ref.shape[0] // (ROW_TILE * MOE_TB), per_block, 0)

    def issue(t, c):
        pltpu.make_async_copy(_token_rows(xt_ref, t), _token_rows(xs_ref, da_ref[0, 0, t]), sem).start(priority=0)
        pltpu.make_async_copy(_token_rows(xt_ref, t), _token_rows(xs_ref, db_ref[0, 0, t]), sem).start(priority=1)
        return c

    lax.fori_loop(0, tm, issue, 0, unroll=DMA_UNROLL)

    def drain(t, c):
        pltpu.make_async_copy(_token_rows(xt_ref, 0), _token_rows(xs_ref, 0), sem).wait()
        pltpu.make_async_copy(_token_rows(xt_ref, 0), _token_rows(xs_ref, 0), sem).wait()
        return c

    lax.fori_loop(0, tm, drain, 0, unroll=DMA_UNROLL)


def _dispatch(cnt, pstart, da, db, xt_rows, n_pad):
    n_tiles, _, tm = da.shape
    smem = pl.BlockSpec((1, 1, tm), lambda i, c, p: (i, 0, 0), memory_space=pltpu.SMEM)
    grid_spec = pltpu.PrefetchScalarGridSpec(
        num_scalar_prefetch=2,
        grid=(n_tiles,),
        in_specs=[smem, smem, pl.BlockSpec((tm * ROW_TILE, LANES), lambda i, c, p: (i, 0))],
        out_specs=pl.BlockSpec(memory_space=pl.ANY),
        scratch_shapes=[pltpu.VMEM((ROW_TILE, LANES), F32), pltpu.SemaphoreType.DMA(()),
                        pltpu.SemaphoreType.DMA(())],
    )
    return pl.pallas_call(
        _dispatch_kernel,
        grid_spec=grid_spec,
        out_shape=jax.ShapeDtypeStruct((n_pad * ROW_TILE, LANES), F32),
        compiler_params=pltpu.CompilerParams(dimension_semantics=("arbitrary",), has_side_effects=True,
                                             vmem_limit_bytes=VMEM_LIMIT),
    )(cnt, pstart, da, db, xt_rows)


def _ffn_kernel(bexp_ref, nused_ref, xs_ref, w1_ref, w3_ref, w2_ref, ys_ref):
    del bexp_ref
    j = pl.program_id(0)

    @pl.when(j < nused_ref[0])
    def _():
        xb = jnp.concatenate(_load_row_tiles(xs_ref, MOE_TB), axis=-1).astype(BF16)
        a = _dot(xb, w1_ref[0].astype(BF16))
        h = a * _sigmoid(a) * _dot(xb, w3_ref[0].astype(BF16))
        _store_row_tiles(ys_ref, _dot(h.astype(BF16), w2_ref[0].astype(BF16)))

    @pl.when(j >= nused_ref[0])
    def _():
        ys_ref[...] = jnp.zeros_like(ys_ref)


def _ffn(blk_exp, n_used, xs, w1, w3, w2):
    n_blocks = xs.shape[0] // (MOE_TB * ROW_TILE)
    rows = pl.BlockSpec((MOE_TB * ROW_TILE, LANES), lambda j, be, nu: (j, 0))
    used_rows = pl.BlockSpec((MOE_TB * ROW_TILE, LANES), lambda j, be, nu: (jnp.minimum(j, nu[0] - 1), 0))
    grid_spec = pltpu.PrefetchScalarGridSpec(
        num_scalar_prefetch=2,
        grid=(n_blocks,),
        in_specs=[used_rows,
                  pl.BlockSpec((1, D_MODEL, D_EXPERT), lambda j, be, nu: (be[j], 0, 0)),
                  pl.BlockSpec((1, D_MODEL, D_EXPERT), lambda j, be, nu: (be[j], 0, 0)),
                  pl.BlockSpec((1, D_EXPERT, D_MODEL), lambda j, be, nu: (be[j], 0, 0))],
        out_specs=rows,
    )
    return pl.pallas_call(
        _ffn_kernel,
        grid_spec=grid_spec,
        out_shape=jax.ShapeDtypeStruct(xs.shape, F32),
        compiler_params=_cparams(1),
    )(blk_exp, n_used, xs, w1, w3, w2)


def _combine_kernel(da_ref, db_ref, da_next_ref, db_next_ref, h_ref, rw_ref, ys_ref, o_ref, ya, yb, sems):
    tm = da_ref.shape[2]
    i = pl.program_id(0)
    slot = i % 2

    def start_gather(a_ref, b_ref, s):
        def issue(t, c):
            pltpu.make_async_copy(_token_rows(ys_ref, a_ref[0, 0, t]), _token_rows(ya.at[s], t),
                                  sems.at[s]).start(priority=0)
            pltpu.make_async_copy(_token_rows(ys_ref, b_ref[0, 0, t]), _token_rows(yb.at[s], t),
                                  sems.at[s]).start(priority=1)
            return c

        lax.fori_loop(0, tm, issue, 0, unroll=DMA_UNROLL)

    @pl.when(i == 0)
    def _():
        start_gather(da_ref, db_ref, slot)

    @pl.when(i + 1 < pl.num_programs(0))
    def _():
        start_gather(da_next_ref, db_next_ref, 1 - slot)

    def drain(t, c):
        pltpu.make_async_copy(_token_rows(ys_ref, 0), _token_rows(ya.at[slot], 0), sems.at[slot]).wait()
        pltpu.make_async_copy(_token_rows(ys_ref, 0), _token_rows(yb.at[slot], 0), sems.at[slot]).wait()
        return c

    lax.fori_loop(0, tm, drain, 0, unroll=DMA_UNROLL)
    rw = rw_ref[...]
    mix = [rw[:, 0:1] * a + rw[:, 1:2] * b
           for a, b in zip(_load_row_tiles(ya.at[slot], tm), _load_row_tiles(yb.at[slot], tm))]
    o_ref[...] = h_ref[...] + jnp.concatenate(mix, axis=-1)


def _combine(da, db, h2, rw, ys):
    n_tiles, _, tm = da.shape
    n_tok = h2.shape[0]
    smem = pl.BlockSpec((1, 1, tm), lambda i: (i, 0, 0), memory_space=pltpu.SMEM)
    smem_next = pl.BlockSpec((1, 1, tm), lambda i: (jnp.minimum(i + 1, n_tiles - 1), 0, 0), memory_space=pltpu.SMEM)
    row = lambda w: pl.BlockSpec((tm, w), lambda i: (i, 0))
    slots = pltpu.VMEM((2, tm * ROW_TILE, LANES), F32)
    return pl.pallas_call(
        _combine_kernel,
        grid=(n_tiles,),
        in_specs=[smem, smem, smem_next, smem_next, row(D_MODEL), row(LANES), pl.BlockSpec(memory_space=pl.ANY)],
        out_specs=row(D_MODEL),
        out_shape=jax.ShapeDtypeStruct((n_tok, D_MODEL), F32),
        scratch_shapes=[slots, slots, pltpu.SemaphoreType.DMA((2,))],
        compiler_params=_cparams(1),
    )(da, db, da, db, h2, rw, ys)


def _rel_bucket_np(dist):
    n = np.maximum(dist, 0)
    max_exact = NUM_BUCKETS // 2
    nf = np.maximum(n, 1).astype(np.float32)
    large = max_exact + (np.log(nf / max_exact) / math.log(MAX_DIST / max_exact)
                         * (NUM_BUCKETS - max_exact)).astype(np.int32)
    large = np.minimum(large, NUM_BUCKETS - 1)
    return np.where(n < max_exact, n, large).astype(np.int32)


def _toeplitz(vec, rows):
    width = vec.shape[-1] - 1
    flat = jnp.tile(vec, (1,) * (vec.ndim - 1) + (rows,))[..., :rows * width]
    return flat.reshape(vec.shape[:-1] + (rows, width))


def _bias_tables(rel_bias, seq):
    n_chunk = seq // CMP_STRIDE
    n_tiles = seq // TQ
    table = rel_bias.T.astype(F32)

    wide = NEAR + TQ
    k = np.arange(wide + 1)
    dw = np.where(k < NEAR, WINDOW - k, WINDOW + wide + 1 - k)
    used = (k < NEAR) | (k > wide + 1 - TQ)
    vals = table[:, _rel_bucket_np(dw)]

    n_var = WINDOW // TQ
    first_key = WINDOW - TQ * np.arange(n_var + 1)[:, None, None]
    in_seq = np.arange(NEAR)[None, None, :] >= first_key

    def near_tile(valid):
        t = _toeplitz(jnp.where(valid[None, :], vals, NEG_INF), TQ)[:, :, :NEAR]
        t = t.reshape(NSA_KV, NSA_HPG, TQ, NEAR).transpose(1, 0, 2, 3).reshape(1, QROWS2, NEAR)
        return jnp.where(in_seq, t, NEG_INF)

    bias_w = near_tile(used & (dw >= 0) & (dw < WINDOW))
    bias_s = near_tile(used & (dw >= 0))
    bias_far = table[:, NUM_BUCKETS - 1].reshape(NSA_KV, NSA_HPG, 1).transpose(1, 0, 2)
    bias_far = jnp.broadcast_to(bias_far, (NSA_HPG, NSA_KV, TQ)).reshape(QROWS2, 1)

    r = np.arange(CMP_STRIDE)[:, None]
    k = np.arange(2 * n_chunk + 1)[None, :]
    lag = 2 * n_chunk + 1 - k
    valid = (k > n_chunk + 1) & (CMP_STRIDE * lag + r >= CMP_L - 1)
    vals = table[:, _rel_bucket_np(CMP_STRIDE * lag + r - CMP_L // 2)]
    full = _toeplitz(jnp.where(valid[None], vals, NEG_INF), n_chunk)[..., :n_chunk]
    full = jnp.where(np.arange(n_chunk) < n_chunk - 1, full, NEG_INF)
    a4 = TQ // CMP_STRIDE
    full = full.reshape(NSA_KV, NSA_HPG, CMP_STRIDE, n_tiles, a4, n_chunk).transpose(3, 1, 0, 4, 2, 5)
    bias_c = full.reshape(n_tiles, QROWS2, n_chunk)
    return tuple(LOG2E * t for t in (bias_c, bias_w, bias_s, bias_far))


def _selection_tables(seq):
    n_chunk = seq // CMP_STRIDE
    n_blk = seq // SEL_L
    c = np.arange(n_chunk)
    n = np.arange(n_blk)
    start = c * CMP_STRIDE
    overlap_t = ((start[None, :] <= n[:, None] * SEL_L + SEL_L - 1) & (start[None, :] + CMP_L - 1 >= n[:, None] * SEL_L)
                 & (c < n_chunk - 1)[None, :])
    pos = np.arange(seq + WINDOW) - WINDOW
    lane_blk = np.arange(LANES) % n_blk
    hit = (pos[:, None] >= 0) & (pos[:, None] // SEL_L == lane_blk[None, :]) & (np.arange(LANES) < 2 * n_blk)[None, :]
    return jnp.asarray(overlap_t, BF16), jnp.asarray(np.where(hit, -UNSEL_PENALTY, 0.0), BF16)


def _block_ones(width, group):
    idx = np.arange(width) // group
    return jnp.asarray((idx[:, None] == idx[None, :]) / group, BF16)


def _block_diag(w):
    nb, n, m = w.shape
    eye = jnp.eye(nb, dtype=w.dtype)
    return jnp.einsum('hij,hg->higj', w, eye).reshape(nb * n, nb * m)


def _compress_weights(w1, w2, pos):
    half_l = CMP_L // 2
    parts = []
    for half in range(2):
        wh = w1[half * half_l * HEAD_DIM:(half + 1) * half_l * HEAD_DIM].reshape(half_l, HEAD_DIM, CMP_HIDDEN)
        z = jnp.zeros_like(wh)
        for g in range(NSA_KV):
            grp = [wh if gg == g else z for gg in range(NSA_KV)]
            parts.append(jnp.stack(grp, axis=1).reshape(half_l * KV_W, CMP_HIDDEN))
    w1cat = jnp.concatenate(parts, axis=1).astype(BF16)
    w2bd = _block_diag(jnp.stack([w2] * NSA_KV)).astype(BF16)
    prow = [jnp.tile(pos[half * half_l:(half + 1) * half_l][:, None, :], (1, NSA_KV, 1)).reshape(-1)
            for half in range(2)]
    pmat = jnp.zeros((8, half_l * KV_W), F32).at[0].set(prow[0]).at[1].set(prow[1]).astype(BF16)
    return w1cat, w2bd, pmat


def kernel(x, mem, rel_bias, norm_mix, w_in, rg_conv_w, rg_conv_b, rg_w_r, rg_b_r, rg_w_i, rg_b_i, rg_lambda, nsa_g_q, nsa_g_kc, nsa_g_ks, nsa_g_kw, cmp_pos_k, cmp_pos_v, cmp_k_w1, cmp_k_w2, cmp_v_w1, cmp_v_w2, out_g_rg, out_g_nsa, w_out, norm_x, norm_mem, xa_w_q, xa_w_kv, xa_w_o, xa_g_q, xa_g_k, norm_moe, router_g_w, router_g_b, router_e_w, router_e_b, exp_w1, exp_w3, exp_w2):
    bsz, seq, _ = x.shape
    n_tok = bsz * seq
    assert seq % FAR_TK == 0 and 2 * (seq // SEL_L) <= LANES and norm_mix.shape[0] == 1
    l = 0
    row = lambda v: v.reshape(1, -1).astype(F32)

    perm = np.array([(half * NSA_HPG + p) * HEAD_DIM + d
                     for p in range(NSA_HPG) for half in range(NSA_KV) for d in range(HEAD_DIM)])
    offs = np.cumsum([0, RG_WIDTH, RG_WIDTH, NSA_WIDTH] + [KV_W] * 6)
    w = w_in[l]
    wrg = w[:, :offs[2]].astype(BF16)
    wq = w[:, offs[2]:offs[3]][:, perm].astype(BF16)
    wkv = w[:, offs[3]:offs[9]].astype(BF16)
    wgl = jnp.pad(w[:, offs[9]:], ((0, 0), (0, LANES - 3 * NSA_HEADS))).astype(BF16)
    ones64 = _block_ones(NSA_WIDTH, HEAD_DIM)
    gq = row(jnp.tile(nsa_g_q[l], NSA_HEADS) * (HEAD_DIM ** -0.5 * LOG2E))
    u, gate, q, kc, vc, ks, vs, kw, vw, gates = _inproj(
        x.reshape(n_tok, D_MODEL), row(norm_mix[l]), wrg, wq, wkv, wgl, gq,
        row(jnp.tile(nsa_g_ks[l], NSA_KV)), row(jnp.tile(nsa_g_kw[l], NSA_KV)), ones64)

    wg = (0.5 * jnp.concatenate([_block_diag(rg_w_r[l]), _block_diag(rg_w_i[l])], axis=1)).astype(BF16)
    bg = 0.5 * jnp.concatenate([rg_b_r[l], rg_b_i[l]]).reshape(1, -1)
    y_rg = _rglru(u.reshape(bsz, seq, RG_WIDTH), gate.reshape(bsz, seq, RG_WIDTH),
                  rg_conv_w[l].reshape(CONV_W, RG_WIDTH), row(rg_conv_b[l]), wg, bg, row(rg_lambda[l]),
                  row(out_g_rg[l]))

    n_chunk = seq // CMP_STRIDE
    w1k, w2k, pk = _compress_weights(cmp_k_w1[l], cmp_k_w2[l], cmp_pos_k[l])
    w1v, w2v, pv = _compress_weights(cmp_v_w1[l], cmp_v_w2[l], cmp_pos_v[l])
    kcmp, vcmp = _compress(kc.reshape(bsz, n_chunk, CMP_STRIDE * KV_W), vc.reshape(bsz, n_chunk, CMP_STRIDE * KV_W),
                           w1k, w2k, pk, w1v, w2v, pv, row(jnp.tile(nsa_g_kc[l], NSA_KV)),
                           ones64[:KV_W, :KV_W])
    padw = lambda t: jnp.pad(t.reshape(bsz, seq, KV_W), ((0, 0), (WINDOW, 0), (0, 0)))
    bias_c, bias_w, bias_s, bias_far = _bias_tables(rel_bias, seq)
    overlap_t, penalty = _selection_tables(seq)
    ksx = jnp.concatenate([padw(ks), jnp.broadcast_to(penalty, (bsz,) + penalty.shape)], axis=-1)
    y_nsa = _nsa(q.reshape(bsz, seq, NSA_WIDTH), gates.reshape(bsz, seq, LANES), kcmp, vcmp,
                 ksx, padw(vs), padw(kw), padw(vw), overlap_t, bias_c, bias_w, bias_s, bias_far,
                 row(out_g_nsa[l][perm]))

    kx, vx = _memkv(mem, row(norm_mem[l]), xa_w_kv[l].astype(BF16), row(xa_g_k[l]))
    wo_mix = w_out[l]
    wr = jnp.pad(jnp.concatenate([router_g_w[l], router_e_w[l]], axis=1),
                 ((0, 0), (0, LANES - N_GROUPS - N_EXPERTS)))
    wr_hi = wr.astype(BF16)
    br = jnp.pad(jnp.concatenate([router_g_b[l], router_e_b[l]]), (0, LANES - N_GROUPS - N_EXPERTS)).reshape(1, -1)
    h2, xt, rw, ri, counts = _mid(
        x, y_rg, y_nsa, wo_mix[:RG_WIDTH].astype(BF16), wo_mix[RG_WIDTH:][perm].astype(BF16), row(norm_x[l]),
        xa_w_q[l].astype(BF16), row(xa_g_q[l] * (X_HEAD_DIM ** -0.5 * LOG2E)), kx, vx, xa_w_o[l].astype(BF16),
        row(norm_moe[l]), jnp.concatenate([wr_hi, (wr - wr_hi.astype(F32)).astype(BF16)], axis=1), br)

    n_slots = 2 * n_tok
    n_blocks = n_slots // MOE_TB + N_EXPERTS
    n_pad = n_blocks * MOE_TB
    cnt = counts[0, :N_EXPERTS].astype(jnp.int32)
    pcnt = (cnt + MOE_TB - 1) // MOE_TB * MOE_TB
    pends = jnp.cumsum(pcnt)
    pstart = jnp.pad((pends - pcnt).astype(F32), (0, LANES - N_EXPERTS)).reshape(1, LANES)
    blk_exp = jnp.minimum(jnp.sum(pends[None, :] <= jnp.arange(n_blocks, dtype=jnp.int32)[:, None] * MOE_TB, axis=1),
                          N_EXPERTS - 1).astype(jnp.int32)
    n_used = (pends[-1:] // MOE_TB).astype(jnp.int32)
    dest = _dest(ri.reshape(n_tok, LANES), pstart)
    tmd = min(TM_DMA, n_tok)
    da = dest[0].reshape(n_tok // tmd, 1, tmd)
    db = dest[1].reshape(n_tok // tmd, 1, tmd)
    xs = _dispatch(cnt, (pends - pcnt).astype(jnp.int32), da, db, xt, n_pad)
    ys = _ffn(blk_exp, n_used, xs, exp_w1[l], exp_w3[l], exp_w2[l])
    out = _combine(da, db, h2.reshape(n_tok, D_MODEL), rw.reshape(n_tok, LANES), ys)
    return out.reshape(bsz, seq, D_MODEL)
```

```python
import math

import numpy as np
import jax
import jax.numpy as jnp
from jax import lax
from jax.experimental import pallas as pl
from jax.experimental.pallas import tpu as pltpu

F32 = jnp.float32
BF16 = jnp.bfloat16

D_MODEL = 1024
RG_WIDTH = 512
RG_BLOCKS = 8
RG_BLOCK = 64
CONV_W = 4
RG_C = 8.0
NSA_WIDTH = 512
NSA_HEADS = 8
HEAD_DIM = 64
NSA_KV = 2
NSA_HPG = 4
KV_W = 128
CMP_L = 32
CMP_STRIDE = 16
CMP_HIDDEN = 256
SEL_L = 64
N_SEL = 8
WINDOW = 512
NUM_BUCKETS = 32
MAX_DIST = 128
X_HEADS = 4
X_HEAD_DIM = 256
N_GROUPS = 4
EXP_PER_GROUP = 8
N_EXPERTS = 32
D_EXPERT = 512
EPS = 1e-6
LOG2E = 1.0 / math.log(2.0)
NEG_INF = -1e30
MASKED_BELOW = -1e29
SEL_FORCE = 1e9
LANES = 128

TQ = 64
NEAR = WINDOW + TQ
FAR_TK = 512
QROWS2 = NSA_HEADS * TQ
UNSEL_PENALTY = 2.0 ** 100
NSA_NB = 4

TM_PROJ = 1024
TM_MID = 1024
TM_DEST = 512
DEST_ROWS = 8
TM_DMA = 512
DMA_UNROLL = 8
MOE_TB = 512
ROW_TILE = D_MODEL // LANES
RG_CHUNK = 256
SCAN_ROWS = 8
SCAN_UNROLL = 8
VMEM_LIMIT = 56 * 1024 * 1024


def _cparams(n_axes):
    return pltpu.CompilerParams(dimension_semantics=("arbitrary",) * n_axes,
                                vmem_limit_bytes=VMEM_LIMIT)


def _dot(a, b):
    return jnp.dot(a, b, preferred_element_type=F32)


def _dot_nt(a, b):
    return lax.dot_general(a, b, (((1,), (1,)), ((), ())), preferred_element_type=F32)


def _gelu_tanh(x):
    c = math.sqrt(2.0 / math.pi)
    half = 0.5 * x
    return half + half * jnp.tanh(x * (c + (c * 0.044715) * (x * x)))


def _sigmoid(x):
    return 0.5 * jnp.tanh(0.5 * x) + 0.5


def _rms(x, g):
    return x * lax.rsqrt(jnp.mean(x * x, axis=-1, keepdims=True) + EPS) * g


def _group_rms(x, ones_blk, g):
    ms = _dot((x * x).astype(BF16), ones_blk)
    return x * lax.rsqrt(ms + EPS) * g


def _inproj_kernel(x_ref, g_ref, wrg_ref, wq_ref, wkv_ref, wgl_ref, gq_ref, gks_ref, gkw_ref, ones_ref,
                   u_ref, gate_ref, q_ref, kc_ref, vc_ref, ks_ref, vs_ref, kw_ref, vw_ref, gates_ref):
    xb = _rms(x_ref[...], g_ref[...]).astype(BF16)
    rg = _dot(xb, wrg_ref[...])
    u_ref[...] = rg[:, :RG_WIDTH].astype(BF16)
    gate_ref[...] = rg[:, RG_WIDTH:].astype(BF16)
    q = _dot(xb, wq_ref[...])
    q_ref[...] = _group_rms(q, ones_ref[...], gq_ref[...]).astype(BF16)
    kv = _dot(xb, wkv_ref[...])
    ones_kv = ones_ref[:KV_W, :KV_W]
    kc_ref[...] = kv[:, 0 * KV_W:1 * KV_W].astype(BF16)
    vc_ref[...] = kv[:, 1 * KV_W:2 * KV_W].astype(BF16)
    ks_ref[...] = _group_rms(kv[:, 2 * KV_W:3 * KV_W], ones_kv, gks_ref[...]).astype(BF16)
    vs_ref[...] = kv[:, 3 * KV_W:4 * KV_W].astype(BF16)
    kw_ref[...] = _group_rms(kv[:, 4 * KV_W:5 * KV_W], ones_kv, gkw_ref[...]).astype(BF16)
    vw_ref[...] = kv[:, 5 * KV_W:6 * KV_W].astype(BF16)
    gates_ref[...] = _sigmoid(_dot(xb, wgl_ref[...]))


def _inproj(x2, g, wrg, wq, wkv, wgl, gq, gks, gkw, ones_blk):
    n_tok = x2.shape[0]
    tm = min(TM_PROJ, n_tok)
    full = lambda a: pl.BlockSpec(a.shape, lambda i: (0,) * a.ndim)
    row = lambda w: pl.BlockSpec((tm, w), lambda i: (i, 0))
    outs = [(RG_WIDTH, BF16), (RG_WIDTH, BF16), (NSA_WIDTH, BF16)] + [(KV_W, BF16)] * 6 + [(LANES, F32)]
    return pl.pallas_call(
        _inproj_kernel,
        grid=(n_tok // tm,),
        in_specs=[row(D_MODEL)] + [full(a) for a in (g, wrg, wq, wkv, wgl, gq, gks, gkw, ones_blk)],
        out_specs=[row(w) for w, _ in outs],
        out_shape=[jax.ShapeDtypeStruct((n_tok, w), dt) for w, dt in outs],
        compiler_params=_cparams(1),
    )(x2, g, wrg, wq, wkv, wgl, gq, gks, gkw, ones_blk)


def _rglru_kernel(u_ref, gate_ref, cw_ref, cb_ref, wg_ref, bg_ref, lam_ref, og_ref, y_ref, upad, a_s, h_s):
    seq = u_ref.shape[1]
    upad[0:8, :] = jnp.zeros((8, RG_WIDTH), F32)
    upad[8:8 + seq, :] = u_ref[0].astype(F32)
    neg_lam = -lam_ref[...]
    softplus = jnp.maximum(neg_lam, 0.0) + jnp.log(1.0 + jnp.exp(-jnp.abs(neg_lam)))
    log2_a_half = (-0.5 * RG_C * LOG2E) * softplus
    ch = min(RG_CHUNK, seq)
    for c in range(seq // ch):
        r0 = c * ch
        uc = cb_ref[...]
        for k in range(CONV_W):
            off = 8 + r0 - (CONV_W - 1) + k
            uc = uc + cw_ref[k:k + 1, :] * upad[off:off + ch, :]
        th = jnp.tanh(_dot(uc.astype(BF16), wg_ref[...]) + bg_ref[...])
        a = jnp.exp2(log2_a_half * th[:, :RG_WIDTH] + log2_a_half)
        a_s[r0:r0 + ch, :] = a
        s = 1.0 - a * a
        h_s[r0:r0 + ch, :] = s * lax.rsqrt(jnp.maximum(s, 1e-30)) * (0.5 * th[:, RG_WIDTH:] + 0.5) * uc

    row = lax.broadcasted_iota(jnp.int32, (SCAN_ROWS, RG_WIDTH), 0)

    def block(j, h_prev):
        rows = pl.ds(pl.multiple_of(j * SCAN_ROWS, SCAN_ROWS), SCAN_ROWS)
        a = a_s[rows, :]
        b = h_s[rows, :]
        k = 1
        while k < SCAN_ROWS:
            keep = row >= k
            b = jnp.where(keep, a * pltpu.roll(b, k, 0) + b, b)
            a = jnp.where(keep, a * pltpu.roll(a, k, 0), a)
            k *= 2
        h = a * h_prev + b
        h_s[rows, :] = h
        return h[SCAN_ROWS - 1:SCAN_ROWS, :]

    lax.fori_loop(0, seq // SCAN_ROWS, block, jnp.zeros((1, RG_WIDTH), F32), unroll=SCAN_UNROLL)

    for c in range(seq // ch):
        r0 = c * ch
        y = _gelu_tanh(gate_ref[0, r0:r0 + ch, :].astype(F32)) * h_s[r0:r0 + ch, :]
        y_ref[0, r0:r0 + ch, :] = _rms(y, og_ref[...]).astype(BF16)


def _rglru(u3, gate3, cw, cb, wg, bg, lam, og):
    bsz, seq, _ = u3.shape
    full = lambda a: pl.BlockSpec(a.shape, lambda b: (0,) * a.ndim)
    blk = pl.BlockSpec((1, seq, RG_WIDTH), lambda b: (b, 0, 0))
    return pl.pallas_call(
        _rglru_kernel,
        grid=(bsz,),
        in_specs=[blk, blk] + [full(a) for a in (cw, cb, wg, bg, lam, og)],
        out_specs=blk,
        out_shape=jax.ShapeDtypeStruct((bsz, seq, RG_WIDTH), BF16),
        scratch_shapes=[pltpu.VMEM((seq + 8, RG_WIDTH), F32), pltpu.VMEM((seq, RG_WIDTH), F32),
                        pltpu.VMEM((seq, RG_WIDTH), F32)],
        compiler_params=_cparams(1),
    )(u3, gate3, cw, cb, wg, bg, lam, og)


def _compress_kernel(kx_ref, vx_ref, w1k_ref, w2k_ref, pk_ref, w1v_ref, w2v_ref, pv_ref, gk_ref, ones_ref,
                     ko_ref, vo_ref):
    n_chunk = kx_ref.shape[1]
    half = NSA_KV * CMP_HIDDEN

    def mlp(x_ref, w1_ref, w2_ref, p_ref):
        ab = _dot(x_ref[0], w1_ref[...])
        pos = _dot(p_ref[...], w1_ref[...])
        hid = ab[:, :half] + pltpu.roll(ab[:, half:], n_chunk - 1, 0) + (pos[0:1, :half] + pos[1:2, half:])
        return _dot(_gelu_tanh(hid).astype(BF16), w2_ref[...])

    kc = mlp(kx_ref, w1k_ref, w2k_ref, pk_ref)
    ko_ref[0] = _group_rms(kc, ones_ref[...], gk_ref[...]).astype(BF16)
    vo_ref[0] = mlp(vx_ref, w1v_ref, w2v_ref, pv_ref).astype(BF16)


def _compress(kx, vx, w1k, w2k, pk, w1v, w2v, pv, gk, ones_kv):
    bsz, n_chunk, width = kx.shape
    full = lambda a: pl.BlockSpec(a.shape, lambda b: (0,) * a.ndim)
    xin = pl.BlockSpec((1, n_chunk, width), lambda b: (b, 0, 0))
    out = pl.BlockSpec((1, n_chunk, KV_W), lambda b: (b, 0, 0))
    return pl.pallas_call(
        _compress_kernel,
        grid=(bsz,),
        in_specs=[xin, xin] + [full(a) for a in (w1k, w2k, pk, w1v, w2v, pv, gk, ones_kv)],
        out_specs=[out, out],
        out_shape=[jax.ShapeDtypeStruct((bsz, n_chunk, KV_W), BF16)] * 2,
        compiler_params=_cparams(1),
    )(kx, vx, w1k, w2k, pk, w1v, w2v, pv, gk, ones_kv)


def _nsa_kernel(q_ref, gates_ref, kcmp_ref, vcmp_ref, ksx_ref, vsp_ref, kwp_ref, vwp_ref, ovt_ref,
                bc_ref, bw_ref, bs_ref, bf_ref, og_ref, y_ref):
    i = pl.program_id(1)
    t0 = pl.multiple_of(i * TQ, TQ)
    n_blk = ovt_ref.shape[0]
    lane = lax.broadcasted_iota(jnp.int32, (TQ, LANES), 1)
    lo_half = lane < HEAD_DIM
    n_batch = q_ref.shape[0]

    def with_ones(v):
        return jnp.concatenate([v, jnp.ones_like(v)], axis=1)

    def near_part(bb):
        pieces = []
        for p in range(NSA_HPG):
            qs = q_ref[bb, :, p * LANES:(p + 1) * LANES]
            zero = jnp.zeros_like(qs)
            pieces += [jnp.where(lo_half, qs, zero), jnp.where(lo_half, zero, qs)]
        q8 = jnp.concatenate(pieces, axis=0)

        bc = bc_ref[0]
        lc = _dot_nt(q8, kcmp_ref[bb]) + bc
        ec = jnp.where(bc > MASKED_BELOW, jnp.exp2(lc - jnp.max(lc, axis=-1, keepdims=True)), 0.0)
        sc = jnp.sum(ec, axis=-1, keepdims=True)
        pc = ec / jnp.where(sc > 0.0, sc, 1.0)
        o_c = _dot(pc.astype(BF16), vcmp_ref[bb])

        blocks = [pc[r * TQ:(r + 1) * TQ] for r in range(NSA_HPG * NSA_KV)]
        pcs = jnp.concatenate([sum(blocks[g::NSA_KV]) for g in range(NSA_KV)], axis=0)
        pcs_hi = pcs.astype(BF16)
        pcs_lo = (pcs - pcs_hi.astype(F32)).astype(BF16)
        imp = _dot_nt(ovt_ref[...], pcs_hi) + _dot_nt(ovt_ref[...], pcs_lo)
        blk = lax.broadcasted_iota(jnp.int32, imp.shape, 0)
        forced = (blk == 0) | (blk == i) | (blk == i - 1)
        score = jnp.where(forced, SEL_FORCE, jnp.where(blk > i, -3e38, imp))
        rank = jnp.zeros(imp.shape, F32)
        for m in range(n_blk):
            row = score[m:m + 1, :]
            rank = rank + jnp.where(blk > m, jnp.where(row >= score, 1.0, 0.0), jnp.where(row > score, 1.0, 0.0))
        unsel = jnp.where(rank < N_SEL, 0.0, 1.0)
        unsel_far = jnp.where(blk >= i - WINDOW // SEL_L, 1.0, unsel)
        pad = jnp.zeros((LANES - 2 * n_blk, imp.shape[1]), F32)
        u_t = jnp.concatenate([unsel, unsel_far, pad], axis=0).T
        u_lane = lax.broadcasted_iota(jnp.int32, u_t.shape, 1)
        u_near = jnp.where(u_lane < n_blk, u_t, 0.0).astype(BF16)
        u_far = jnp.where(u_lane >= n_blk, u_t, 0.0).astype(BF16)
        qx_near = jnp.concatenate([q8, jnp.concatenate([u_near] * NSA_HPG, axis=0)], axis=1)
        qx_far = jnp.concatenate([q8, jnp.concatenate([u_far] * NSA_HPG, axis=0)], axis=1)

        lw = _dot_nt(q8, kwp_ref[bb, pl.ds(t0, NEAR), :]) + bw_ref[0]

        ls = _dot_nt(qx_near, ksx_ref[bb, pl.ds(t0, NEAR), :]) + bs_ref[0]
        m1 = jnp.max(ls, axis=-1, keepdims=True)
        return o_c, lw, qx_far, (m1, jnp.zeros((QROWS2, 2 * LANES), F32)), ls

    near = [near_part(bb) for bb in range(n_batch)]
    bfar = bf_ref[...]

    def far_step(kf, carry):
        base = pl.multiple_of(WINDOW + kf * FAR_TK, FAR_TK)
        new = []
        for bb in range(n_batch):
            m, acc = carry[bb]
            lf = _dot_nt(near[bb][2], ksx_ref[bb, pl.ds(base, FAR_TK), :]) + bfar
            m_new = jnp.maximum(m, jnp.max(lf, axis=-1, keepdims=True))
            alpha = jnp.exp2(m - m_new)
            e = jnp.exp2(lf - m_new)
            new.append((m_new, alpha * acc + _dot(e.astype(BF16), with_ones(vsp_ref[bb, pl.ds(base, FAR_TK), :]))))
        return tuple(new)

    n_far = (jnp.maximum(t0 - WINDOW, 0) + FAR_TK - 1) // FAR_TK
    far = lax.fori_loop(0, n_far, far_step, tuple(part[3] for part in near))

    for bb in range(n_batch):
        o_c, lw = near[bb][0], near[bb][1]
        ew = jnp.exp2(lw - jnp.max(lw, axis=-1, keepdims=True))
        ow2 = _dot(ew.astype(BF16), with_ones(vwp_ref[bb, pl.ds(t0, NEAR), :]))
        o_w = ow2[:, :LANES] / ow2[:, LANES:]
        m_all, acc_far = far[bb]
        e1 = jnp.exp2(near[bb][4] - m_all)
        acc_s = acc_far + _dot(e1.astype(BF16), with_ones(vsp_ref[bb, pl.ds(t0, NEAR), :]))
        o_s = acc_s[:, :LANES] / acc_s[:, LANES:]
        gates = gates_ref[bb]

        def gate_col(j):
            cols = [gates[:, (g * NSA_HPG + p) * 3 + j:(g * NSA_HPG + p) * 3 + j + 1]
                    for p in range(NSA_HPG) for g in range(NSA_KV)]
            return jnp.concatenate(cols, axis=0)

        out = gate_col(0) * o_c + gate_col(1) * o_s + gate_col(2) * o_w
        slabs = [jnp.where(lo_half, out[(2 * p) * TQ:(2 * p + 1) * TQ], out[(2 * p + 1) * TQ:(2 * p + 2) * TQ])
                 for p in range(NSA_HPG)]
        y_ref[bb] = _rms(jnp.concatenate(slabs, axis=-1), og_ref[...]).astype(BF16)


def _nsa(q3, gates3, kcmp, vcmp, ksx, vsp, kwp, vwp, ovt, bias_c, bias_w, bias_s, bias_far, og):
    bsz, seq, _ = q3.shape
    n_chunk = kcmp.shape[1]
    n_var = bias_w.shape[0] - 1
    nb = NSA_NB if bsz % NSA_NB == 0 else 1
    full = lambda a: pl.BlockSpec(a.shape, lambda b, i: (0,) * a.ndim)
    per_b = lambda a: pl.BlockSpec((nb,) + a.shape[1:], lambda b, i: (b,) + (0,) * (a.ndim - 1))
    near = pl.BlockSpec((1, QROWS2, NEAR), lambda b, i: (jnp.minimum(i, n_var), 0, 0))
    return pl.pallas_call(
        _nsa_kernel,
        grid=(bsz // nb, seq // TQ),
        in_specs=[pl.BlockSpec((nb, TQ, NSA_WIDTH), lambda b, i: (b, i, 0)),
                  pl.BlockSpec((nb, TQ, LANES), lambda b, i: (b, i, 0)),
                  per_b(kcmp), per_b(vcmp), per_b(ksx), per_b(vsp), per_b(kwp), per_b(vwp),
                  full(ovt),
                  pl.BlockSpec((1, QROWS2, n_chunk), lambda b, i: (i, 0, 0)),
                  near, near, full(bias_far), full(og)],
        out_specs=pl.BlockSpec((nb, TQ, NSA_WIDTH), lambda b, i: (b, i, 0)),
        out_shape=jax.ShapeDtypeStruct((bsz, seq, NSA_WIDTH), BF16),
        compiler_params=_cparams(2),
    )(q3, gates3, kcmp, vcmp, ksx, vsp, kwp, vwp, ovt, bias_c, bias_w, bias_s, bias_far, og)


def _memkv_kernel(mem_ref, g_ref, wkv_ref, gk_ref, k_ref, v_ref):
    mn = _rms(mem_ref[0], g_ref[...]).astype(BF16)
    kv = _dot(mn, wkv_ref[...])
    for h in range(X_HEADS):
        sl = slice(h * X_HEAD_DIM, (h + 1) * X_HEAD_DIM)
        k_ref[0, :, sl] = _rms(kv[:, sl], gk_ref[...]).astype(BF16)
    v_ref[0] = kv[:, D_MODEL:].astype(BF16)


def _memkv(mem, g, wkv, gk):
    bsz, mlen, _ = mem.shape
    full = lambda a: pl.BlockSpec(a.shape, lambda b: (0,) * a.ndim)
    blk = pl.BlockSpec((1, mlen, D_MODEL), lambda b: (b, 0, 0))
    return pl.pallas_call(
        _memkv_kernel,
        grid=(bsz,),
        in_specs=[blk, full(g), full(wkv), full(gk)],
        out_specs=[blk, blk],
        out_shape=[jax.ShapeDtypeStruct((bsz, mlen, D_MODEL), BF16)] * 2,
        compiler_params=_cparams(1),
    )(mem, g, wkv, gk)


def _mid_kernel(x_ref, yrg_ref, ynsa_ref, woa_ref, wob_ref, gx_ref, wq_ref, gq_ref, k_ref, v_ref, wo_ref,
                gm_ref, wrh_ref, br_ref, h_ref, xt_ref, rw_ref, ri_ref, cnt_ref):
    h1 = x_ref[0] + _dot(yrg_ref[0], woa_ref[...]) + _dot(ynsa_ref[0], wob_ref[...])

    q = _dot(_rms(h1, gx_ref[...]).astype(BF16), wq_ref[...])
    heads = []
    for h in range(X_HEADS):
        sl = slice(h * X_HEAD_DIM, (h + 1) * X_HEAD_DIM)
        qh = _rms(q[:, sl], gq_ref[...]).astype(BF16)
        lg = _dot_nt(qh, k_ref[0, :, sl])
        e = jnp.exp2(lg - jnp.max(lg, axis=-1, keepdims=True))
        heads.append(_dot(e.astype(BF16), v_ref[0, :, sl]) / jnp.sum(e, axis=-1, keepdims=True))
    h2 = h1 + _dot(jnp.concatenate(heads, axis=-1).astype(BF16), wo_ref[...])
    h_ref[0] = h2

    xt = _rms(h2, gm_ref[...])
    _store_row_tiles(xt_ref, xt)
    xt_hi = xt.astype(BF16)
    xt_lo = (xt - xt_hi.astype(F32)).astype(BF16)
    hi2 = _dot(xt_hi, wrh_ref[...])
    lg = hi2[:, :LANES] + hi2[:, LANES:] + _dot(xt_lo, wrh_ref[:, :LANES]) + br_ref[...]
    lane = lax.broadcasted_iota(jnp.int32, lg.shape, 1)
    lane_f = lane.astype(F32)
    first_of = lambda hit: jnp.min(jnp.where(hit, lane_f, 1e9), axis=-1, keepdims=True)
    glog = jnp.where(lane < N_GROUPS, lg, -3e38)
    gmax = jnp.max(glog, axis=-1, keepdims=True)
    gsel = first_of(glog == gmax)
    p_g = 1.0 / jnp.sum(jnp.exp(glog - gmax), axis=-1, keepdims=True)
    lo = N_GROUPS + EXP_PER_GROUP * gsel
    el = jnp.where((lane_f >= lo) & (lane_f < lo + EXP_PER_GROUP), lg, -3e38)
    m_a = jnp.max(el, axis=-1, keepdims=True)
    i_a = first_of(el == m_a)
    el2 = jnp.where(lane_f == i_a, -3e38, el)
    m_b = jnp.max(el2, axis=-1, keepdims=True)
    i_b = first_of(el2 == m_b)
    r = jnp.exp(m_b - m_a)
    w_a = p_g / (1.0 + r)
    w_b = p_g * r / (1.0 + r)
    e_a = i_a - N_GROUPS
    e_b = i_b - N_GROUPS
    rw_ref[0] = jnp.where(lane == 0, w_a, jnp.where(lane == 1, w_b, 0.0))
    ri_ref[0] = jnp.where(lane == 0, e_a, jnp.where(lane == 1, e_b, 0.0)).astype(jnp.int32)

    @pl.when((pl.program_id(0) == 0) & (pl.program_id(1) == 0))
    def _():
        cnt_ref[...] = jnp.zeros_like(cnt_ref)

    hot = jnp.where((lane_f == e_a) | (lane_f == e_b), 1.0, 0.0)
    cnt_ref[...] += jnp.sum(hot, axis=0, keepdims=True)


def _mid(x, yrg, ynsa, woa, wob, gx, wq, gq, kx, vx, wo, gm, wrh, br):
    bsz, seq, _ = x.shape
    tm = min(TM_MID, seq)
    mlen = kx.shape[1]
    n_i = seq // tm
    full = lambda a: pl.BlockSpec(a.shape, lambda b, i: (0,) * a.ndim)
    tok = lambda w: pl.BlockSpec((1, tm, w), lambda b, i: (b, i, 0))
    memb = pl.BlockSpec((1, mlen, D_MODEL), lambda b, i: (b, 0, 0))
    xt_spec = pl.BlockSpec((tm * ROW_TILE, LANES), lambda b, i: (b * n_i + i, 0))
    return pl.pallas_call(
        _mid_kernel,
        grid=(bsz, seq // tm),
        in_specs=[tok(D_MODEL), tok(RG_WIDTH), tok(NSA_WIDTH), full(woa), full(wob), full(gx), full(wq), full(gq),
                  memb, memb, full(wo), full(gm), full(wrh), full(br)],
        out_specs=[tok(D_MODEL), xt_spec, tok(LANES), tok(LANES), pl.BlockSpec((1, LANES), lambda b, i: (0, 0))],
        out_shape=[jax.ShapeDtypeStruct((bsz, seq, D_MODEL), F32),
                   jax.ShapeDtypeStruct((bsz * seq * ROW_TILE, LANES), F32),
                   jax.ShapeDtypeStruct((bsz, seq, LANES), F32), jax.ShapeDtypeStruct((bsz, seq, LANES), jnp.int32),
                   jax.ShapeDtypeStruct((1, LANES), F32)],
        compiler_params=_cparams(2),
    )(x, yrg, ynsa, woa, wob, gx, wq, gq, kx, vx, wo, gm, wrh, br)


def _dest_kernel(ri_ref, pstart_ref, dest_ref, run_ref):
    @pl.when(pl.program_id(0) == 0)
    def _():
        run_ref[...] = jnp.zeros_like(run_ref)

    ri = ri_ref[...]
    tm = ri.shape[0]
    lane = lax.broadcasted_iota(jnp.int32, ri.shape, 1)
    e_a = ri[:, 0:1]
    e_b = ri[:, 1:2]
    hot_a = lane == e_a
    hot_b = lane == e_b
    hot = jnp.where(hot_a | hot_b, 1.0, 0.0)
    row = lax.broadcasted_iota(jnp.int32, (tm, tm), 0)
    col = lax.broadcasted_iota(jnp.int32, (tm, tm), 1)
    earlier = jnp.where(col < row, 1.0, 0.0).astype(BF16)
    base = _dot(earlier, hot.astype(BF16)) + run_ref[...] + pstart_ref[...]
    d_a = jnp.sum(jnp.where(hot_a, base, 0.0), axis=-1, keepdims=True)
    d_b = jnp.sum(jnp.where(hot_b, base, 0.0), axis=-1, keepdims=True)
    both = jnp.where(lane == 0, d_a, jnp.where(lane == 1, d_b, 0.0))
    dest_ref[...] = both.T[:DEST_ROWS, :].astype(jnp.int32)
    run_ref[...] += jnp.sum(hot, axis=0, keepdims=True)


def _dest(ri2, pstart):
    n_tok = ri2.shape[0]
    tm = min(TM_DEST, n_tok)
    return pl.pallas_call(
        _dest_kernel,
        grid=(n_tok // tm,),
        in_specs=[pl.BlockSpec((tm, LANES), lambda i: (i, 0)), pl.BlockSpec((1, LANES), lambda i: (0, 0))],
        out_specs=pl.BlockSpec((DEST_ROWS, tm), lambda i: (0, i)),
        out_shape=jax.ShapeDtypeStruct((DEST_ROWS, n_tok), jnp.int32),
        scratch_shapes=[pltpu.VMEM((1, LANES), F32)],
        compiler_params=_cparams(1),
    )(ri2, pstart)


def _store_row_tiles(ref, val):
    n = val.shape[0]
    for c in range(ROW_TILE):
        ref[pl.ds(c, n, stride=ROW_TILE), :] = val[:, c * LANES:(c + 1) * LANES]


def _load_row_tiles(ref, n):
    return [ref[pl.ds(c, n, stride=ROW_TILE), :] for c in range(ROW_TILE)]


def _token_rows(ref, t):
    return ref.at[pl.ds(pl.multiple_of(t * ROW_TILE, ROW_TILE), ROW_TILE), :]


def _dispatch_kernel(cnt_ref, pstart_ref, da_ref, db_ref, xt_ref, xs_ref, zrow, sem, zsem):
    tm = da_ref.shape[2]

    @pl.when(pl.program_id(0) == 0)
    def _():
        zrow[...] = jnp.zeros_like(zrow)

        def per_expert(e, c):
            used = cnt_ref[e]
            padded = (used + MOE_TB - 1) // MOE_TB * MOE_TB
            base = pstart_ref[e]

            def fill(r, c2):
                pltpu.make_async_copy(zrow, _token_rows(xs_ref, base + r), zsem).start()
                return c2

            def fill_done(r, c2):
                pltpu.make_async_copy(zrow, _token_rows(xs_ref, 0), zsem).wait()
                return c2

            lax.fori_loop(used, padded, fill, 0)
            lax.fori_loop(used, padded, fill_done, 0)
            return c

        lax.fori_loop(0, N_EXPERTS, per_expert, 0)

        last = N_EXPERTS - 1
        first_unused = (pstart_ref[last] + (cnt_ref[last] + MOE_TB - 1) // MOE_TB * MOE_TB) // MOE_TB

        def per_block(j, c):
            def fill(r, c2):
                pltpu.make_async_copy(zrow, _token_rows(xs_ref, j * MOE_TB + r), zsem).start()
                return c2

            def fill_done(r, c2):
                pltpu.make_async_copy(zrow, _token_rows(xs_ref, 0), zsem).wait()
                return c2

            lax.fori_loop(0, MOE_TB, fill, 0, unroll=DMA_UNROLL)
            lax.fori_loop(0, MOE_TB, fill_done, 0, unroll=DMA_UNROLL)
            return c

        lax.fori_loop(first_unused, xs_ref.shape[0] // (ROW_TILE * MOE_TB), per_block, 0)

    def issue(t, c):
        pltpu.make_async_copy(_token_rows(xt_ref, t), _token_rows(xs_ref, da_ref[0, 0, t]), sem).start(priority=0)
        pltpu.make_async_copy(_token_rows(xt_ref, t), _token_rows(xs_ref, db_ref[0, 0, t]), sem).start(priority=1)
        return c

    lax.fori_loop(0, tm, issue, 0, unroll=DMA_UNROLL)

    def drain(t, c):
        pltpu.make_async_copy(_token_rows(xt_ref, 0), _token_rows(xs_ref, 0), sem).wait()
        pltpu.make_async_copy(_token_rows(xt_ref, 0), _token_rows(xs_ref, 0), sem).wait()
        return c

    lax.fori_loop(0, tm, drain, 0, unroll=DMA_UNROLL)


def _dispatch(cnt, pstart, da, db, xt_rows, n_pad):
    n_tiles, _, tm = da.shape
    smem = pl.BlockSpec((1, 1, tm), lambda i, c, p: (i, 0, 0), memory_space=pltpu.SMEM)
    grid_spec = pltpu.PrefetchScalarGridSpec(
        num_scalar_prefetch=2,
        grid=(n_tiles,),
        in_specs=[smem, smem, pl.BlockSpec((tm * ROW_TILE, LANES), lambda i, c, p: (i, 0))],
        out_specs=pl.BlockSpec(memory_space=pl.ANY),
        scratch_shapes=[pltpu.VMEM((ROW_TILE, LANES), F32), pltpu.SemaphoreType.DMA(()),
                        pltpu.SemaphoreType.DMA(())],
    )
    return pl.pallas_call(
        _dispatch_kernel,
        grid_spec=grid_spec,
        out_shape=jax.ShapeDtypeStruct((n_pad * ROW_TILE, LANES), F32),
        compiler_params=pltpu.CompilerParams(dimension_semantics=("arbitrary",), has_side_effects=True,
                                             vmem_limit_bytes=VMEM_LIMIT),
    )(cnt, pstart, da, db, xt_rows)


def _ffn_kernel(bexp_ref, nused_ref, xs_ref, w1_ref, w3_ref, w2_ref, ys_ref):
    del bexp_ref
    j = pl.program_id(0)

    @pl.when(j < nused_ref[0])
    def _():
        xb = jnp.concatenate(_load_row_tiles(xs_ref, MOE_TB), axis=-1).astype(BF16)
        a = _dot(xb, w1_ref[0].astype(BF16))
        h = a * _sigmoid(a) * _dot(xb, w3_ref[0].astype(BF16))
        _store_row_tiles(ys_ref, _dot(h.astype(BF16), w2_ref[0].astype(BF16)))

    @pl.when(j >= nused_ref[0])
    def _():
        ys_ref[...] = jnp.zeros_like(ys_ref)


def _ffn(blk_exp, n_used, xs, w1, w3, w2):
    n_blocks = xs.shape[0] // (MOE_TB * ROW_TILE)
    rows = pl.BlockSpec((MOE_TB * ROW_TILE, LANES), lambda j, be, nu: (j, 0))
    used_rows = pl.BlockSpec((MOE_TB * ROW_TILE, LANES), lambda j, be, nu: (jnp.minimum(j, nu[0] - 1), 0))
    grid_spec = pltpu.PrefetchScalarGridSpec(
        num_scalar_prefetch=2,
        grid=(n_blocks,),
        in_specs=[used_rows,
                  pl.BlockSpec((1, D_MODEL, D_EXPERT), lambda j, be, nu: (be[j], 0, 0)),
                  pl.BlockSpec((1, D_MODEL, D_EXPERT), lambda j, be, nu: (be[j], 0, 0)),
                  pl.BlockSpec((1, D_EXPERT, D_MODEL), lambda j, be, nu: (be[j], 0, 0))],
        out_specs=rows,
    )
    return pl.pallas_call(
        _ffn_kernel,
        grid_spec=grid_spec,
        out_shape=jax.ShapeDtypeStruct(xs.shape, F32),
        compiler_params=_cparams(1),
    )(blk_exp, n_used, xs, w1, w3, w2)


def _combine_kernel(da_ref, db_ref, da_next_ref, db_next_ref, h_ref, rw_ref, ys_ref, o_ref, ya, yb, sems):
    tm = da_ref.shape[2]
    i = pl.program_id(0)
    slot = i % 2

    def start_gather(a_ref, b_ref, s):
        def issue(t, c):
            pltpu.make_async_copy(_token_rows(ys_ref, a_ref[0, 0, t]), _token_rows(ya.at[s], t),
                                  sems.at[s]).start(priority=0)
            pltpu.make_async_copy(_token_rows(ys_ref, b_ref[0, 0, t]), _token_rows(yb.at[s], t),
                                  sems.at[s]).start(priority=1)
            return c

        lax.fori_loop(0, tm, issue, 0, unroll=DMA_UNROLL)

    @pl.when(i == 0)
    def _():
        start_gather(da_ref, db_ref, slot)

    @pl.when(i + 1 < pl.num_programs(0))
    def _():
        start_gather(da_next_ref, db_next_ref, 1 - slot)

    def drain(t, c):
        pltpu.make_async_copy(_token_rows(ys_ref, 0), _token_rows(ya.at[slot], 0), sems.at[slot]).wait()
        pltpu.make_async_copy(_token_rows(ys_ref, 0), _token_rows(yb.at[slot], 0), sems.at[slot]).wait()
        return c

    lax.fori_loop(0, tm, drain, 0, unroll=DMA_UNROLL)
    rw = rw_ref[...]
    mix = [rw[:, 0:1] * a + rw[:, 1:2] * b
           for a, b in zip(_load_row_tiles(ya.at[slot], tm), _load_row_tiles(yb.at[slot], tm))]
    o_ref[...] = h_ref[...] + jnp.concatenate(mix, axis=-1)


def _combine(da, db, h2, rw, ys):
    n_tiles, _, tm = da.shape
    n_tok = h2.shape[0]
    smem = pl.BlockSpec((1, 1, tm), lambda i: (i, 0, 0), memory_space=pltpu.SMEM)
    smem_next = pl.BlockSpec((1, 1, tm), lambda i: (jnp.minimum(i + 1, n_tiles - 1), 0, 0), memory_space=pltpu.SMEM)
    row = lambda w: pl.BlockSpec((tm, w), lambda i: (i, 0))
    slots = pltpu.VMEM((2, tm * ROW_TILE, LANES), F32)
    return pl.pallas_call(
        _combine_kernel,
        grid=(n_tiles,),
        in_specs=[smem, smem, smem_next, smem_next, row(D_MODEL), row(LANES), pl.BlockSpec(memory_space=pl.ANY)],
        out_specs=row(D_MODEL),
        out_shape=jax.ShapeDtypeStruct((n_tok, D_MODEL), F32),
        scratch_shapes=[slots, slots, pltpu.SemaphoreType.DMA((2,))],
        compiler_params=_cparams(1),
    )(da, db, da, db, h2, rw, ys)


def _rel_bucket_np(dist):
    n = np.maximum(dist, 0)
    max_exact = NUM_BUCKETS // 2
    nf = np.maximum(n, 1).astype(np.float32)
    large = max_exact + (np.log(nf / max_exact) / math.log(MAX_DIST / max_exact)
                         * (NUM_BUCKETS - max_exact)).astype(np.int32)
    large = np.minimum(large, NUM_BUCKETS - 1)
    return np.where(n < max_exact, n, large).astype(np.int32)


def _toeplitz(vec, rows):
    width = vec.shape[-1] - 1
    flat = jnp.tile(vec, (1,) * (vec.ndim - 1) + (rows,))[..., :rows * width]
    return flat.reshape(vec.shape[:-1] + (rows, width))


def _bias_tables(rel_bias, seq):
    n_chunk = seq // CMP_STRIDE
    n_tiles = seq // TQ
    table = rel_bias.T.astype(F32)

    wide = NEAR + TQ
    k = np.arange(wide + 1)
    dw = np.where(k < NEAR, WINDOW - k, WINDOW + wide + 1 - k)
    used = (k < NEAR) | (k > wide + 1 - TQ)
    vals = table[:, _rel_bucket_np(dw)]

    n_var = WINDOW // TQ
    first_key = WINDOW - TQ * np.arange(n_var + 1)[:, None, None]
    in_seq = np.arange(NEAR)[None, None, :] >= first_key

    def near_tile(valid):
        t = _toeplitz(jnp.where(valid[None, :], vals, NEG_INF), TQ)[:, :, :NEAR]
        t = t.reshape(NSA_KV, NSA_HPG, TQ, NEAR).transpose(1, 0, 2, 3).reshape(1, QROWS2, NEAR)
        return jnp.where(in_seq, t, NEG_INF)

    bias_w = near_tile(used & (dw >= 0) & (dw < WINDOW))
    bias_s = near_tile(used & (dw >= 0))
    bias_far = table[:, NUM_BUCKETS - 1].reshape(NSA_KV, NSA_HPG, 1).transpose(1, 0, 2)
    bias_far = jnp.broadcast_to(bias_far, (NSA_HPG, NSA_KV, TQ)).reshape(QROWS2, 1)

    r = np.arange(CMP_STRIDE)[:, None]
    k = np.arange(2 * n_chunk + 1)[None, :]
    lag = 2 * n_chunk + 1 - k
    valid = (k > n_chunk + 1) & (CMP_STRIDE * lag + r >= CMP_L - 1)
    vals = table[:, _rel_bucket_np(CMP_STRIDE * lag + r - CMP_L // 2)]
    full = _toeplitz(jnp.where(valid[None], vals, NEG_INF), n_chunk)[..., :n_chunk]
    full = jnp.where(np.arange(n_chunk) < n_chunk - 1, full, NEG_INF)
    a4 = TQ // CMP_STRIDE
    full = full.reshape(NSA_KV, NSA_HPG, CMP_STRIDE, n_tiles, a4, n_chunk).transpose(3, 1, 0, 4, 2, 5)
    bias_c = full.reshape(n_tiles, QROWS2, n_chunk)
    return tuple(LOG2E * t for t in (bias_c, bias_w, bias_s, bias_far))


def _selection_tables(seq):
    n_chunk = seq // CMP_STRIDE
    n_blk = seq // SEL_L
    c = np.arange(n_chunk)
    n = np.arange(n_blk)
    start = c * CMP_STRIDE
    overlap_t = ((start[None, :] <= n[:, None] * SEL_L + SEL_L - 1) & (start[None, :] + CMP_L - 1 >= n[:, None] * SEL_L)
                 & (c < n_chunk - 1)[None, :])
    pos = np.arange(seq + WINDOW) - WINDOW
    lane_blk = np.arange(LANES) % n_blk
    hit = (pos[:, None] >= 0) & (pos[:, None] // SEL_L == lane_blk[None, :]) & (np.arange(LANES) < 2 * n_blk)[None, :]
    return jnp.asarray(overlap_t, BF16), jnp.asarray(np.where(hit, -UNSEL_PENALTY, 0.0), BF16)


def _block_ones(width, group):
    idx = np.arange(width) // group
    return jnp.asarray((idx[:, None] == idx[None, :]) / group, BF16)


def _block_diag(w):
    nb, n, m = w.shape
    eye = jnp.eye(nb, dtype=w.dtype)
    return jnp.einsum('hij,hg->higj', w, eye).reshape(nb * n, nb * m)


def _compress_weights(w1, w2, pos):
    half_l = CMP_L // 2
    parts = []
    for half in range(2):
        wh = w1[half * half_l * HEAD_DIM:(half + 1) * half_l * HEAD_DIM].reshape(half_l, HEAD_DIM, CMP_HIDDEN)
        z = jnp.zeros_like(wh)
        for g in range(NSA_KV):
            grp = [wh if gg == g else z for gg in range(NSA_KV)]
            parts.append(jnp.stack(grp, axis=1).reshape(half_l * KV_W, CMP_HIDDEN))
    w1cat = jnp.concatenate(parts, axis=1).astype(BF16)
    w2bd = _block_diag(jnp.stack([w2] * NSA_KV)).astype(BF16)
    prow = [jnp.tile(pos[half * half_l:(half + 1) * half_l][:, None, :], (1, NSA_KV, 1)).reshape(-1)
            for half in range(2)]
    pmat = jnp.zeros((8, half_l * KV_W), F32).at[0].set(prow[0]).at[1].set(prow[1]).astype(BF16)
    return w1cat, w2bd, pmat


def kernel(x, mem, rel_bias, norm_mix, w_in, rg_conv_w, rg_conv_b, rg_w_r, rg_b_r, rg_w_i, rg_b_i, rg_lambda, nsa_g_q, nsa_g_kc, nsa_g_ks, nsa_g_kw, cmp_pos_k, cmp_pos_v, cmp_k_w1, cmp_k_w2, cmp_v_w1, cmp_v_w2, out_g_rg, out_g_nsa, w_out, norm_x, norm_mem, xa_w_q, xa_w_kv, xa_w_o, xa_g_q, xa_g_k, norm_moe, router_g_w, router_g_b, router_e_w, router_e_b, exp_w1, exp_w3, exp_w2):
    bsz, seq, _ = x.shape
    n_tok = bsz * seq
    assert seq % FAR_TK == 0 and 2 * (seq // SEL_L) <= LANES and norm_mix.shape[0] == 1
    l = 0
    row = lambda v: v.reshape(1, -1).astype(F32)

    perm = np.array([(half * NSA_HPG + p) * HEAD_DIM + d
                     for p in range(NSA_HPG) for half in range(NSA_KV) for d in range(HEAD_DIM)])
    offs = np.cumsum([0, RG_WIDTH, RG_WIDTH, NSA_WIDTH] + [KV_W] * 6)
    w = w_in[l]
    wrg = w[:, :offs[2]].astype(BF16)
    wq = w[:, offs[2]:offs[3]][:, perm].astype(BF16)
    wkv = w[:, offs[3]:offs[9]].astype(BF16)
    wgl = jnp.pad(w[:, offs[9]:], ((0, 0), (0, LANES - 3 * NSA_HEADS))).astype(BF16)
    ones64 = _block_ones(NSA_WIDTH, HEAD_DIM)
    gq = row(jnp.tile(nsa_g_q[l], NSA_HEADS) * (HEAD_DIM ** -0.5 * LOG2E))
    u, gate, q, kc, vc, ks, vs, kw, vw, gates = _inproj(
        x.reshape(n_tok, D_MODEL), row(norm_mix[l]), wrg, wq, wkv, wgl, gq,
        row(jnp.tile(nsa_g_ks[l], NSA_KV)), row(jnp.tile(nsa_g_kw[l], NSA_KV)), ones64)

    wg = (0.5 * jnp.concatenate([_block_diag(rg_w_r[l]), _block_diag(rg_w_i[l])], axis=1)).astype(BF16)
    bg = 0.5 * jnp.concatenate([rg_b_r[l], rg_b_i[l]]).reshape(1, -1)
    y_rg = _rglru(u.reshape(bsz, seq, RG_WIDTH), gate.reshape(bsz, seq, RG_WIDTH),
                  rg_conv_w[l].reshape(CONV_W, RG_WIDTH), row(rg_conv_b[l]), wg, bg, row(rg_lambda[l]),
                  row(out_g_rg[l]))

    n_chunk = seq // CMP_STRIDE
    w1k, w2k, pk = _compress_weights(cmp_k_w1[l], cmp_k_w2[l], cmp_pos_k[l])
    w1v, w2v, pv = _compress_weights(cmp_v_w1[l], cmp_v_w2[l], cmp_pos_v[l])
    kcmp, vcmp = _compress(kc.reshape(bsz, n_chunk, CMP_STRIDE * KV_W), vc.reshape(bsz, n_chunk, CMP_STRIDE * KV_W),
                           w1k, w2k, pk, w1v, w2v, pv, row(jnp.tile(nsa_g_kc[l], NSA_KV)),
                           ones64[:KV_W, :KV_W])
    padw = lambda t: jnp.pad(t.reshape(bsz, seq, KV_W), ((0, 0), (WINDOW, 0), (0, 0)))
    bias_c, bias_w, bias_s, bias_far = _bias_tables(rel_bias, seq)
    overlap_t, penalty = _selection_tables(seq)
    ksx = jnp.concatenate([padw(ks), jnp.broadcast_to(penalty, (bsz,) + penalty.shape)], axis=-1)
    y_nsa = _nsa(q.reshape(bsz, seq, NSA_WIDTH), gates.reshape(bsz, seq, LANES), kcmp, vcmp,
                 ksx, padw(vs), padw(kw), padw(vw), overlap_t, bias_c, bias_w, bias_s, bias_far,
                 row(out_g_nsa[l][perm]))

    kx, vx = _memkv(mem, row(norm_mem[l]), xa_w_kv[l].astype(BF16), row(xa_g_k[l]))
    wo_mix = w_out[l]
    wr = jnp.pad(jnp.concatenate([router_g_w[l], router_e_w[l]], axis=1),
                 ((0, 0), (0, LANES - N_GROUPS - N_EXPERTS)))
    wr_hi = wr.astype(BF16)
    br = jnp.pad(jnp.concatenate([router_g_b[l], router_e_b[l]]), (0, LANES - N_GROUPS - N_EXPERTS)).reshape(1, -1)
    h2, xt, rw, ri, counts = _mid(
        x, y_rg, y_nsa, wo_mix[:RG_WIDTH].astype(BF16), wo_mix[RG_WIDTH:][perm].astype(BF16), row(norm_x[l]),
        xa_w_q[l].astype(BF16), row(xa_g_q[l] * (X_HEAD_DIM ** -0.5 * LOG2E)), kx, vx, xa_w_o[l].astype(BF16),
        row(norm_moe[l]), jnp.concatenate([wr_hi, (wr - wr_hi.astype(F32)).astype(BF16)], axis=1), br)

    n_slots = 2 * n_tok
    n_blocks = n_slots // MOE_TB + N_EXPERTS
    n_pad = n_blocks * MOE_TB
    cnt = counts[0, :N_EXPERTS].astype(jnp.int32)
    pcnt = (cnt + MOE_TB - 1) // MOE_TB * MOE_TB
    pends = jnp.cumsum(pcnt)
    pstart = jnp.pad((pends - pcnt).astype(F32), (0, LANES - N_EXPERTS)).reshape(1, LANES)
    blk_exp = jnp.minimum(jnp.sum(pends[None, :] <= jnp.arange(n_blocks, dtype=jnp.int32)[:, None] * MOE_TB, axis=1),
                          N_EXPERTS - 1).astype(jnp.int32)
    n_used = (pends[-1:] // MOE_TB).astype(jnp.int32)
    dest = _dest(ri.reshape(n_tok, LANES), pstart)
    tmd = min(TM_DMA, n_tok)
    da = dest[0].reshape(n_tok // tmd, 1, tmd)
    db = dest[1].reshape(n_tok // tmd, 1, tmd)
    xs = _dispatch(cnt, (pends - pcnt).astype(jnp.int32), da, db, xt, n_pad)
    ys = _ffn(blk_exp, n_used, xs, exp_w1[l], exp_w3[l], exp_w2[l])
    out = _combine(da, db, h2.reshape(n_tok, D_MODEL), rw.reshape(n_tok, LANES), ys)
    return out.reshape(bsz, seq, D_MODEL)
```
